```python
import math
import jax, jax.numpy as jnp
from jax import lax
import numpy as np

D_MODEL = 1024
BATCH = 8
SEQ = 2048
DEPTH = 2
DEC_BATCH = 128
DEC_SEQ = 4
PAST_LEN = 16384
PAGE_SIZE = 128

N_MIXERS = 2
N_RWKV_LAYERS = (DEPTH + 1) // 2
N_MLSTM_LAYERS = DEPTH // 2

RW_HEAD = 64
RW_HEADS = D_MODEL // RW_HEAD
RW_DECAY_LORA = 64
RW_AAA_LORA = 64
RW_GATE_LORA = 160
RW_GN_EPS = 64e-5

ML_HEADS = 8
ML_DV = D_MODEL // ML_HEADS
ML_DK = ML_DV // 2
ML_QK = ML_HEADS * ML_DK
ML_CONV = 4
ML_CHUNK = 64
ML_IN = 2 * ML_QK + ML_HEADS * ML_DV + D_MODEL + 2 * ML_HEADS

D_FF = 2816
NORM_EPS = 1e-6

RW_NAMES = ('rw_mu', 'rw_wr', 'rw_wk', 'rw_wv', 'rw_wo', 'rw_w0', 'rw_w1', 'rw_w2', 'rw_a0', 'rw_a1', 'rw_a2',
            'rw_g1', 'rw_g2', 'rw_k_k', 'rw_k_a', 'rw_r_k', 'rw_gn_w', 'rw_gn_b')
ML_NAMES = ('ml_w_in', 'ml_b_if', 'ml_conv_w', 'ml_conv_b', 'ml_norm_w', 'ml_w_out')

kernel_name = 'rwkv7_mlstm_macaron_step'


def rmsnorm(x, g):
    xf = x.astype(jnp.float32)
    y = xf * lax.rsqrt(jnp.mean(xf * xf, -1, keepdims=True) + NORM_EPS)
    return (y * g.astype(jnp.float32)).astype(x.dtype)


def swiglu(x, wg, wu, wd):
    return (jax.nn.silu(x @ wg) * (x @ wu)) @ wd


def rwkv7_time_mix(xn, shift0, S0, mu, wr, wk, wv, wo, w0, w1, w2, a0, a1, a2, g1, g2, k_k, k_a, r_k, gn_w, gn_b):
    B, T, D = xn.shape
    H, N = RW_HEADS, RW_HEAD
    xf = xn.astype(jnp.float32)
    xprev = jnp.concatenate([shift0.astype(jnp.float32)[:, None], xf[:, :-1]], axis=1)
    xx = xprev - xf
    mu = mu.astype(jnp.float32)
    xr, xw, xk, xv, xa, xg = [xf + xx * mu[c] for c in range(6)]
    r = xr @ wr
    w = -jax.nn.softplus(-(w0 + jnp.tanh(xw @ w1) @ w2)) - 0.5
    k = xk @ wk
    v = xv @ wv
    a = jax.nn.sigmoid(a0 + (xa @ a1) @ a2)
    g = jax.nn.sigmoid(xg @ g1) @ g2
    heads = lambda z: z.reshape(B, T, H, N).astype(jnp.float32)
    kk = heads(k * k_k)
    kk = kk / jnp.maximum(jnp.linalg.norm(kk, axis=-1, keepdims=True), 1e-12)
    k = k * (1.0 + (a - 1.0) * k_a)
    r_h, k_h, v_h, a_h = heads(r), heads(k), heads(v), heads(a)
    decay = jnp.exp(-jnp.exp(heads(w)))
    b_h = kk * a_h

    def step(S, inp):
        r_t, d_t, k_t, v_t, kk_t, b_t = inp
        sa = jnp.einsum('bhvk,bhk->bhv', S, -kk_t)
        S = S * d_t[:, :, None, :] + sa[..., None] * b_t[:, :, None, :] + v_t[..., None] * k_t[:, :, None, :]
        return S, jnp.einsum('bhvk,bhk->bhv', S, r_t)

    tm = lambda z: jnp.swapaxes(z, 0, 1)
    S_T, y = lax.scan(step, S0.astype(jnp.float32), (tm(r_h), tm(decay), tm(k_h), tm(v_h), tm(kk), tm(b_h)))
    y = tm(y)
    mean = jnp.mean(y, -1, keepdims=True)
    var = jnp.mean(jnp.square(y - mean), -1, keepdims=True)
    y = ((y - mean) * lax.rsqrt(var + RW_GN_EPS)).reshape(B, T, D) * gn_w + gn_b
    bonus = jnp.sum(r_h * k_h * r_k, -1, keepdims=True) * v_h
    y = y + bonus.reshape(B, T, D)
    out = (y * g) @ wo
    return out.astype(xn.dtype), S_T, xf[:, -1]


def mlstm_mix(xn, conv0, C0, n0, m0, w_in, b_if, conv_w, conv_b, norm_w, w_out):
    B, T, D = xn.shape
    H, DK, DV = ML_HEADS, ML_DK, ML_DV
    proj = (xn @ w_in).astype(jnp.float32)
    qk_raw, v, o_pre, if_pre = jnp.split(proj, [2 * ML_QK, 2 * ML_QK + H * DV, 2 * ML_QK + H * DV + D], axis=-1)
    xpad = jnp.concatenate([conv0.astype(jnp.float32), qk_raw], axis=1)
    qk = conv_b.astype(jnp.float32) + sum(conv_w[j].astype(jnp.float32) * xpad[:, j:j + T] for j in range(ML_CONV))
    qk = jax.nn.silu(qk)
    new_conv = xpad[:, T:]
    q, k = jnp.split(qk, 2, axis=-1)
    q = q.reshape(B, T, H, DK) * (DK ** -0.5)
    k = k.reshape(B, T, H, DK)
    v = v.reshape(B, T, H, DV)
    o = jax.nn.sigmoid(o_pre)
    if_pre = if_pre + b_if.astype(jnp.float32)
    logi = if_pre[..., :H]
    logf = jax.nn.log_sigmoid(if_pre[..., H:])

    L = math.gcd(T, ML_CHUNK)
    NC = T // L

    def chunks(z):
        z = z.reshape(B, NC, L, H, *z.shape[3:])
        return jnp.moveaxis(z, (1, 3), (0, 2))

    causal = jnp.tril(jnp.ones((L, L), bool))

    def chunk_step(carry, inp):
        C, n, m = carry
        q_c, k_c, v_c, li, lf = inp
        b = jnp.cumsum(lf, axis=-1)
        g_inter = b + m[..., None]
        Dlog = jnp.where(causal, b[..., :, None] - b[..., None, :] + li[..., None, :], -jnp.inf)
        m_t = jnp.maximum(g_inter, Dlog.max(-1))
        Dw = jnp.exp(Dlog - m_t[..., None])
        w_inter = jnp.exp(g_inter - m_t)
        S = jnp.einsum('bhtk,bhsk->bhts', q_c, k_c) * Dw
        num = w_inter[..., None] * jnp.einsum('bhvk,bhtk->bhtv', C, q_c) + jnp.einsum('bhts,bhsv->bhtv', S, v_c)
        den = w_inter * jnp.einsum('bhk,bhtk->bht', n, q_c) + S.sum(-1)
        h = num / jnp.maximum(jnp.abs(den), jnp.exp(-m_t))[..., None]
        m_new = m_t[..., -1]
        w_state = jnp.exp(b[..., -1] + m - m_new)
        w_s = jnp.exp(b[..., -1:] - b + li - m_new[..., None])
        C = w_state[..., None, None] * C + jnp.einsum('bhs,bhsv,bhsk->bhvk', w_s, v_c, k_c)
        n = w_state[..., None] * n + jnp.einsum('bhs,bhsk->bhk', w_s, k_c)
        return (C, n, m_new), h

    carry0 = (C0.astype(jnp.float32), n0.astype(jnp.float32), m0.astype(jnp.float32))
    (C_T, n_T, m_T), h = lax.scan(chunk_step, carry0, (chunks(q), chunks(k), chunks(v), chunks(logi), chunks(logf)))
    h = jnp.moveaxis(h, (0, 2), (1, 3)).reshape(B, T, H, DV)
    h = h * lax.rsqrt(jnp.mean(h * h, -1, keepdims=True) + NORM_EPS)
    h = h.reshape(B, T, H * DV) * norm_w
    out = (h * o) @ w_out
    return out.astype(xn.dtype), C_T, n_T, m_T, new_conv


def trunk(x, rw_S, rw_shift, ml_C, ml_n, ml_m, ml_conv, p):
    new_rw_S, new_rw_shift = [], []
    new_C, new_n, new_m, new_conv = [], [], [], []
    for i in range(DEPTH):
        x = x + 0.5 * swiglu(rmsnorm(x, p['norm_ffa'][i]), p['ffa_wg'][i], p['ffa_wu'][i], p['ffa_wd'][i])
        h = rmsnorm(x, p['norm_mix'][i])
        j = i // N_MIXERS
        if i % N_MIXERS == 0:
            out, S, sh = rwkv7_time_mix(h, rw_shift[j], rw_S[j], *[p[nm][j] for nm in RW_NAMES])
            new_rw_S.append(S)
            new_rw_shift.append(sh)
        else:
            out, C, n, m, cb = mlstm_mix(h, ml_conv[j], ml_C[j], ml_n[j], ml_m[j], *[p[nm][j] for nm in ML_NAMES])
            new_C.append(C)
            new_n.append(n)
            new_m.append(m)
            new_conv.append(cb)
        x = x + out
        x = x + 0.5 * swiglu(rmsnorm(x, p['norm_ffb'][i]), p['ffb_wg'][i], p['ffb_wu'][i], p['ffb_wd'][i])
    y = rmsnorm(x, p['norm_final'])
    return (y, jnp.stack(new_rw_S), jnp.stack(new_rw_shift), jnp.stack(new_C), jnp.stack(new_n),
            jnp.stack(new_m), jnp.stack(new_conv))


def setup_inputs(seed: int = 0) -> dict:
    key = jax.random.key(seed)
    keys = iter(jax.random.split(key, 64))

    def nrm(shape, scale):
        return scale * jax.random.normal(next(keys), shape, jnp.float32)

    def uni(shape, lo, hi):
        return jax.random.uniform(next(keys), shape, jnp.float32, lo, hi)

    def gain(shape):
        return 1.0 + nrm(shape, 0.02)

    NR, NM, D = N_RWKV_LAYERS, N_MLSTM_LAYERS, D_MODEL
    return {
        'x_prompt': nrm((BATCH, SEQ, D), 1.0),
        'x_sample': nrm((DEC_BATCH, DEC_SEQ, D), 1.0),
        'state_rwkv_S': nrm((NR, DEC_BATCH, RW_HEADS, RW_HEAD, RW_HEAD), 0.2),
        'state_rwkv_shift': nrm((NR, DEC_BATCH, D), 1.0),
        'state_mlstm_C': nrm((NM, DEC_BATCH, ML_HEADS, ML_DV, ML_DK), 0.1),
        'state_mlstm_n': nrm((NM, DEC_BATCH, ML_HEADS, ML_DK), 0.1),
        'state_mlstm_m': nrm((NM, DEC_BATCH, ML_HEADS), 1.0),
        'state_mlstm_conv': nrm((NM, DEC_BATCH, ML_CONV - 1, 2 * ML_QK), 1.0),
        'norm_ffa': gain((DEPTH, D)),
        'ffa_wg': nrm((DEPTH, D, D_FF), D ** -0.5),
        'ffa_wu': nrm((DEPTH, D, D_FF), D ** -0.5),
        'ffa_wd': nrm((DEPTH, D_FF, D), D_FF ** -0.5),
        'norm_mix': gain((DEPTH, D)),
        'norm_ffb': gain((DEPTH, D)),
        'ffb_wg': nrm((DEPTH, D, D_FF), D ** -0.5),
        'ffb_wu': nrm((DEPTH, D, D_FF), D ** -0.5),
        'ffb_wd': nrm((DEPTH, D_FF, D), D_FF ** -0.5),
        'rw_mu': uni((NR, 6, D), 0.0, 1.0),
        'rw_wr': nrm((NR, D, D), D ** -0.5),
        'rw_wk': nrm((NR, D, D), D ** -0.5),
        'rw_wv': nrm((NR, D, D), D ** -0.5),
        'rw_wo': nrm((NR, D, D), D ** -0.5),
        'rw_w0': uni((NR, D), -6.0, -1.0),
        'rw_w1': nrm((NR, D, RW_DECAY_LORA), D ** -0.5),
        'rw_w2': nrm((NR, RW_DECAY_LORA, D), 0.1 * RW_DECAY_LORA ** -0.5),
        'rw_a0': nrm((NR, D), 0.1),
        'rw_a1': nrm((NR, D, RW_AAA_LORA), D ** -0.5),
        'rw_a2': nrm((NR, RW_AAA_LORA, D), 0.1 * RW_AAA_LORA ** -0.5),
        'rw_g1': nrm((NR, D, RW_GATE_LORA), D ** -0.5),
        'rw_g2': nrm((NR, RW_GATE_LORA, D), RW_GATE_LORA ** -0.5),
        'rw_k_k': 0.85 + nrm((NR, D), 0.05),
        'rw_k_a': 1.0 + nrm((NR, D), 0.05),
        'rw_r_k': nrm((NR, RW_HEADS, RW_HEAD), 0.1),
        'rw_gn_w': gain((NR, D)),
        'rw_gn_b': nrm((NR, D), 0.02),
        'ml_w_in': nrm((NM, D, ML_IN), D ** -0.5),
        'ml_b_if': jnp.concatenate([nrm((NM, ML_HEADS), 0.1), 3.0 + nrm((NM, ML_HEADS), 0.5)], axis=-1),
        'ml_conv_w': nrm((NM, ML_CONV, 2 * ML_QK), 0.5),
        'ml_conv_b': nrm((NM, 2 * ML_QK), 0.02),
        'ml_norm_w': gain((NM, ML_HEADS * ML_DV)),
        'ml_w_out': nrm((NM, ML_HEADS * ML_DV, D), (ML_HEADS * ML_DV) ** -0.5),
        'norm_final': gain((D,)),
    }


def reference(x_prompt, x_sample, state_rwkv_S, state_rwkv_shift, state_mlstm_C, state_mlstm_n, state_mlstm_m,
              state_mlstm_conv, norm_ffa, ffa_wg, ffa_wu, ffa_wd, norm_mix, norm_ffb, ffb_wg, ffb_wu, ffb_wd,
              rw_mu, rw_wr, rw_wk, rw_wv, rw_wo, rw_w0, rw_w1, rw_w2, rw_a0, rw_a1, rw_a2, rw_g1, rw_g2,
              rw_k_k, rw_k_a, rw_r_k, rw_gn_w, rw_gn_b, ml_w_in, ml_b_if, ml_conv_w, ml_conv_b, ml_norm_w,
              ml_w_out, norm_final):
    p = dict(norm_ffa=norm_ffa, ffa_wg=ffa_wg, ffa_wu=ffa_wu, ffa_wd=ffa_wd, norm_mix=norm_mix,
             norm_ffb=norm_ffb, ffb_wg=ffb_wg, ffb_wu=ffb_wu, ffb_wd=ffb_wd,
             rw_mu=rw_mu, rw_wr=rw_wr, rw_wk=rw_wk, rw_wv=rw_wv, rw_wo=rw_wo, rw_w0=rw_w0, rw_w1=rw_w1,
             rw_w2=rw_w2, rw_a0=rw_a0, rw_a1=rw_a1, rw_a2=rw_a2, rw_g1=rw_g1, rw_g2=rw_g2, rw_k_k=rw_k_k,
             rw_k_a=rw_k_a, rw_r_k=rw_r_k, rw_gn_w=rw_gn_w, rw_gn_b=rw_gn_b,
             ml_w_in=ml_w_in, ml_b_if=ml_b_if, ml_conv_w=ml_conv_w, ml_conv_b=ml_conv_b,
             ml_norm_w=ml_norm_w, ml_w_out=ml_w_out, norm_final=norm_final)
    B = x_prompt.shape[0]
    f32 = jnp.float32
    z_rw_S = jnp.zeros((N_RWKV_LAYERS, B, RW_HEADS, RW_HEAD, RW_HEAD), f32)
    z_rw_shift = jnp.zeros((N_RWKV_LAYERS, B, D_MODEL), f32)
    z_C = jnp.zeros((N_MLSTM_LAYERS, B, ML_HEADS, ML_DV, ML_DK), f32)
    z_n = jnp.zeros((N_MLSTM_LAYERS, B, ML_HEADS, ML_DK), f32)
    z_m = jnp.zeros((N_MLSTM_LAYERS, B, ML_HEADS), f32)
    z_conv = jnp.zeros((N_MLSTM_LAYERS, B, ML_CONV - 1, 2 * ML_QK), f32)
    y_prompt, p_rw_S, p_rw_shift, p_C, p_n, p_m, p_conv = trunk(
        x_prompt, z_rw_S, z_rw_shift, z_C, z_n, z_m, z_conv, p)
    y_sample, s_rw_S, s_rw_shift, s_C, s_n, s_m, s_conv = trunk(
        x_sample, state_rwkv_S, state_rwkv_shift, state_mlstm_C, state_mlstm_n, state_mlstm_m,
        state_mlstm_conv, p)
    return (y_prompt, y_sample, p_rw_S, p_rw_shift, p_C, p_n, p_m, p_conv,
            s_rw_S, s_rw_shift, s_C, s_n, s_m, s_conv)
```

```python
import functools
import math

import jax
import jax.numpy as jnp
from jax import lax
from jax.experimental import pallas as pl
from jax.experimental.pallas import tpu as pltpu

F32 = jnp.float32
BF16 = jnp.bfloat16

NORM_EPS = 1e-6
RW_GN_EPS = 64e-5
RW_HEAD = 64
ML_DK = 64
ML_DV = 128
ML_CONV = 4

LANES = 128
SUBLANES = 8
CHUNK = 64
NEG_BIG = -1e30
VMEM_LIMIT = 56 * 1024 * 1024
HI = lax.Precision.HIGHEST


def _dot(a, b, precision=None):
    return jnp.dot(a, b, preferred_element_type=F32, precision=precision)


def _dot_nt(a, b, precision=None):
    return lax.dot_general(a, b, (((1,), (1,)), ((), ())), preferred_element_type=F32, precision=precision)


def _dot_tn(a, b, precision=None):
    return lax.dot_general(a, b, (((0,), (0,)), ((), ())), preferred_element_type=F32, precision=precision)


def _b(x):
    return x.astype(BF16)


def _rms(x, g):
    return x * lax.rsqrt(jnp.mean(x * x, axis=-1, keepdims=True) + NORM_EPS) * g


def _sigmoid(x):
    return 1.0 / (1.0 + jnp.exp(-x))


def _softplus(x):
    return jnp.maximum(x, 0.0) + jnp.log1p(jnp.exp(-jnp.abs(x)))


def _segsum(x, mlo):
    lo = jnp.sum(jnp.where(mlo, x, 0.0), axis=-1, keepdims=True)
    hi = jnp.sum(jnp.where(mlo, 0.0, x), axis=-1, keepdims=True)
    return jnp.where(mlo, lo, hi)


def _colsel(x, lane_idx, j):
    return jnp.sum(jnp.where(lane_idx == j, x, 0.0), axis=-1, keepdims=True)


def _seq_last(x, lsub):
    n = x.shape[0]
    if lsub == n:
        return x[n - 1:n]
    parts = [jnp.broadcast_to(x[q * lsub + lsub - 1:q * lsub + lsub], (lsub,) + x.shape[1:])
             for q in range(n // lsub)]
    return jnp.concatenate(parts, axis=0)


def _const_spec(shape, single_buffer=False):
    nd = len(shape)
    if single_buffer:
        return pl.BlockSpec(shape, lambda *_: (0,) * nd, pipeline_mode=pl.Buffered(1))
    return pl.BlockSpec(shape, lambda *_: (0,) * nd)


def _log2(n):
    k = int(math.log2(n))
    assert 1 << k == n, n
    return k


def _ffn_kernel(x_ref, g_ref, wg_ref, wu_ref, wd_ref, gf_ref, o_ref, *, fchunk, final_norm):
    x = x_ref[...]
    xb = _b(_rms(x, g_ref[...]))
    nf = wg_ref.shape[1]
    acc = jnp.zeros(x.shape, F32)
    for c in range(nf // fchunk):
        sl = slice(c * fchunk, (c + 1) * fchunk)
        gate = _dot(xb, wg_ref[:, sl])
        up = _dot(xb, wu_ref[:, sl])
        h = _b(gate * _sigmoid(gate) * up)
        acc = acc + _dot(h, wd_ref[sl, :])
    out = x + 0.5 * acc
    if final_norm:
        out = _rms(out, gf_ref[...])
    o_ref[...] = out


def _ffn_call(x, g, wg, wu, wd, gf, *, final_norm, tm):
    n, d = x.shape
    nf = wg.shape[1]
    fchunk = nf // 2 if (nf // 2) % LANES == 0 else nf
    kern = functools.partial(_ffn_kernel, fchunk=fchunk, final_norm=final_norm)
    return pl.pallas_call(
        kern,
        grid=(n // tm,),
        in_specs=[pl.BlockSpec((tm, d), lambda i: (i, 0)),
                  _const_spec((1, d)),
                  _const_spec((d, nf), True), _const_spec((d, nf), True), _const_spec((nf, d), True),
                  _const_spec((1, d))],
        out_specs=pl.BlockSpec((tm, d), lambda i: (i, 0)),
        out_shape=jax.ShapeDtypeStruct((n, d), F32),
        compiler_params=pltpu.CompilerParams(dimension_semantics=("arbitrary",), vmem_limit_bytes=VMEM_LIMIT),
        name="ffn",
    )(x, g, wg, wu, wd, gf)


def _rwkv_kernel(*refs, tm, seq_rows, npad, sample, lsub):
    L = CHUNK
    nsub = L // lsub
    nchunk = tm // L
    nsteps = _log2(lsub)
    it = iter(refs)
    x_ref = next(it)
    shx_ref = next(it) if sample else None
    (gn_ref, mu_ref, wr_ref, wk_ref, wv_ref, wo_ref, w0_ref, w1_ref, w2_ref, a0_ref, a1_ref, a2_ref,
     g1_ref, g2_ref, kk_ref, ka_ref, rk_ref, gw_ref, gb_ref) = [next(it) for _ in range(19)]
    s0_ref = next(it) if sample else None
    o_ref, sout_ref, xn_ref = next(it), next(it), next(it)
    r_s, ld_s, k_s, v_s, kn_s, b_s, g_s, bon_s, yg_s = [next(it) for _ in range(9)]
    if not sample:
        carry_s, sbd_s = next(it), next(it)
    d = x_ref.shape[1]
    npair = d // LANES

    lane = lax.broadcasted_iota(jnp.int32, (1, LANES), 1)
    mlo = lane < RW_HEAD

    x = x_ref[...]
    xn = _rms(x, gn_ref[...])
    rolled = pltpu.roll(xn, 1, 0)
    row = lax.broadcasted_iota(jnp.int32, (tm, 1), 0)
    if sample:
        xprev = jnp.where((row & (seq_rows - 1)) == npad, shx_ref[...], rolled)
        xn_ref[...] = xn
    else:
        j = pl.program_id(1)

        @pl.when(j == 0)
        def _():
            carry_s[...] = jnp.zeros(carry_s.shape, F32)
            sbd_s[...] = jnp.zeros(sbd_s.shape, F32)

        xprev = jnp.where(row == 0, carry_s[SUBLANES - 1:SUBLANES, :], rolled)
        carry_s[...] = xn[tm - SUBLANES:tm, :]
        xn_ref[...] = xn[tm - SUBLANES:tm, :]
    xx = xprev - xn

    def mix(i):
        return _b(xn + xx * mu_ref[i:i + 1, :])

    r = _dot(mix(0), wr_ref[...])
    wl = _b(jnp.tanh(_dot(mix(1), w1_ref[...])))
    w = -_softplus(-(w0_ref[...] + _dot(wl, w2_ref[...]))) - 0.5
    k = _dot(mix(2), wk_ref[...])
    v = _dot(mix(3), wv_ref[...])
    al = _b(_dot(mix(4), a1_ref[...]))
    a = _sigmoid(a0_ref[...] + _dot(al, a2_ref[...]))
    gl = _b(_sigmoid(_dot(mix(5), g1_ref[...])))
    g_s[...] = _dot(gl, g2_ref[...])
    k2 = k * (1.0 + (a - 1.0) * ka_ref[...])
    kk = k * kk_ref[...]
    rkk = r * k2 * rk_ref[...]
    ld = -jnp.exp(w)
    if npad:
        keep = (row & (seq_rows - 1)) >= npad
        ld = jnp.where(keep, ld, 0.0)
        k2 = jnp.where(keep, k2, 0.0)
        kk = jnp.where(keep, kk, 0.0)
        v = jnp.where(keep, v, 0.0)
    r_s[...] = r
    ld_s[...] = ld
    k_s[...] = k2
    v_s[...] = v
    for p in range(npair):
        cs = slice(p * LANES, (p + 1) * LANES)
        kkp = kk[:, cs]
        kn = kkp / jnp.maximum(jnp.sqrt(_segsum(kkp * kkp, mlo)), 1e-12)
        kn_s[:, cs] = kn
        b_s[:, cs] = kn * a[:, cs]
        bon_s[:, cs] = _segsum(rkk[:, cs], mlo) * v[:, cs]

    ti = lax.broadcasted_iota(jnp.int32, (L, L), 0)
    si = lax.broadcasted_iota(jnp.int32, (L, L), 1)
    sh = _log2(lsub)
    tril = jnp.where(((ti >> sh) == (si >> sh)) & (si <= ti), 1.0, 0.0).astype(F32)
    gi = lax.broadcasted_iota(jnp.int32, (2 * L, 4 * L), 0)
    gj = lax.broadcasted_iota(jnp.int32, (2 * L, 4 * L), 1)
    gt = gi & (L - 1)
    gs = gj & (L - 1)
    gmask = ((gt >> sh) == (gs >> sh)) & ((gs < gt) | ((gi >= L) & (gs == gt)))

    def chunk(c, carry):
        r0 = pl.multiple_of(c * L, L)
        rows = pl.ds(r0, L)
        for p in range(npair):
            cs = slice(p * LANES, (p + 1) * LANES)
            rr, ldc, kc, vc, knc, bc = (r_s[rows, cs], ld_s[rows, cs], k_s[rows, cs], v_s[rows, cs],
                                        kn_s[rows, cs], b_s[rows, cs])
            cum = _dot(tril, ldc, HI)
            ep = jnp.exp(cum)
            em = jnp.exp(-cum)
            at = -(knc * jnp.exp(cum - ldc))
            rt = rr * ep
            bt = bc * em
            kt = kc * em
            a2 = jnp.concatenate([at, rt], axis=0)
            q4 = jnp.concatenate([jnp.where(mlo, bt, 0.0), jnp.where(mlo, 0.0, bt),
                                  jnp.where(mlo, kt, 0.0), jnp.where(mlo, 0.0, kt)], axis=0)
            g = jnp.where(gmask, _dot_nt(a2, q4, HI), 0.0)
            nab = g[:L, :LANES]
            nbd = jnp.concatenate([jnp.where(mlo, nab, 0.0), jnp.where(mlo, 0.0, nab)], axis=0)
            vv2 = jnp.concatenate([jnp.where(mlo, vc, 0.0), jnp.where(mlo, 0.0, vc)], axis=0)
            if sample:
                s_in = [s0_ref[c * nsub + q, p] for q in range(nsub)]
                asa, asr = [], []
                for q in range(nsub):
                    qs = slice(q * lsub, (q + 1) * lsub)
                    asq = _dot_nt(jnp.concatenate([at[qs], rt[qs]], axis=0), s_in[q], HI)
                    asa.append(asq[:lsub])
                    asr.append(asq[lsub:])
                as_a = jnp.concatenate(asa, axis=0)
                as_r = jnp.concatenate(asr, axis=0)
            else:
                s_prev = sbd_s[p]
                as_ = _dot_nt(a2, s_prev, HI)
                as_a, as_r = as_[:L], as_[L:]
            rhs = as_a + _dot(g[:L, LANES:], vv2, HI)
            xs = jnp.concatenate([jnp.where(mlo, rhs, 0.0), jnp.where(mlo, 0.0, rhs)], axis=0)
            npow = nbd
            for i in range(nsteps):
                xs = xs + _dot(npow, xs, HI)
                if i + 1 < nsteps:
                    npow = _dot(npow, npow, HI)
            uv = jnp.concatenate([xs, vv2], axis=0)
            y = as_r + _dot(g[L:, :], uv, HI)
            if sample:
                for q in range(nsub):
                    sel = [slice(blk * L + q * lsub, blk * L + (q + 1) * lsub) for blk in range(4)]
                    pl_q = jnp.exp(cum[(q + 1) * lsub - 1:(q + 1) * lsub, :])
                    uvq = jnp.concatenate([uv[s_] for s_ in sel], axis=0)
                    qq = jnp.concatenate([q4[s_] for s_ in sel], axis=0) * pl_q
                    sout_ref[c * nsub + q, p] = s_in[q] * pl_q + _dot_tn(uvq, qq, HI)
            else:
                pl_ = jnp.exp(cum[L - 1:L, :])
                sbd_s[p] = s_prev * pl_ + _dot_tn(uv, q4 * pl_, HI)
            mean = _segsum(y, mlo) * (1.0 / RW_HEAD)
            yc = y - mean
            var = _segsum(yc * yc, mlo) * (1.0 / RW_HEAD)
            yn = yc * lax.rsqrt(var + RW_GN_EPS) * gw_ref[:, cs] + gb_ref[:, cs] + bon_s[rows, cs]
            yg_s[rows, cs] = _b(yn * g_s[rows, cs])
        return carry

    lax.fori_loop(0, nchunk, chunk, 0)
    if not sample:
        sout_ref[0] = sbd_s[...]
    o_ref[...] = x_ref[...] + _dot(yg_s[...], wo_ref[...])


def _rwkv_call(x, shx, s0bd, wts, *, nseq, seq_rows, npad, sample, tm):
    n, d = x.shape
    npair = d // LANES
    lsub = seq_rows if sample else CHUNK
    kern = functools.partial(_rwkv_kernel, tm=tm, seq_rows=seq_rows, npad=npad, sample=sample, lsub=lsub)
    if sample:
        grid = (n // tm,)
        tile = lambda i: (i, 0)
        spt = tm // seq_rows
        st_spec = pl.BlockSpec((spt, npair, LANES, LANES), lambda i: (i, 0, 0, 0))
        xn_spec = pl.BlockSpec((tm, d), tile)
        xn_shape = (n, d)
        sem = ("arbitrary",)
    else:
        tps = seq_rows // tm
        grid = (nseq, tps)
        tile = lambda b, j: (b * tps + j, 0)
        st_spec = pl.BlockSpec((1, npair, LANES, LANES), lambda b, j: (b, 0, 0, 0))
        xn_spec = pl.BlockSpec((SUBLANES, d), lambda b, j: (b, 0))
        xn_shape = (nseq * SUBLANES, d)
        sem = ("arbitrary", "arbitrary")
    x_spec = pl.BlockSpec((tm, d), tile)
    in_specs = [x_spec] + ([x_spec] if sample else [])
    args = [x] + ([shx] if sample else [])
    for wt in wts:
        in_specs.append(_const_spec(wt.shape))
        args.append(wt)
    if sample:
        in_specs.append(st_spec)
        args.append(s0bd)
    scratch = [pltpu.VMEM((tm, d), F32) for _ in range(8)] + [pltpu.VMEM((tm, d), BF16)]
    if not sample:
        scratch += [pltpu.VMEM((SUBLANES, d), F32), pltpu.VMEM((npair, LANES, LANES), F32)]
    return pl.pallas_call(
        kern,
        grid=grid,
        in_specs=in_specs,
        out_specs=[x_spec, st_spec, xn_spec],
        out_shape=[jax.ShapeDtypeStruct((n, d), F32),
                   jax.ShapeDtypeStruct((nseq, npair, LANES, LANES), F32),
                   jax.ShapeDtypeStruct(xn_shape, F32)],
        scratch_shapes=scratch,
        compiler_params=pltpu.CompilerParams(dimension_semantics=sem, vmem_limit_bytes=VMEM_LIMIT),
        name="rwkv_sample" if sample else "rwkv_prompt",
    )(*args)


def _mlstm_kernel(*refs, tm, seq_rows, npad, sample, lsub):
    L = CHUNK
    nsub = L // lsub
    nchunk = tm // L
    it = iter(refs)
    x_ref = next(it)
    if sample:
        cvx_ref, mrow_ref = next(it), next(it)
    (gn_ref, wqk_ref, wv_ref, wo_ref, wif_ref, wift_ref, bif_ref, bift_ref, cw_ref, cb_ref, nw_ref,
     wout_ref) = [next(it) for _ in range(12)]
    if sample:
        c0_ref, n0_ref = next(it), next(it)
    o_ref, cout_ref, nout_ref, mout_ref, cvout_ref = [next(it) for _ in range(5)]
    xb_s, q_s, k_s, v_s, og_s, gc_s, ho_s = [next(it) for _ in range(7)]
    if not sample:
        carry_s, cp_s, np_s, m_s = [next(it) for _ in range(4)]
    nqk = wqk_ref.shape[1]
    half = nqk // 2
    nheads = half // ML_DK
    ngate = 2 * nheads
    npair = nheads // 2

    lane = lax.broadcasted_iota(jnp.int32, (1, LANES), 1)
    mlo = lane < ML_DK
    glane = lax.broadcasted_iota(jnp.int32, (1, ngate), 1)
    isf_c = glane >= nheads
    grow = lax.broadcasted_iota(jnp.int32, (ngate, 1), 0)
    isf_r = grow >= nheads
    hlane = lax.broadcasted_iota(jnp.int32, (1, nheads), 1)

    x = x_ref[...]
    xb = _b(_rms(x, gn_ref[...]))
    xb_s[...] = xb
    raw = _dot(xb, wqk_ref[...])
    row = lax.broadcasted_iota(jnp.int32, (tm, 1), 0)
    if sample:
        srow = row & (seq_rows - 1)
        raw = jnp.where((srow >= npad - (ML_CONV - 1)) & (srow < npad), cvx_ref[...], raw)
        cvout_ref[...] = raw
        ext = jnp.concatenate([jnp.zeros((SUBLANES, nqk), F32), raw], axis=0)
    else:
        j = pl.program_id(1)

        @pl.when(j == 0)
        def _():
            carry_s[...] = jnp.zeros(carry_s.shape, F32)
            cp_s[...] = jnp.zeros(cp_s.shape, F32)
            np_s[...] = jnp.zeros(np_s.shape, F32)
            m_s[...] = jnp.zeros(m_s.shape, F32)

        ext = jnp.concatenate([carry_s[...], raw], axis=0)
        carry_s[...] = raw[tm - SUBLANES:tm, :]
        cvout_ref[...] = raw[tm - SUBLANES:tm, :]
    qk = cb_ref[...] + cw_ref[ML_CONV - 1:ML_CONV, :] * raw
    for s in range(1, ML_CONV):
        qk = qk + cw_ref[ML_CONV - 1 - s:ML_CONV - s, :] * pltpu.roll(ext, s, 0)[SUBLANES:, :]
    qk = qk * _sigmoid(qk)
    q_s[...] = qk[:, :half] * (ML_DK ** -0.5)
    k_s[...] = qk[:, half:]
    v_s[...] = _dot(xb, wv_ref[...])
    og_s[...] = _sigmoid(_dot(xb, wo_ref[...]))
    ifp = _dot(xb, wif_ref[...]) + bif_ref[...]
    gcol = jnp.where(isf_c, -_softplus(-ifp), ifp)
    if npad:
        keep = (row & (seq_rows - 1)) >= npad
        gcol = jnp.where(keep, gcol, jnp.where(isf_c, 0.0, NEG_BIG))
    gc_s[...] = gcol

    ti = lax.broadcasted_iota(jnp.int32, (L, L), 0)
    si = lax.broadcasted_iota(jnp.int32, (L, L), 1)
    sh = _log2(lsub)
    same = (ti >> sh) == (si >> sh)
    causal = same & (si <= ti)
    tril = jnp.where(causal, 1.0, 0.0).astype(F32)
    triu = jnp.where(same & (ti <= si), 1.0, 0.0).astype(F32)
    lrow = lax.broadcasted_iota(jnp.int32, (1, L), 1)

    def chunk(c, carry):
        r0 = pl.multiple_of(c * L, L)
        rows = pl.ds(r0, L)
        gc = gc_s[rows, :]
        gt = _dot_nt(wift_ref[...], xb_s[rows, :]) + bift_ref[...]
        gt = jnp.where(isf_r, -_softplus(-gt), gt)
        if npad:
            gt = jnp.where((lrow & (seq_rows - 1)) >= npad, gt, jnp.where(isf_r, 0.0, NEG_BIG))
        bcs = _dot(tril, jnp.where(isf_c, gc, 0.0), HI)
        brs = _dot(jnp.where(isf_r, gt, 0.0), triu, HI)
        blast = _seq_last(bcs, lsub)
        if sample:
            mcols = mrow_ref[rows, :]
        else:
            mcols = m_s[...]
        mt_all = jnp.zeros((L, nheads), F32)
        for pp in range(npair):
            ps = slice(pp * LANES, (pp + 1) * LANES)
            q2 = q_s[rows, ps]
            k2 = k_s[rows, ps]
            if sample:
                c_in = [c0_ref[c * nsub + q, pp] for q in range(nsub)]
                n_in = [n0_ref[c * nsub + q][:, ps] for q in range(nsub)]
                n_rows = jnp.concatenate([jnp.broadcast_to(n_in[q], (lsub, LANES)) for q in range(nsub)], axis=0)
            else:
                c_prev = cp_s[pp]
                n_prev = np_s[:, ps]
                n_rows = n_prev
            wv_parts, km_parts, ws_cols, wst_cols = [], [], [], []
            for hh in range(2):
                h = 2 * pp + hh
                hs = slice(h * ML_DV, (h + 1) * ML_DV)
                mh = mlo if hh == 0 else jnp.logical_not(mlo)
                vh = v_s[rows, hs]
                bcol = _colsel(bcs, glane, nheads + h)
                licol = _colsel(gc, glane, h)
                mcol = _colsel(mcols, hlane, h)
                blcol = _colsel(blast, glane, nheads + h)
                rrow = brs[nheads + h:nheads + h + 1, :] - gt[h:h + 1, :]
                dlog = jnp.where(causal, bcol - rrow, -jnp.inf)
                ginter = bcol + mcol
                m_t = jnp.maximum(ginter, jnp.max(dlog, axis=-1, keepdims=True))
                dw = jnp.exp(dlog - m_t)
                winter = jnp.exp(ginter - m_t)
                qh = jnp.where(mh, q2, 0.0)
                km = jnp.where(mh, k2, 0.0)
                qhb = _b(qh)
                sc = _dot_nt(qhb, _b(k2)) * dw
                if sample:
                    inter = jnp.concatenate(
                        [_dot_nt(qhb[q * lsub:(q + 1) * lsub], _b(c_in[q])) for q in range(nsub)], axis=0)
                else:
                    inter = _dot_nt(qhb, _b(c_prev))
                num = winter * inter + _dot(_b(sc), _b(vh))
                qn = jnp.sum(qh * n_rows, axis=-1, keepdims=True)
                den = winter * qn + jnp.sum(sc, axis=-1, keepdims=True)
                hout = num / jnp.maximum(jnp.abs(den), jnp.exp(-m_t))
                hn = hout * lax.rsqrt(jnp.mean(hout * hout, axis=-1, keepdims=True) + NORM_EPS) * nw_ref[:, hs]
                ho_s[rows, hs] = _b(hn * og_s[rows, hs])
                mnew = _seq_last(m_t, lsub)
                ws = jnp.exp(blcol - bcol + licol - mnew)
                wst_cols.append(jnp.exp(blcol + mcol - mnew))
                ws_cols.append(ws)
                wv_parts.append(ws * vh)
                km_parts.append(km)
                mt_all = jnp.where(hlane == h, m_t, mt_all)
            wsk = jnp.where(mlo, ws_cols[0], ws_cols[1]) * k2
            wst = jnp.where(mlo, wst_cols[0], wst_cols[1])
            wvb = _b(jnp.concatenate(wv_parts, axis=0))
            kmb = _b(jnp.concatenate(km_parts, axis=0))
            if sample:
                for q in range(nsub):
                    last = (q + 1) * lsub - 1
                    sel = [slice(blk * L + q * lsub, blk * L + (q + 1) * lsub) for blk in range(2)]
                    wq = wst[last:last + 1, :]
                    upd = _dot_tn(jnp.concatenate([wvb[s_] for s_ in sel], axis=0),
                                  jnp.concatenate([kmb[s_] for s_ in sel], axis=0))
                    cout_ref[c * nsub + q, pp] = c_in[q] * wq + upd
                    nq = n_in[q] * wq + jnp.sum(wsk[q * lsub:(q + 1) * lsub], axis=0, keepdims=True)
                    nout_ref[c * nsub + q, :, ps] = nq
            else:
                wq = wst
                cp_s[pp] = c_prev * wq + _dot_tn(wvb, kmb)
                np_s[:, ps] = n_prev * wq + jnp.sum(wsk, axis=0, keepdims=True)
        if sample:
            mout_ref[rows, :] = mt_all
        else:
            m_s[...] = mt_all[L - 1:L, :]
        return carry

    lax.fori_loop(0, nchunk, chunk, 0)
    if not sample:
        cout_ref[0] = cp_s[...]
        nout_ref[0] = np_s[...]
        mout_ref[0] = m_s[...]
    o_ref[...] = x_ref[...] + _dot(ho_s[...], wout_ref[...])


def _mlstm_call(x, cvx, mrow, c0p, n0, wts, *, nseq, seq_rows, npad, sample, tm):
    n, d = x.shape
    nqk = wts[1].shape[1]
    nv = wts[2].shape[1]
    nheads = nqk // 2 // ML_DK
    npair = nheads // 2
    lsub = seq_rows if sample else CHUNK
    kern = functools.partial(_mlstm_kernel, tm=tm, seq_rows=seq_rows, npad=npad, sample=sample, lsub=lsub)
    if sample:
        grid = (n // tm,)
        tile = lambda i: (i, 0)
        spt = tm // seq_rows
        c_spec = pl.BlockSpec((spt, npair, LANES, LANES), lambda i: (i, 0, 0, 0))
        n_spec = pl.BlockSpec((spt, 1, nqk // 2), lambda i: (i, 0, 0))
        m_spec = pl.BlockSpec((tm, nheads), tile)
        m_shape = (n, nheads)
        cv_spec = pl.BlockSpec((tm, nqk), tile)
        cv_shape = (n, nqk)
        sem = ("arbitrary",)
    else:
        tps = seq_rows // tm
        grid = (nseq, tps)
        tile = lambda b, j: (b * tps + j, 0)
        c_spec = pl.BlockSpec((1, npair, LANES, LANES), lambda b, j: (b, 0, 0, 0))
        n_spec = pl.BlockSpec((1, 1, nqk // 2), lambda b, j: (b, 0, 0))
        m_spec = pl.BlockSpec((1, 1, nheads), lambda b, j: (b, 0, 0))
        m_shape = (nseq, 1, nheads)
        cv_spec = pl.BlockSpec((SUBLANES, nqk), lambda b, j: (b, 0))
        cv_shape = (nseq * SUBLANES, nqk)
        sem = ("arbitrary", "arbitrary")
    x_spec = pl.BlockSpec((tm, d), tile)
    in_specs = [x_spec]
    args = [x]
    if sample:
        in_specs += [pl.BlockSpec((tm, nqk), tile), pl.BlockSpec((tm, nheads), tile)]
        args += [cvx, mrow]
    for wt in wts:
        in_specs.append(_const_spec(wt.shape))
        args.append(wt)
    if sample:
        in_specs += [c_spec, n_spec]
        args += [c0p, n0]
    scratch = [pltpu.VMEM((tm, d), BF16), pltpu.VMEM((tm, nqk // 2), F32), pltpu.VMEM((tm, nqk // 2), F32),
               pltpu.VMEM((tm, nv), F32), pltpu.VMEM((tm, nv), F32), pltpu.VMEM((tm, 2 * nheads), F32),
               pltpu.VMEM((tm, nv), BF16)]
    if not sample:
        scratch += [pltpu.VMEM((SUBLANES, nqk), F32), pltpu.VMEM((npair, LANES, LANES), F32),
                    pltpu.VMEM((1, nqk // 2), F32), pltpu.VMEM((1, nheads), F32)]
    return pl.pallas_call(
        kern,
        grid=grid,
        in_specs=in_specs,
        out_specs=[x_spec, c_spec, n_spec, m_spec, cv_spec],
        out_shape=[jax.ShapeDtypeStruct((n, d), F32),
                   jax.ShapeDtypeStruct((nseq, npair, LANES, LANES), F32),
                   jax.ShapeDtypeStruct((nseq, 1, nqk // 2), F32),
                   jax.ShapeDtypeStruct(m_shape, F32),
                   jax.ShapeDtypeStruct(cv_shape, F32)],
        scratch_shapes=scratch,
        compiler_params=pltpu.CompilerParams(dimension_semantics=sem, vmem_limit_bytes=VMEM_LIMIT),
        name="mlstm_sample" if sample else "mlstm_prompt",
    )(*args)


def _rw_state_to_pairs(s):
    b, h, n, _ = s.shape
    r = s.reshape(b, h // 2, 2, n, n)
    z = jnp.zeros((b, h // 2, n, n), s.dtype)
    top = jnp.concatenate([r[:, :, 0], z], axis=-1)
    bot = jnp.concatenate([z, r[:, :, 1]], axis=-1)
    return jnp.concatenate([top, bot], axis=-2)


def _rw_pairs_to_state(sbd):
    b, hp, _, _ = sbd.shape
    n = RW_HEAD
    return jnp.stack([sbd[:, :, :n, :n], sbd[:, :, n:, n:]], axis=2).reshape(b, 2 * hp, n, n)


def _ml_c_to_pairs(c):
    b, h, dv, dk = c.shape
    return c.reshape(b, h // 2, 2, dv, dk).transpose(0, 1, 3, 2, 4).reshape(b, h // 2, dv, 2 * dk)


def _ml_pairs_to_c(cp):
    b, hp, dv, dk2 = cp.shape
    return cp.reshape(b, hp, dv, 2, dk2 // 2).transpose(0, 1, 3, 2, 4).reshape(b, 2 * hp, dv, dk2 // 2)


def kernel(x_prompt, x_sample, state_rwkv_S, state_rwkv_shift, state_mlstm_C, state_mlstm_n, state_mlstm_m,
           state_mlstm_conv, norm_ffa, ffa_wg, ffa_wu, ffa_wd, norm_mix, norm_ffb, ffb_wg, ffb_wu, ffb_wd,
           rw_mu, rw_wr, rw_wk, rw_wv, rw_wo, rw_w0, rw_w1, rw_w2, rw_a0, rw_a1, rw_a2, rw_g1, rw_g2,
           rw_k_k, rw_k_a, rw_r_k, rw_gn_w, rw_gn_b, ml_w_in, ml_b_if, ml_conv_w, ml_conv_b, ml_norm_w,
           ml_w_out, norm_final):
    bp, tp, d = x_prompt.shape
    bs, ts, _ = x_sample.shape
    depth = norm_ffa.shape[0]
    slot = SUBLANES
    npad = slot - ts
    assert 0 < ts <= slot and npad >= ML_CONV - 1
    tm_p = min(256, tp)
    assert tp % tm_p == 0 and tm_p % CHUNK == 0 and (bs * slot) % CHUNK == 0
    ml_heads = ml_b_if.shape[1] // 2
    nqk = 2 * ml_heads * ML_DK
    nv = ml_heads * ML_DV

    xp = x_prompt.reshape(bp * tp, d)
    xs = jnp.concatenate([jnp.zeros((bs, npad, d), F32), x_sample], axis=1).reshape(bs * slot, d)
    row2 = lambda a: a.reshape(1, -1)

    def ffn(x, norm, wg, wu, wd, final, tm):
        return _ffn_call(x, row2(norm), _b(wg), _b(wu), _b(wd), row2(norm_final), final_norm=final, tm=tm)

    tm_ffn_p = min(512, bp * tp)
    tm_ffn_s = min(256, bs * slot)
    new_p = {k_: [] for k_ in ("S", "shift", "C", "n", "m", "conv")}
    new_s = {k_: [] for k_ in ("S", "shift", "C", "n", "m", "conv")}
    for i in range(depth):
        xp = ffn(xp, norm_ffa[i], ffa_wg[i], ffa_wu[i], ffa_wd[i], False, tm_ffn_p)
        xs = ffn(xs, norm_ffa[i], ffa_wg[i], ffa_wu[i], ffa_wd[i], False, tm_ffn_s)
        j = i // 2
        if i % 2 == 0:
            wts = [row2(norm_mix[i]), rw_mu[j], _b(rw_wr[j]), _b(rw_wk[j]), _b(rw_wv[j]), _b(rw_wo[j]),
                   row2(rw_w0[j]), _b(rw_w1[j]), _b(rw_w2[j]), row2(rw_a0[j]), _b(rw_a1[j]), _b(rw_a2[j]),
                   _b(rw_g1[j]), _b(rw_g2[j]), row2(rw_k_k[j]), row2(rw_k_a[j]), row2(rw_r_k[j]),
                   row2(rw_gn_w[j]), row2(rw_gn_b[j])]
            xp, sbd, tail = _rwkv_call(xp, None, None, wts, nseq=bp, seq_rows=tp, npad=0, sample=False, tm=tm_p)
            new_p["S"].append(_rw_pairs_to_state(sbd))
            new_p["shift"].append(tail.reshape(bp, SUBLANES, d)[:, SUBLANES - 1])
            shx = jnp.repeat(state_rwkv_shift[j], slot, axis=0)
            xs, sbd, xn = _rwkv_call(xs, shx, _rw_state_to_pairs(state_rwkv_S[j]), wts, nseq=bs, seq_rows=slot,
                                     npad=npad, sample=True, tm=CHUNK)
            new_s["S"].append(_rw_pairs_to_state(sbd))
            new_s["shift"].append(xn.reshape(bs, slot, d)[:, slot - 1])
        else:
            w_in = ml_w_in[j]
            w_if = w_in[:, nqk + nv + d:]
            wts = [row2(norm_mix[i]), _b(w_in[:, :nqk]), _b(w_in[:, nqk:nqk + nv]), _b(w_in[:, nqk + nv:nqk + nv + d]),
                   _b(w_if), _b(w_if.T), row2(ml_b_if[j]), ml_b_if[j].reshape(-1, 1), ml_conv_w[j],
                   row2(ml_conv_b[j]), row2(ml_norm_w[j]), _b(ml_w_out[j])]
            xp, cp, n_, m_, tail = _mlstm_call(xp, None, None, None, None, wts, nseq=bp, seq_rows=tp, npad=0,
                                               sample=False, tm=tm_p)
            new_p["C"].append(_ml_pairs_to_c(cp))
            new_p["n"].append(n_.reshape(bp, ml_heads, ML_DK))
            new_p["m"].append(m_.reshape(bp, ml_heads))
            new_p["conv"].append(tail.reshape(bp, SUBLANES, nqk)[:, SUBLANES - (ML_CONV - 1):])
            conv0 = state_mlstm_conv[j]
            cvx = jnp.concatenate([jnp.zeros((bs, npad - (ML_CONV - 1), nqk), F32), conv0,
                                   jnp.zeros((bs, slot - npad, nqk), F32)], axis=1).reshape(bs * slot, nqk)
            mrow = jnp.repeat(state_mlstm_m[j], slot, axis=0)
            xs, cp, n_, mt, raw = _mlstm_call(xs, cvx, mrow, _ml_c_to_pairs(state_mlstm_C[j]),
                                              state_mlstm_n[j].reshape(bs, 1, ml_heads * ML_DK), wts, nseq=bs,
                                              seq_rows=slot, npad=npad, sample=True, tm=CHUNK)
            new_s["C"].append(_ml_pairs_to_c(cp))
            new_s["n"].append(n_.reshape(bs, ml_heads, ML_DK))
            new_s["m"].append(mt.reshape(bs, slot, ml_heads)[:, slot - 1])
            new_s["conv"].append(raw.reshape(bs, slot, nqk)[:, slot - (ML_CONV - 1):])
        last = i == depth - 1
        xp = ffn(xp, norm_ffb[i], ffb_wg[i], ffb_wu[i], ffb_wd[i], last, tm_ffn_p)
        xs = ffn(xs, norm_ffb[i], ffb_wg[i], ffb_wu[i], ffb_wd[i], last, tm_ffn_s)
    y_prompt = xp.reshape(bp, tp, d)
    y_sample = xs.reshape(bs, slot, d)[:, npad:]
    st = lambda lst: jnp.stack(lst)
    return (y_prompt, y_sample,
            st(new_p["S"]), st(new_p["shift"]), st(new_p["C"]), st(new_p["n"]), st(new_p["m"]), st(new_p["conv"]),
            st(new_s["S"]), st(new_s["shift"]), st(new_s["C"]), st(new_s["n"]), st(new_s["m"]), st(new_s["conv"]))
```

```python
import functools
import math

import jax
import jax.numpy as jnp
from jax import lax
from jax.experimental import pallas as pl
from jax.experimental.pallas import tpu as pltpu

F32 = jnp.float32
BF16 = jnp.bfloat16

NORM_EPS = 1e-6
RW_GN_EPS = 64e-5
RW_HEAD = 64
ML_DK = 64
ML_DV = 128
ML_CONV = 4

LANES = 128
SUBLANES = 8
CHUNK = 64
NEG_BIG = -1e30
VMEM_LIMIT = 56 * 1024 * 1024
HI = lax.Precision.HIGHEST


def _dot(a, b, precision=None):
    return jnp.dot(a, b, preferred_element_type=F32, precision=precision)


def _dot_nt(a, b, precision=None):
    return lax.dot_general(a, b, (((1,), (1,)), ((), ())), preferred_element_type=F32, precision=precision)


def _dot_tn(a, b, precision=None):
    return lax.dot_general(a, b, (((0,), (0,)), ((), ())), preferred_element_type=F32, precision=precision)


def _b(x):
    return x.astype(BF16)


def _rms(x, g):
    return x * lax.rsqrt(jnp.mean(x * x, axis=-1, keepdims=True) + NORM_EPS) * g


def _sigmoid(x):
    return 1.0 / (1.0 + jnp.exp(-x))


def _softplus(x):
    return jnp.maximum(x, 0.0) + jnp.log1p(jnp.exp(-jnp.abs(x)))


def _segsum(x, mlo):
    lo = jnp.sum(jnp.where(mlo, x, 0.0), axis=-1, keepdims=True)
    hi = jnp.sum(jnp.where(mlo, 0.0, x), axis=-1, keepdims=True)
    return jnp.where(mlo, lo, hi)


def _colsel(x, lane_idx, j):
    return jnp.sum(jnp.where(lane_idx == j, x, 0.0), axis=-1, keepdims=True)


def _seq_last(x, lsub):
    n = x.shape[0]
    if lsub == n:
        return x[n - 1:n]
    parts = [jnp.broadcast_to(x[q * lsub + lsub - 1:q * lsub + lsub], (lsub,) + x.shape[1:])
             for q in range(n // lsub)]
    return jnp.concatenate(parts, axis=0)


def _const_spec(shape, single_buffer=False):
    nd = len(shape)
    if single_buffer:
        return pl.BlockSpec(shape, lambda *_: (0,) * nd, pipeline_mode=pl.Buffered(1))
    return pl.BlockSpec(shape, lambda *_: (0,) * nd)


def _log2(n):
    k = int(math.log2(n))
    assert 1 << k == n, n
    return k


def _ffn_kernel(x_ref, g_ref, wg_ref, wu_ref, wd_ref, gf_ref, o_ref, *, fchunk, final_norm):
    x = x_ref[...]
    xb = _b(_rms(x, g_ref[...]))
    nf = wg_ref.shape[1]
    acc = jnp.zeros(x.shape, F32)
    for c in range(nf // fchunk):
        sl = slice(c * fchunk, (c + 1) * fchunk)
        gate = _dot(xb, wg_ref[:, sl])
        up = _dot(xb, wu_ref[:, sl])
        h = _b(gate * _sigmoid(gate) * up)
        acc = acc + _dot(h, wd_ref[sl, :])
    out = x + 0.5 * acc
    if final_norm:
        out = _rms(out, gf_ref[...])
    o_ref[...] = out


def _ffn_call(x, g, wg, wu, wd, gf, *, final_norm, tm):
    n, d = x.shape
    nf = wg.shape[1]
    fchunk = nf // 2 if (nf // 2) % LANES == 0 else nf
    kern = functools.partial(_ffn_kernel, fchunk=fchunk, final_norm=final_norm)
    return pl.pallas_call(
        kern,
        grid=(n // tm,),
        in_specs=[pl.BlockSpec((tm, d), lambda i: (i, 0)),
                  _const_spec((1, d)),
                  _const_spec((d, nf), True), _const_spec((d, nf), True), _const_spec((nf, d), True),
                  _const_spec((1, d))],
        out_specs=pl.BlockSpec((tm, d), lambda i: (i, 0)),
        out_shape=jax.ShapeDtypeStruct((n, d), F32),
        compiler_params=pltpu.CompilerParams(dimension_semantics=("arbitrary",), vmem_limit_bytes=VMEM_LIMIT),
        name="ffn",
    )(x, g, wg, wu, wd, gf)


def _rwkv_kernel(*refs, tm, seq_rows, npad, sample, lsub):
    L = CHUNK
    nsub = L // lsub
    nchunk = tm // L
    nsteps = _log2(lsub)
    it = iter(refs)
    x_ref = next(it)
    shx_ref = next(it) if sample else None
    (gn_ref, mu_ref, wr_ref, wk_ref, wv_ref, wo_ref, w0_ref, w1_ref, w2_ref, a0_ref, a1_ref, a2_ref,
     g1_ref, g2_ref, kk_ref, ka_ref, rk_ref, gw_ref, gb_ref) = [next(it) for _ in range(19)]
    s0_ref = next(it) if sample else None
    o_ref, sout_ref, xn_ref = next(it), next(it), next(it)
    r_s, ld_s, k_s, v_s, kn_s, b_s, g_s, bon_s, yg_s = [next(it) for _ in range(9)]
    if not sample:
        carry_s, sbd_s = next(it), next(it)
    d = x_ref.shape[1]
    npair = d // LANES

    lane = lax.broadcasted_iota(jnp.int32, (1, LANES), 1)
    mlo = lane < RW_HEAD

    x = x_ref[...]
    xn = _rms(x, gn_ref[...])
    rolled = pltpu.roll(xn, 1, 0)
    row = lax.broadcasted_iota(jnp.int32, (tm, 1), 0)
    if sample:
        xprev = jnp.where((row & (seq_rows - 1)) == npad, shx_ref[...], rolled)
        xn_ref[...] = xn
    else:
        j = pl.program_id(1)

        @pl.when(j == 0)
        def _():
            carry_s[...] = jnp.zeros(carry_s.shape, F32)
            sbd_s[...] = jnp.zeros(sbd_s.shape, F32)

        xprev = jnp.where(row == 0, carry_s[SUBLANES - 1:SUBLANES, :], rolled)
        carry_s[...] = xn[tm - SUBLANES:tm, :]
        xn_ref[...] = xn[tm - SUBLANES:tm, :]
    xx = xprev - xn

    def mix(i):
        return _b(xn + xx * mu_ref[i:i + 1, :])

    r = _dot(mix(0), wr_ref[...])
    wl = _b(jnp.tanh(_dot(mix(1), w1_ref[...])))
    w = -_softplus(-(w0_ref[...] + _dot(wl, w2_ref[...]))) - 0.5
    k = _dot(mix(2), wk_ref[...])
    v = _dot(mix(3), wv_ref[...])
    al = _b(_dot(mix(4), a1_ref[...]))
    a = _sigmoid(a0_ref[...] + _dot(al, a2_ref[...]))
    gl = _b(_sigmoid(_dot(mix(5), g1_ref[...])))
    g_s[...] = _dot(gl, g2_ref[...])
    k2 = k * (1.0 + (a - 1.0) * ka_ref[...])
    kk = k * kk_ref[...]
    rkk = r * k2 * rk_ref[...]
    ld = -jnp.exp(w)
    if npad:
        keep = (row & (seq_rows - 1)) >= npad
        ld = jnp.where(keep, ld, 0.0)
        k2 = jnp.where(keep, k2, 0.0)
        kk = jnp.where(keep, kk, 0.0)
        v = jnp.where(keep, v, 0.0)
    r_s[...] = r
    ld_s[...] = ld
    k_s[...] = k2
    v_s[...] = v
    for p in range(npair):
        cs = slice(p * LANES, (p + 1) * LANES)
        kkp = kk[:, cs]
        kn = kkp / jnp.maximum(jnp.sqrt(_segsum(kkp * kkp, mlo)), 1e-12)
        kn_s[:, cs] = kn
        b_s[:, cs] = kn * a[:, cs]
        bon_s[:, cs] = _segsum(rkk[:, cs], mlo) * v[:, cs]

    ti = lax.broadcasted_iota(jnp.int32, (L, L), 0)
    si = lax.broadcasted_iota(jnp.int32, (L, L), 1)
    sh = _log2(lsub)
    ti = lax.broadcasted_iota(jnp.int32, (L, 3 * L), 0)
    si = lax.broadcasted_iota(jnp.int32, (L, 3 * L), 1) & (L - 1)
    tril3 = jnp.where(((ti >> sh) == (si >> sh)) & (si <= ti), 1.0, 0.0).astype(BF16)
    gi =lax.broadcasted_iota(jnp.int32, (2 * L, 4 * L), 0)
    gj = lax.broadcasted_iota(jnp.int32, (2 * L, 4 * L), 1)
    gt = gi & (L - 1)
    gs = gj & (L - 1)
    gmask = ((gt >> sh) == (gs >> sh)) & ((gs < gt) | ((gi >= L) & (gs == gt)))
    pairs = range(npair)

    def stack2(z):
        return jnp.concatenate([jnp.where(mlo, z, 0), jnp.where(mlo, 0, z)], axis=0)

    def chunk(c, carry):
        r0 = pl.multiple_of(c * L, L)
        rows = pl.ds(r0, L)
        cs = [slice(p * LANES, (p + 1) * LANES) for p in pairs]
        cum = []
        for p in pairs:
            ldc = ld_s[rows, cs[p]]
            hi = _b(ldc)
            r1 = ldc - hi.astype(F32)
            mid = _b(r1)
            lo = _b(r1 - mid.astype(F32))
            cum.append(_dot(tril3, jnp.concatenate([hi, mid, lo], axis=0)))
        a2f, a2, q4, vv2f, vv2 = [], [], [], [], []
        for p in pairs:
            ep = jnp.exp(cum[p])
            em = jnp.exp(-cum[p])
            at = -(kn_s[rows, cs[p]] * jnp.exp(cum[p] - ld_s[rows, cs[p]]))
            a2f.append(jnp.concatenate([at, r_s[rows, cs[p]] * ep], axis=0))
            a2.append(_b(a2f[p]))
            q4.append(jnp.concatenate([stack2(b_s[rows, cs[p]] * em), stack2(k_s[rows, cs[p]] * em)], axis=0))
            vv2f.append(stack2(v_s[rows, cs[p]]))
            vv2.append(_b(vv2f[p]))
        g = [jnp.where(gmask, _dot_nt(a2[p], _b(q4[p])), 0.0) for p in pairs]
        if sample:
            s_in = [[s0_ref[c * nsub + q, p] for q in range(nsub)] for p in pairs]
            as_a, as_r = [], []
            for p in pairs:
                asa, asr = [], []
                for q in range(nsub):
                    a2q = _b(jnp.concatenate([a2f[p][q * lsub:(q + 1) * lsub],
                                              a2f[p][L + q * lsub:L + (q + 1) * lsub]], axis=0))
                    asq = _dot_nt(a2q, _b(s_in[p][q]))
                    asa.append(asq[:lsub])
                    asr.append(asq[lsub:])
                as_a.append(jnp.concatenate(asa, axis=0))
                as_r.append(jnp.concatenate(asr, axis=0))
        else:
            s_prev = [sbd_s[p] for p in pairs]
            as_ = [_dot_nt(a2[p], _b(s_prev[p])) for p in pairs]
            as_a = [z[:L] for z in as_]
            as_r = [z[L:] for z in as_]
        xp = [as_a[p] + _dot(_b(g[p][:L, LANES:]), vv2[p]) for p in pairs]
        ncat = [_b(g[p][:L, :LANES]) for p in pairs]
        for i in range(nsteps):
            xp = [xp[p] + _dot(ncat[p], stack2(_b(xp[p]))) for p in pairs]
            if i + 1 < nsteps:
                ncat = [_b(_dot(ncat[p], stack2(ncat[p]))) for p in pairs]
        uvf = [jnp.concatenate([stack2(xp[p]), vv2f[p]], axis=0) for p in pairs]
        uv = [_b(z) for z in uvf]
        y = [as_r[p] + _dot(_b(g[p][L:, :]), uv[p]) for p in pairs]
        for p in pairs:
            if sample:
                for q in range(nsub):
                    sel = [slice(blk * L + q * lsub, blk * L + (q + 1) * lsub) for blk in range(4)]
                    pl_q = jnp.exp(cum[p][(q + 1) * lsub - 1:(q + 1) * lsub, :])
                    uvq = _b(jnp.concatenate([uvf[p][s_] for s_ in sel], axis=0))
                    qq = _b(jnp.concatenate([q4[p][s_] for s_ in sel], axis=0) * pl_q)
                    sout_ref[c * nsub + q, p] = s_in[p][q] * pl_q + _dot_tn(uvq, qq)
            else:
                pl_ = jnp.exp(cum[p][L - 1:L, :])
                sbd_s[p] = s_prev[p] * pl_ + _dot_tn(uv[p], _b(q4[p] * pl_))
        for p in pairs:
            mean = _segsum(y[p], mlo) * (1.0 / RW_HEAD)
            yc = y[p] - mean
            var = _segsum(yc * yc, mlo) * (1.0 / RW_HEAD)
            yn = yc * lax.rsqrt(var + RW_GN_EPS) * gw_ref[:, cs[p]] + gb_ref[:, cs[p]] + bon_s[rows, cs[p]]
            yg_s[rows, cs[p]] = _b(yn * g_s[rows, cs[p]])
        return carry

    lax.fori_loop(0, nchunk, chunk, 0)
    if not sample:
        sout_ref[0] = sbd_s[...]
    o_ref[...] = x_ref[...] + _dot(yg_s[...], wo_ref[...])


def _rwkv_call(x, shx, s0bd, wts, *, nseq, seq_rows, npad, sample, tm):
    n, d = x.shape
    npair = d // LANES
    lsub = seq_rows if sample else CHUNK
    kern = functools.partial(_rwkv_kernel, tm=tm, seq_rows=seq_rows, npad=npad, sample=sample, lsub=lsub)
    if sample:
        grid = (n // tm,)
        tile = lambda i: (i, 0)
        spt = tm // seq_rows
        st_spec = pl.BlockSpec((spt, npair, LANES, LANES), lambda i: (i, 0, 0, 0))
        xn_spec = pl.BlockSpec((tm, d), tile)
        xn_shape = (n, d)
        sem = ("arbitrary",)
    else:
        tps = seq_rows // tm
        grid = (nseq, tps)
        tile = lambda b, j: (b * tps + j, 0)
        st_spec = pl.BlockSpec((1, npair, LANES, LANES), lambda b, j: (b, 0, 0, 0))
        xn_spec = pl.BlockSpec((SUBLANES, d), lambda b, j: (b, 0))
        xn_shape = (nseq * SUBLANES, d)
        sem = ("arbitrary", "arbitrary")
    x_spec = pl.BlockSpec((tm, d), tile)
    in_specs = [x_spec] + ([x_spec] if sample else [])
    args = [x] + ([shx] if sample else [])
    for wt in wts:
        in_specs.append(_const_spec(wt.shape))
        args.append(wt)
    if sample:
        in_specs.append(st_spec)
        args.append(s0bd)
    scratch = [pltpu.VMEM((tm, d), F32) for _ in range(8)] + [pltpu.VMEM((tm, d), BF16)]
    if not sample:
        scratch += [pltpu.VMEM((SUBLANES, d), F32), pltpu.VMEM((npair, LANES, LANES), F32)]
    return pl.pallas_call(
        kern,
        grid=grid,
        in_specs=in_specs,
        out_specs=[x_spec, st_spec, xn_spec],
        out_shape=[jax.ShapeDtypeStruct((n, d), F32),
                   jax.ShapeDtypeStruct((nseq, npair, LANES, LANES), F32),
                   jax.ShapeDtypeStruct(xn_shape, F32)],
        scratch_shapes=scratch,
        compiler_params=pltpu.CompilerParams(dimension_semantics=sem, vmem_limit_bytes=VMEM_LIMIT),
        name="rwkv_sample" if sample else "rwkv_prompt",
    )(*args)


def _mlstm_kernel(*refs, tm, seq_rows, npad, sample, lsub):
    L = CHUNK
    nsub = L // lsub
    nchunk = tm // L
    it = iter(refs)
    x_ref = next(it)
    if sample:
        cvx_ref, mrow_ref = next(it), next(it)
    (gn_ref, wqk_ref, wv_ref, wo_ref, wif_ref, wift_ref, bif_ref, bift_ref, cw_ref, cb_ref, nw_ref,
     wout_ref) = [next(it) for _ in range(12)]
    if sample:
        c0_ref, n0_ref = next(it), next(it)
    o_ref, cout_ref, nout_ref, mout_ref, cvout_ref = [next(it) for _ in range(5)]
    xb_s, q_s, k_s, v_s, og_s, gc_s, ho_s = [next(it) for _ in range(7)]
    if not sample:
        carry_s, cp_s, np_s, m_s = [next(it) for _ in range(4)]
    nqk = wqk_ref.shape[1]
    half = nqk // 2
    nheads = half // ML_DK
    ngate = 2 * nheads
    npair = nheads // 2

    lane = lax.broadcasted_iota(jnp.int32, (1, LANES), 1)
    mlo = lane < ML_DK
    glane = lax.broadcasted_iota(jnp.int32, (1, ngate), 1)
    isf_c = glane >= nheads
    grow = lax.broadcasted_iota(jnp.int32, (ngate, 1), 0)
    isf_r = grow >= nheads
    hlane = lax.broadcasted_iota(jnp.int32, (1, nheads), 1)

    x = x_ref[...]
    xb = _b(_rms(x, gn_ref[...]))
    xb_s[...] = xb
    raw = _dot(xb, wqk_ref[...])
    row = lax.broadcasted_iota(jnp.int32, (tm, 1), 0)
    if sample:
        srow = row & (seq_rows - 1)
        raw = jnp.where((srow >= npad - (ML_CONV - 1)) & (srow < npad), cvx_ref[...], raw)
        cvout_ref[...] = raw
        ext = jnp.concatenate([jnp.zeros((SUBLANES, nqk), F32), raw], axis=0)
    else:
        j = pl.program_id(1)

        @pl.when(j == 0)
        def _():
            carry_s[...] = jnp.zeros(carry_s.shape, F32)
            cp_s[...] = jnp.zeros(cp_s.shape, F32)
            np_s[...] = jnp.zeros(np_s.shape, F32)
            m_s[...] = jnp.zeros(m_s.shape, F32)

        ext = jnp.concatenate([carry_s[...], raw], axis=0)
        carry_s[...] = raw[tm - SUBLANES:tm, :]
        cvout_ref[...] = raw[tm - SUBLANES:tm, :]
    qk = cb_ref[...] + cw_ref[ML_CONV - 1:ML_CONV, :] * raw
    for s in range(1, ML_CONV):
        qk = qk + cw_ref[ML_CONV - 1 - s:ML_CONV - s, :] * pltpu.roll(ext, s, 0)[SUBLANES:, :]
    qk = qk * _sigmoid(qk)
    q_s[...] = qk[:, :half] * (ML_DK ** -0.5)
    k_s[...] = qk[:, half:]
    v_s[...] = _dot(xb, wv_ref[...])
    og_s[...] = _sigmoid(_dot(xb, wo_ref[...]))
    ifp = _dot(xb, wif_ref[...]) + bif_ref[...]
    gcol = jnp.where(isf_c, -_softplus(-ifp), ifp)
    if npad:
        keep = (row & (seq_rows - 1)) >= npad
        gcol = jnp.where(keep, gcol, jnp.where(isf_c, 0.0, NEG_BIG))
    gc_s[...] = gcol

    ti = lax.broadcasted_iota(jnp.int32, (L, L), 0)
    si = lax.broadcasted_iota(jnp.int32, (L, L), 1)
    sh = _log2(lsub)
    same = (ti >> sh) == (si >> sh)
    causal = same & (si <= ti)
    tril = jnp.where(causal, 1.0, 0.0).astype(F32)
    triu = jnp.where(same & (ti <= si), 1.0, 0.0).astype(F32)
    lrow = lax.broadcasted_iota(jnp.int32, (1, L), 1)

    def chunk(c, carry):
        r0 = pl.multiple_of(c * L, L)
        rows = pl.ds(r0, L)
        gc = gc_s[rows, :]
        gt = _dot_nt(wift_ref[...], xb_s[rows, :]) + bift_ref[...]
        gt = jnp.where(isf_r, -_softplus(-gt), gt)
        if npad:
            gt = jnp.where((lrow & (seq_rows - 1)) >= npad, gt, jnp.where(isf_r, 0.0, NEG_BIG))
        bcs = _dot(tril, jnp.where(isf_c, gc, 0.0), HI)
        brs = _dot(jnp.where(isf_r, gt, 0.0), triu, HI)
        blast = _seq_last(bcs, lsub)
        if sample:
            mcols = mrow_ref[rows, :]
        else:
            mcols = m_s[...]
        mt_all = jnp.zeros((L, nheads), F32)
        for pp in range(npair):
            ps = slice(pp * LANES, (pp + 1) * LANES)
            q2 = q_s[rows, ps]
            k2 = k_s[rows, ps]
            if sample:
                c_in = [c0_ref[c * nsub + q, pp] for q in range(nsub)]
                n_in = [n0_ref[c * nsub + q][:, ps] for q in range(nsub)]
                n_rows = jnp.concatenate([jnp.broadcast_to(n_in[q], (lsub, LANES)) for q in range(nsub)], axis=0)
            else:
                c_prev = cp_s[pp]
                n_prev = np_s[:, ps]
                n_rows = n_prev
            wv_parts, km_parts, ws_cols, wst_cols = [], [], [], []
            for hh in range(2):
                h = 2 * pp + hh
                hs = slice(h * ML_DV, (h + 1) * ML_DV)
                mh = mlo if hh == 0 else jnp.logical_not(mlo)
                vh = v_s[rows, hs]
                bcol = _colsel(bcs, glane, nheads + h)
                licol = _colsel(gc, glane, h)
                mcol = _colsel(mcols, hlane, h)
                blcol = _colsel(blast, glane, nheads + h)
                rrow = brs[nheads + h:nheads + h + 1, :] - gt[h:h + 1, :]
                dlog = jnp.where(causal, bcol - rrow, -jnp.inf)
                ginter = bcol + mcol
                m_t = jnp.maximum(ginter, jnp.max(dlog, axis=-1, keepdims=True))
                dw = jnp.exp(dlog - m_t)
                winter = jnp.exp(ginter - m_t)
                qh = jnp.where(mh, q2, 0.0)
                km = jnp.where(mh, k2, 0.0)
                qhb = _b(qh)
                sc = _dot_nt(qhb, _b(k2)) * dw
                if sample:
                    inter = jnp.concatenate(
                        [_dot_nt(qhb[q * lsub:(q + 1) * lsub], _b(c_in[q])) for q in range(nsub)], axis=0)
                else:
                    inter = _dot_nt(qhb, _b(c_prev))
                num = winter * inter + _dot(_b(sc), _b(vh))
                qn = jnp.sum(qh * n_rows, axis=-1, keepdims=True)
                den = winter * qn + jnp.sum(sc, axis=-1, keepdims=True)
                hout = num / jnp.maximum(jnp.abs(den), jnp.exp(-m_t))
                hn = hout * lax.rsqrt(jnp.mean(hout * hout, axis=-1, keepdims=True) + NORM_EPS) * nw_ref[:, hs]
                ho_s[rows, hs] = _b(hn * og_s[rows, hs])
                mnew = _seq_last(m_t, lsub)
                ws = jnp.exp(blcol - bcol + licol - mnew)
                wst_cols.append(jnp.exp(blcol + mcol - mnew))
                ws_cols.append(ws)
                wv_parts.append(ws * vh)
                km_parts.append(km)
                mt_all = jnp.where(hlane == h, m_t, mt_all)
            wsk = jnp.where(mlo, ws_cols[0], ws_cols[1]) * k2
            wst = jnp.where(mlo, wst_cols[0], wst_cols[1])
            wvb = _b(jnp.concatenate(wv_parts, axis=0))
            kmb = _b(jnp.concatenate(km_parts, axis=0))
            if sample:
                for q in range(nsub):
                    last = (q + 1) * lsub - 1
                    sel = [slice(blk * L + q * lsub, blk * L + (q + 1) * lsub) for blk in range(2)]
                    wq = wst[last:last + 1, :]
                    upd = _dot_tn(jnp.concatenate([wvb[s_] for s_ in sel], axis=0),
                                  jnp.concatenate([kmb[s_] for s_ in sel], axis=0))
                    cout_ref[c * nsub + q, pp] = c_in[q] * wq + upd
                    nq = n_in[q] * wq + jnp.sum(wsk[q * lsub:(q + 1) * lsub], axis=0, keepdims=True)
                    nout_ref[c * nsub + q, :, ps] = nq
            else:
                wq = wst
                cp_s[pp] = c_prev * wq + _dot_tn(wvb, kmb)
                np_s[:, ps] = n_prev * wq + jnp.sum(wsk, axis=0, keepdims=True)
        if sample:
            mout_ref[rows, :] = mt_all
        else:
            m_s[...] = mt_all[L - 1:L, :]
        return carry

    lax.fori_loop(0, nchunk, chunk, 0)
    if not sample:
        cout_ref[0] = cp_s[...]
        nout_ref[0] = np_s[...]
        mout_ref[0] = m_s[...]
    o_ref[...] = x_ref[...] + _dot(ho_s[...], wout_ref[...])


def _mlstm_call(x, cvx, mrow, c0p, n0, wts, *, nseq, seq_rows, npad, sample, tm):
    n, d = x.shape
    nqk = wts[1].shape[1]
    nv = wts[2].shape[1]
    nheads = nqk // 2 // ML_DK
    npair = nheads // 2
    lsub = seq_rows if sample else CHUNK
    kern = functools.partial(_mlstm_kernel, tm=tm, seq_rows=seq_rows, npad=npad, sample=sample, lsub=lsub)
    if sample:
        grid = (n // tm,)
        tile = lambda i: (i, 0)
        spt = tm // seq_rows
        c_spec = pl.BlockSpec((spt, npair, LANES, LANES), lambda i: (i, 0, 0, 0))
        n_spec = pl.BlockSpec((spt, 1, nqk // 2), lambda i: (i, 0, 0))
        m_spec = pl.BlockSpec((tm, nheads), tile)
        m_shape = (n, nheads)
        cv_spec = pl.BlockSpec((tm, nqk), tile)
        cv_shape = (n, nqk)
        sem = ("arbitrary",)
    else:
        tps = seq_rows // tm
        grid = (nseq, tps)
        tile = lambda b, j: (b * tps + j, 0)
        c_spec = pl.BlockSpec((1, npair, LANES, LANES), lambda b, j: (b, 0, 0, 0))
        n_spec = pl.BlockSpec((1, 1, nqk // 2), lambda b, j: (b, 0, 0))
        m_spec = pl.BlockSpec((1, 1, nheads), lambda b, j: (b, 0, 0))
        m_shape = (nseq, 1, nheads)
        cv_spec = pl.BlockSpec((SUBLANES, nqk), lambda b, j: (b, 0))
        cv_shape = (nseq * SUBLANES, nqk)
        sem = ("arbitrary", "arbitrary")
    x_spec = pl.BlockSpec((tm, d), tile)
    in_specs = [x_spec]
    args = [x]
    if sample:
        in_specs += [pl.BlockSpec((tm, nqk), tile), pl.BlockSpec((tm, nheads), tile)]
        args += [cvx, mrow]
    for wt in wts:
        in_specs.append(_const_spec(wt.shape))
        args.append(wt)
    if sample:
        in_specs += [c_spec, n_spec]
        args += [c0p, n0]
    scratch = [pltpu.VMEM((tm, d), BF16), pltpu.VMEM((tm, nqk // 2), F32), pltpu.VMEM((tm, nqk // 2), F32),
               pltpu.VMEM((tm, nv), F32), pltpu.VMEM((tm, nv), F32), pltpu.VMEM((tm, 2 * nheads), F32),
               pltpu.VMEM((tm, nv), BF16)]
    if not sample:
        scratch += [pltpu.VMEM((SUBLANES, nqk), F32), pltpu.VMEM((npair, LANES, LANES), F32),
                    pltpu.VMEM((1, nqk // 2), F32), pltpu.VMEM((1, nheads), F32)]
    return pl.pallas_call(
        kern,
        grid=grid,
        in_specs=in_specs,
        out_specs=[x_spec, c_spec, n_spec, m_spec, cv_spec],
        out_shape=[jax.ShapeDtypeStruct((n, d), F32),
                   jax.ShapeDtypeStruct((nseq, npair, LANES, LANES), F32),
                   jax.ShapeDtypeStruct((nseq, 1, nqk // 2), F32),
                   jax.ShapeDtypeStruct(m_shape, F32),
                   jax.ShapeDtypeStruct(cv_shape, F32)],
        scratch_shapes=scratch,
        compiler_params=pltpu.CompilerParams(dimension_semantics=sem, vmem_limit_bytes=VMEM_LIMIT),
        name="mlstm_sample" if sample else "mlstm_prompt",
    )(*args)


def _rw_state_to_pairs(s):
    b, h, n, _ = s.shape
    r = s.reshape(b, h // 2, 2, n, n)
    z = jnp.zeros((b, h // 2, n, n), s.dtype)
    top = jnp.concatenate([r[:, :, 0], z], axis=-1)
    bot = jnp.concatenate([z, r[:, :, 1]], axis=-1)
    return jnp.concatenate([top, bot], axis=-2)


def _rw_pairs_to_state(sbd):
    b, hp, _, _ = sbd.shape
    n = RW_HEAD
    return jnp.stack([sbd[:, :, :n, :n], sbd[:, :, n:, n:]], axis=2).reshape(b, 2 * hp, n, n)


def _ml_c_to_pairs(c):
    b, h, dv, dk = c.shape
    return c.reshape(b, h // 2, 2, dv, dk).transpose(0, 1, 3, 2, 4).reshape(b, h // 2, dv, 2 * dk)


def _ml_pairs_to_c(cp):
    b, hp, dv, dk2 = cp.shape
    return cp.reshape(b, hp, dv, 2, dk2 // 2).transpose(0, 1, 3, 2, 4).reshape(b, 2 * hp, dv, dk2 // 2)


def kernel(x_prompt, x_sample, state_rwkv_S, state_rwkv_shift, state_mlstm_C, state_mlstm_n, state_mlstm_m,
           state_mlstm_conv, norm_ffa, ffa_wg, ffa_wu, ffa_wd, norm_mix, norm_ffb, ffb_wg, ffb_wu, ffb_wd,
           rw_mu, rw_wr, rw_wk, rw_wv, rw_wo, rw_w0, rw_w1, rw_w2, rw_a0, rw_a1, rw_a2, rw_g1, rw_g2,
           rw_k_k, rw_k_a, rw_r_k, rw_gn_w, rw_gn_b, ml_w_in, ml_b_if, ml_conv_w, ml_conv_b, ml_norm_w,
           ml_w_out, norm_final):
    bp, tp, d = x_prompt.shape
    bs, ts, _ = x_sample.shape
    depth = norm_ffa.shape[0]
    slot = SUBLANES
    npad = slot - ts
    assert 0 < ts <= slot and npad >= ML_CONV - 1
    tm_p = min(256, tp)
    assert tp % tm_p == 0 and tm_p % CHUNK == 0 and (bs * slot) % CHUNK == 0
    ml_heads = ml_b_if.shape[1] // 2
    nqk = 2 * ml_heads * ML_DK
    nv = ml_heads * ML_DV

    xp = x_prompt.reshape(bp * tp, d)
    xs = jnp.concatenate([jnp.zeros((bs, npad, d), F32), x_sample], axis=1).reshape(bs * slot, d)
    row2 = lambda a: a.reshape(1, -1)

    def ffn(x, norm, wg, wu, wd, final, tm):
        return _ffn_call(x, row2(norm), _b(wg), _b(wu), _b(wd), row2(norm_final), final_norm=final, tm=tm)

    tm_ffn_p = min(512, bp * tp)
    tm_ffn_s = min(256, bs * slot)
    new_p = {k_: [] for k_ in ("S", "shift", "C", "n", "m", "conv")}
    new_s = {k_: [] for k_ in ("S", "shift", "C", "n", "m", "conv")}
    for i in range(depth):
        xp = ffn(xp, norm_ffa[i], ffa_wg[i], ffa_wu[i], ffa_wd[i], False, tm_ffn_p)
        xs = ffn(xs, norm_ffa[i], ffa_wg[i], ffa_wu[i], ffa_wd[i], False, tm_ffn_s)
        j = i // 2
        if i % 2 == 0:
            wts = [row2(norm_mix[i]), rw_mu[j], _b(rw_wr[j]), _b(rw_wk[j]), _b(rw_wv[j]), _b(rw_wo[j]),
                   row2(rw_w0[j]), _b(rw_w1[j]), _b(rw_w2[j]), row2(rw_a0[j]), _b(rw_a1[j]), _b(rw_a2[j]),
                   _b(rw_g1[j]), _b(rw_g2[j]), row2(rw_k_k[j]), row2(rw_k_a[j]), row2(rw_r_k[j]),
                   row2(rw_gn_w[j]), row2(rw_gn_b[j])]
            xp, sbd, tail = _rwkv_call(xp, None, None, wts, nseq=bp, seq_rows=tp, npad=0, sample=False, tm=tm_p)
            new_p["S"].append(_rw_pairs_to_state(sbd))
            new_p["shift"].append(tail.reshape(bp, SUBLANES, d)[:, SUBLANES - 1])
            shx = jnp.repeat(state_rwkv_shift[j], slot, axis=0)
            xs, sbd, xn = _rwkv_call(xs, shx, _rw_state_to_pairs(state_rwkv_S[j]), wts, nseq=bs, seq_rows=slot,
                                     npad=npad, sample=True, tm=CHUNK)
            new_s["S"].append(_rw_pairs_to_state(sbd))
            new_s["shift"].append(xn.reshape(bs, slot, d)[:, slot - 1])
        else:
            w_in = ml_w_in[j]
            w_if = w_in[:, nqk + nv + d:]
            wts = [row2(norm_mix[i]), _b(w_in[:, :nqk]), _b(w_in[:, nqk:nqk + nv]), _b(w_in[:, nqk + nv:nqk + nv + d]),
                   _b(w_if), _b(w_if.T), row2(ml_b_if[j]), ml_b_if[j].reshape(-1, 1), ml_conv_w[j],
                   row2(ml_conv_b[j]), row2(ml_norm_w[j]), _b(ml_w_out[j])]
            xp, cp, n_, m_, tail = _mlstm_call(xp, None, None, None, None, wts, nseq=bp, seq_rows=tp, npad=0,
                                               sample=False, tm=tm_p)
            new_p["C"].append(_ml_pairs_to_c(cp))
            new_p["n"].append(n_.reshape(bp, ml_heads, ML_DK))
            new_p["m"].append(m_.reshape(bp, ml_heads))
            new_p["conv"].append(tail.reshape(bp, SUBLANES, nqk)[:, SUBLANES - (ML_CONV - 1):])
            conv0 = state_mlstm_conv[j]
            cvx = jnp.concatenate([jnp.zeros((bs, npad - (ML_CONV - 1), nqk), F32), conv0,
                                   jnp.zeros((bs, slot - npad, nqk), F32)], axis=1).reshape(bs * slot, nqk)
            mrow = jnp.repeat(state_mlstm_m[j], slot, axis=0)
            xs, cp, n_, mt, raw = _mlstm_call(xs, cvx, mrow, _ml_c_to_pairs(state_mlstm_C[j]),
                                              state_mlstm_n[j].reshape(bs, 1, ml_heads * ML_DK), wts, nseq=bs,
                                              seq_rows=slot, npad=npad, sample=True, tm=CHUNK)
            new_s["C"].append(_ml_pairs_to_c(cp))
            new_s["n"].append(n_.reshape(bs, ml_heads, ML_DK))
            new_s["m"].append(mt.reshape(bs, slot, ml_heads)[:, slot - 1])
            new_s["conv"].append(raw.reshape(bs, slot, nqk)[:, slot - (ML_CONV - 1):])
        last = i == depth - 1
        xp = ffn(xp, norm_ffb[i], ffb_wg[i], ffb_wu[i], ffb_wd[i], last, tm_ffn_p)
        xs = ffn(xs, norm_ffb[i], ffb_wg[i], ffb_wu[i], ffb_wd[i], last, tm_ffn_s)
    y_prompt = xp.reshape(bp, tp, d)
    y_sample = xs.reshape(bs, slot, d)[:, npad:]
    st = lambda lst: jnp.stack(lst)
    return (y_prompt, y_sample,
            st(new_p["S"]), st(new_p["shift"]), st(new_p["C"]), st(new_p["n"]), st(new_p["m"]), st(new_p["conv"]),
            st(new_s["S"]), st(new_s["shift"]), st(new_s["C"]), st(new_s["n"]), st(new_s["m"]), st(new_s["conv"]))
```

```python
import functools
import math

import jax
import jax.numpy as jnp
from jax import lax
from jax.experimental import pallas as pl
from jax.experimental.pallas import tpu as pltpu

F32 = jnp.float32
BF16 = jnp.bfloat16

NORM_EPS = 1e-6
RW_GN_EPS = 64e-5
RW_HEAD = 64
ML_DK = 64
ML_DV = 128
ML_CONV = 4

LANES = 128
SUBLANES = 8
CHUNK = 64
NEG_BIG = -1e30
VMEM_LIMIT = 56 * 1024 * 1024
HI = lax.Precision.HIGHEST


def _dot(a, b, precision=None):
    return jnp.dot(a, b, preferred_element_type=F32, precision=precision)


def _dot_nt(a, b, precision=None):
    return lax.dot_general(a, b, (((1,), (1,)), ((), ())), preferred_element_type=F32, precision=precision)


def _dot_tn(a, b, precision=None):
    return lax.dot_general(a, b, (((0,), (0,)), ((), ())), preferred_element_type=F32, precision=precision)


def _b(x):
    return x.astype(BF16)


def _rms(x, g):
    return x * lax.rsqrt(jnp.mean(x * x, axis=-1, keepdims=True) + NORM_EPS) * g


def _sigmoid(x):
    return 1.0 / (1.0 + jnp.exp(-x))


def _softplus(x):
    return jnp.maximum(x, 0.0) + jnp.log1p(jnp.exp(-jnp.abs(x)))


def _segsum(x, mlo):
    lo = jnp.sum(jnp.where(mlo, x, 0.0), axis=-1, keepdims=True)
    hi = jnp.sum(jnp.where(mlo, 0.0, x), axis=-1, keepdims=True)
    return jnp.where(mlo, lo, hi)


def _colsel(x, lane_idx, j):
    return jnp.sum(jnp.where(lane_idx == j, x, 0.0), axis=-1, keepdims=True)


def _seq_last(x, lsub):
    n = x.shape[0]
    if lsub == n:
        return x[n - 1:n]
    parts = [jnp.broadcast_to(x[q * lsub + lsub - 1:q * lsub + lsub], (lsub,) + x.shape[1:])
             for q in range(n // lsub)]
    return jnp.concatenate(parts, axis=0)


def _const_spec(shape, single_buffer=False):
    nd = len(shape)
    if single_buffer:
        return pl.BlockSpec(shape, lambda *_: (0,) * nd, pipeline_mode=pl.Buffered(1))
    return pl.BlockSpec(shape, lambda *_: (0,) * nd)


def _log2(n):
    k = int(math.log2(n))
    assert 1 << k == n, n
    return k


def _ffn_kernel(x_ref, g_ref, wg_ref, wu_ref, wd_ref, gf_ref, o_ref, *, fchunk, final_norm):
    x = x_ref[...]
    xb = _b(_rms(x, g_ref[...]))
    nf = wg_ref.shape[1]
    acc = jnp.zeros(x.shape, F32)
    for c in range(nf // fchunk):
        sl = slice(c * fchunk, (c + 1) * fchunk)
        gate = _dot(xb, wg_ref[:, sl])
        up = _dot(xb, wu_ref[:, sl])
        h = _b(gate * _sigmoid(gate) * up)
        acc = acc + _dot(h, wd_ref[sl, :])
    out = x + 0.5 * acc
    if final_norm:
        out = _rms(out, gf_ref[...])
    o_ref[...] = out


def _ffn_call(x, g, wg, wu, wd, gf, *, final_norm, tm):
    n, d = x.shape
    nf = wg.shape[1]
    fchunk = nf // 2 if (nf // 2) % LANES == 0 else nf
    kern = functools.partial(_ffn_kernel, fchunk=fchunk, final_norm=final_norm)
    return pl.pallas_call(
        kern,
        grid=(n // tm,),
        in_specs=[pl.BlockSpec((tm, d), lambda i: (i, 0)),
                  _const_spec((1, d)),
                  _const_spec((d, nf), True), _const_spec((d, nf), True), _const_spec((nf, d), True),
                  _const_spec((1, d))],
        out_specs=pl.BlockSpec((tm, d), lambda i: (i, 0)),
        out_shape=jax.ShapeDtypeStruct((n, d), F32),
        compiler_params=pltpu.CompilerParams(dimension_semantics=("arbitrary",), vmem_limit_bytes=VMEM_LIMIT),
        name="ffn",
    )(x, g, wg, wu, wd, gf)


def _rwkv_kernel(*refs, tm, seq_rows, npad, sample, lsub):
    L = CHUNK
    nsub = L // lsub
    nchunk = tm // L
    nsteps = _log2(lsub)
    it = iter(refs)
    x_ref = next(it)
    shx_ref = next(it) if sample else None
    (gn_ref, mu_ref, wr_ref, wk_ref, wv_ref, wo_ref, w0_ref, w1_ref, w2_ref, a0_ref, a1_ref, a2_ref,
     g1_ref, g2_ref, kk_ref, ka_ref, rk_ref, gw_ref, gb_ref) = [next(it) for _ in range(19)]
    s0_ref = next(it) if sample else None
    o_ref, sout_ref, xn_ref = next(it), next(it), next(it)
    r_s, ld_s, k_s, v_s, kn_s, b_s, g_s, bon_s, yg_s = [next(it) for _ in range(9)]
    if not sample:
        carry_s, sbd_s = next(it), next(it)
    d = x_ref.shape[1]
    npair = d // LANES

    lane = lax.broadcasted_iota(jnp.int32, (1, LANES), 1)
    mlo = lane < RW_HEAD

    x = x_ref[...]
    xn = _rms(x, gn_ref[...])
    rolled = pltpu.roll(xn, 1, 0)
    row = lax.broadcasted_iota(jnp.int32, (tm, 1), 0)
    if sample:
        xprev = jnp.where((row & (seq_rows - 1)) == npad, shx_ref[...], rolled)
        xn_ref[...] = xn
    else:
        j = pl.program_id(1)

        @pl.when(j == 0)
        def _():
            carry_s[...] = jnp.zeros(carry_s.shape, F32)
            sbd_s[...] = jnp.zeros(sbd_s.shape, F32)

        xprev = jnp.where(row == 0, carry_s[SUBLANES - 1:SUBLANES, :], rolled)
        carry_s[...] = xn[tm - SUBLANES:tm, :]
        xn_ref[...] = xn[tm - SUBLANES:tm, :]
    xx = xprev - xn

    def mix(i):
        return _b(xn + xx * mu_ref[i:i + 1, :])

    r = _dot(mix(0), wr_ref[...])
    wl = _b(jnp.tanh(_dot(mix(1), w1_ref[...])))
    w = -_softplus(-(w0_ref[...] + _dot(wl, w2_ref[...]))) - 0.5
    k = _dot(mix(2), wk_ref[...])
    v = _dot(mix(3), wv_ref[...])
    al = _b(_dot(mix(4), a1_ref[...]))
    a = _sigmoid(a0_ref[...] + _dot(al, a2_ref[...]))
    gl = _b(_sigmoid(_dot(mix(5), g1_ref[...])))
    g_s[...] = _dot(gl, g2_ref[...])
    k2 = k * (1.0 + (a - 1.0) * ka_ref[...])
    kk = k * kk_ref[...]
    rkk = r * k2 * rk_ref[...]
    ld = -jnp.exp(w)
    if npad:
        keep = (row & (seq_rows - 1)) >= npad
        ld = jnp.where(keep, ld, 0.0)
        k2 = jnp.where(keep, k2, 0.0)
        kk = jnp.where(keep, kk, 0.0)
        v = jnp.where(keep, v, 0.0)
    r_s[...] = r
    ld_s[...] = ld
    k_s[...] = k2
    v_s[...] = v
    for p in range(npair):
        cs = slice(p * LANES, (p + 1) * LANES)
        kkp = kk[:, cs]
        kn = kkp / jnp.maximum(jnp.sqrt(_segsum(kkp * kkp, mlo)), 1e-12)
        kn_s[:, cs] = kn
        b_s[:, cs] = kn * a[:, cs]
        bon_s[:, cs] = _segsum(rkk[:, cs], mlo) * v[:, cs]

    sh = _log2(lsub)
    ti = lax.broadcasted_iota(jnp.int32, (L, 3 * L), 0)
    si = lax.broadcasted_iota(jnp.int32, (L, 3 * L), 1) & (L - 1)
    tril3 = jnp.where(((ti >> sh) == (si >> sh)) & (si <= ti), 1.0, 0.0).astype(BF16)
    gi = lax.broadcasted_iota(jnp.int32, (2 * L, 4 * L), 0)
    gj = lax.broadcasted_iota(jnp.int32, (2 * L, 4 * L), 1)
    gt = gi & (L - 1)
    gs = gj & (L - 1)
    gmask = ((gt >> sh) == (gs >> sh)) & ((gs < gt) | ((gi >= L) & (gs == gt)))
    pairs = range(npair)

    def stack2(z):
        return jnp.concatenate([jnp.where(mlo, z, 0), jnp.where(mlo, 0, z)], axis=0)

    cs = [slice(p * LANES, (p + 1) * LANES) for p in pairs]
    chains = [(c, p) for c in range(nchunk) for p in pairs]

    rw = {key: slice(key[0] * L, (key[0] + 1) * L) for key in chains}
    cum = {}
    for key in chains:
        ldc = ld_s[rw[key], cs[key[1]]]
        hi = _b(ldc)
        r1 = ldc - hi.astype(F32)
        mid = _b(r1)
        lo = _b(r1 - mid.astype(F32))
        cum[key] = _dot(tril3, jnp.concatenate([hi, mid, lo], axis=0))
    a2f, a2, q4, vv2f, vv2 = {}, {}, {}, {}, {}
    for key in chains:
        rows, c_ = rw[key], cs[key[1]]
        ep = jnp.exp(cum[key])
        em = jnp.exp(-cum[key])
        at = -(kn_s[rows, c_] * jnp.exp(cum[key] - ld_s[rows, c_]))
        a2f[key] = jnp.concatenate([at, r_s[rows, c_] * ep], axis=0)
        a2[key] = _b(a2f[key])
        q4[key] = jnp.concatenate([stack2(b_s[rows, c_] * em), stack2(k_s[rows, c_] * em)], axis=0)
        vv2f[key] = stack2(v_s[rows, c_])
        vv2[key] = _b(vv2f[key])
    g = {key: jnp.where(gmask, _dot_nt(a2[key], _b(q4[key])), 0.0) for key in chains}
    makv = {key: _dot(_b(g[key][:L, LANES:]), vv2[key]) for key in chains}
    mr = {key: _b(g[key][L:, :]) for key in chains}
    w = {key: g[key][:L, :LANES] for key in chains}
    pc = {key: _b(w[key]) for key in chains}
    for _ in range(1, nsteps):
        pf = {key: _dot(pc[key], stack2(pc[key])) for key in chains}
        pc = {key: _b(pf[key]) for key in chains}
        w = {key: w[key] + pf[key] + _dot(pc[key], stack2(_b(w[key]))) for key in chains}
    wcat = {key: _b(w[key]) for key in chains}
    if not sample:
        pl_ = {key: jnp.exp(cum[key][L - 1:L, :]) for key in chains}
        qpl = {key: _b(q4[key] * pl_[key]) for key in chains}

    if not sample:
        state = [sbd_s[p] for p in pairs]
    for c in range(nchunk):
        keys = [(c, p) for p in pairs]
        if sample:
            s_in = [[s0_ref[c * nsub + q, p] for q in range(nsub)] for p in pairs]
            as_a, as_r = [], []
            for p in pairs:
                asa, asr = [], []
                for q in range(nsub):
                    a2q = _b(jnp.concatenate([a2f[c, p][q * lsub:(q + 1) * lsub],
                                              a2f[c, p][L + q * lsub:L + (q + 1) * lsub]], axis=0))
                    asq = _dot_nt(a2q, _b(s_in[p][q]))
                    asa.append(asq[:lsub])
                    asr.append(asq[lsub:])
                as_a.append(jnp.concatenate(asa, axis=0))
                as_r.append(jnp.concatenate(asr, axis=0))
        else:
            as_ = [_dot_nt(a2[key], _b(state[key[1]])) for key in keys]
            as_a = [z[:L] for z in as_]
            as_r = [z[L:] for z in as_]
        rhs = [as_a[p] + makv[c, p] for p in pairs]
        u = [rhs[p] + _dot(wcat[c, p], stack2(_b(rhs[p]))) for p in pairs]
        if sample:
            uvf = [jnp.concatenate([stack2(u[p]), vv2f[c, p]], axis=0) for p in pairs]
            uv = [_b(z) for z in uvf]
        else:
            uv = [jnp.concatenate([stack2(_b(u[p])), vv2[c, p]], axis=0) for p in pairs]
        y = [as_r[p] + _dot(mr[c, p], uv[p]) for p in pairs]
        for p in pairs:
            if sample:
                for q in range(nsub):
                    sel = [slice(blk * L + q * lsub, blk * L + (q + 1) * lsub) for blk in range(4)]
                    pl_q = jnp.exp(cum[c, p][(q + 1) * lsub - 1:(q + 1) * lsub, :])
                    uvq = _b(jnp.concatenate([uvf[p][s_] for s_ in sel], axis=0))
                    qq = _b(jnp.concatenate([q4[c, p][s_] for s_ in sel], axis=0) * pl_q)
                    sout_ref[c * nsub + q, p] = s_in[p][q] * pl_q + _dot_tn(uvq, qq)
            else:
                state[p] = state[p] * pl_[c, p] + _dot_tn(uv[p], qpl[c, p])
        rows = rw[c, 0]
        for p in pairs:
            mean = _segsum(y[p], mlo) * (1.0 / RW_HEAD)
            yc = y[p] - mean
            var = _segsum(yc * yc, mlo) * (1.0 / RW_HEAD)
            yn = yc * lax.rsqrt(var + RW_GN_EPS) * gw_ref[:, cs[p]] + gb_ref[:, cs[p]] + bon_s[rows, cs[p]]
            yg_s[rows, cs[p]] = _b(yn * g_s[rows, cs[p]])
    if not sample:
        for p in pairs:
            sbd_s[p] = state[p]
        sout_ref[0] = sbd_s[...]
    o_ref[...] = x_ref[...] + _dot(yg_s[...], wo_ref[...])


def _rwkv_call(x, shx, s0bd, wts, *, nseq, seq_rows, npad, sample, tm):
    n, d = x.shape
    npair = d // LANES
    lsub = seq_rows if sample else CHUNK
    kern = functools.partial(_rwkv_kernel, tm=tm, seq_rows=seq_rows, npad=npad, sample=sample, lsub=lsub)
    if sample:
        grid = (n // tm,)
        tile = lambda i: (i, 0)
        spt = tm // seq_rows
        st_spec = pl.BlockSpec((spt, npair, LANES, LANES), lambda i: (i, 0, 0, 0))
        xn_spec = pl.BlockSpec((tm, d), tile)
        xn_shape = (n, d)
        sem = ("arbitrary",)
    else:
        tps = seq_rows // tm
        grid = (nseq, tps)
        tile = lambda b, j: (b * tps + j, 0)
        st_spec = pl.BlockSpec((1, npair, LANES, LANES), lambda b, j: (b, 0, 0, 0))
        xn_spec = pl.BlockSpec((SUBLANES, d), lambda b, j: (b, 0))
        xn_shape = (nseq * SUBLANES, d)
        sem = ("arbitrary", "arbitrary")
    x_spec = pl.BlockSpec((tm, d), tile)
    in_specs = [x_spec] + ([x_spec] if sample else [])
    args = [x] + ([shx] if sample else [])
    for wt in wts:
        in_specs.append(_const_spec(wt.shape))
        args.append(wt)
    if sample:
        in_specs.append(st_spec)
        args.append(s0bd)
    scratch = [pltpu.VMEM((tm, d), F32) for _ in range(8)] + [pltpu.VMEM((tm, d), BF16)]
    if not sample:
        scratch += [pltpu.VMEM((SUBLANES, d), F32), pltpu.VMEM((npair, LANES, LANES), F32)]
    return pl.pallas_call(
        kern,
        grid=grid,
        in_specs=in_specs,
        out_specs=[x_spec, st_spec, xn_spec],
        out_shape=[jax.ShapeDtypeStruct((n, d), F32),
                   jax.ShapeDtypeStruct((nseq, npair, LANES, LANES), F32),
                   jax.ShapeDtypeStruct(xn_shape, F32)],
        scratch_shapes=scratch,
        compiler_params=pltpu.CompilerParams(dimension_semantics=sem, vmem_limit_bytes=VMEM_LIMIT),
        name="rwkv_sample" if sample else "rwkv_prompt",
    )(*args)


def _mlstm_kernel(*refs, tm, seq_rows, npad, sample, lsub):
    L = CHUNK
    nsub = L // lsub
    nchunk = tm // L
    it = iter(refs)
    x_ref = next(it)
    if sample:
        cvx_ref, mrow_ref = next(it), next(it)
    (gn_ref, wqk_ref, wv_ref, wo_ref, wif_ref, wift_ref, bif_ref, bift_ref, cw_ref, cb_ref, nw_ref,
     wout_ref) = [next(it) for _ in range(12)]
    if sample:
        c0_ref, n0_ref = next(it), next(it)
    o_ref, cout_ref, nout_ref, mout_ref, cvout_ref = [next(it) for _ in range(5)]
    xb_s, q_s, k_s, v_s, og_s, gc_s, ho_s = [next(it) for _ in range(7)]
    if not sample:
        carry_s, cp_s, np_s, m_s = [next(it) for _ in range(4)]
    nqk = wqk_ref.shape[1]
    half = nqk // 2
    nheads = half // ML_DK
    ngate = 2 * nheads
    npair = nheads // 2

    lane = lax.broadcasted_iota(jnp.int32, (1, LANES), 1)
    mlo = lane < ML_DK
    glane = lax.broadcasted_iota(jnp.int32, (1, ngate), 1)
    isf_c = glane >= nheads
    grow = lax.broadcasted_iota(jnp.int32, (ngate, 1), 0)
    isf_r = grow >= nheads
    hlane = lax.broadcasted_iota(jnp.int32, (1, nheads), 1)

    x = x_ref[...]
    xb = _b(_rms(x, gn_ref[...]))
    xb_s[...] = xb
    raw = _dot(xb, wqk_ref[...])
    row = lax.broadcasted_iota(jnp.int32, (tm, 1), 0)
    if sample:
        srow = row & (seq_rows - 1)
        raw = jnp.where((srow >= npad - (ML_CONV - 1)) & (srow < npad), cvx_ref[...], raw)
        cvout_ref[...] = raw
        ext = jnp.concatenate([jnp.zeros((SUBLANES, nqk), F32), raw], axis=0)
    else:
        j = pl.program_id(1)

        @pl.when(j == 0)
        def _():
            carry_s[...] = jnp.zeros(carry_s.shape, F32)
            cp_s[...] = jnp.zeros(cp_s.shape, F32)
            np_s[...] = jnp.zeros(np_s.shape, F32)
            m_s[...] = jnp.zeros(m_s.shape, F32)

        ext = jnp.concatenate([carry_s[...], raw], axis=0)
        carry_s[...] = raw[tm - SUBLANES:tm, :]
        cvout_ref[...] = raw[tm - SUBLANES:tm, :]
    qk = cb_ref[...] + cw_ref[ML_CONV - 1:ML_CONV, :] * raw
    for s in range(1, ML_CONV):
        qk = qk + cw_ref[ML_CONV - 1 - s:ML_CONV - s, :] * pltpu.roll(ext, s, 0)[SUBLANES:, :]
    qk = qk * _sigmoid(qk)
    q_s[...] = qk[:, :half] * (ML_DK ** -0.5)
    k_s[...] = qk[:, half:]
    v_s[...] = _dot(xb, wv_ref[...])
    og_s[...] = _sigmoid(_dot(xb, wo_ref[...]))
    ifp = _dot(xb, wif_ref[...]) + bif_ref[...]
    gcol = jnp.where(isf_c, -_softplus(-ifp), ifp)
    if npad:
        keep = (row & (seq_rows - 1)) >= npad
        gcol = jnp.where(keep, gcol, jnp.where(isf_c, 0.0, NEG_BIG))
    gc_s[...] = gcol

    ti = lax.broadcasted_iota(jnp.int32, (L, L), 0)
    si = lax.broadcasted_iota(jnp.int32, (L, L), 1)
    sh = _log2(lsub)
    same = (ti >> sh) == (si >> sh)
    causal = same & (si <= ti)
    tril = jnp.where(causal, 1.0, 0.0).astype(F32)
    triu = jnp.where(same & (ti <= si), 1.0, 0.0).astype(F32)
    lrow = lax.broadcasted_iota(jnp.int32, (1, L), 1)

    def chunk(c, carry):
        r0 = pl.multiple_of(c * L, L)
        rows = pl.ds(r0, L)
        gc = gc_s[rows, :]
        gt = _dot_nt(wift_ref[...], xb_s[rows, :]) + bift_ref[...]
        gt = jnp.where(isf_r, -_softplus(-gt), gt)
        if npad:
            gt = jnp.where((lrow & (seq_rows - 1)) >= npad, gt, jnp.where(isf_r, 0.0, NEG_BIG))
        bcs = _dot(tril, jnp.where(isf_c, gc, 0.0), HI)
        brs = _dot(jnp.where(isf_r, gt, 0.0), triu, HI)
        blast = _seq_last(bcs, lsub)
        if sample:
            mcols = mrow_ref[rows, :]
        else:
            mcols = m_s[...]
        heads = range(nheads)
        prs = range(npair)
        ps = [slice(pp * LANES, (pp + 1) * LANES) for pp in prs]
        hs = [slice(h * ML_DV, (h + 1) * ML_DV) for h in heads]
        q2 = [q_s[rows, ps[pp]] for pp in prs]
        k2 = [k_s[rows, ps[pp]] for pp in prs]
        k2b = [_b(z) for z in k2]
        if sample:
            c_in = [[c0_ref[c * nsub + q, pp] for q in range(nsub)] for pp in prs]
            n_in = [[n0_ref[c * nsub + q][:, ps[pp]] for q in range(nsub)] for pp in prs]
            n_rows = [jnp.concatenate([jnp.broadcast_to(n_in[pp][q], (lsub, LANES)) for q in range(nsub)], axis=0)
                      for pp in prs]
        else:
            c_prev = [cp_s[pp] for pp in prs]
            n_prev = [np_s[:, ps[pp]] for pp in prs]
            n_rows = n_prev
        bcol = [_colsel(bcs, glane, nheads + h) for h in heads]
        licol = [_colsel(gc, glane, h) for h in heads]
        mcol = [_colsel(mcols, hlane, h) for h in heads]
        blcol = [_colsel(blast, glane, nheads + h) for h in heads]
        dlog = [jnp.where(causal, bcol[h] - (brs[nheads + h:nheads + h + 1, :] - gt[h:h + 1, :]), -jnp.inf)
                for h in heads]
        ginter = [bcol[h] + mcol[h] for h in heads]
        m_t = [jnp.maximum(ginter[h], jnp.max(dlog[h], axis=-1, keepdims=True)) for h in heads]
        dw = [jnp.exp(dlog[h] - m_t[h]) for h in heads]
        winter = [jnp.exp(ginter[h] - m_t[h]) for h in heads]
        qh = [jnp.where(mlo if h % 2 == 0 else jnp.logical_not(mlo), q2[h // 2], 0.0) for h in heads]
        qhb = [_b(z) for z in qh]
        sc = [_dot_nt(qhb[h], k2b[h // 2]) * dw[h] for h in heads]
        if sample:
            inter = [jnp.concatenate([_dot_nt(qhb[h][q * lsub:(q + 1) * lsub], _b(c_in[h // 2][q]))
                                      for q in range(nsub)], axis=0) for h in heads]
        else:
            cb = [_b(z) for z in c_prev]
            inter = [_dot_nt(qhb[h], cb[h // 2]) for h in heads]
        vh = [v_s[rows, hs[h]] for h in heads]
        num = [winter[h] * inter[h] + _dot(_b(sc[h]), _b(vh[h])) for h in heads]
        den = [winter[h] * jnp.sum(qh[h] * n_rows[h // 2], axis=-1, keepdims=True)
               + jnp.sum(sc[h], axis=-1, keepdims=True) for h in heads]
        for h in heads:
            hout = num[h] / jnp.maximum(jnp.abs(den[h]), jnp.exp(-m_t[h]))
            hn = hout * lax.rsqrt(jnp.mean(hout * hout, axis=-1, keepdims=True) + NORM_EPS) * nw_ref[:, hs[h]]
            ho_s[rows, hs[h]] = _b(hn * og_s[rows, hs[h]])
        mnew = [_seq_last(m_t[h], lsub) for h in heads]
        ws = [jnp.exp(blcol[h] - bcol[h] + licol[h] - mnew[h]) for h in heads]
        wstc = [jnp.exp(blcol[h] + mcol[h] - mnew[h]) for h in heads]
        mt_all = jnp.zeros((L, nheads), F32)
        for h in heads:
            mt_all = jnp.where(hlane == h, m_t[h], mt_all)
        for pp in prs:
            lo, hi = 2 * pp, 2 * pp + 1
            wsk = jnp.where(mlo, ws[lo], ws[hi]) * k2[pp]
            wst = jnp.where(mlo, wstc[lo], wstc[hi])
            wvb = _b(jnp.concatenate([ws[lo] * vh[lo], ws[hi] * vh[hi]], axis=0))
            kmb = jnp.concatenate([jnp.where(mlo, k2b[pp], 0), jnp.where(mlo, 0, k2b[pp])], axis=0)
            if sample:
                for q in range(nsub):
                    last = (q + 1) * lsub - 1
                    sel = [slice(blk * L + q * lsub, blk * L + (q + 1) * lsub) for blk in range(2)]
                    wq = wst[last:last + 1, :]
                    upd = _dot_tn(jnp.concatenate([wvb[s_] for s_ in sel], axis=0),
                                  jnp.concatenate([kmb[s_] for s_ in sel], axis=0))
                    cout_ref[c * nsub + q, pp] = c_in[pp][q] * wq + upd
                    nq = n_in[pp][q] * wq + jnp.sum(wsk[q * lsub:(q + 1) * lsub], axis=0, keepdims=True)
                    nout_ref[c * nsub + q, :, ps[pp]] = nq
            else:
                cp_s[pp] = c_prev[pp] * wst + _dot_tn(wvb, kmb)
                np_s[:, ps[pp]] = n_prev[pp] * wst + jnp.sum(wsk, axis=0, keepdims=True)
        if sample:
            mout_ref[rows, :] = mt_all
        else:
            m_s[...] = mt_all[L - 1:L, :]
        return carry

    lax.fori_loop(0, nchunk, chunk, 0)
    if not sample:
        cout_ref[0] = cp_s[...]
        nout_ref[0] = np_s[...]
        mout_ref[0] = m_s[...]
    o_ref[...] = x_ref[...] + _dot(ho_s[...], wout_ref[...])


def _mlstm_call(x, cvx, mrow, c0p, n0, wts, *, nseq, seq_rows, npad, sample, tm):
    n, d = x.shape
    nqk = wts[1].shape[1]
    nv = wts[2].shape[1]
    nheads = nqk // 2 // ML_DK
    npair = nheads // 2
    lsub = seq_rows if sample else CHUNK
    kern = functools.partial(_mlstm_kernel, tm=tm, seq_rows=seq_rows, npad=npad, sample=sample, lsub=lsub)
    if sample:
        grid = (n // tm,)
        tile = lambda i: (i, 0)
        spt = tm // seq_rows
        c_spec = pl.BlockSpec((spt, npair, LANES, LANES), lambda i: (i, 0, 0, 0))
        n_spec = pl.BlockSpec((spt, 1, nqk // 2), lambda i: (i, 0, 0))
        m_spec = pl.BlockSpec((tm, nheads), tile)
        m_shape = (n, nheads)
        cv_spec = pl.BlockSpec((tm, nqk), tile)
        cv_shape = (n, nqk)
        sem = ("arbitrary",)
    else:
        tps = seq_rows // tm
        grid = (nseq, tps)
        tile = lambda b, j: (b * tps + j, 0)
        c_spec = pl.BlockSpec((1, npair, LANES, LANES), lambda b, j: (b, 0, 0, 0))
        n_spec = pl.BlockSpec((1, 1, nqk // 2), lambda b, j: (b, 0, 0))
        m_spec = pl.BlockSpec((1, 1, nheads), lambda b, j: (b, 0, 0))
        m_shape = (nseq, 1, nheads)
        cv_spec = pl.BlockSpec((SUBLANES, nqk), lambda b, j: (b, 0))
        cv_shape = (nseq * SUBLANES, nqk)
        sem = ("arbitrary", "arbitrary")
    x_spec = pl.BlockSpec((tm, d), tile)
    in_specs = [x_spec]
    args = [x]
    if sample:
        in_specs += [pl.BlockSpec((tm, nqk), tile), pl.BlockSpec((tm, nheads), tile)]
        args += [cvx, mrow]
    for wt in wts:
        in_specs.append(_const_spec(wt.shape))
        args.append(wt)
    if sample:
        in_specs += [c_spec, n_spec]
        args += [c0p, n0]
    scratch = [pltpu.VMEM((tm, d), BF16), pltpu.VMEM((tm, nqk // 2), F32), pltpu.VMEM((tm, nqk // 2), F32),
               pltpu.VMEM((tm, nv), F32), pltpu.VMEM((tm, nv), F32), pltpu.VMEM((tm, 2 * nheads), F32),
               pltpu.VMEM((tm, nv), BF16)]
    if not sample:
        scratch += [pltpu.VMEM((SUBLANES, nqk), F32), pltpu.VMEM((npair, LANES, LANES), F32),
                    pltpu.VMEM((1, nqk // 2), F32), pltpu.VMEM((1, nheads), F32)]
    return pl.pallas_call(
        kern,
        grid=grid,
        in_specs=in_specs,
        out_specs=[x_spec, c_spec, n_spec, m_spec, cv_spec],
        out_shape=[jax.ShapeDtypeStruct((n, d), F32),
                   jax.ShapeDtypeStruct((nseq, npair, LANES, LANES), F32),
                   jax.ShapeDtypeStruct((nseq, 1, nqk // 2), F32),
                   jax.ShapeDtypeStruct(m_shape, F32),
                   jax.ShapeDtypeStruct(cv_shape, F32)],
        scratch_shapes=scratch,
        compiler_params=pltpu.CompilerParams(dimension_semantics=sem, vmem_limit_bytes=VMEM_LIMIT),
        name="mlstm_sample" if sample else "mlstm_prompt",
    )(*args)


def _rw_state_to_pairs(s):
    b, h, n, _ = s.shape
    r = s.reshape(b, h // 2, 2, n, n)
    z = jnp.zeros((b, h // 2, n, n), s.dtype)
    top = jnp.concatenate([r[:, :, 0], z], axis=-1)
    bot = jnp.concatenate([z, r[:, :, 1]], axis=-1)
    return jnp.concatenate([top, bot], axis=-2)


def _rw_pairs_to_state(sbd):
    b, hp, _, _ = sbd.shape
    n = RW_HEAD
    return jnp.stack([sbd[:, :, :n, :n], sbd[:, :, n:, n:]], axis=2).reshape(b, 2 * hp, n, n)


def _ml_c_to_pairs(c):
    b, h, dv, dk = c.shape
    return c.reshape(b, h // 2, 2, dv, dk).transpose(0, 1, 3, 2, 4).reshape(b, h // 2, dv, 2 * dk)


def _ml_pairs_to_c(cp):
    b, hp, dv, dk2 = cp.shape
    return cp.reshape(b, hp, dv, 2, dk2 // 2).transpose(0, 1, 3, 2, 4).reshape(b, 2 * hp, dv, dk2 // 2)


def kernel(x_prompt, x_sample, state_rwkv_S, state_rwkv_shift, state_mlstm_C, state_mlstm_n, state_mlstm_m,
           state_mlstm_conv, norm_ffa, ffa_wg, ffa_wu, ffa_wd, norm_mix, norm_ffb, ffb_wg, ffb_wu, ffb_wd,
           rw_mu, rw_wr, rw_wk, rw_wv, rw_wo, rw_w0, rw_w1, rw_w2, rw_a0, rw_a1, rw_a2, rw_g1, rw_g2,
           rw_k_k, rw_k_a, rw_r_k, rw_gn_w, rw_gn_b, ml_w_in, ml_b_if, ml_conv_w, ml_conv_b, ml_norm_w,
           ml_w_out, norm_final):
    bp, tp, d = x_prompt.shape
    bs, ts, _ = x_sample.shape
    depth = norm_ffa.shape[0]
    slot = SUBLANES
    npad = slot - ts
    assert 0 < ts <= slot and npad >= ML_CONV - 1
    tm_p = min(256, tp)
    assert tp % tm_p == 0 and tm_p % CHUNK == 0 and (bs * slot) % CHUNK == 0
    ml_heads = ml_b_if.shape[1] // 2
    nqk = 2 * ml_heads * ML_DK
    nv = ml_heads * ML_DV

    xp = x_prompt.reshape(bp * tp, d)
    xs = jnp.concatenate([jnp.zeros((bs, npad, d), F32), x_sample], axis=1).reshape(bs * slot, d)
    row2 = lambda a: a.reshape(1, -1)

    def ffn(x, norm, wg, wu, wd, final, tm):
        return _ffn_call(x, row2(norm), _b(wg), _b(wu), _b(wd), row2(norm_final), final_norm=final, tm=tm)

    tm_ffn_p = min(512, bp * tp)
    tm_ffn_s = min(256, bs * slot)
    new_p = {k_: [] for k_ in ("S", "shift", "C", "n", "m", "conv")}
    new_s = {k_: [] for k_ in ("S", "shift", "C", "n", "m", "conv")}
    for i in range(depth):
        xp = ffn(xp, norm_ffa[i], ffa_wg[i], ffa_wu[i], ffa_wd[i], False, tm_ffn_p)
        xs = ffn(xs, norm_ffa[i], ffa_wg[i], ffa_wu[i], ffa_wd[i], False, tm_ffn_s)
        j = i // 2
        if i % 2 == 0:
            wts = [row2(norm_mix[i]), rw_mu[j], _b(rw_wr[j]), _b(rw_wk[j]), _b(rw_wv[j]), _b(rw_wo[j]),
                   row2(rw_w0[j]), _b(rw_w1[j]), _b(rw_w2[j]), row2(rw_a0[j]), _b(rw_a1[j]), _b(rw_a2[j]),
                   _b(rw_g1[j]), _b(rw_g2[j]), row2(rw_k_k[j]), row2(rw_k_a[j]), row2(rw_r_k[j]),
                   row2(rw_gn_w[j]), row2(rw_gn_b[j])]
            xp, sbd, tail = _rwkv_call(xp, None, None, wts, nseq=bp, seq_rows=tp, npad=0, sample=False, tm=tm_p)
            new_p["S"].append(_rw_pairs_to_state(sbd))
            new_p["shift"].append(tail.reshape(bp, SUBLANES, d)[:, SUBLANES - 1])
            shx = jnp.repeat(state_rwkv_shift[j], slot, axis=0)
            xs, sbd, xn = _rwkv_call(xs, shx, _rw_state_to_pairs(state_rwkv_S[j]), wts, nseq=bs, seq_rows=slot,
                                     npad=npad, sample=True, tm=CHUNK)
            new_s["S"].append(_rw_pairs_to_state(sbd))
            new_s["shift"].append(xn.reshape(bs, slot, d)[:, slot - 1])
        else:
            w_in = ml_w_in[j]
            w_if = w_in[:, nqk + nv + d:]
            wts = [row2(norm_mix[i]), _b(w_in[:, :nqk]), _b(w_in[:, nqk:nqk + nv]), _b(w_in[:, nqk + nv:nqk + nv + d]),
                   _b(w_if), _b(w_if.T), row2(ml_b_if[j]), ml_b_if[j].reshape(-1, 1), ml_conv_w[j],
                   row2(ml_conv_b[j]), row2(ml_norm_w[j]), _b(ml_w_out[j])]
            xp, cp, n_, m_, tail = _mlstm_call(xp, None, None, None, None, wts, nseq=bp, seq_rows=tp, npad=0,
                                               sample=False, tm=tm_p)
            new_p["C"].append(_ml_pairs_to_c(cp))
            new_p["n"].append(n_.reshape(bp, ml_heads, ML_DK))
            new_p["m"].append(m_.reshape(bp, ml_heads))
            new_p["conv"].append(tail.reshape(bp, SUBLANES, nqk)[:, SUBLANES - (ML_CONV - 1):])
            conv0 = state_mlstm_conv[j]
            cvx = jnp.concatenate([jnp.zeros((bs, npad - (ML_CONV - 1), nqk), F32), conv0,
                                   jnp.zeros((bs, slot - npad, nqk), F32)], axis=1).reshape(bs * slot, nqk)
            mrow = jnp.repeat(state_mlstm_m[j], slot, axis=0)
            xs, cp, n_, mt, raw = _mlstm_call(xs, cvx, mrow, _ml_c_to_pairs(state_mlstm_C[j]),
                                              state_mlstm_n[j].reshape(bs, 1, ml_heads * ML_DK), wts, nseq=bs,
                                              seq_rows=slot, npad=npad, sample=True, tm=CHUNK)
            new_s["C"].append(_ml_pairs_to_c(cp))
            new_s["n"].append(n_.reshape(bs, ml_heads, ML_DK))
            new_s["m"].append(mt.reshape(bs, slot, ml_heads)[:, slot - 1])
            new_s["conv"].append(raw.reshape(bs, slot, nqk)[:, slot - (ML_CONV - 1):])
        last = i == depth - 1
        xp = ffn(xp, norm_ffb[i], ffb_wg[i], ffb_wu[i], ffb_wd[i], last, tm_ffn_p)
        xs = ffn(xs, norm_ffb[i], ffb_wg[i], ffb_wu[i], ffb_wd[i], last, tm_ffn_s)
    y_prompt = xp.reshape(bp, tp, d)
    y_sample = xs.reshape(bs, slot, d)[:, npad:]
    st = lambda lst: jnp.stack(lst)
    return (y_prompt, y_sample,
            st(new_p["S"]), st(new_p["shift"]), st(new_p["C"]), st(new_p["n"]), st(new_p["m"]), st(new_p["conv"]),
            st(new_s["S"]), st(new_s["shift"]), st(new_s["C"]), st(new_s["n"]), st(new_s["m"]), st(new_s["conv"]))
```

```python
import functools
import math

import jax
import jax.numpy as jnp
from jax import lax
from jax.experimental import pallas as pl
from jax.experimental.pallas import tpu as pltpu

F32 = jnp.float32
BF16 = jnp.bfloat16

NORM_EPS = 1e-6
RW_GN_EPS = 64e-5
RW_HEAD = 64
ML_DK = 64
ML_DV = 128
ML_CONV = 4

LANES = 128
SUBLANES = 8
CHUNK = 64
NEG_BIG = -1e30
VMEM_LIMIT = 56 * 1024 * 1024
HI = lax.Precision.HIGHEST


def _dot(a, b, precision=None):
    return jnp.dot(a, b, preferred_element_type=F32, precision=precision)


def _dot_nt(a, b, precision=None):
    return lax.dot_general(a, b, (((1,), (1,)), ((), ())), preferred_element_type=F32, precision=precision)


def _dot_tn(a, b, precision=None):
    return lax.dot_general(a, b, (((0,), (0,)), ((), ())), preferred_element_type=F32, precision=precision)


def _b(x):
    return x.astype(BF16)


def _rms(x, g):
    return x * lax.rsqrt(jnp.mean(x * x, axis=-1, keepdims=True) + NORM_EPS) * g


def _sigmoid(x):
    return 1.0 / (1.0 + jnp.exp(-x))


def _softplus(x):
    return jnp.maximum(x, 0.0) + jnp.log1p(jnp.exp(-jnp.abs(x)))


def _segsum(x, mlo):
    lo = jnp.sum(jnp.where(mlo, x, 0.0), axis=-1, keepdims=True)
    hi = jnp.sum(jnp.where(mlo, 0.0, x), axis=-1, keepdims=True)
    return jnp.where(mlo, lo, hi)


def _colsel(x, lane_idx, j):
    return jnp.sum(jnp.where(lane_idx == j, x, 0.0), axis=-1, keepdims=True)


def _seq_last(x, lsub):
    n = x.shape[0]
    if lsub == n:
        return x[n - 1:n]
    parts = [jnp.broadcast_to(x[q * lsub + lsub - 1:q * lsub + lsub], (lsub,) + x.shape[1:])
             for q in range(n // lsub)]
    return jnp.concatenate(parts, axis=0)


def _blockdiag(a, b):
    z = jnp.zeros(a.shape, a.dtype)
    return jnp.concatenate([jnp.concatenate([a, z], axis=1), jnp.concatenate([z, b], axis=1)], axis=0)


def _const_spec(shape, single_buffer=False):
    nd = len(shape)
    if single_buffer:
        return pl.BlockSpec(shape, lambda *_: (0,) * nd, pipeline_mode=pl.Buffered(1))
    return pl.BlockSpec(shape, lambda *_: (0,) * nd)


def _log2(n):
    k = int(math.log2(n))
    assert 1 << k == n, n
    return k


def _ffn_kernel(x_ref, g_ref, wg_ref, wu_ref, wd_ref, gf_ref, o_ref, *, fchunk, final_norm):
    x = x_ref[...]
    xb = _b(_rms(x, g_ref[...]))
    nf = wg_ref.shape[1]
    acc = jnp.zeros(x.shape, F32)
    for c in range(nf // fchunk):
        sl = slice(c * fchunk, (c + 1) * fchunk)
        gate = _dot(xb, wg_ref[:, sl])
        up = _dot(xb, wu_ref[:, sl])
        h = _b(gate * _sigmoid(gate) * up)
        acc = acc + _dot(h, wd_ref[sl, :])
    out = x + 0.5 * acc
    if final_norm:
        out = _rms(out, gf_ref[...])
    o_ref[...] = out


def _ffn_call(x, g, wg, wu, wd, gf, *, final_norm, tm):
    n, d = x.shape
    nf = wg.shape[1]
    fchunk = nf // 2 if (nf // 2) % LANES == 0 else nf
    kern = functools.partial(_ffn_kernel, fchunk=fchunk, final_norm=final_norm)
    return pl.pallas_call(
        kern,
        grid=(n // tm,),
        in_specs=[pl.BlockSpec((tm, d), lambda i: (i, 0)),
                  _const_spec((1, d)),
                  _const_spec((d, nf), True), _const_spec((d, nf), True), _const_spec((nf, d), True),
                  _const_spec((1, d))],
        out_specs=pl.BlockSpec((tm, d), lambda i: (i, 0)),
        out_shape=jax.ShapeDtypeStruct((n, d), F32),
        compiler_params=pltpu.CompilerParams(dimension_semantics=("arbitrary",), vmem_limit_bytes=VMEM_LIMIT),
        name="ffn",
    )(x, g, wg, wu, wd, gf)


def _rwkv_kernel(*refs, tm, seq_rows, npad, sample, lsub):
    L = CHUNK
    nsub = L // lsub
    nchunk = tm // L
    nsteps = _log2(lsub)
    it = iter(refs)
    x_ref = next(it)
    shx_ref = next(it) if sample else None
    (gn_ref, mu_ref, wr_ref, wk_ref, wv_ref, wo_ref, w0_ref, w1_ref, w2_ref, a0_ref, a1_ref, a2_ref,
     g1_ref, g2_ref, kk_ref, ka_ref, rk_ref, gw_ref, gb_ref) = [next(it) for _ in range(19)]
    s0_ref = next(it) if sample else None
    o_ref, sout_ref, xn_ref = next(it), next(it), next(it)
    r_s, ld_s, k_s, v_s, kn_s, b_s, g_s, bon_s, yg_s = [next(it) for _ in range(9)]
    if not sample:
        carry_s, sbd_s = next(it), next(it)
    d = x_ref.shape[1]
    npair = d // LANES

    lane = lax.broadcasted_iota(jnp.int32, (1, LANES), 1)
    mlo = lane < RW_HEAD

    x = x_ref[...]
    xn = _rms(x, gn_ref[...])
    rolled = pltpu.roll(xn, 1, 0)
    row = lax.broadcasted_iota(jnp.int32, (tm, 1), 0)
    if sample:
        xprev = jnp.where((row & (seq_rows - 1)) == npad, shx_ref[...], rolled)
        xn_ref[...] = xn
    else:
        j = pl.program_id(1)

        @pl.when(j == 0)
        def _():
            carry_s[...] = jnp.zeros(carry_s.shape, F32)
            sbd_s[...] = jnp.zeros(sbd_s.shape, F32)

        xprev = jnp.where(row == 0, carry_s[SUBLANES - 1:SUBLANES, :], rolled)
        carry_s[...] = xn[tm - SUBLANES:tm, :]
        xn_ref[...] = xn[tm - SUBLANES:tm, :]
    xx = xprev - xn

    def mix(i):
        return _b(xn + xx * mu_ref[i:i + 1, :])

    r = _dot(mix(0), wr_ref[...])
    wl = _b(jnp.tanh(_dot(mix(1), w1_ref[...])))
    ld = -math.exp(-0.5) * _sigmoid(w0_ref[...] + _dot(wl, w2_ref[...]))
    k = _dot(mix(2), wk_ref[...])
    v = _dot(mix(3), wv_ref[...])
    al = _b(_dot(mix(4), a1_ref[...]))
    a = _sigmoid(a0_ref[...] + _dot(al, a2_ref[...]))
    gl = _b(_sigmoid(_dot(mix(5), g1_ref[...])))
    g_s[...] = _dot(gl, g2_ref[...])
    k2 = k * (1.0 + (a - 1.0) * ka_ref[...])
    kk = k * kk_ref[...]
    rkk = r * k2 * rk_ref[...]
    if npad:
        keep = (row & (seq_rows - 1)) >= npad
        ld = jnp.where(keep, ld, 0.0)
        k2 = jnp.where(keep, k2, 0.0)
        kk = jnp.where(keep, kk, 0.0)
        v = jnp.where(keep, v, 0.0)
    r_s[...] = r
    ld_s[...] = ld
    k_s[...] = k2
    v_s[...] = v
    for p in range(npair):
        cs = slice(p * LANES, (p + 1) * LANES)
        kkp = kk[:, cs]
        kn = kkp / jnp.maximum(jnp.sqrt(_segsum(kkp * kkp, mlo)), 1e-12)
        kn_s[:, cs] = kn
        b_s[:, cs] = kn * a[:, cs]
        bon_s[:, cs] = _segsum(rkk[:, cs], mlo) * v[:, cs]

    sh = _log2(lsub)
    ti = lax.broadcasted_iota(jnp.int32, (L, 3 * L), 0)
    si = lax.broadcasted_iota(jnp.int32, (L, 3 * L), 1) & (L - 1)
    tril3 = jnp.where(((ti >> sh) == (si >> sh)) & (si <= ti), 1.0, 0.0).astype(BF16)
    gi = lax.broadcasted_iota(jnp.int32, (2 * L, 4 * L), 0)
    gj = lax.broadcasted_iota(jnp.int32, (2 * L, 4 * L), 1)
    gt = gi & (L - 1)
    gs = gj & (L - 1)
    gmask = ((gt >> sh) == (gs >> sh)) & ((gs < gt) | ((gi >= L) & (gs == gt)))
    pairs = range(npair)

    def stack2(z):
        return jnp.concatenate([jnp.where(mlo, z, 0), jnp.where(mlo, 0, z)], axis=0)

    cs = [slice(p * LANES, (p + 1) * LANES) for p in pairs]
    rw = {(c, p): slice(c * L, (c + 1) * L) for c in range(nchunk) for p in pairs}
    cum, a2f, a2, q4, vv2f, vv2, makv, mr, wcat, pl_, qpl = ({} for _ in range(11))

    def phase_a(chunks):
        chains = [(c, p) for c in chunks for p in pairs]
        for key in chains:
            ldc = ld_s[rw[key], cs[key[1]]]
            hi = _b(ldc)
            r1 = ldc - hi.astype(F32)
            mid = _b(r1)
            lo = _b(r1 - mid.astype(F32))
            cum[key] = _dot(tril3, jnp.concatenate([hi, mid, lo], axis=0))
        for key in chains:
            rows, c_ = rw[key], cs[key[1]]
            ep = jnp.exp(cum[key])
            em = jnp.exp(-cum[key])
            at = -(kn_s[rows, c_] * jnp.exp(cum[key] - ld_s[rows, c_]))
            a2f[key] = jnp.concatenate([at, r_s[rows, c_] * ep], axis=0)
            a2[key] = _b(a2f[key])
            q4[key] = jnp.concatenate([stack2(b_s[rows, c_] * em), stack2(k_s[rows, c_] * em)], axis=0)
            vv2f[key] = stack2(v_s[rows, c_])
            vv2[key] = _b(vv2f[key])
        g = {key: jnp.where(gmask, _dot_nt(a2[key], _b(q4[key])), 0.0) for key in chains}
        for key in chains:
            makv[key] = _dot(_b(g[key][:, LANES:]), vv2[key])
            mr[key] = _b(g[key][L:, :LANES])
        w = {key: g[key][:L, :LANES] for key in chains}
        pf = {}
        for key in chains:
            pc = _b(w[key])
            pf[key] = _dot(pc, stack2(pc))
        for k in range(1, nsteps):
            for key in chains:
                pc = _b(pf[key])
                if k + 1 < nsteps:
                    both = _dot(pc, jnp.concatenate([stack2(pc), stack2(_b(w[key]))], axis=1))
                    w[key] = w[key] + pf[key] + both[:, LANES:]
                    pf[key] = both[:, :LANES]
                else:
                    w[key] = w[key] + pf[key] + _dot(pc, stack2(_b(w[key])))
        for key in chains:
            wcat[key] = _b(w[key])
            if not sample:
                pl_[key] = jnp.exp(cum[key][L - 1:L, :])
                qpl[key] = _b(q4[key] * pl_[key])

    if not sample:
        state = [sbd_s[p] for p in pairs]
    phase_a(range(nchunk))
    for c in range(nchunk):
        keys = [(c, p) for p in pairs]
        if sample:
            s_in = [[_blockdiag(s0_ref[c * nsub + q, 2 * p], s0_ref[c * nsub + q, 2 * p + 1]) for q in range(nsub)]
                    for p in pairs]
            as_a, as_r = [], []
            for p in pairs:
                asa, asr = [], []
                for q in range(nsub):
                    a2q = _b(jnp.concatenate([a2f[c, p][q * lsub:(q + 1) * lsub],
                                              a2f[c, p][L + q * lsub:L + (q + 1) * lsub]], axis=0))
                    asq = _dot_nt(a2q, _b(s_in[p][q]))
                    asa.append(asq[:lsub])
                    asr.append(asq[lsub:])
                as_a.append(jnp.concatenate(asa, axis=0))
                as_r.append(jnp.concatenate(asr, axis=0))
        else:
            as_ = [_dot_nt(a2[key], _b(state[key[1]])) for key in keys]
            as_a = [z[:L] for z in as_]
            as_r = [z[L:] for z in as_]
        rhs = [as_a[p] + makv[c, p][:L] for p in pairs]
        u = [rhs[p] + _dot(wcat[c, p], stack2(_b(rhs[p]))) for p in pairs]
        u2 = [stack2(_b(u[p])) for p in pairs]
        if sample:
            uvf = [jnp.concatenate([stack2(u[p]), vv2f[c, p]], axis=0) for p in pairs]
        else:
            uv = [jnp.concatenate([u2[p], vv2[c, p]], axis=0) for p in pairs]
        y = [as_r[p] + makv[c, p][L:] + _dot(mr[c, p], u2[p]) for p in pairs]
        for p in pairs:
            if sample:
                for q in range(nsub):
                    sel = [slice(blk * L + q * lsub, blk * L + (q + 1) * lsub) for blk in range(4)]
                    pl_q = jnp.exp(cum[c, p][(q + 1) * lsub - 1:(q + 1) * lsub, :])
                    uvq = _b(jnp.concatenate([uvf[p][s_] for s_ in sel], axis=0))
                    qq = _b(jnp.concatenate([q4[c, p][s_] for s_ in sel], axis=0) * pl_q)
                    s_new = s_in[p][q] * pl_q + _dot_tn(uvq, qq)
                    sout_ref[c * nsub + q, 2 * p] = s_new[:RW_HEAD, :RW_HEAD]
                    sout_ref[c * nsub + q, 2 * p + 1] = s_new[RW_HEAD:, RW_HEAD:]
            else:
                state[p] = state[p] * pl_[c, p] + _dot_tn(uv[p], qpl[c, p])
        rows = rw[c, 0]
        for p in pairs:
            mean = _segsum(y[p], mlo) * (1.0 / RW_HEAD)
            yc = y[p] - mean
            var = _segsum(yc * yc, mlo) * (1.0 / RW_HEAD)
            yn = yc * lax.rsqrt(var + RW_GN_EPS) * gw_ref[:, cs[p]] + gb_ref[:, cs[p]] + bon_s[rows, cs[p]]
            yg_s[rows, cs[p]] = _b(yn * g_s[rows, cs[p]])
    if not sample:
        for p in pairs:
            sbd_s[p] = state[p]
            sout_ref[0, 2 * p] = state[p][:RW_HEAD, :RW_HEAD]
            sout_ref[0, 2 * p + 1] = state[p][RW_HEAD:, RW_HEAD:]
    o_ref[...] = x_ref[...] + _dot(yg_s[...], wo_ref[...])


def _rwkv_call(x, shx, s0bd, wts, *, nseq, seq_rows, npad, sample, tm):
    n, d = x.shape
    npair = d // LANES
    lsub = seq_rows if sample else CHUNK
    kern = functools.partial(_rwkv_kernel, tm=tm, seq_rows=seq_rows, npad=npad, sample=sample, lsub=lsub)
    if sample:
        grid = (n // tm,)
        tile = lambda i: (i, 0)
        spt = tm // seq_rows
        st_spec = pl.BlockSpec((spt, 2 * npair, RW_HEAD, RW_HEAD), lambda i: (i, 0, 0, 0))
        xn_spec = pl.BlockSpec((tm, d), tile)
        xn_shape = (n, d)
        sem = ("arbitrary",)
    else:
        tps = seq_rows // tm
        grid = (nseq, tps)
        tile = lambda b, j: (b * tps + j, 0)
        st_spec = pl.BlockSpec((1, 2 * npair, RW_HEAD, RW_HEAD), lambda b, j: (b, 0, 0, 0))
        xn_spec = pl.BlockSpec((SUBLANES, d), lambda b, j: (b, 0))
        xn_shape = (nseq * SUBLANES, d)
        sem = ("arbitrary", "arbitrary")
    x_spec = pl.BlockSpec((tm, d), tile)
    in_specs = [x_spec] + ([x_spec] if sample else [])
    args = [x] + ([shx] if sample else [])
    for wt in wts:
        in_specs.append(_const_spec(wt.shape))
        args.append(wt)
    if sample:
        in_specs.append(st_spec)
        args.append(s0bd)
    scratch = [pltpu.VMEM((tm, d), F32) for _ in range(8)] + [pltpu.VMEM((tm, d), BF16)]
    if not sample:
        scratch += [pltpu.VMEM((SUBLANES, d), F32), pltpu.VMEM((npair, LANES, LANES), F32)]
    return pl.pallas_call(
        kern,
        grid=grid,
        in_specs=in_specs,
        out_specs=[x_spec, st_spec, xn_spec],
        out_shape=[jax.ShapeDtypeStruct((n, d), F32),
                   jax.ShapeDtypeStruct((nseq, 2 * npair, RW_HEAD, RW_HEAD), F32),
                   jax.ShapeDtypeStruct(xn_shape, F32)],
        scratch_shapes=scratch,
        compiler_params=pltpu.CompilerParams(dimension_semantics=sem, vmem_limit_bytes=VMEM_LIMIT),
        name="rwkv_sample" if sample else "rwkv_prompt",
    )(*args)


def _mlstm_kernel(*refs, tm, seq_rows, npad, sample, lsub):
    L = CHUNK
    nsub = L // lsub
    nchunk = tm // L
    it = iter(refs)
    x_ref = next(it)
    if sample:
        cvx_ref, mrow_ref = next(it), next(it)
    (gn_ref, wqk_ref, wv_ref, wo_ref, wif_ref, wift_ref, bif_ref, bift_ref, cw_ref, cb_ref, nw_ref,
     wout_ref) = [next(it) for _ in range(12)]
    if sample:
        c0_ref, n0_ref = next(it), next(it)
    o_ref, cout_ref, nout_ref, mout_ref, cvout_ref = [next(it) for _ in range(5)]
    xb_s, q_s, k_s, v_s, og_s, gc_s, ho_s = [next(it) for _ in range(7)]
    if not sample:
        carry_s, cp_s, np_s, m_s = [next(it) for _ in range(4)]
    nqk = wqk_ref.shape[1]
    half = nqk // 2
    nheads = half // ML_DK
    ngate = 2 * nheads
    npair = nheads // 2

    lane = lax.broadcasted_iota(jnp.int32, (1, LANES), 1)
    mlo = lane < ML_DK
    glane = lax.broadcasted_iota(jnp.int32, (1, ngate), 1)
    isf_c = glane >= nheads
    grow = lax.broadcasted_iota(jnp.int32, (ngate, 1), 0)
    isf_r = grow >= nheads
    hlane = lax.broadcasted_iota(jnp.int32, (1, nheads), 1)

    x = x_ref[...]
    xb = _b(_rms(x, gn_ref[...]))
    xb_s[...] = xb
    raw = _dot(xb, wqk_ref[...])
    row = lax.broadcasted_iota(jnp.int32, (tm, 1), 0)
    if sample:
        srow = row & (seq_rows - 1)
        raw = jnp.where((srow >= npad - (ML_CONV - 1)) & (srow < npad), cvx_ref[...], raw)
        cvout_ref[...] = raw
        ext = jnp.concatenate([jnp.zeros((SUBLANES, nqk), F32), raw], axis=0)
    else:
        j = pl.program_id(1)

        @pl.when(j == 0)
        def _():
            carry_s[...] = jnp.zeros(carry_s.shape, F32)
            cp_s[...] = jnp.zeros(cp_s.shape, F32)
            np_s[...] = jnp.zeros(np_s.shape, F32)
            m_s[...] = jnp.zeros(m_s.shape, F32)

        ext = jnp.concatenate([carry_s[...], raw], axis=0)
        carry_s[...] = raw[tm - SUBLANES:tm, :]
        cvout_ref[...] = raw[tm - SUBLANES:tm, :]
    qk = cb_ref[...] + cw_ref[ML_CONV - 1:ML_CONV, :] * raw
    for s in range(1, ML_CONV):
        qk = qk + cw_ref[ML_CONV - 1 - s:ML_CONV - s, :] * pltpu.roll(ext, s, 0)[SUBLANES:, :]
    qk = qk * _sigmoid(qk)
    q_s[...] = qk[:, :half] * (ML_DK ** -0.5)
    k_s[...] = qk[:, half:]
    v_s[...] = _dot(xb, wv_ref[...])
    og_s[...] = _sigmoid(_dot(xb, wo_ref[...]))
    ifp = _dot(xb, wif_ref[...]) + bif_ref[...]
    gcol = jnp.where(isf_c, -_softplus(-ifp), ifp)
    if npad:
        keep = (row & (seq_rows - 1)) >= npad
        gcol = jnp.where(keep, gcol, jnp.where(isf_c, 0.0, NEG_BIG))
    gc_s[...] = gcol

    ti = lax.broadcasted_iota(jnp.int32, (L, L), 0)
    si = lax.broadcasted_iota(jnp.int32, (L, L), 1)
    sh = _log2(lsub)
    same = (ti >> sh) == (si >> sh)
    causal = same & (si <= ti)
    tril = jnp.where(causal, 1.0, 0.0).astype(F32)
    triu = jnp.where(same & (ti <= si), 1.0, 0.0).astype(F32)
    lrow = lax.broadcasted_iota(jnp.int32, (1, L), 1)

    def chunk(c, carry):
        r0 = pl.multiple_of(c * L, L)
        rows = pl.ds(r0, L)
        gc = gc_s[rows, :]
        gt = _dot_nt(wift_ref[...], xb_s[rows, :]) + bift_ref[...]
        gt = jnp.where(isf_r, -_softplus(-gt), gt)
        if npad:
            gt = jnp.where((lrow & (seq_rows - 1)) >= npad, gt, jnp.where(isf_r, 0.0, NEG_BIG))
        bcs = _dot(tril, jnp.where(isf_c, gc, 0.0), HI)
        brs = _dot(jnp.where(isf_r, gt, 0.0), triu, HI)
        blast = _seq_last(bcs, lsub)
        if sample:
            mcols = mrow_ref[rows, :]
        else:
            mcols = m_s[...]
        heads = range(nheads)
        prs = range(npair)
        ps = [slice(pp * LANES, (pp + 1) * LANES) for pp in prs]
        hs = [slice(h * ML_DV, (h + 1) * ML_DV) for h in heads]
        q2 = [q_s[rows, ps[pp]] for pp in prs]
        k2 = [k_s[rows, ps[pp]] for pp in prs]
        k2b = [_b(z) for z in k2]
        if sample:
            c_in = [[jnp.concatenate([c0_ref[c * nsub + q, 2 * pp], c0_ref[c * nsub + q, 2 * pp + 1]], axis=1)
                     for q in range(nsub)] for pp in prs]
            n_in = [[n0_ref[c * nsub + q][:, ps[pp]] for q in range(nsub)] for pp in prs]
            n_rows = [jnp.concatenate([jnp.broadcast_to(n_in[pp][q], (lsub, LANES)) for q in range(nsub)], axis=0)
                      for pp in prs]
        else:
            c_prev = [cp_s[pp] for pp in prs]
            n_prev = [np_s[:, ps[pp]] for pp in prs]
            n_rows = n_prev
        bcol = [_colsel(bcs, glane, nheads + h) for h in heads]
        licol = [_colsel(gc, glane, h) for h in heads]
        mcol = [_colsel(mcols, hlane, h) for h in heads]
        blcol = [_colsel(blast, glane, nheads + h) for h in heads]
        dlog = [jnp.where(causal, bcol[h] - (brs[nheads + h:nheads + h + 1, :] - gt[h:h + 1, :]), -jnp.inf)
                for h in heads]
        ginter = [bcol[h] + mcol[h] for h in heads]
        m_t = [jnp.maximum(ginter[h], jnp.max(dlog[h], axis=-1, keepdims=True)) for h in heads]
        dw = [jnp.exp(dlog[h] - m_t[h]) for h in heads]
        winter = [jnp.exp(ginter[h] - m_t[h]) for h in heads]
        qh = [jnp.where(mlo if h % 2 == 0 else jnp.logical_not(mlo), q2[h // 2], 0.0) for h in heads]
        qhb = [_b(z) for z in qh]
        sc = [_dot_nt(qhb[h], k2b[h // 2]) * dw[h] for h in heads]
        if sample:
            inter = [jnp.concatenate([_dot_nt(qhb[h][q * lsub:(q + 1) * lsub], _b(c_in[h // 2][q]))
                                      for q in range(nsub)], axis=0) for h in heads]
        else:
            cb = [_b(z) for z in c_prev]
            inter = [_dot_nt(qhb[h], cb[h // 2]) for h in heads]
        vh = [v_s[rows, hs[h]] for h in heads]
        num = [winter[h] * inter[h] + _dot(_b(sc[h]), _b(vh[h])) for h in heads]
        den = [winter[h] * jnp.sum(qh[h] * n_rows[h // 2], axis=-1, keepdims=True)
               + jnp.sum(sc[h], axis=-1, keepdims=True) for h in heads]
        for h in heads:
            hout = num[h] / jnp.maximum(jnp.abs(den[h]), jnp.exp(-m_t[h]))
            hn = hout * lax.rsqrt(jnp.mean(hout * hout, axis=-1, keepdims=True) + NORM_EPS) * nw_ref[:, hs[h]]
            ho_s[rows, hs[h]] = _b(hn * og_s[rows, hs[h]])
        mnew = [_seq_last(m_t[h], lsub) for h in heads]
        ws = [jnp.exp(blcol[h] - bcol[h] + licol[h] - mnew[h]) for h in heads]
        wstc = [jnp.exp(blcol[h] + mcol[h] - mnew[h]) for h in heads]
        mt_all = jnp.zeros((L, nheads), F32)
        for h in heads:
            mt_all = jnp.where(hlane == h, m_t[h], mt_all)
        for pp in prs:
            lo, hi = 2 * pp, 2 * pp + 1
            wsk = jnp.where(mlo, ws[lo], ws[hi]) * k2[pp]
            wst = jnp.where(mlo, wstc[lo], wstc[hi])
            wvb = _b(jnp.concatenate([ws[lo] * vh[lo], ws[hi] * vh[hi]], axis=0))
            kmb = jnp.concatenate([jnp.where(mlo, k2b[pp], 0), jnp.where(mlo, 0, k2b[pp])], axis=0)
            if sample:
                for q in range(nsub):
                    last = (q + 1) * lsub - 1
                    sel = [slice(blk * L + q * lsub, blk * L + (q + 1) * lsub) for blk in range(2)]
                    wq = wst[last:last + 1, :]
                    upd = _dot_tn(jnp.concatenate([wvb[s_] for s_ in sel], axis=0),
                                  jnp.concatenate([kmb[s_] for s_ in sel], axis=0))
                    c_new = c_in[pp][q] * wq + upd
                    cout_ref[c * nsub + q, 2 * pp] = c_new[:, :ML_DK]
                    cout_ref[c * nsub + q, 2 * pp + 1] = c_new[:, ML_DK:]
                    nq = n_in[pp][q] * wq + jnp.sum(wsk[q * lsub:(q + 1) * lsub], axis=0, keepdims=True)
                    nout_ref[c * nsub + q, :, ps[pp]] = nq
            else:
                cp_s[pp] = c_prev[pp] * wst + _dot_tn(wvb, kmb)
                np_s[:, ps[pp]] = n_prev[pp] * wst + jnp.sum(wsk, axis=0, keepdims=True)
        if sample:
            mout_ref[rows, :] = mt_all
        else:
            m_s[...] = mt_all[L - 1:L, :]
        return carry

    lax.fori_loop(0, nchunk, chunk, 0)
    if not sample:
        for pp in range(npair):
            cout_ref[0, 2 * pp] = cp_s[pp][:, :ML_DK]
            cout_ref[0, 2 * pp + 1] = cp_s[pp][:, ML_DK:]
        nout_ref[0] = np_s[...]
        mout_ref[0] = m_s[...]
    o_ref[...] = x_ref[...] + _dot(ho_s[...], wout_ref[...])


def _mlstm_call(x, cvx, mrow, c0p, n0, wts, *, nseq, seq_rows, npad, sample, tm):
    n, d = x.shape
    nqk = wts[1].shape[1]
    nv = wts[2].shape[1]
    nheads = nqk // 2 // ML_DK
    npair = nheads // 2
    lsub = seq_rows if sample else CHUNK
    kern = functools.partial(_mlstm_kernel, tm=tm, seq_rows=seq_rows, npad=npad, sample=sample, lsub=lsub)
    if sample:
        grid = (n // tm,)
        tile = lambda i: (i, 0)
        spt = tm // seq_rows
        c_spec = pl.BlockSpec((spt, nheads, ML_DV, ML_DK), lambda i: (i, 0, 0, 0))
        n_spec = pl.BlockSpec((spt, 1, nqk // 2), lambda i: (i, 0, 0))
        m_spec = pl.BlockSpec((tm, nheads), tile)
        m_shape = (n, nheads)
        cv_spec = pl.BlockSpec((tm, nqk), tile)
        cv_shape = (n, nqk)
        sem = ("arbitrary",)
    else:
        tps = seq_rows // tm
        grid = (nseq, tps)
        tile = lambda b, j: (b * tps + j, 0)
        c_spec = pl.BlockSpec((1, nheads, ML_DV, ML_DK), lambda b, j: (b, 0, 0, 0))
        n_spec = pl.BlockSpec((1, 1, nqk // 2), lambda b, j: (b, 0, 0))
        m_spec = pl.BlockSpec((1, 1, nheads), lambda b, j: (b, 0, 0))
        m_shape = (nseq, 1, nheads)
        cv_spec = pl.BlockSpec((SUBLANES, nqk), lambda b, j: (b, 0))
        cv_shape = (nseq * SUBLANES, nqk)
        sem = ("arbitrary", "arbitrary")
    x_spec = pl.BlockSpec((tm, d), tile)
    in_specs = [x_spec]
    args = [x]
    if sample:
        in_specs += [pl.BlockSpec((tm, nqk), tile), pl.BlockSpec((tm, nheads), tile)]
        args += [cvx, mrow]
    for wt in wts:
        in_specs.append(_const_spec(wt.shape))
        args.append(wt)
    if sample:
        in_specs += [c_spec, n_spec]
        args += [c0p, n0]
    scratch = [pltpu.VMEM((tm, d), BF16), pltpu.VMEM((tm, nqk // 2), F32), pltpu.VMEM((tm, nqk // 2), F32),
               pltpu.VMEM((tm, nv), F32), pltpu.VMEM((tm, nv), F32), pltpu.VMEM((tm, 2 * nheads), F32),
               pltpu.VMEM((tm, nv), BF16)]
    if not sample:
        scratch += [pltpu.VMEM((SUBLANES, nqk), F32), pltpu.VMEM((npair, LANES, LANES), F32),
                    pltpu.VMEM((1, nqk // 2), F32), pltpu.VMEM((1, nheads), F32)]
    return pl.pallas_call(
        kern,
        grid=grid,
        in_specs=in_specs,
        out_specs=[x_spec, c_spec, n_spec, m_spec, cv_spec],
        out_shape=[jax.ShapeDtypeStruct((n, d), F32),
                   jax.ShapeDtypeStruct((nseq, nheads, ML_DV, ML_DK), F32),
                   jax.ShapeDtypeStruct((nseq, 1, nqk // 2), F32),
                   jax.ShapeDtypeStruct(m_shape, F32),
                   jax.ShapeDtypeStruct(cv_shape, F32)],
        scratch_shapes=scratch,
        compiler_params=pltpu.CompilerParams(dimension_semantics=sem, vmem_limit_bytes=VMEM_LIMIT),
        name="mlstm_sample" if sample else "mlstm_prompt",
    )(*args)


def kernel(x_prompt, x_sample, state_rwkv_S, state_rwkv_shift, state_mlstm_C, state_mlstm_n, state_mlstm_m,
           state_mlstm_conv, norm_ffa, ffa_wg, ffa_wu, ffa_wd, norm_mix, norm_ffb, ffb_wg, ffb_wu, ffb_wd,
           rw_mu, rw_wr, rw_wk, rw_wv, rw_wo, rw_w0, rw_w1, rw_w2, rw_a0, rw_a1, rw_a2, rw_g1, rw_g2,
           rw_k_k, rw_k_a, rw_r_k, rw_gn_w, rw_gn_b, ml_w_in, ml_b_if, ml_conv_w, ml_conv_b, ml_norm_w,
           ml_w_out, norm_final):
    bp, tp, d = x_prompt.shape
    bs, ts, _ = x_sample.shape
    depth = norm_ffa.shape[0]
    slot = SUBLANES
    npad = slot - ts
    assert 0 < ts <= slot and npad >= ML_CONV - 1
    tm_p = min(256, tp)
    assert tp % tm_p == 0 and tm_p % CHUNK == 0 and (bs * slot) % CHUNK == 0
    ml_heads = ml_b_if.shape[1] // 2
    nqk = 2 * ml_heads * ML_DK
    nv = ml_heads * ML_DV

    xp = x_prompt.reshape(bp * tp, d)
    xs = jnp.concatenate([jnp.zeros((bs, npad, d), F32), x_sample], axis=1).reshape(bs * slot, d)
    row2 = lambda a: a.reshape(1, -1)

    def ffn(x, norm, wg, wu, wd, final, tm):
        return _ffn_call(x, row2(norm), _b(wg), _b(wu), _b(wd), row2(norm_final), final_norm=final, tm=tm)

    tm_ffn_p = min(512, bp * tp)
    tm_ffn_s = min(256, bs * slot)
    new_p = {k_: [] for k_ in ("S", "shift", "C", "n", "m", "conv")}
    new_s = {k_: [] for k_ in ("S", "shift", "C", "n", "m", "conv")}
    for i in range(depth):
        xp = ffn(xp, norm_ffa[i], ffa_wg[i], ffa_wu[i], ffa_wd[i], False, tm_ffn_p)
        xs = ffn(xs, norm_ffa[i], ffa_wg[i], ffa_wu[i], ffa_wd[i], False, tm_ffn_s)
        j = i // 2
        if i % 2 == 0:
            wts = [row2(norm_mix[i]), rw_mu[j], _b(rw_wr[j]), _b(rw_wk[j]), _b(rw_wv[j]), _b(rw_wo[j]),
                   row2(rw_w0[j]), _b(rw_w1[j]), _b(rw_w2[j]), row2(rw_a0[j]), _b(rw_a1[j]), _b(rw_a2[j]),
                   _b(rw_g1[j]), _b(rw_g2[j]), row2(rw_k_k[j]), row2(rw_k_a[j]), row2(rw_r_k[j]),
                   row2(rw_gn_w[j]), row2(rw_gn_b[j])]
            xp, sbd, tail = _rwkv_call(xp, None, None, wts, nseq=bp, seq_rows=tp, npad=0, sample=False, tm=tm_p)
            new_p["S"].append(sbd)
            new_p["shift"].append(tail.reshape(bp, SUBLANES, d)[:, SUBLANES - 1])
            shx = jnp.repeat(state_rwkv_shift[j], slot, axis=0)
            xs, sbd, xn = _rwkv_call(xs, shx, state_rwkv_S[j], wts, nseq=bs, seq_rows=slot,
                                     npad=npad, sample=True, tm=CHUNK)
            new_s["S"].append(sbd)
            new_s["shift"].append(xn.reshape(bs, slot, d)[:, slot - 1])
        else:
            w_in = ml_w_in[j]
            w_if = w_in[:, nqk + nv + d:]
            wts = [row2(norm_mix[i]), _b(w_in[:, :nqk]), _b(w_in[:, nqk:nqk + nv]), _b(w_in[:, nqk + nv:nqk + nv + d]),
                   _b(w_if), _b(w_if.T), row2(ml_b_if[j]), ml_b_if[j].reshape(-1, 1), ml_conv_w[j],
                   row2(ml_conv_b[j]), row2(ml_norm_w[j]), _b(ml_w_out[j])]
            xp, cp, n_, m_, tail = _mlstm_call(xp, None, None, None, None, wts, nseq=bp, seq_rows=tp, npad=0,
                                               sample=False, tm=tm_p)
            new_p["C"].append(cp)
            new_p["n"].append(n_.reshape(bp, ml_heads, ML_DK))
            new_p["m"].append(m_.reshape(bp, ml_heads))
            new_p["conv"].append(tail.reshape(bp, SUBLANES, nqk)[:, SUBLANES - (ML_CONV - 1):])
            conv0 = state_mlstm_conv[j]
            cvx = jnp.concatenate([jnp.zeros((bs, npad - (ML_CONV - 1), nqk), F32), conv0,
                                   jnp.zeros((bs, slot - npad, nqk), F32)], axis=1).reshape(bs * slot, nqk)
            mrow = jnp.repeat(state_mlstm_m[j], slot, axis=0)
            xs, cp, n_, mt, raw = _mlstm_call(xs, cvx, mrow, state_mlstm_C[j],
                                              state_mlstm_n[j].reshape(bs, 1, ml_heads * ML_DK), wts, nseq=bs,
                                              seq_rows=slot, npad=npad, sample=True, tm=CHUNK)
            new_s["C"].append(cp)
            new_s["n"].append(n_.reshape(bs, ml_heads, ML_DK))
            new_s["m"].append(mt.reshape(bs, slot, ml_heads)[:, slot - 1])
            new_s["conv"].append(raw.reshape(bs, slot, nqk)[:, slot - (ML_CONV - 1):])
        last = i == depth - 1
        xp = ffn(xp, norm_ffb[i], ffb_wg[i], ffb_wu[i], ffb_wd[i], last, tm_ffn_p)
        xs = ffn(xs, norm_ffb[i], ffb_wg[i], ffb_wu[i], ffb_wd[i], last, tm_ffn_s)
    y_prompt = xp.reshape(bp, tp, d)
    y_sample = xs.reshape(bs, slot, d)[:, npad:]
    st = lambda lst: jnp.stack(lst)
    return (y_prompt, y_sample,
            st(new_p["S"]), st(new_p["shift"]), st(new_p["C"]), st(new_p["n"]), st(new_p["m"]), st(new_p["conv"]),
            st(new_s["S"]), st(new_s["shift"]), st(new_s["C"]), st(new_s["n"]), st(new_s["m"]), st(new_s["conv"]))
```

```python
import functools
import math

import jax
import jax.numpy as jnp
from jax import lax
from jax.experimental import pallas as pl
from jax.experimental.pallas import tpu as pltpu

F32 = jnp.float32
BF16 = jnp.bfloat16

NORM_EPS = 1e-6
RW_GN_EPS = 64e-5
RW_HEAD = 64
ML_DK = 64
ML_DV = 128
ML_CONV = 4

LANES = 128
SUBLANES = 8
CHUNK = 64
NEG_BIG = -1e30
VMEM_LIMIT = 56 * 1024 * 1024
HI = lax.Precision.HIGHEST


def _dot(a, b, precision=None):
    return jnp.dot(a, b, preferred_element_type=F32, precision=precision)


def _dot_nt(a, b, precision=None):
    return lax.dot_general(a, b, (((1,), (1,)), ((), ())), preferred_element_type=F32, precision=precision)


def _dot_tn(a, b, precision=None):
    return lax.dot_general(a, b, (((0,), (0,)), ((), ())), preferred_element_type=F32, precision=precision)


def _b(x):
    return x.astype(BF16)


def _rms(x, g):
    return x * lax.rsqrt(jnp.mean(x * x, axis=-1, keepdims=True) + NORM_EPS) * g


def _sigmoid(x):
    return 1.0 / (1.0 + jnp.exp(-x))


def _softplus(x):
    return jnp.maximum(x, 0.0) + jnp.log1p(jnp.exp(-jnp.abs(x)))


def _segsum(x, mlo):
    lo = jnp.sum(jnp.where(mlo, x, 0.0), axis=-1, keepdims=True)
    hi = jnp.sum(jnp.where(mlo, 0.0, x), axis=-1, keepdims=True)
    return jnp.where(mlo, lo, hi)


def _colsel(x, lane_idx, j):
    return jnp.sum(jnp.where(lane_idx == j, x, 0.0), axis=-1, keepdims=True)


def _seq_last(x, lsub):
    n = x.shape[0]
    if lsub == n:
        return x[n - 1:n]
    parts = [jnp.broadcast_to(x[q * lsub + lsub - 1:q * lsub + lsub], (lsub,) + x.shape[1:])
             for q in range(n // lsub)]
    return jnp.concatenate(parts, axis=0)


def _blockdiag(a, b):
    z = jnp.zeros(a.shape, a.dtype)
    return jnp.concatenate([jnp.concatenate([a, z], axis=1), jnp.concatenate([z, b], axis=1)], axis=0)


def _const_spec(shape, single_buffer=False):
    nd = len(shape)
    if single_buffer:
        return pl.BlockSpec(shape, lambda *_: (0,) * nd, pipeline_mode=pl.Buffered(1))
    return pl.BlockSpec(shape, lambda *_: (0,) * nd)


def _log2(n):
    k = int(math.log2(n))
    assert 1 << k == n, n
    return k


def _ffn_kernel(xp_ref, xs_ref, g_ref, wg_ref, wu_ref, wd_ref, gf_ref, op_ref, os_ref, *, fchunk, final_norm, np_steps):
    is_p = pl.program_id(0) < np_steps
    x = jnp.where(is_p, xp_ref[...], xs_ref[...])
    xb = _b(_rms(x, g_ref[...]))
    nf = wg_ref.shape[1]
    acc = jnp.zeros(x.shape, F32)
    for c in range(nf // fchunk):
        sl = slice(c * fchunk, (c + 1) * fchunk)
        gate = _dot(xb, wg_ref[:, sl])
        up = _dot(xb, wu_ref[:, sl])
        h = _b(gate * _sigmoid(gate) * up)
        acc = acc + _dot(h, wd_ref[sl, :])
    out = x + 0.5 * acc
    if final_norm:
        out = _rms(out, gf_ref[...])

    @pl.when(is_p)
    def _():
        op_ref[...] = out

    @pl.when(jnp.logical_not(is_p))
    def _():
        os_ref[...] = out


def _ffn_call(xp, xs, g_all, wg_all, wu_all, wd_all, gf, layer, *, final_norm, tm):
    (n_p, d), n_s = xp.shape, xs.shape[0]
    nf = wg_all.shape[2]
    np_steps, ns_steps = n_p // tm, n_s // tm
    fchunk = nf // 2 if (nf // 2) % LANES == 0 else nf
    kern = functools.partial(_ffn_kernel, fchunk=fchunk, final_norm=final_norm, np_steps=np_steps)
    p_spec = pl.BlockSpec((tm, d), lambda i: (jnp.minimum(i, np_steps - 1), 0))
    s_spec = pl.BlockSpec((tm, d), lambda i: (jnp.maximum(i - np_steps, 0), 0))
    w_spec = lambda a, b: pl.BlockSpec((None, a, b), lambda i: (layer, 0, 0), pipeline_mode=pl.Buffered(1))
    return pl.pallas_call(
        kern,
        grid=(np_steps + ns_steps,),
        in_specs=[p_spec, s_spec, pl.BlockSpec((None, 1, d), lambda i: (layer, 0, 0)),
                  w_spec(d, nf), w_spec(d, nf), w_spec(nf, d), _const_spec((1, d))],
        out_specs=[p_spec, s_spec],
        out_shape=[jax.ShapeDtypeStruct((n_p, d), F32), jax.ShapeDtypeStruct((n_s, d), F32)],
        compiler_params=pltpu.CompilerParams(dimension_semantics=("arbitrary",), vmem_limit_bytes=VMEM_LIMIT),
        name="ffn",
    )(xp, xs, g_all, wg_all, wu_all, wd_all, gf)


def _rw_token_part(xn, xx, row, mlo, sset, seq_rows, npad, wrefs):
    r_s, ld_s, k_s, v_s, kn_s, b_s, g_s, bon_s = sset
    mu_ref, wr_ref, wk_ref, wv_ref, w0_ref, w1_ref, w2_ref, a0_ref, a1_ref, a2_ref, g1_ref, g2_ref, kk_ref, ka_ref, rk_ref = wrefs

    def mix(i):
        return _b(xn + xx * mu_ref[i:i + 1, :])

    r = _dot(mix(0), wr_ref[...])
    r_s[...] = r
    wl = _b(jnp.tanh(_dot(mix(1), w1_ref[...])))
    ld = -math.exp(-0.5) * _sigmoid(w0_ref[...] + _dot(wl, w2_ref[...]))
    k = _dot(mix(2), wk_ref[...])
    v = _dot(mix(3), wv_ref[...])
    al = _b(_dot(mix(4), a1_ref[...]))
    a = _sigmoid(a0_ref[...] + _dot(al, a2_ref[...]))
    gl = _b(_sigmoid(_dot(mix(5), g1_ref[...])))
    g_s[...] = _dot(gl, g2_ref[...])
    k2 = k * (1.0 + (a - 1.0) * ka_ref[...])
    kk = k * kk_ref[...]
    rkk = r * k2 * rk_ref[...]
    if npad:
        keep = (row & (seq_rows - 1)) >= npad
        ld = jnp.where(keep, ld, 0.0)
        k2 = jnp.where(keep, k2, 0.0)
        kk = jnp.where(keep, kk, 0.0)
        v = jnp.where(keep, v, 0.0)
    ld_s[...] = ld
    k_s[...] = k2
    v_s[...] = v
    for p in range(xn.shape[1] // LANES):
        cs = slice(p * LANES, (p + 1) * LANES)
        kkp = kk[:, cs]
        kn = kkp / jnp.maximum(jnp.sqrt(_segsum(kkp * kkp, mlo)), 1e-12)
        kn_s[:, cs] = kn
        b_s[:, cs] = kn * a[:, cs]
        bon_s[:, cs] = _segsum(rkk[:, cs], mlo) * v[:, cs]


def _rwkv_kernel(*refs, tm, seq_rows, npad, sample, lsub):
    L = CHUNK
    nsub = L // lsub
    nchunk = tm // L
    nsteps = _log2(lsub)
    it = iter(refs)
    x_ref = next(it)
    shx_ref = next(it) if sample else None
    (gn_ref, mu_ref, wr_ref, wk_ref, wv_ref, wo_ref, w0_ref, w1_ref, w2_ref, a0_ref, a1_ref, a2_ref,
     g1_ref, g2_ref, kk_ref, ka_ref, rk_ref, gw_ref, gb_ref) = [next(it) for _ in range(19)]
    s0_ref = next(it) if sample else None
    o_ref, sout_ref, xn_ref = next(it), next(it), next(it)
    set_a = [next(it) for _ in range(8)]
    yg_s = next(it)
    if not sample:
        carry_s, sbd_s = next(it), next(it)
    d = x_ref.shape[1]
    npair = d // LANES

    lane = lax.broadcasted_iota(jnp.int32, (1, LANES), 1)
    mlo = lane < RW_HEAD
    row = lax.broadcasted_iota(jnp.int32, (tm, 1), 0)

    xn = _rms(x_ref[...], gn_ref[...])
    rolled = pltpu.roll(xn, 1, 0)
    if sample:
        sbd_s = None
        xprev = jnp.where((row & (seq_rows - 1)) == npad, shx_ref[...], rolled)
        xn_ref[...] = xn
    else:
        @pl.when(pl.program_id(1) == 0)
        def _():
            carry_s[...] = jnp.zeros(carry_s.shape, F32)
            sbd_s[...] = jnp.zeros(sbd_s.shape, F32)

        xprev = jnp.where(row == 0, carry_s[SUBLANES - 1:SUBLANES, :], rolled)
        carry_s[...] = xn[tm - SUBLANES:tm, :]
        xn_ref[...] = xn[tm - SUBLANES:tm, :]
    _rw_token_part(xn, xprev - xn, row, mlo, set_a, seq_rows, npad,
                   (mu_ref, wr_ref, wk_ref, wv_ref, w0_ref, w1_ref, w2_ref, a0_ref, a1_ref, a2_ref,
                    g1_ref, g2_ref, kk_ref, ka_ref, rk_ref))
    _rw_back(set_a, sbd_s, sample=sample, lsub=lsub, nchunk=nchunk, nsub=nsub, nsteps=nsteps, npair=npair, mlo=mlo,
             s0_ref=s0_ref, sout_ref=sout_ref, yg_s=yg_s, gw_ref=gw_ref, gb_ref=gb_ref, wo_ref=wo_ref,
             xres_ref=x_ref, o_ref=o_ref)


def _rw_back(sset, sbd_s, *, sample, lsub, nchunk, nsub, nsteps, npair, mlo, s0_ref, sout_ref, yg_s,
             gw_ref, gb_ref, wo_ref, xres_ref, o_ref):
    r_s, ld_s, k_s, v_s, kn_s, b_s, g_s, bon_s = sset
    L = CHUNK
    sh = _log2(lsub)
    ti = lax.broadcasted_iota(jnp.int32, (L, 3 * L), 0)
    si = lax.broadcasted_iota(jnp.int32, (L, 3 * L), 1) & (L - 1)
    tril3 = jnp.where(((ti >> sh) == (si >> sh)) & (si <= ti), 1.0, 0.0).astype(BF16)
    gi = lax.broadcasted_iota(jnp.int32, (2 * L, 4 * L), 0)
    gj = lax.broadcasted_iota(jnp.int32, (2 * L, 4 * L), 1)
    gt = gi & (L - 1)
    gs = gj & (L - 1)
    gmask = ((gt >> sh) == (gs >> sh)) & ((gs < gt) | ((gi >= L) & (gs == gt)))
    pairs = range(npair)

    def stack2(z):
        return jnp.concatenate([jnp.where(mlo, z, 0), jnp.where(mlo, 0, z)], axis=0)

    cs = [slice(p * LANES, (p + 1) * LANES) for p in pairs]
    rw = {(c, p): slice(c * L, (c + 1) * L) for c in range(nchunk) for p in pairs}
    cum, a2f, a2, q4, vv2f, vv2, makv, mr, wcat, pl_, qpl = ({} for _ in range(11))

    def phase_a(chunks):
        chains = [(c, p) for c in chunks for p in pairs]
        for key in chains:
            ldc = ld_s[rw[key], cs[key[1]]]
            hi = _b(ldc)
            r1 = ldc - hi.astype(F32)
            mid = _b(r1)
            lo = _b(r1 - mid.astype(F32))
            cum[key] = _dot(tril3, jnp.concatenate([hi, mid, lo], axis=0))
        for key in chains:
            rows, c_ = rw[key], cs[key[1]]
            ep = jnp.exp(cum[key])
            em = jnp.exp(-cum[key])
            at = -(kn_s[rows, c_] * jnp.exp(cum[key] - ld_s[rows, c_]))
            a2f[key] = jnp.concatenate([at, r_s[rows, c_] * ep], axis=0)
            a2[key] = _b(a2f[key])
            q4[key] = jnp.concatenate([stack2(b_s[rows, c_] * em), stack2(k_s[rows, c_] * em)], axis=0)
            vv2f[key] = stack2(v_s[rows, c_])
            vv2[key] = _b(vv2f[key])
        g = {key: jnp.where(gmask, _dot_nt(a2[key], _b(q4[key])), 0.0) for key in chains}
        for key in chains:
            makv[key] = _dot(_b(g[key][:, LANES:]), vv2[key])
            mr[key] = _b(g[key][L:, :LANES])
        w = {key: g[key][:L, :LANES] for key in chains}
        pf = {}
        for key in chains:
            pc = _b(w[key])
            pf[key] = _dot(pc, stack2(pc))
        for k in range(1, nsteps):
            for key in chains:
                pc = _b(pf[key])
                if k + 1 < nsteps:
                    both = _dot(pc, jnp.concatenate([stack2(pc), stack2(_b(w[key]))], axis=1))
                    w[key] = w[key] + pf[key] + both[:, LANES:]
                    pf[key] = both[:, :LANES]
                else:
                    w[key] = w[key] + pf[key] + _dot(pc, stack2(_b(w[key])))
        for key in chains:
            wcat[key] = _b(w[key])
            if not sample:
                pl_[key] = jnp.exp(cum[key][L - 1:L, :])
                qpl[key] = _b(q4[key] * pl_[key])

    if not sample:
        state = [sbd_s[p] for p in pairs]
    phase_a(range(nchunk))
    for c in range(nchunk):
        keys = [(c, p) for p in pairs]
        if sample:
            s_in = [[_blockdiag(s0_ref[c * nsub + q, 2 * p], s0_ref[c * nsub + q, 2 * p + 1]) for q in range(nsub)]
                    for p in pairs]
            as_a, as_r = [], []
            for p in pairs:
                asa, asr = [], []
                for q in range(nsub):
                    a2q = _b(jnp.concatenate([a2f[c, p][q * lsub:(q + 1) * lsub],
                                              a2f[c, p][L + q * lsub:L + (q + 1) * lsub]], axis=0))
                    asq = _dot_nt(a2q, _b(s_in[p][q]))
                    asa.append(asq[:lsub])
                    asr.append(asq[lsub:])
                as_a.append(jnp.concatenate(asa, axis=0))
                as_r.append(jnp.concatenate(asr, axis=0))
        else:
            as_ = [_dot_nt(a2[key], _b(state[key[1]])) for key in keys]
            as_a = [z[:L] for z in as_]
            as_r = [z[L:] for z in as_]
        rhs = [as_a[p] + makv[c, p][:L] for p in pairs]
        u = [rhs[p] + _dot(wcat[c, p], stack2(_b(rhs[p]))) for p in pairs]
        u2 = [stack2(_b(u[p])) for p in pairs]
        if sample:
            uvf = [jnp.concatenate([stack2(u[p]), vv2f[c, p]], axis=0) for p in pairs]
        else:
            uv = [jnp.concatenate([u2[p], vv2[c, p]], axis=0) for p in pairs]
        y = [as_r[p] + makv[c, p][L:] + _dot(mr[c, p], u2[p]) for p in pairs]
        for p in pairs:
            if sample:
                for q in range(nsub):
                    sel = [slice(blk * L + q * lsub, blk * L + (q + 1) * lsub) for blk in range(4)]
                    pl_q = jnp.exp(cum[c, p][(q + 1) * lsub - 1:(q + 1) * lsub, :])
                    uvq = _b(jnp.concatenate([uvf[p][s_] for s_ in sel], axis=0))
                    qq = _b(jnp.concatenate([q4[c, p][s_] for s_ in sel], axis=0) * pl_q)
                    s_new = s_in[p][q] * pl_q + _dot_tn(uvq, qq)
                    sout_ref[c * nsub + q, 2 * p] = s_new[:RW_HEAD, :RW_HEAD]
                    sout_ref[c * nsub + q, 2 * p + 1] = s_new[RW_HEAD:, RW_HEAD:]
            else:
                state[p] = state[p] * pl_[c, p] + _dot_tn(uv[p], qpl[c, p])
        rows = rw[c, 0]
        for p in pairs:
            mean = _segsum(y[p], mlo) * (1.0 / RW_HEAD)
            yc = y[p] - mean
            var = _segsum(yc * yc, mlo) * (1.0 / RW_HEAD)
            yn = yc * lax.rsqrt(var + RW_GN_EPS) * gw_ref[:, cs[p]] + gb_ref[:, cs[p]] + bon_s[rows, cs[p]]
            yg_s[rows, cs[p]] = _b(yn * g_s[rows, cs[p]])
    if not sample:
        for p in pairs:
            sbd_s[p] = state[p]
            sout_ref[0, 2 * p] = state[p][:RW_HEAD, :RW_HEAD]
            sout_ref[0, 2 * p + 1] = state[p][RW_HEAD:, RW_HEAD:]
    o_ref[...] = xres_ref[...] + _dot(yg_s[...], wo_ref[...])


def _rwkv_call(x, shx, s0bd, wts, *, nseq, seq_rows, npad, sample, tm):
    n, d = x.shape
    npair = d // LANES
    lsub = seq_rows if sample else CHUNK
    kern = functools.partial(_rwkv_kernel, tm=tm, seq_rows=seq_rows, npad=npad, sample=sample, lsub=lsub)
    if sample:
        grid = (n // tm,)
        x_spec = pl.BlockSpec((tm, d), lambda i: (i, 0))
        spt = tm // seq_rows
        st_spec = pl.BlockSpec((spt, 2 * npair, RW_HEAD, RW_HEAD), lambda i: (i, 0, 0, 0))
        xn_spec = x_spec
        xn_shape = (n, d)
        in_specs = [x_spec, x_spec]
        args = [x, shx]
        sem = ("arbitrary",)
    else:
        tps = seq_rows // tm
        grid = (nseq, tps)
        x_spec = pl.BlockSpec((tm, d), lambda b, j: (b * tps + j, 0))
        st_spec = pl.BlockSpec((1, 2 * npair, RW_HEAD, RW_HEAD), lambda b, j: (b, 0, 0, 0))
        xn_spec = pl.BlockSpec((SUBLANES, d), lambda b, j: (b, 0))
        xn_shape = (nseq * SUBLANES, d)
        in_specs = [x_spec]
        args = [x]
        sem = ("arbitrary", "arbitrary")
    for wt in wts:
        in_specs.append(_const_spec(wt.shape, single_buffer=True))
        args.append(wt)
    if sample:
        in_specs.append(st_spec)
        args.append(s0bd)
    scratch = [pltpu.VMEM((tm, d), F32) for _ in range(8)] + [pltpu.VMEM((tm, d), BF16)]
    if not sample:
        scratch += [pltpu.VMEM((SUBLANES, d), F32), pltpu.VMEM((npair, LANES, LANES), F32)]
    return pl.pallas_call(
        kern,
        grid=grid,
        in_specs=in_specs,
        out_specs=[x_spec, st_spec, xn_spec],
        out_shape=[jax.ShapeDtypeStruct((n, d), F32),
                   jax.ShapeDtypeStruct((nseq, 2 * npair, RW_HEAD, RW_HEAD), F32),
                   jax.ShapeDtypeStruct(xn_shape, F32)],
        scratch_shapes=scratch,
        compiler_params=pltpu.CompilerParams(dimension_semantics=sem, vmem_limit_bytes=VMEM_LIMIT),
        name="rwkv_sample" if sample else "rwkv_prompt",
    )(*args)


def _mlstm_kernel(*refs, tm, seq_rows, npad, sample, lsub):
    L = CHUNK
    nsub = L // lsub
    nchunk = tm // L
    it = iter(refs)
    x_ref = next(it)
    if sample:
        cvx_ref, mrow_ref = next(it), next(it)
    (gn_ref, wqk_ref, wv_ref, wo_ref, wif_ref, wift_ref, bif_ref, bift_ref, cw_ref, cb_ref, nw_ref,
     wout_ref) = [next(it) for _ in range(12)]
    if sample:
        c0_ref, n0_ref = next(it), next(it)
    o_ref, cout_ref, nout_ref, mout_ref, cvout_ref = [next(it) for _ in range(5)]
    xb_s, q_s, k_s, v_s, og_s, gc_s, ho_s = [next(it) for _ in range(7)]
    if not sample:
        carry_s, cp_s, np_s, m_s = [next(it) for _ in range(4)]
    nqk = wqk_ref.shape[1]
    half = nqk // 2
    nheads = half // ML_DK
    ngate = 2 * nheads
    npair = nheads // 2

    lane = lax.broadcasted_iota(jnp.int32, (1, LANES), 1)
    mlo = lane < ML_DK
    glane = lax.broadcasted_iota(jnp.int32, (1, ngate), 1)
    isf_c = glane >= nheads
    grow = lax.broadcasted_iota(jnp.int32, (ngate, 1), 0)
    isf_r = grow >= nheads
    hlane = lax.broadcasted_iota(jnp.int32, (1, nheads), 1)

    x = x_ref[...]
    xb = _b(_rms(x, gn_ref[...]))
    xb_s[...] = xb
    raw = _dot(xb, wqk_ref[...])
    row = lax.broadcasted_iota(jnp.int32, (tm, 1), 0)
    if sample:
        srow = row & (seq_rows - 1)
        raw = jnp.where((srow >= npad - (ML_CONV - 1)) & (srow < npad), cvx_ref[...], raw)
        cvout_ref[...] = raw
        ext = jnp.concatenate([jnp.zeros((SUBLANES, nqk), F32), raw], axis=0)
    else:
        j = pl.program_id(1)

        @pl.when(j == 0)
        def _():
            carry_s[...] = jnp.zeros(carry_s.shape, F32)
            cp_s[...] = jnp.zeros(cp_s.shape, F32)
            np_s[...] = jnp.zeros(np_s.shape, F32)
            m_s[...] = jnp.zeros(m_s.shape, F32)

        ext = jnp.concatenate([carry_s[...], raw], axis=0)
        carry_s[...] = raw[tm - SUBLANES:tm, :]
        cvout_ref[...] = raw[tm - SUBLANES:tm, :]
    qk = cb_ref[...] + cw_ref[ML_CONV - 1:ML_CONV, :] * raw
    for s in range(1, ML_CONV):
        qk = qk + cw_ref[ML_CONV - 1 - s:ML_CONV - s, :] * pltpu.roll(ext, s, 0)[SUBLANES:, :]
    qk = qk * _sigmoid(qk)
    q_s[...] = qk[:, :half] * (ML_DK ** -0.5)
    k_s[...] = qk[:, half:]
    v_s[...] = _dot(xb, wv_ref[...])
    og_s[...] = _sigmoid(_dot(xb, wo_ref[...]))
    ifp = _dot(xb, wif_ref[...]) + bif_ref[...]
    gcol = jnp.where(isf_c, -_softplus(-ifp), ifp)
    if npad:
        keep = (row & (seq_rows - 1)) >= npad
        gcol = jnp.where(keep, gcol, jnp.where(isf_c, 0.0, NEG_BIG))
    gc_s[...] = gcol

    ti = lax.broadcasted_iota(jnp.int32, (L, L), 0)
    si = lax.broadcasted_iota(jnp.int32, (L, L), 1)
    sh = _log2(lsub)
    same = (ti >> sh) == (si >> sh)
    causal = same & (si <= ti)
    tril = jnp.where(causal, 1.0, 0.0).astype(F32)
    triu = jnp.where(same & (ti <= si), 1.0, 0.0).astype(F32)
    lrow = lax.broadcasted_iota(jnp.int32, (1, L), 1)

    def chunk(c, carry):
        r0 = pl.multiple_of(c * L, L)
        rows = pl.ds(r0, L)
        gc = gc_s[rows, :]
        gt = _dot_nt(wift_ref[...], xb_s[rows, :]) + bift_ref[...]
        gt = jnp.where(isf_r, -_softplus(-gt), gt)
        if npad:
            gt = jnp.where((lrow & (seq_rows - 1)) >= npad, gt, jnp.where(isf_r, 0.0, NEG_BIG))
        bcs = _dot(tril, jnp.where(isf_c, gc, 0.0), HI)
        brs = _dot(jnp.where(isf_r, gt, 0.0), triu, HI)
        blast = _seq_last(bcs, lsub)
        if sample:
            mcols = mrow_ref[rows, :]
        else:
            mcols = m_s[...]
        heads = range(nheads)
        prs = range(npair)
        ps = [slice(pp * LANES, (pp + 1) * LANES) for pp in prs]
        hs = [slice(h * ML_DV, (h + 1) * ML_DV) for h in heads]
        q2 = [q_s[rows, ps[pp]] for pp in prs]
        k2 = [k_s[rows, ps[pp]] for pp in prs]
        k2b = [_b(z) for z in k2]
        if sample:
            c_in = [[jnp.concatenate([c0_ref[c * nsub + q, 2 * pp], c0_ref[c * nsub + q, 2 * pp + 1]], axis=0)
                     for q in range(nsub)] for pp in prs]
            n_in = [[n0_ref[c * nsub + q][:, ps[pp]] for q in range(nsub)] for pp in prs]
            n_rows = [jnp.concatenate([jnp.broadcast_to(n_in[pp][q], (lsub, LANES)) for q in range(nsub)], axis=0)
                      for pp in prs]
        else:
            c_prev = [cp_s[pp] for pp in prs]
            n_prev = [np_s[:, ps[pp]] for pp in prs]
            n_rows = n_prev
        bcol = [_colsel(bcs, glane, nheads + h) for h in heads]
        licol = [_colsel(gc, glane, h) for h in heads]
        mcol = [_colsel(mcols, hlane, h) for h in heads]
        blcol = [_colsel(blast, glane, nheads + h) for h in heads]
        dlog = [jnp.where(causal, bcol[h] - (brs[nheads + h:nheads + h + 1, :] - gt[h:h + 1, :]), -jnp.inf)
                for h in heads]
        ginter = [bcol[h] + mcol[h] for h in heads]
        m_t = [jnp.maximum(ginter[h], jnp.max(dlog[h], axis=-1, keepdims=True)) for h in heads]
        dw = [jnp.exp(dlog[h] - m_t[h]) for h in heads]
        winter = [jnp.exp(ginter[h] - m_t[h]) for h in heads]
        qh = [jnp.where(mlo if h % 2 == 0 else jnp.logical_not(mlo), q2[h // 2], 0.0) for h in heads]
        qhb = [_b(z) for z in qh]
        sc = [_dot_nt(qhb[h], k2b[h // 2]) * dw[h] for h in heads]
        if sample:
            inter = [jnp.concatenate([_dot(qhb[h][q * lsub:(q + 1) * lsub], _b(c_in[h // 2][q]))
                                      for q in range(nsub)], axis=0) for h in heads]
        else:
            cb = [_b(z) for z in c_prev]
            inter = [_dot(qhb[h], cb[h // 2]) for h in heads]
        vh = [v_s[rows, hs[h]] for h in heads]
        num = [winter[h] * inter[h] + _dot(_b(sc[h]), _b(vh[h])) for h in heads]
        den = [winter[h] * jnp.sum(qh[h] * n_rows[h // 2], axis=-1, keepdims=True)
               + jnp.sum(sc[h], axis=-1, keepdims=True) for h in heads]
        for h in heads:
            hout = num[h] / jnp.maximum(jnp.abs(den[h]), jnp.exp(-m_t[h]))
            hn = hout * lax.rsqrt(jnp.mean(hout * hout, axis=-1, keepdims=True) + NORM_EPS) * nw_ref[:, hs[h]]
            ho_s[rows, hs[h]] = _b(hn * og_s[rows, hs[h]])
        mnew = [_seq_last(m_t[h], lsub) for h in heads]
        ws = [jnp.exp(blcol[h] - bcol[h] + licol[h] - mnew[h]) for h in heads]
        wstc = [jnp.exp(blcol[h] + mcol[h] - mnew[h]) for h in heads]
        mt_all = jnp.zeros((L, nheads), F32)
        for h in heads:
            mt_all = jnp.where(hlane == h, m_t[h], mt_all)
        for pp in prs:
            lo, hi = 2 * pp, 2 * pp + 1
            wsk = jnp.where(mlo, ws[lo], ws[hi]) * k2[pp]
            wst = jnp.where(mlo, wstc[lo], wstc[hi])
            wvb = _b(jnp.concatenate([ws[lo] * vh[lo], ws[hi] * vh[hi]], axis=0))
            kmb = jnp.concatenate([jnp.where(mlo, k2b[pp], 0), jnp.where(mlo, 0, k2b[pp])], axis=0)
            if sample:
                for q in range(nsub):
                    last = (q + 1) * lsub - 1
                    sel = [slice(blk * L + q * lsub, blk * L + (q + 1) * lsub) for blk in range(2)]
                    wq = wst[last:last + 1, :]
                    wrow = jnp.concatenate([jnp.broadcast_to(wstc[h_][last:last + 1, :], (ML_DK, 1)) for h_ in (lo, hi)],
                                           axis=0)
                    upd = _dot_tn(jnp.concatenate([kmb[s_] for s_ in sel], axis=0),
                                  jnp.concatenate([wvb[s_] for s_ in sel], axis=0))
                    c_new = c_in[pp][q] * wrow + upd
                    cout_ref[c * nsub + q, 2 * pp] = c_new[:ML_DK]
                    cout_ref[c * nsub + q, 2 * pp + 1] = c_new[ML_DK:]
                    nq = n_in[pp][q] * wq + jnp.sum(wsk[q * lsub:(q + 1) * lsub], axis=0, keepdims=True)
                    nout_ref[c * nsub + q, :, ps[pp]] = nq
            else:
                wrow = jnp.concatenate([jnp.broadcast_to(wstc[h_], (ML_DK, 1)) for h_ in (lo, hi)], axis=0)
                cp_s[pp] = c_prev[pp] * wrow + _dot_tn(kmb, wvb)
                np_s[:, ps[pp]] = n_prev[pp] * wst + jnp.sum(wsk, axis=0, keepdims=True)
        if sample:
            mout_ref[rows, :] = mt_all
        else:
            m_s[...] = mt_all[L - 1:L, :]
        return carry

    lax.fori_loop(0, nchunk, chunk, 0)
    if not sample:
        for pp in range(npair):
            cout_ref[0, 2 * pp] = cp_s[pp][:ML_DK]
            cout_ref[0, 2 * pp + 1] = cp_s[pp][ML_DK:]
        nout_ref[0] = np_s[...]
        mout_ref[0] = m_s[...]
    o_ref[...] = x_ref[...] + _dot(ho_s[...], wout_ref[...])


def _mlstm_call(x, cvx, mrow, c0p, n0, wts, *, nseq, seq_rows, npad, sample, tm):
    n, d = x.shape
    nqk = wts[1].shape[1]
    nv = wts[2].shape[1]
    nheads = nqk // 2 // ML_DK
    npair = nheads // 2
    lsub = seq_rows if sample else CHUNK
    kern = functools.partial(_mlstm_kernel, tm=tm, seq_rows=seq_rows, npad=npad, sample=sample, lsub=lsub)
    if sample:
        grid = (n // tm,)
        tile = lambda i: (i, 0)
        spt = tm // seq_rows
        c_spec = pl.BlockSpec((spt, nheads, ML_DK, ML_DV), lambda i: (i, 0, 0, 0))
        n_spec = pl.BlockSpec((spt, 1, nqk // 2), lambda i: (i, 0, 0))
        m_spec = pl.BlockSpec((tm, nheads), tile)
        m_shape = (n, nheads)
        cv_spec = pl.BlockSpec((tm, nqk), tile)
        cv_shape = (n, nqk)
        sem = ("arbitrary",)
    else:
        tps = seq_rows // tm
        grid = (nseq, tps)
        tile = lambda b, j: (b * tps + j, 0)
        c_spec = pl.BlockSpec((1, nheads, ML_DK, ML_DV), lambda b, j: (b, 0, 0, 0))
        n_spec = pl.BlockSpec((1, 1, nqk // 2), lambda b, j: (b, 0, 0))
        m_spec = pl.BlockSpec((1, 1, nheads), lambda b, j: (b, 0, 0))
        m_shape = (nseq, 1, nheads)
        cv_spec = pl.BlockSpec((SUBLANES, nqk), lambda b, j: (b, 0))
        cv_shape = (nseq * SUBLANES, nqk)
        sem = ("arbitrary", "arbitrary")
    x_spec = pl.BlockSpec((tm, d), tile)
    in_specs = [x_spec]
    args = [x]
    if sample:
        in_specs += [pl.BlockSpec((tm, nqk), tile), pl.BlockSpec((tm, nheads), tile)]
        args += [cvx, mrow]
    for wt in wts:
        in_specs.append(_const_spec(wt.shape))
        args.append(wt)
    if sample:
        in_specs += [c_spec, n_spec]
        args += [c0p, n0]
    scratch = [pltpu.VMEM((tm, d), BF16), pltpu.VMEM((tm, nqk // 2), F32), pltpu.VMEM((tm, nqk // 2), F32),
               pltpu.VMEM((tm, nv), F32), pltpu.VMEM((tm, nv), F32), pltpu.VMEM((tm, 2 * nheads), F32),
               pltpu.VMEM((tm, nv), BF16)]
    if not sample:
        scratch += [pltpu.VMEM((SUBLANES, nqk), F32), pltpu.VMEM((npair, LANES, LANES), F32),
                    pltpu.VMEM((1, nqk // 2), F32), pltpu.VMEM((1, nheads), F32)]
    return pl.pallas_call(
        kern,
        grid=grid,
        in_specs=in_specs,
        out_specs=[x_spec, c_spec, n_spec, m_spec, cv_spec],
        out_shape=[jax.ShapeDtypeStruct((n, d), F32),
                   jax.ShapeDtypeStruct((nseq, nheads, ML_DK, ML_DV), F32),
                   jax.ShapeDtypeStruct((nseq, 1, nqk // 2), F32),
                   jax.ShapeDtypeStruct(m_shape, F32),
                   jax.ShapeDtypeStruct(cv_shape, F32)],
        scratch_shapes=scratch,
        compiler_params=pltpu.CompilerParams(dimension_semantics=sem, vmem_limit_bytes=VMEM_LIMIT),
        name="mlstm_sample" if sample else "mlstm_prompt",
    )(*args)


def kernel(x_prompt, x_sample, state_rwkv_S, state_rwkv_shift, state_mlstm_C, state_mlstm_n, state_mlstm_m,
           state_mlstm_conv, norm_ffa, ffa_wg, ffa_wu, ffa_wd, norm_mix, norm_ffb, ffb_wg, ffb_wu, ffb_wd,
           rw_mu, rw_wr, rw_wk, rw_wv, rw_wo, rw_w0, rw_w1, rw_w2, rw_a0, rw_a1, rw_a2, rw_g1, rw_g2,
           rw_k_k, rw_k_a, rw_r_k, rw_gn_w, rw_gn_b, ml_w_in, ml_b_if, ml_conv_w, ml_conv_b, ml_norm_w,
           ml_w_out, norm_final):
    bp, tp, d = x_prompt.shape
    bs, ts, _ = x_sample.shape
    depth = norm_ffa.shape[0]
    slot = SUBLANES
    npad = slot - ts
    assert 0 < ts <= slot and npad >= ML_CONV - 1
    tm_p = min(256, tp)
    assert tp % tm_p == 0 and tm_p % CHUNK == 0 and (bs * slot) % CHUNK == 0
    ml_heads = ml_b_if.shape[1] // 2
    nqk = 2 * ml_heads * ML_DK
    nv = ml_heads * ML_DV

    xp = x_prompt.reshape(bp * tp, d)
    xs = jnp.concatenate([jnp.zeros((bs, npad, d), F32), x_sample], axis=1).reshape(bs * slot, d)
    row2 = lambda a: a.reshape(1, -1)

    tm_ffn = min(512, bs * slot)
    assert (bp * tp) % tm_ffn == 0 and (bs * slot) % tm_ffn == 0
    ffa = (norm_ffa.reshape(depth, 1, d), _b(ffa_wg), _b(ffa_wu), _b(ffa_wd))
    ffb = (norm_ffb.reshape(depth, 1, d), _b(ffb_wg), _b(ffb_wu), _b(ffb_wd))

    def ffn(xp_, xs_, wset, layer, final):
        return _ffn_call(xp_, xs_, *wset, row2(norm_final), layer, final_norm=final, tm=tm_ffn)

    new_p = {k_: [] for k_ in ("S", "shift", "C", "n", "m", "conv")}
    new_s = {k_: [] for k_ in ("S", "shift", "C", "n", "m", "conv")}
    for i in range(depth):
        xp, xs = ffn(xp, xs, ffa, i, False)
        j = i // 2
        if i % 2 == 0:
            wts = [row2(norm_mix[i]), rw_mu[j], _b(rw_wr[j]), _b(rw_wk[j]), _b(rw_wv[j]), _b(rw_wo[j]),
                   row2(rw_w0[j]), _b(rw_w1[j]), _b(rw_w2[j]), row2(rw_a0[j]), _b(rw_a1[j]), _b(rw_a2[j]),
                   _b(rw_g1[j]), _b(rw_g2[j]), row2(rw_k_k[j]), row2(rw_k_a[j]), row2(rw_r_k[j]),
                   row2(rw_gn_w[j]), row2(rw_gn_b[j])]
            xp, sbd, tail = _rwkv_call(xp, None, None, wts, nseq=bp, seq_rows=tp, npad=0, sample=False, tm=tm_p)
            new_p["S"].append(sbd)
            new_p["shift"].append(tail.reshape(bp, SUBLANES, d)[:, SUBLANES - 1])
            shx = jnp.repeat(state_rwkv_shift[j], slot, axis=0)
            xs, sbd, xn = _rwkv_call(xs, shx, state_rwkv_S[j], wts, nseq=bs, seq_rows=slot,
                                     npad=npad, sample=True, tm=CHUNK)
            new_s["S"].append(sbd)
            new_s["shift"].append(xn.reshape(bs, slot, d)[:, slot - 1])
        else:
            w_in = ml_w_in[j]
            w_if = w_in[:, nqk + nv + d:]
            wts = [row2(norm_mix[i]), _b(w_in[:, :nqk]), _b(w_in[:, nqk:nqk + nv]), _b(w_in[:, nqk + nv:nqk + nv + d]),
                   _b(w_if), _b(w_if.T), row2(ml_b_if[j]), ml_b_if[j].reshape(-1, 1), ml_conv_w[j],
                   row2(ml_conv_b[j]), row2(ml_norm_w[j]), _b(ml_w_out[j])]
            xp, cp, n_, m_, tail = _mlstm_call(xp, None, None, None, None, wts, nseq=bp, seq_rows=tp, npad=0,
                                               sample=False, tm=tm_p)
            new_p["C"].append(jnp.swapaxes(cp, -1, -2))
            new_p["n"].append(n_.reshape(bp, ml_heads, ML_DK))
            new_p["m"].append(m_.reshape(bp, ml_heads))
            new_p["conv"].append(tail.reshape(bp, SUBLANES, nqk)[:, SUBLANES - (ML_CONV - 1):])
            conv0 = state_mlstm_conv[j]
            cvx = jnp.concatenate([jnp.zeros((bs, npad - (ML_CONV - 1), nqk), F32), conv0,
                                   jnp.zeros((bs, slot - npad, nqk), F32)], axis=1).reshape(bs * slot, nqk)
            mrow = jnp.repeat(state_mlstm_m[j], slot, axis=0)
            xs, cp, n_, mt, raw = _mlstm_call(xs, cvx, mrow, jnp.swapaxes(state_mlstm_C[j], -1, -2),
                                              state_mlstm_n[j].reshape(bs, 1, ml_heads * ML_DK), wts, nseq=bs,
                                              seq_rows=slot, npad=npad, sample=True, tm=CHUNK)
            new_s["C"].append(jnp.swapaxes(cp, -1, -2))
            new_s["n"].append(n_.reshape(bs, ml_heads, ML_DK))
            new_s["m"].append(mt.reshape(bs, slot, ml_heads)[:, slot - 1])
            new_s["conv"].append(raw.reshape(bs, slot, nqk)[:, slot - (ML_CONV - 1):])
        xp, xs = ffn(xp, xs, ffb, i, i == depth - 1)
    y_prompt = xp.reshape(bp, tp, d)
    y_sample = xs.reshape(bs, slot, d)[:, npad:]
    st = lambda lst: jnp.stack(lst)
    return (y_prompt, y_sample,
            st(new_p["S"]), st(new_p["shift"]), st(new_p["C"]), st(new_p["n"]), st(new_p["m"]), st(new_p["conv"]),
            st(new_s["S"]), st(new_s["shift"]), st(new_s["C"]), st(new_s["n"]), st(new_s["m"]), st(new_s["conv"]))
```

```python
import functools
import math

import jax
import jax.numpy as jnp
from jax import lax
from jax.experimental import pallas as pl
from jax.experimental.pallas import tpu as pltpu

F32 = jnp.float32
BF16 = jnp.bfloat16

NORM_EPS = 1e-6
RW_GN_EPS = 64e-5
RW_HEAD = 64
ML_DK = 64
ML_DV = 128
ML_CONV = 4

LANES = 128
SUBLANES = 8
MXU_COLS = 256
CHUNK = 64
NEG_BIG = -1e30
VMEM_LIMIT = 58 * 1024 * 1024
HI = lax.Precision.HIGHEST


def _dot(a, b, precision=None):
    return jnp.dot(a, b, preferred_element_type=F32, precision=precision)


def _dot_nt(a, b, precision=None):
    return lax.dot_general(a, b, (((1,), (1,)), ((), ())), preferred_element_type=F32, precision=precision)


def _dot_tn(a, b, precision=None):
    return lax.dot_general(a, b, (((0,), (0,)), ((), ())), preferred_element_type=F32, precision=precision)


def _b(x):
    return x.astype(BF16)


def _rms(x, g):
    return x * lax.rsqrt(jnp.mean(x * x, axis=-1, keepdims=True) + NORM_EPS) * g


def _sigmoid(x):
    return 1.0 / (1.0 + jnp.exp(-x))


def _softplus(x):
    return jnp.maximum(x, 0.0) + jnp.log1p(jnp.exp(-jnp.abs(x)))


def _segsum(x, mlo):
    lo = jnp.sum(jnp.where(mlo, x, 0.0), axis=-1, keepdims=True)
    hi = jnp.sum(jnp.where(mlo, 0.0, x), axis=-1, keepdims=True)
    return jnp.where(mlo, lo, hi)


def _colsel(x, lane_idx, j):
    return jnp.sum(jnp.where(lane_idx == j, x, 0.0), axis=-1, keepdims=True)


def _seq_last(x, lsub):
    n = x.shape[0]
    if lsub == n:
        return x[n - 1:n]
    parts = [jnp.broadcast_to(x[q * lsub + lsub - 1:q * lsub + lsub], (lsub,) + x.shape[1:])
             for q in range(n // lsub)]
    return jnp.concatenate(parts, axis=0)


def _blockdiag(a, b):
    z = jnp.zeros(a.shape, a.dtype)
    return jnp.concatenate([jnp.concatenate([a, z], axis=1), jnp.concatenate([z, b], axis=1)], axis=0)


def _const_spec(shape, single_buffer=False):
    nd = len(shape)
    if single_buffer:
        return pl.BlockSpec(shape, lambda *_: (0,) * nd, pipeline_mode=pl.Buffered(1))
    return pl.BlockSpec(shape, lambda *_: (0,) * nd)


def _log2(n):
    k = int(math.log2(n))
    assert 1 << k == n, n
    return k


def _ffn_kernel(xp_ref, xs_ref, g_ref, wg_ref, wu_ref, wd_ref, gf_ref, op_ref, os_ref, *, fchunk, final_norm, np_steps):
    is_p = pl.program_id(0) < np_steps
    x = jnp.where(is_p, xp_ref[...], xs_ref[...])
    xb = _b(_rms(x, g_ref[...]))
    acc = jnp.zeros(x.shape, F32)
    for lo, hi in zip(fchunk[:-1], fchunk[1:]):
        sl = slice(lo, hi)
        gate = _dot(xb, wg_ref[:, sl])
        up = _dot(xb, wu_ref[:, sl])
        h = _b(gate * _sigmoid(gate) * up)
        acc = acc + _dot(h, wd_ref[sl, :])
    out = x + 0.5 * acc
    if final_norm:
        out = _rms(out, gf_ref[...])

    @pl.when(is_p)
    def _():
        op_ref[...] = out

    @pl.when(jnp.logical_not(is_p))
    def _():
        os_ref[...] = out


def _ffn_call(xp, xs, g_all, wg_all, wu_all, wd_all, gf, layer, *, final_norm, tm):
    (n_p, d), n_s = xp.shape, xs.shape[0]
    nf = wg_all.shape[2]
    np_steps, ns_steps = n_p // tm, n_s // tm
    ntile = nf // MXU_COLS
    fchunk = (0, -(-ntile // 2) * MXU_COLS, nf) if nf % MXU_COLS == 0 and ntile > 1 else (0, nf)
    kern = functools.partial(_ffn_kernel, fchunk=fchunk, final_norm=final_norm, np_steps=np_steps)
    p_spec = pl.BlockSpec((tm, d), lambda i: (jnp.minimum(i, np_steps - 1), 0))
    s_spec = pl.BlockSpec((tm, d), lambda i: (jnp.maximum(i - np_steps, 0), 0))
    w_spec = lambda a, b: pl.BlockSpec((None, a, b), lambda i: (layer, 0, 0), pipeline_mode=pl.Buffered(1))
    return pl.pallas_call(
        kern,
        grid=(np_steps + ns_steps,),
        in_specs=[p_spec, s_spec, pl.BlockSpec((None, 1, d), lambda i: (layer, 0, 0)),
                  w_spec(d, nf), w_spec(d, nf), w_spec(nf, d), _const_spec((1, d))],
        out_specs=[p_spec, s_spec],
        out_shape=[jax.ShapeDtypeStruct((n_p, d), F32), jax.ShapeDtypeStruct((n_s, d), F32)],
        compiler_params=pltpu.CompilerParams(dimension_semantics=("arbitrary",), vmem_limit_bytes=VMEM_LIMIT),
        name="ffn",
    )(xp, xs, g_all, wg_all, wu_all, wd_all, gf)


def _rw_token_part(xn, xx, row, mlo, sset, seq_rows, npad, wrefs):
    r_s, ld_s, k_s, v_s, kn_s, b_s, g_s, bon_s = sset
    mu_ref, wr_ref, wk_ref, wv_ref, w0_ref, w1_ref, w2_ref, a0_ref, a1_ref, a2_ref, g1_ref, g2_ref, kk_ref, ka_ref, rk_ref = wrefs

    def mix(i):
        return _b(xn + xx * mu_ref[i:i + 1, :])

    r = _dot(mix(0), wr_ref[...])
    r_s[...] = r
    wl = _b(jnp.tanh(_dot(mix(1), w1_ref[...])))
    ld = -math.exp(-0.5) * _sigmoid(w0_ref[...] + _dot(wl, w2_ref[...]))
    k = _dot(mix(2), wk_ref[...])
    v = _dot(mix(3), wv_ref[...])
    al = _b(_dot(mix(4), a1_ref[...]))
    a = _sigmoid(a0_ref[...] + _dot(al, a2_ref[...]))
    gl = _b(_sigmoid(_dot(mix(5), g1_ref[...])))
    g_s[...] = _dot(gl, g2_ref[...])
    k2 = k * (1.0 + (a - 1.0) * ka_ref[...])
    kk = k * kk_ref[...]
    rkk = r * k2 * rk_ref[...]
    if npad:
        keep = (row & (seq_rows - 1)) >= npad
        ld = jnp.where(keep, ld, 0.0)
        k2 = jnp.where(keep, k2, 0.0)
        kk = jnp.where(keep, kk, 0.0)
        v = jnp.where(keep, v, 0.0)
    ld_s[...] = ld
    k_s[...] = k2
    v_s[...] = v
    for p in range(xn.shape[1] // LANES):
        cs = slice(p * LANES, (p + 1) * LANES)
        kkp = kk[:, cs]
        kn = kkp / jnp.maximum(jnp.sqrt(_segsum(kkp * kkp, mlo)), 1e-12)
        kn_s[:, cs] = kn
        b_s[:, cs] = kn * a[:, cs]
        bon_s[:, cs] = _segsum(rkk[:, cs], mlo) * v[:, cs]


def _rwkv_kernel(*refs, tm, seq_rows, npad, sample, lsub):
    L = CHUNK
    nsub = L // lsub
    nchunk = tm // L
    nsteps = _log2(lsub)
    it = iter(refs)
    x_ref = next(it)
    shx_ref = next(it) if sample else None
    (gn_ref, mu_ref, wr_ref, wk_ref, wv_ref, wo_ref, w0_ref, w1_ref, w2_ref, a0_ref, a1_ref, a2_ref,
     g1_ref, g2_ref, kk_ref, ka_ref, rk_ref, gw_ref, gb_ref) = [next(it) for _ in range(19)]
    s0_ref = next(it) if sample else None
    o_ref, sout_ref, xn_ref = next(it), next(it), next(it)
    set_a = [next(it) for _ in range(8)]
    yg_s = next(it)
    if not sample:
        carry_s, sbd_s = next(it), next(it)
    d = x_ref.shape[1]
    npair = d // LANES

    lane = lax.broadcasted_iota(jnp.int32, (1, LANES), 1)
    mlo = lane < RW_HEAD
    row = lax.broadcasted_iota(jnp.int32, (tm, 1), 0)

    xn = _rms(x_ref[...], gn_ref[...])
    rolled = pltpu.roll(xn, 1, 0)
    if sample:
        sbd_s = None
        xprev = jnp.where((row & (seq_rows - 1)) == npad, shx_ref[...], rolled)
        xn_ref[...] = xn
    else:
        @pl.when(pl.program_id(1) == 0)
        def _():
            carry_s[...] = jnp.zeros(carry_s.shape, F32)
            sbd_s[...] = jnp.zeros(sbd_s.shape, F32)

        xprev = jnp.where(row == 0, carry_s[SUBLANES - 1:SUBLANES, :], rolled)
        carry_s[...] = xn[tm - SUBLANES:tm, :]
        xn_ref[...] = xn[tm - SUBLANES:tm, :]
    _rw_token_part(xn, xprev - xn, row, mlo, set_a, seq_rows, npad,
                   (mu_ref, wr_ref, wk_ref, wv_ref, w0_ref, w1_ref, w2_ref, a0_ref, a1_ref, a2_ref,
                    g1_ref, g2_ref, kk_ref, ka_ref, rk_ref))
    _rw_back(set_a, sbd_s, sample=sample, lsub=lsub, nchunk=nchunk, nsub=nsub, nsteps=nsteps, npair=npair, mlo=mlo,
             s0_ref=s0_ref, sout_ref=sout_ref, yg_s=yg_s, gw_ref=gw_ref, gb_ref=gb_ref, wo_ref=wo_ref,
             xres_ref=x_ref, o_ref=o_ref)


def _rw_back(sset, sbd_s, *, sample, lsub, nchunk, nsub, nsteps, npair, mlo, s0_ref, sout_ref, yg_s,
             gw_ref, gb_ref, wo_ref, xres_ref, o_ref):
    r_s, ld_s, k_s, v_s, kn_s, b_s, g_s, bon_s = sset
    L = CHUNK
    sh = _log2(lsub)
    ti = lax.broadcasted_iota(jnp.int32, (L, 3 * L), 0)
    si = lax.broadcasted_iota(jnp.int32, (L, 3 * L), 1) & (L - 1)
    tril3 = jnp.where(((ti >> sh) == (si >> sh)) & (si <= ti), 1.0, 0.0).astype(BF16)
    gi = lax.broadcasted_iota(jnp.int32, (2 * L, 4 * L), 0)
    gj = lax.broadcasted_iota(jnp.int32, (2 * L, 4 * L), 1)
    gt = gi & (L - 1)
    gs = gj & (L - 1)
    gmask = ((gt >> sh) == (gs >> sh)) & ((gs < gt) | ((gi >= L) & (gs == gt)))
    pairs = range(npair)

    def stack2(z):
        return jnp.concatenate([jnp.where(mlo, z, 0), jnp.where(mlo, 0, z)], axis=0)

    cs = [slice(p * LANES, (p + 1) * LANES) for p in pairs]
    rw = {(c, p): slice(c * L, (c + 1) * L) for c in range(nchunk) for p in pairs}
    cum, a2f, a2, q4, vv2f, vv2, makv, mr, wcat, pl_, qpl = ({} for _ in range(11))

    def phase_a(chunks):
        chains = [(c, p) for c in chunks for p in pairs]
        for key in chains:
            ldc = ld_s[rw[key], cs[key[1]]]
            hi = _b(ldc)
            r1 = ldc - hi.astype(F32)
            mid = _b(r1)
            lo = _b(r1 - mid.astype(F32))
            cum[key] = _dot(tril3, jnp.concatenate([hi, mid, lo], axis=0))
        for key in chains:
            rows, c_ = rw[key], cs[key[1]]
            ep = jnp.exp(cum[key])
            em = jnp.exp(-cum[key])
            at = -(kn_s[rows, c_] * jnp.exp(cum[key] - ld_s[rows, c_]))
            a2f[key] = jnp.concatenate([at, r_s[rows, c_] * ep], axis=0)
            a2[key] = _b(a2f[key])
            q4[key] = jnp.concatenate([stack2(b_s[rows, c_] * em), stack2(k_s[rows, c_] * em)], axis=0)
            vv2f[key] = stack2(v_s[rows, c_])
            vv2[key] = _b(vv2f[key])
        g = {key: jnp.where(gmask, _dot_nt(a2[key], _b(q4[key])), 0.0) for key in chains}
        for key in chains:
            makv[key] = _dot(_b(g[key][:, LANES:]), vv2[key])
            mr[key] = _b(g[key][L:, :LANES])
        w = {key: g[key][:L, :LANES] for key in chains}
        pf = {}
        for key in chains:
            pc = _b(w[key])
            pf[key] = _dot(pc, stack2(pc))
        for k in range(1, nsteps):
            for key in chains:
                pc = _b(pf[key])
                if k + 1 < nsteps:
                    both = _dot(pc, jnp.concatenate([stack2(pc), stack2(_b(w[key]))], axis=1))
                    w[key] = w[key] + pf[key] + both[:, LANES:]
                    pf[key] = both[:, :LANES]
                else:
                    w[key] = w[key] + pf[key] + _dot(pc, stack2(_b(w[key])))
        for key in chains:
            wcat[key] = _b(w[key])
            if not sample:
                pl_[key] = jnp.exp(cum[key][L - 1:L, :])
                qpl[key] = _b(q4[key] * pl_[key])

    if not sample:
        state = [sbd_s[p] for p in pairs]
    phase_a(range(nchunk))
    for c in range(nchunk):
        keys = [(c, p) for p in pairs]
        if sample:
            s_in = [[_blockdiag(s0_ref[c * nsub + q, 2 * p], s0_ref[c * nsub + q, 2 * p + 1]) for q in range(nsub)]
                    for p in pairs]
            as_a, as_r = [], []
            for p in pairs:
                asa, asr = [], []
                for q in range(nsub):
                    a2q = _b(jnp.concatenate([a2f[c, p][q * lsub:(q + 1) * lsub],
                                              a2f[c, p][L + q * lsub:L + (q + 1) * lsub]], axis=0))
                    asq = _dot_nt(a2q, _b(s_in[p][q]))
                    asa.append(asq[:lsub])
                    asr.append(asq[lsub:])
                as_a.append(jnp.concatenate(asa, axis=0))
                as_r.append(jnp.concatenate(asr, axis=0))
        else:
            as_ = [_dot_nt(a2[key], _b(state[key[1]])) for key in keys]
            as_a = [z[:L] for z in as_]
            as_r = [z[L:] for z in as_]
        rhs = [as_a[p] + makv[c, p][:L] for p in pairs]
        u = [rhs[p] + _dot(wcat[c, p], stack2(_b(rhs[p]))) for p in pairs]
        u2 = [stack2(_b(u[p])) for p in pairs]
        if sample:
            uvf = [jnp.concatenate([stack2(u[p]), vv2f[c, p]], axis=0) for p in pairs]
        else:
            uv = [jnp.concatenate([u2[p], vv2[c, p]], axis=0) for p in pairs]
        y = [as_r[p] + makv[c, p][L:] + _dot(mr[c, p], u2[p]) for p in pairs]
        for p in pairs:
            if sample:
                for q in range(nsub):
                    sel = [slice(blk * L + q * lsub, blk * L + (q + 1) * lsub) for blk in range(4)]
                    pl_q = jnp.exp(cum[c, p][(q + 1) * lsub - 1:(q + 1) * lsub, :])
                    uvq = _b(jnp.concatenate([uvf[p][s_] for s_ in sel], axis=0))
                    qq = _b(jnp.concatenate([q4[c, p][s_] for s_ in sel], axis=0) * pl_q)
                    s_new = s_in[p][q] * pl_q + _dot_tn(uvq, qq)
                    sout_ref[c * nsub + q, 2 * p] = s_new[:RW_HEAD, :RW_HEAD]
                    sout_ref[c * nsub + q, 2 * p + 1] = s_new[RW_HEAD:, RW_HEAD:]
            else:
                state[p] = state[p] * pl_[c, p] + _dot_tn(uv[p], qpl[c, p])
        rows = rw[c, 0]
        for p in pairs:
            mean = _segsum(y[p], mlo) * (1.0 / RW_HEAD)
            yc = y[p] - mean
            var = _segsum(yc * yc, mlo) * (1.0 / RW_HEAD)
            yn = yc * lax.rsqrt(var + RW_GN_EPS) * gw_ref[:, cs[p]] + gb_ref[:, cs[p]] + bon_s[rows, cs[p]]
            yg_s[rows, cs[p]] = _b(yn * g_s[rows, cs[p]])
    if not sample:
        for p in pairs:
            sbd_s[p] = state[p]
            sout_ref[0, 2 * p] = state[p][:RW_HEAD, :RW_HEAD]
            sout_ref[0, 2 * p + 1] = state[p][RW_HEAD:, RW_HEAD:]
    o_ref[...] = xres_ref[...] + _dot(yg_s[...], wo_ref[...])


def _rwkv_call(x, shx, s0bd, wts, *, nseq, seq_rows, npad, sample, tm):
    n, d = x.shape
    npair = d // LANES
    lsub = seq_rows if sample else CHUNK
    kern = functools.partial(_rwkv_kernel, tm=tm, seq_rows=seq_rows, npad=npad, sample=sample, lsub=lsub)
    if sample:
        grid = (n // tm,)
        x_spec = pl.BlockSpec((tm, d), lambda i: (i, 0))
        spt = tm // seq_rows
        st_spec = pl.BlockSpec((spt, 2 * npair, RW_HEAD, RW_HEAD), lambda i: (i, 0, 0, 0))
        xn_spec = x_spec
        xn_shape = (n, d)
        in_specs = [x_spec, x_spec]
        args = [x, shx]
        sem = ("arbitrary",)
    else:
        tps = seq_rows // tm
        grid = (nseq, tps)
        x_spec = pl.BlockSpec((tm, d), lambda b, j: (b * tps + j, 0))
        st_spec = pl.BlockSpec((1, 2 * npair, RW_HEAD, RW_HEAD), lambda b, j: (b, 0, 0, 0))
        xn_spec = pl.BlockSpec((SUBLANES, d), lambda b, j: (b, 0))
        xn_shape = (nseq * SUBLANES, d)
        in_specs = [x_spec]
        args = [x]
        sem = ("arbitrary", "arbitrary")
    for wt in wts:
        in_specs.append(_const_spec(wt.shape, single_buffer=True))
        args.append(wt)
    if sample:
        in_specs.append(st_spec)
        args.append(s0bd)
    scratch = [pltpu.VMEM((tm, d), F32) for _ in range(8)] + [pltpu.VMEM((tm, d), BF16)]
    if not sample:
        scratch += [pltpu.VMEM((SUBLANES, d), F32), pltpu.VMEM((npair, LANES, LANES), F32)]
    return pl.pallas_call(
        kern,
        grid=grid,
        in_specs=in_specs,
        out_specs=[x_spec, st_spec, xn_spec],
        out_shape=[jax.ShapeDtypeStruct((n, d), F32),
                   jax.ShapeDtypeStruct((nseq, 2 * npair, RW_HEAD, RW_HEAD), F32),
                   jax.ShapeDtypeStruct(xn_shape, F32)],
        scratch_shapes=scratch,
        compiler_params=pltpu.CompilerParams(dimension_semantics=sem, vmem_limit_bytes=VMEM_LIMIT),
        name="rwkv_sample" if sample else "rwkv_prompt",
    )(*args)


def _mlstm_kernel(*refs, tm, seq_rows, npad, sample, lsub):
    L = CHUNK
    nsub = L // lsub
    nchunk = tm // L
    it = iter(refs)
    x_ref = next(it)
    if sample:
        cvx_ref, mrow_ref = next(it), next(it)
    (gn_ref, wqk_ref, wv_ref, wo_ref, wif_ref, wift_ref, bif_ref, bift_ref, cw_ref, cb_ref, nw_ref,
     wout_ref) = [next(it) for _ in range(12)]
    if sample:
        c0_ref, n0_ref = next(it), next(it)
    o_ref, cout_ref, nout_ref, mout_ref, cvout_ref = [next(it) for _ in range(5)]
    xb_s, q_s, k_s, v_s, og_s, gc_s, ho_s, ext_s = [next(it) for _ in range(8)]
    if not sample:
        cp_s, np_s, m_s = [next(it) for _ in range(3)]
    nqk = wqk_ref.shape[1]
    half = nqk // 2
    nheads = half // ML_DK
    ngate = 2 * nheads
    npair = nheads // 2

    lane = lax.broadcasted_iota(jnp.int32, (1, LANES), 1)
    mlo = lane < ML_DK
    glane = lax.broadcasted_iota(jnp.int32, (1, ngate), 1)
    isf_c = glane >= nheads
    grow = lax.broadcasted_iota(jnp.int32, (ngate, 1), 0)
    isf_r = grow >= nheads
    hlane = lax.broadcasted_iota(jnp.int32, (1, nheads), 1)

    x = x_ref[...]
    xb = _b(_rms(x, gn_ref[...]))
    xb_s[...] = xb
    raw = _dot(xb, wqk_ref[...])
    row = lax.broadcasted_iota(jnp.int32, (tm, 1), 0)
    if sample:
        srow = row & (seq_rows - 1)
        raw = jnp.where((srow >= npad - (ML_CONV - 1)) & (srow < npad), cvx_ref[...], raw)
        cvout_ref[...] = raw
        ext_s[:SUBLANES, :] = jnp.zeros((SUBLANES, nqk), F32)
    else:
        j = pl.program_id(1)

        @pl.when(j == 0)
        def _():
            ext_s[:SUBLANES, :] = jnp.zeros((SUBLANES, nqk), F32)
            cp_s[...] = jnp.zeros(cp_s.shape, F32)
            np_s[...] = jnp.zeros(np_s.shape, F32)
            m_s[...] = jnp.zeros(m_s.shape, F32)

        cvout_ref[...] = raw[tm - SUBLANES:tm, :]
    ext_s[SUBLANES:, :] = raw
    qk = cb_ref[...] + cw_ref[ML_CONV - 1:ML_CONV, :] * raw
    for s in range(1, ML_CONV):
        qk = qk + cw_ref[ML_CONV - 1 - s:ML_CONV - s, :] * ext_s[SUBLANES - s:SUBLANES - s + tm, :]
    if not sample:
        ext_s[:SUBLANES, :] = raw[tm - SUBLANES:tm, :]
    qk = qk * _sigmoid(qk)
    q_s[...] = qk[:, :half] * (ML_DK ** -0.5)
    k_s[...] = qk[:, half:]
    v_s[...] = _dot(xb, wv_ref[...])
    og_s[...] = _sigmoid(_dot(xb, wo_ref[...]))
    ifp = _dot(xb, wif_ref[...]) + bif_ref[...]
    gcol = jnp.where(isf_c, -_softplus(-ifp), ifp)
    if npad:
        keep = (row & (seq_rows - 1)) >= npad
        gcol = jnp.where(keep, gcol, jnp.where(isf_c, 0.0, NEG_BIG))
    gc_s[...] = gcol

    ti = lax.broadcasted_iota(jnp.int32, (L, L), 0)
    si = lax.broadcasted_iota(jnp.int32, (L, L), 1)
    sh = _log2(lsub)
    same = (ti >> sh) == (si >> sh)
    causal = same & (si <= ti)
    tril = jnp.where(causal, 1.0, 0.0).astype(F32)
    triu = jnp.where(same & (ti <= si), 1.0, 0.0).astype(F32)
    lrow = lax.broadcasted_iota(jnp.int32, (1, L), 1)

    def chunk(c, carry):
        r0 = pl.multiple_of(c * L, L)
        rows = pl.ds(r0, L)
        gc = gc_s[rows, :]
        gt = _dot_nt(wift_ref[...], xb_s[rows, :]) + bift_ref[...]
        gt = jnp.where(isf_r, -_softplus(-gt), gt)
        if npad:
            gt = jnp.where((lrow & (seq_rows - 1)) >= npad, gt, jnp.where(isf_r, 0.0, NEG_BIG))
        bcs = _dot(tril, jnp.where(isf_c, gc, 0.0), HI)
        brs = _dot(jnp.where(isf_r, gt, 0.0), triu, HI)
        blast = _seq_last(bcs, lsub)
        if sample:
            mcols = mrow_ref[rows, :]
        else:
            mcols = m_s[...]
        heads = range(nheads)
        prs = range(npair)
        ps = [slice(pp * LANES, (pp + 1) * LANES) for pp in prs]
        hs = [slice(h * ML_DV, (h + 1) * ML_DV) for h in heads]
        q2 = [q_s[rows, ps[pp]] for pp in prs]
        k2 = [k_s[rows, ps[pp]] for pp in prs]
        k2b = [_b(z) for z in k2]
        if sample:
            c_in = [[jnp.concatenate([c0_ref[c * nsub + q, 2 * pp], c0_ref[c * nsub + q, 2 * pp + 1]], axis=0)
                     for q in range(nsub)] for pp in prs]
            n_in = [[n0_ref[c * nsub + q][:, ps[pp]] for q in range(nsub)] for pp in prs]
            n_rows = [jnp.concatenate([jnp.broadcast_to(n_in[pp][q], (lsub, LANES)) for q in range(nsub)], axis=0)
                      for pp in prs]
        else:
            c_prev = [cp_s[pp] for pp in prs]
            n_prev = [np_s[:, ps[pp]] for pp in prs]
            n_rows = n_prev
        bcol = [_colsel(bcs, glane, nheads + h) for h in heads]
        licol = [_colsel(gc, glane, h) for h in heads]
        mcol = [_colsel(mcols, hlane, h) for h in heads]
        blcol = [_colsel(blast, glane, nheads + h) for h in heads]
        dlog = [jnp.where(causal, bcol[h] - (brs[nheads + h:nheads + h + 1, :] - gt[h:h + 1, :]), -jnp.inf)
                for h in heads]
        ginter = [bcol[h] + mcol[h] for h in heads]
        m_t = [jnp.maximum(ginter[h], jnp.max(dlog[h], axis=-1, keepdims=True)) for h in heads]
        dw = [jnp.exp(dlog[h] - m_t[h]) for h in heads]
        winter = [jnp.exp(ginter[h] - m_t[h]) for h in heads]
        qh = [jnp.where(mlo if h % 2 == 0 else jnp.logical_not(mlo), q2[h // 2], 0.0) for h in heads]
        qhb = [_b(z) for z in qh]
        sc = [_dot_nt(qhb[h], k2b[h // 2]) * dw[h] for h in heads]
        if sample:
            inter = [jnp.concatenate([_dot(qhb[h][q * lsub:(q + 1) * lsub], _b(c_in[h // 2][q]))
                                      for q in range(nsub)], axis=0) for h in heads]
        else:
            cb = [_b(z) for z in c_prev]
            inter = [_dot(qhb[h], cb[h // 2]) for h in heads]
        vh = [v_s[rows, hs[h]] for h in heads]
        num = [winter[h] * inter[h] + _dot(_b(sc[h]), _b(vh[h])) for h in heads]
        den = [winter[h] * jnp.sum(qh[h] * n_rows[h // 2], axis=-1, keepdims=True)
               + jnp.sum(sc[h], axis=-1, keepdims=True) for h in heads]
        for h in heads:
            hout = num[h] / jnp.maximum(jnp.abs(den[h]), jnp.exp(-m_t[h]))
            hn = hout * lax.rsqrt(jnp.mean(hout * hout, axis=-1, keepdims=True) + NORM_EPS) * nw_ref[:, hs[h]]
            ho_s[rows, hs[h]] = _b(hn * og_s[rows, hs[h]])
        mnew = [_seq_last(m_t[h], lsub) for h in heads]
        ws = [jnp.exp(blcol[h] - bcol[h] + licol[h] - mnew[h]) for h in heads]
        wstc = [jnp.exp(blcol[h] + mcol[h] - mnew[h]) for h in heads]
        mt_all = jnp.zeros((L, nheads), F32)
        for h in heads:
            mt_all = jnp.where(hlane == h, m_t[h], mt_all)
        for pp in prs:
            lo, hi = 2 * pp, 2 * pp + 1
            wsk = jnp.where(mlo, ws[lo], ws[hi]) * k2[pp]
            wst = jnp.where(mlo, wstc[lo], wstc[hi])
            wvb = _b(jnp.concatenate([ws[lo] * vh[lo], ws[hi] * vh[hi]], axis=0))
            kmb = jnp.concatenate([jnp.where(mlo, k2b[pp], 0), jnp.where(mlo, 0, k2b[pp])], axis=0)
            if sample:
                for q in range(nsub):
                    last = (q + 1) * lsub - 1
                    sel = [slice(blk * L + q * lsub, blk * L + (q + 1) * lsub) for blk in range(2)]
                    wq = wst[last:last + 1, :]
                    wrow = jnp.concatenate([jnp.broadcast_to(wstc[h_][last:last + 1, :], (ML_DK, 1)) for h_ in (lo, hi)],
                                           axis=0)
                    upd = _dot_tn(jnp.concatenate([kmb[s_] for s_ in sel], axis=0),
                                  jnp.concatenate([wvb[s_] for s_ in sel], axis=0))
                    c_new = c_in[pp][q] * wrow + upd
                    cout_ref[c * nsub + q, 2 * pp] = c_new[:ML_DK]
                    cout_ref[c * nsub + q, 2 * pp + 1] = c_new[ML_DK:]
                    nq = n_in[pp][q] * wq + jnp.sum(wsk[q * lsub:(q + 1) * lsub], axis=0, keepdims=True)
                    nout_ref[c * nsub + q, :, ps[pp]] = nq
            else:
                wrow = jnp.concatenate([jnp.broadcast_to(wstc[h_], (ML_DK, 1)) for h_ in (lo, hi)], axis=0)
                cp_s[pp] = c_prev[pp] * wrow + _dot_tn(kmb, wvb)
                np_s[:, ps[pp]] = n_prev[pp] * wst + jnp.sum(wsk, axis=0, keepdims=True)
        if sample:
            mout_ref[rows, :] = mt_all
        else:
            m_s[...] = mt_all[L - 1:L, :]
        return carry

    lax.fori_loop(0, nchunk, chunk, 0)
    if not sample:
        for pp in range(npair):
            cout_ref[0, 2 * pp] = cp_s[pp][:ML_DK]
            cout_ref[0, 2 * pp + 1] = cp_s[pp][ML_DK:]
        nout_ref[0] = np_s[...]
        mout_ref[0] = m_s[...]
    o_ref[...] = x_ref[...] + _dot(ho_s[...], wout_ref[...])


def _mlstm_call(x, cvx, mrow, c0p, n0, wts, *, nseq, seq_rows, npad, sample, tm):
    n, d = x.shape
    nqk = wts[1].shape[1]
    nv = wts[2].shape[1]
    nheads = nqk // 2 // ML_DK
    npair = nheads // 2
    lsub = seq_rows if sample else CHUNK
    kern = functools.partial(_mlstm_kernel, tm=tm, seq_rows=seq_rows, npad=npad, sample=sample, lsub=lsub)
    if sample:
        grid = (n // tm,)
        tile = lambda i: (i, 0)
        spt = tm // seq_rows
        c_spec = pl.BlockSpec((spt, nheads, ML_DK, ML_DV), lambda i: (i, 0, 0, 0))
        n_spec = pl.BlockSpec((spt, 1, nqk // 2), lambda i: (i, 0, 0))
        m_spec = pl.BlockSpec((tm, nheads), tile)
        m_shape = (n, nheads)
        cv_spec = pl.BlockSpec((tm, nqk), tile)
        cv_shape = (n, nqk)
        sem = ("arbitrary",)
    else:
        tps = seq_rows // tm
        grid = (nseq, tps)
        tile = lambda b, j: (b * tps + j, 0)
        c_spec = pl.BlockSpec((1, nheads, ML_DK, ML_DV), lambda b, j: (b, 0, 0, 0))
        n_spec = pl.BlockSpec((1, 1, nqk // 2), lambda b, j: (b, 0, 0))
        m_spec = pl.BlockSpec((1, 1, nheads), lambda b, j: (b, 0, 0))
        m_shape = (nseq, 1, nheads)
        cv_spec = pl.BlockSpec((SUBLANES, nqk), lambda b, j: (b, 0))
        cv_shape = (nseq * SUBLANES, nqk)
        sem = ("arbitrary", "arbitrary")
    x_spec = pl.BlockSpec((tm, d), tile)
    in_specs = [x_spec]
    args = [x]
    if sample:
        in_specs += [pl.BlockSpec((tm, nqk), tile), pl.BlockSpec((tm, nheads), tile)]
        args += [cvx, mrow]
    for wt in wts:
        in_specs.append(_const_spec(wt.shape))
        args.append(wt)
    if sample:
        in_specs += [c_spec, n_spec]
        args += [c0p, n0]
    scratch = [pltpu.VMEM((tm, d), BF16), pltpu.VMEM((tm, nqk // 2), F32), pltpu.VMEM((tm, nqk // 2), F32),
               pltpu.VMEM((tm, nv), F32), pltpu.VMEM((tm, nv), F32), pltpu.VMEM((tm, 2 * nheads), F32),
               pltpu.VMEM((tm, nv), BF16), pltpu.VMEM((tm + SUBLANES, nqk), F32)]
    if not sample:
        scratch += [pltpu.VMEM((npair, LANES, LANES), F32),
                    pltpu.VMEM((1, nqk // 2), F32), pltpu.VMEM((1, nheads), F32)]
    return pl.pallas_call(
        kern,
        grid=grid,
        in_specs=in_specs,
        out_specs=[x_spec, c_spec, n_spec, m_spec, cv_spec],
        out_shape=[jax.ShapeDtypeStruct((n, d), F32),
                   jax.ShapeDtypeStruct((nseq, nheads, ML_DK, ML_DV), F32),
                   jax.ShapeDtypeStruct((nseq, 1, nqk // 2), F32),
                   jax.ShapeDtypeStruct(m_shape, F32),
                   jax.ShapeDtypeStruct(cv_shape, F32)],
        scratch_shapes=scratch,
        compiler_params=pltpu.CompilerParams(dimension_semantics=sem, vmem_limit_bytes=VMEM_LIMIT),
        name="mlstm_sample" if sample else "mlstm_prompt",
    )(*args)


def kernel(x_prompt, x_sample, state_rwkv_S, state_rwkv_shift, state_mlstm_C, state_mlstm_n, state_mlstm_m,
           state_mlstm_conv, norm_ffa, ffa_wg, ffa_wu, ffa_wd, norm_mix, norm_ffb, ffb_wg, ffb_wu, ffb_wd,
           rw_mu, rw_wr, rw_wk, rw_wv, rw_wo, rw_w0, rw_w1, rw_w2, rw_a0, rw_a1, rw_a2, rw_g1, rw_g2,
           rw_k_k, rw_k_a, rw_r_k, rw_gn_w, rw_gn_b, ml_w_in, ml_b_if, ml_conv_w, ml_conv_b, ml_norm_w,
           ml_w_out, norm_final):
    bp, tp, d = x_prompt.shape
    bs, ts, _ = x_sample.shape
    depth = norm_ffa.shape[0]
    slot = SUBLANES
    npad = slot - ts
    assert 0 < ts <= slot and npad >= ML_CONV - 1
    tm_p = min(256, tp)
    tm_s = min(2 * CHUNK, bs * slot)
    assert tp % tm_p == 0 and tm_p % CHUNK == 0 and (bs * slot) % tm_s == 0 and tm_s % CHUNK == 0
    ml_heads = ml_b_if.shape[1] // 2
    nqk = 2 * ml_heads * ML_DK
    nv = ml_heads * ML_DV

    xp = x_prompt.reshape(bp * tp, d)
    xs = jnp.concatenate([jnp.zeros((bs, npad, d), F32), x_sample], axis=1).reshape(bs * slot, d)
    row2 = lambda a: a.reshape(1, -1)

    tm_ffn = min(512, bs * slot)
    assert (bp * tp) % tm_ffn == 0 and (bs * slot) % tm_ffn == 0
    ffa = (norm_ffa.reshape(depth, 1, d), _b(ffa_wg), _b(ffa_wu), _b(ffa_wd))
    ffb = (norm_ffb.reshape(depth, 1, d), _b(ffb_wg), _b(ffb_wu), _b(ffb_wd))

    def ffn(xp_, xs_, wset, layer, final):
        return _ffn_call(xp_, xs_, *wset, row2(norm_final), layer, final_norm=final, tm=tm_ffn)

    new_p = {k_: [] for k_ in ("S", "shift", "C", "n", "m", "conv")}
    new_s = {k_: [] for k_ in ("S", "shift", "C", "n", "m", "conv")}
    for i in range(depth):
        xp, xs = ffn(xp, xs, ffa, i, False)
        j = i // 2
        if i % 2 == 0:
            wts = [row2(norm_mix[i]), rw_mu[j], _b(rw_wr[j]), _b(rw_wk[j]), _b(rw_wv[j]), _b(rw_wo[j]),
                   row2(rw_w0[j]), _b(rw_w1[j]), _b(rw_w2[j]), row2(rw_a0[j]), _b(rw_a1[j]), _b(rw_a2[j]),
                   _b(rw_g1[j]), _b(rw_g2[j]), row2(rw_k_k[j]), row2(rw_k_a[j]), row2(rw_r_k[j]),
                   row2(rw_gn_w[j]), row2(rw_gn_b[j])]
            xp, sbd, tail = _rwkv_call(xp, None, None, wts, nseq=bp, seq_rows=tp, npad=0, sample=False, tm=tm_p)
            new_p["S"].append(sbd)
            new_p["shift"].append(tail.reshape(bp, SUBLANES, d)[:, SUBLANES - 1])
            shx = jnp.repeat(state_rwkv_shift[j], slot, axis=0)
            xs, sbd, xn = _rwkv_call(xs, shx, state_rwkv_S[j], wts, nseq=bs, seq_rows=slot,
                                     npad=npad, sample=True, tm=tm_s)
            new_s["S"].append(sbd)
            new_s["shift"].append(xn.reshape(bs, slot, d)[:, slot - 1])
        else:
            w_in = ml_w_in[j]
            w_if = w_in[:, nqk + nv + d:]
            wts = [row2(norm_mix[i]), _b(w_in[:, :nqk]), _b(w_in[:, nqk:nqk + nv]), _b(w_in[:, nqk + nv:nqk + nv + d]),
                   _b(w_if), _b(w_if.T), row2(ml_b_if[j]), ml_b_if[j].reshape(-1, 1), ml_conv_w[j],
                   row2(ml_conv_b[j]), row2(ml_norm_w[j]), _b(ml_w_out[j])]
            xp, cp, n_, m_, tail = _mlstm_call(xp, None, None, None, None, wts, nseq=bp, seq_rows=tp, npad=0,
                                               sample=False, tm=tm_p)
            new_p["C"].append(jnp.swapaxes(cp, -1, -2))
            new_p["n"].append(n_.reshape(bp, ml_heads, ML_DK))
            new_p["m"].append(m_.reshape(bp, ml_heads))
            new_p["conv"].append(tail.reshape(bp, SUBLANES, nqk)[:, SUBLANES - (ML_CONV - 1):])
            conv0 = state_mlstm_conv[j]
            cvx = jnp.concatenate([jnp.zeros((bs, npad - (ML_CONV - 1), nqk), F32), conv0,
                                   jnp.zeros((bs, slot - npad, nqk), F32)], axis=1).reshape(bs * slot, nqk)
            mrow = jnp.repeat(state_mlstm_m[j], slot, axis=0)
            xs, cp, n_, mt, raw = _mlstm_call(xs, cvx, mrow, jnp.swapaxes(state_mlstm_C[j], -1, -2),
                                              state_mlstm_n[j].reshape(bs, 1, ml_heads * ML_DK), wts, nseq=bs,
                                              seq_rows=slot, npad=npad, sample=True, tm=tm_s)
            new_s["C"].append(jnp.swapaxes(cp, -1, -2))
            new_s["n"].append(n_.reshape(bs, ml_heads, ML_DK))
            new_s["m"].append(mt.reshape(bs, slot, ml_heads)[:, slot - 1])
            new_s["conv"].append(raw.reshape(bs, slot, nqk)[:, slot - (ML_CONV - 1):])
        xp, xs = ffn(xp, xs, ffb, i, i == depth - 1)
    y_prompt = xp.reshape(bp, tp, d)
    y_sample = xs.reshape(bs, slot, d)[:, npad:]
    st = lambda lst: jnp.stack(lst)
    return (y_prompt, y_sample,
            st(new_p["S"]), st(new_p["shift"]), st(new_p["C"]), st(new_p["n"]), st(new_p["m"]), st(new_p["conv"]),
            st(new_s["S"]), st(new_s["shift"]), st(new_s["C"]), st(new_s["n"]), st(new_s["m"]), st(new_s["conv"]))
```

```python
import functools
import math

import jax
import jax.numpy as jnp
from jax import lax
from jax.experimental import pallas as pl
from jax.experimental.pallas import tpu as pltpu

F32 = jnp.float32
BF16 = jnp.bfloat16

NORM_EPS = 1e-6
RW_GN_EPS = 64e-5
RW_HEAD = 64
ML_DK = 64
ML_DV = 128
ML_CONV = 4

LANES = 128
SUBLANES = 8
MXU_COLS = 256
CHUNK = 64
NEG_BIG = -1e30
VMEM_LIMIT = 58 * 1024 * 1024
HI = lax.Precision.HIGHEST


def _dot(a, b, precision=None):
    return jnp.dot(a, b, preferred_element_type=F32, precision=precision)


def _dot_nt(a, b, precision=None):
    return lax.dot_general(a, b, (((1,), (1,)), ((), ())), preferred_element_type=F32, precision=precision)


def _dot_tn(a, b, precision=None):
    return lax.dot_general(a, b, (((0,), (0,)), ((), ())), preferred_element_type=F32, precision=precision)


def _b(x):
    return x.astype(BF16)


def _rms(x, g):
    return x * lax.rsqrt(jnp.mean(x * x, axis=-1, keepdims=True) + NORM_EPS) * g


def _sigmoid(x):
    return 1.0 / (1.0 + jnp.exp(-x))


def _softplus(x):
    return jnp.maximum(x, 0.0) + jnp.log1p(jnp.exp(-jnp.abs(x)))


def _segsum(x, mlo):
    lo = jnp.sum(jnp.where(mlo, x, 0.0), axis=-1, keepdims=True)
    hi = jnp.sum(jnp.where(mlo, 0.0, x), axis=-1, keepdims=True)
    return jnp.where(mlo, lo, hi)


def _colsel(x, lane_idx, j):
    return jnp.sum(jnp.where(lane_idx == j, x, 0.0), axis=-1, keepdims=True)


def _seq_last(x, lsub):
    n = x.shape[0]
    if lsub == n:
        return x[n - 1:n]
    parts = [jnp.broadcast_to(x[q * lsub + lsub - 1:q * lsub + lsub], (lsub,) + x.shape[1:])
             for q in range(n // lsub)]
    return jnp.concatenate(parts, axis=0)


def _blockdiag(a, b):
    z = jnp.zeros(a.shape, a.dtype)
    return jnp.concatenate([jnp.concatenate([a, z], axis=1), jnp.concatenate([z, b], axis=1)], axis=0)


def _const_spec(shape, single_buffer=False):
    nd = len(shape)
    if single_buffer:
        return pl.BlockSpec(shape, lambda *_: (0,) * nd, pipeline_mode=pl.Buffered(1))
    return pl.BlockSpec(shape, lambda *_: (0,) * nd)


def _log2(n):
    k = int(math.log2(n))
    assert 1 << k == n, n
    return k


def _stream_cast(src, dst, stage, sem, rows):
    assert src.shape[0] % rows == 0 and stage.shape[1:] == (rows, src.shape[1])
    n = src.shape[0] // rows

    def copy(k):
        return pltpu.make_async_copy(src.at[pl.ds(k * rows, rows), :], stage.at[k % 2], sem.at[k % 2])

    copy(0).start()
    for k in range(n):
        if k + 1 < n:
            copy(k + 1).start()
        copy(k).wait()
        dst[k * rows:(k + 1) * rows, :] = _b(stage[k % 2])


def _ffn_kernel(xp_ref, xs_ref, g_ref, wg_hbm, wu_hbm, wd_hbm, gf_ref, op_ref, os_ref, wg_ref, wu_ref, wd_ref,
                stage_up, stage_dn, sem, *, layer, fchunk, final_norm, np_steps):
    @pl.when(pl.program_id(0) == 0)
    def _():
        _stream_cast(wg_hbm.at[layer], wg_ref, stage_up, sem, stage_up.shape[1])
        _stream_cast(wu_hbm.at[layer], wu_ref, stage_up, sem, stage_up.shape[1])
        _stream_cast(wd_hbm.at[layer], wd_ref, stage_dn, sem, stage_dn.shape[1])

    is_p = pl.program_id(0) < np_steps
    x = jnp.where(is_p, xp_ref[...], xs_ref[...])
    xb = _b(_rms(x, g_ref[...]))
    acc = jnp.zeros(x.shape, F32)
    for lo, hi in zip(fchunk[:-1], fchunk[1:]):
        sl = slice(lo, hi)
        gate = _dot(xb, wg_ref[:, sl])
        up = _dot(xb, wu_ref[:, sl])
        h = _b(gate * _sigmoid(gate) * up)
        acc = acc + _dot(h, wd_ref[sl, :])
    out = x + 0.5 * acc
    if final_norm:
        out = _rms(out, gf_ref[...])

    @pl.when(is_p)
    def _():
        op_ref[...] = out

    @pl.when(jnp.logical_not(is_p))
    def _():
        os_ref[...] = out


def _ffn_call(xp, xs, g_all, wg_all, wu_all, wd_all, gf, layer, *, final_norm, tm):
    (n_p, d), n_s = xp.shape, xs.shape[0]
    nf = wg_all.shape[2]
    np_steps, ns_steps = n_p // tm, n_s // tm
    ntile = nf // MXU_COLS
    fchunk = (0, -(-ntile // 2) * MXU_COLS, nf) if nf % MXU_COLS == 0 and ntile > 1 else (0, nf)
    up_rows, dn_rows = math.gcd(d, LANES), math.gcd(nf, MXU_COLS)
    kern = functools.partial(_ffn_kernel, layer=layer, fchunk=fchunk, final_norm=final_norm, np_steps=np_steps)
    p_spec = pl.BlockSpec((tm, d), lambda i: (jnp.minimum(i, np_steps - 1), 0))
    s_spec = pl.BlockSpec((tm, d), lambda i: (jnp.maximum(i - np_steps, 0), 0))
    hbm = pl.BlockSpec(memory_space=pl.ANY)
    return pl.pallas_call(
        kern,
        grid=(np_steps + ns_steps,),
        in_specs=[p_spec, s_spec, pl.BlockSpec((None, 1, d), lambda i: (layer, 0, 0)), hbm, hbm, hbm,
                  _const_spec((1, d))],
        out_specs=[p_spec, s_spec],
        out_shape=[jax.ShapeDtypeStruct((n_p, d), F32), jax.ShapeDtypeStruct((n_s, d), F32)],
        scratch_shapes=[pltpu.VMEM((d, nf), BF16), pltpu.VMEM((d, nf), BF16), pltpu.VMEM((nf, d), BF16),
                        pltpu.VMEM((2, up_rows, nf), F32), pltpu.VMEM((2, dn_rows, d), F32),
                        pltpu.SemaphoreType.DMA((2,))],
        compiler_params=pltpu.CompilerParams(dimension_semantics=("arbitrary",), vmem_limit_bytes=VMEM_LIMIT),
        name="ffn",
    )(xp, xs, g_all, wg_all, wu_all, wd_all, gf)


def _rw_token_part(xn, xx, row, mlo, sset, seq_rows, npad, wrefs):
    r_s, ld_s, k_s, v_s, kn_s, b_s, g_s, bon_s = sset
    mu_ref, wr_ref, wk_ref, wv_ref, w0_ref, w1_ref, w2_ref, a0_ref, a1_ref, a2_ref, g1_ref, g2_ref, kk_ref, ka_ref, rk_ref = wrefs

    def mix(i):
        return _b(xn + xx * mu_ref[i:i + 1, :])

    r = _dot(mix(0), wr_ref[...])
    r_s[...] = r
    wl = _b(jnp.tanh(_dot(mix(1), w1_ref[...])))
    ld = -math.exp(-0.5) * _sigmoid(w0_ref[...] + _dot(wl, w2_ref[...]))
    k = _dot(mix(2), wk_ref[...])
    v = _dot(mix(3), wv_ref[...])
    al = _b(_dot(mix(4), a1_ref[...]))
    a = _sigmoid(a0_ref[...] + _dot(al, a2_ref[...]))
    gl = _b(_sigmoid(_dot(mix(5), g1_ref[...])))
    g_s[...] = _dot(gl, g2_ref[...])
    k2 = k * (1.0 + (a - 1.0) * ka_ref[...])
    kk = k * kk_ref[...]
    rkk = r * k2 * rk_ref[...]
    if npad:
        keep = (row & (seq_rows - 1)) >= npad
        ld = jnp.where(keep, ld, 0.0)
        k2 = jnp.where(keep, k2, 0.0)
        kk = jnp.where(keep, kk, 0.0)
        v = jnp.where(keep, v, 0.0)
    ld_s[...] = ld
    k_s[...] = k2
    v_s[...] = v
    for p in range(xn.shape[1] // LANES):
        cs = slice(p * LANES, (p + 1) * LANES)
        kkp = kk[:, cs]
        kn = kkp / jnp.maximum(jnp.sqrt(_segsum(kkp * kkp, mlo)), 1e-12)
        kn_s[:, cs] = kn
        b_s[:, cs] = kn * a[:, cs]
        bon_s[:, cs] = _segsum(rkk[:, cs], mlo) * v[:, cs]


def _rwkv_kernel(*refs, tm, seq_rows, npad, sample, lsub):
    L = CHUNK
    nsub = L // lsub
    nchunk = tm // L
    nsteps = _log2(lsub)
    it = iter(refs)
    x_ref = next(it)
    shx_ref = next(it) if sample else None
    (gn_ref, mu_ref, wr_ref, wk_ref, wv_ref, wo_ref, w0_ref, w1_ref, w2_ref, a0_ref, a1_ref, a2_ref,
     g1_ref, g2_ref, kk_ref, ka_ref, rk_ref, gw_ref, gb_ref) = [next(it) for _ in range(19)]
    s0_ref = next(it) if sample else None
    o_ref, sout_ref, xn_ref = next(it), next(it), next(it)
    set_a = [next(it) for _ in range(8)]
    yg_s = next(it)
    if not sample:
        carry_s, sbd_s = next(it), next(it)
    d = x_ref.shape[1]
    npair = d // LANES

    lane = lax.broadcasted_iota(jnp.int32, (1, LANES), 1)
    mlo = lane < RW_HEAD
    row = lax.broadcasted_iota(jnp.int32, (tm, 1), 0)

    xn = _rms(x_ref[...], gn_ref[...])
    rolled = pltpu.roll(xn, 1, 0)
    if sample:
        sbd_s = None
        xprev = jnp.where((row & (seq_rows - 1)) == npad, shx_ref[...], rolled)
        xn_ref[...] = xn
    else:
        @pl.when(pl.program_id(1) == 0)
        def _():
            carry_s[...] = jnp.zeros(carry_s.shape, F32)
            sbd_s[...] = jnp.zeros(sbd_s.shape, F32)

        xprev = jnp.where(row == 0, carry_s[SUBLANES - 1:SUBLANES, :], rolled)
        carry_s[...] = xn[tm - SUBLANES:tm, :]
        xn_ref[...] = xn[tm - SUBLANES:tm, :]
    _rw_token_part(xn, xprev - xn, row, mlo, set_a, seq_rows, npad,
                   (mu_ref, wr_ref, wk_ref, wv_ref, w0_ref, w1_ref, w2_ref, a0_ref, a1_ref, a2_ref,
                    g1_ref, g2_ref, kk_ref, ka_ref, rk_ref))
    _rw_back(set_a, sbd_s, sample=sample, lsub=lsub, nchunk=nchunk, nsub=nsub, nsteps=nsteps, npair=npair, mlo=mlo,
             s0_ref=s0_ref, sout_ref=sout_ref, yg_s=yg_s, gw_ref=gw_ref, gb_ref=gb_ref, wo_ref=wo_ref,
             xres_ref=x_ref, o_ref=o_ref)


def _rw_back(sset, sbd_s, *, sample, lsub, nchunk, nsub, nsteps, npair, mlo, s0_ref, sout_ref, yg_s,
             gw_ref, gb_ref, wo_ref, xres_ref, o_ref):
    r_s, ld_s, k_s, v_s, kn_s, b_s, g_s, bon_s = sset
    L = CHUNK
    sh = _log2(lsub)
    ti = lax.broadcasted_iota(jnp.int32, (L, 3 * L), 0)
    si = lax.broadcasted_iota(jnp.int32, (L, 3 * L), 1) & (L - 1)
    tril3 = jnp.where(((ti >> sh) == (si >> sh)) & (si <= ti), 1.0, 0.0).astype(BF16)
    gi = lax.broadcasted_iota(jnp.int32, (2 * L, 4 * L), 0)
    gj = lax.broadcasted_iota(jnp.int32, (2 * L, 4 * L), 1)
    gt = gi & (L - 1)
    gs = gj & (L - 1)
    gmask = ((gt >> sh) == (gs >> sh)) & ((gs < gt) | ((gi >= L) & (gs == gt)))
    pairs = range(npair)

    def stack2(z):
        return jnp.concatenate([jnp.where(mlo, z, 0), jnp.where(mlo, 0, z)], axis=0)

    cs = [slice(p * LANES, (p + 1) * LANES) for p in pairs]
    rw = {(c, p): slice(c * L, (c + 1) * L) for c in range(nchunk) for p in pairs}
    cum, a2f, a2, q4, vv2f, vv2, makv, mr, wcat, pl_, qpl = ({} for _ in range(11))

    def phase_a(chunks):
        chains = [(c, p) for c in chunks for p in pairs]
        for key in chains:
            ldc = ld_s[rw[key], cs[key[1]]]
            hi = _b(ldc)
            r1 = ldc - hi.astype(F32)
            mid = _b(r1)
            lo = _b(r1 - mid.astype(F32))
            cum[key] = _dot(tril3, jnp.concatenate([hi, mid, lo], axis=0))
        for key in chains:
            rows, c_ = rw[key], cs[key[1]]
            ep = jnp.exp(cum[key])
            em = jnp.exp(-cum[key])
            at = -(kn_s[rows, c_] * jnp.exp(cum[key] - ld_s[rows, c_]))
            a2f[key] = jnp.concatenate([at, r_s[rows, c_] * ep], axis=0)
            a2[key] = _b(a2f[key])
            q4[key] = jnp.concatenate([stack2(b_s[rows, c_] * em), stack2(k_s[rows, c_] * em)], axis=0)
            vv2f[key] = stack2(v_s[rows, c_])
            vv2[key] = _b(vv2f[key])
        g = {key: jnp.where(gmask, _dot_nt(a2[key], _b(q4[key])), 0.0) for key in chains}
        for key in chains:
            makv[key] = _dot(_b(g[key][:, LANES:]), vv2[key])
            mr[key] = _b(g[key][L:, :LANES])
        w = {key: g[key][:L, :LANES] for key in chains}
        pf = {}
        for key in chains:
            pc = _b(w[key])
            pf[key] = _dot(pc, stack2(pc))
        for k in range(1, nsteps):
            for key in chains:
                pc = _b(pf[key])
                if k + 1 < nsteps:
                    both = _dot(pc, jnp.concatenate([stack2(pc), stack2(_b(w[key]))], axis=1))
                    w[key] = w[key] + pf[key] + both[:, LANES:]
                    pf[key] = both[:, :LANES]
                else:
                    w[key] = w[key] + pf[key] + _dot(pc, stack2(_b(w[key])))
        for key in chains:
            wcat[key] = _b(w[key])
            if not sample:
                pl_[key] = jnp.exp(cum[key][L - 1:L, :])
                qpl[key] = _b(q4[key] * pl_[key])

    if not sample:
        state = [sbd_s[p] for p in pairs]
    phase_a(range(nchunk))
    for c in range(nchunk):
        keys = [(c, p) for p in pairs]
        if sample:
            s_in = [[_blockdiag(s0_ref[c * nsub + q, 2 * p], s0_ref[c * nsub + q, 2 * p + 1]) for q in range(nsub)]
                    for p in pairs]
            as_a, as_r = [], []
            for p in pairs:
                asa, asr = [], []
                for q in range(nsub):
                    a2q = _b(jnp.concatenate([a2f[c, p][q * lsub:(q + 1) * lsub],
                                              a2f[c, p][L + q * lsub:L + (q + 1) * lsub]], axis=0))
                    asq = _dot_nt(a2q, _b(s_in[p][q]))
                    asa.append(asq[:lsub])
                    asr.append(asq[lsub:])
                as_a.append(jnp.concatenate(asa, axis=0))
                as_r.append(jnp.concatenate(asr, axis=0))
        else:
            as_ = [_dot_nt(a2[key], _b(state[key[1]])) for key in keys]
            as_a = [z[:L] for z in as_]
            as_r = [z[L:] for z in as_]
        rhs = [as_a[p] + makv[c, p][:L] for p in pairs]
        u = [rhs[p] + _dot(wcat[c, p], stack2(_b(rhs[p]))) for p in pairs]
        u2 = [stack2(_b(u[p])) for p in pairs]
        if sample:
            uvf = [jnp.concatenate([stack2(u[p]), vv2f[c, p]], axis=0) for p in pairs]
        else:
            uv = [jnp.concatenate([u2[p], vv2[c, p]], axis=0) for p in pairs]
        y = [as_r[p] + makv[c, p][L:] + _dot(mr[c, p], u2[p]) for p in pairs]
        for p in pairs:
            if sample:
                for q in range(nsub):
                    sel = [slice(blk * L + q * lsub, blk * L + (q + 1) * lsub) for blk in range(4)]
                    pl_q = jnp.exp(cum[c, p][(q + 1) * lsub - 1:(q + 1) * lsub, :])
                    uvq = _b(jnp.concatenate([uvf[p][s_] for s_ in sel], axis=0))
                    qq = _b(jnp.concatenate([q4[c, p][s_] for s_ in sel], axis=0) * pl_q)
                    s_new = s_in[p][q] * pl_q + _dot_tn(uvq, qq)
                    sout_ref[c * nsub + q, 2 * p] = s_new[:RW_HEAD, :RW_HEAD]
                    sout_ref[c * nsub + q, 2 * p + 1] = s_new[RW_HEAD:, RW_HEAD:]
            else:
                state[p] = state[p] * pl_[c, p] + _dot_tn(uv[p], qpl[c, p])
        rows = rw[c, 0]
        for p in pairs:
            mean = _segsum(y[p], mlo) * (1.0 / RW_HEAD)
            yc = y[p] - mean
            var = _segsum(yc * yc, mlo) * (1.0 / RW_HEAD)
            yn = yc * lax.rsqrt(var + RW_GN_EPS) * gw_ref[:, cs[p]] + gb_ref[:, cs[p]] + bon_s[rows, cs[p]]
            yg_s[rows, cs[p]] = _b(yn * g_s[rows, cs[p]])
    if not sample:
        for p in pairs:
            sbd_s[p] = state[p]
            sout_ref[0, 2 * p] = state[p][:RW_HEAD, :RW_HEAD]
            sout_ref[0, 2 * p + 1] = state[p][RW_HEAD:, RW_HEAD:]
    o_ref[...] = xres_ref[...] + _dot(yg_s[...], wo_ref[...])


def _rwkv_call(x, shx, s0bd, wts, *, nseq, seq_rows, npad, sample, tm):
    n, d = x.shape
    npair = d // LANES
    lsub = seq_rows if sample else CHUNK
    kern = functools.partial(_rwkv_kernel, tm=tm, seq_rows=seq_rows, npad=npad, sample=sample, lsub=lsub)
    if sample:
        grid = (n // tm,)
        x_spec = pl.BlockSpec((tm, d), lambda i: (i, 0))
        spt = tm // seq_rows
        st_spec = pl.BlockSpec((spt, 2 * npair, RW_HEAD, RW_HEAD), lambda i: (i, 0, 0, 0))
        xn_spec = x_spec
        xn_shape = (n, d)
        in_specs = [x_spec, x_spec]
        args = [x, shx]
        sem = ("arbitrary",)
    else:
        tps = seq_rows // tm
        grid = (nseq, tps)
        x_spec = pl.BlockSpec((tm, d), lambda b, j: (b * tps + j, 0))
        st_spec = pl.BlockSpec((1, 2 * npair, RW_HEAD, RW_HEAD), lambda b, j: (b, 0, 0, 0))
        xn_spec = pl.BlockSpec((SUBLANES, d), lambda b, j: (b, 0))
        xn_shape = (nseq * SUBLANES, d)
        in_specs = [x_spec]
        args = [x]
        sem = ("arbitrary", "arbitrary")
    for wt in wts:
        in_specs.append(_const_spec(wt.shape, single_buffer=True))
        args.append(wt)
    if sample:
        in_specs.append(st_spec)
        args.append(s0bd)
    scratch = [pltpu.VMEM((tm, d), F32) for _ in range(8)] + [pltpu.VMEM((tm, d), BF16)]
    if not sample:
        scratch += [pltpu.VMEM((SUBLANES, d), F32), pltpu.VMEM((npair, LANES, LANES), F32)]
    return pl.pallas_call(
        kern,
        grid=grid,
        in_specs=in_specs,
        out_specs=[x_spec, st_spec, xn_spec],
        out_shape=[jax.ShapeDtypeStruct((n, d), F32),
                   jax.ShapeDtypeStruct((nseq, 2 * npair, RW_HEAD, RW_HEAD), F32),
                   jax.ShapeDtypeStruct(xn_shape, F32)],
        scratch_shapes=scratch,
        compiler_params=pltpu.CompilerParams(dimension_semantics=sem, vmem_limit_bytes=VMEM_LIMIT),
        name="rwkv_sample" if sample else "rwkv_prompt",
    )(*args)


def _mlstm_kernel(*refs, tm, seq_rows, npad, sample, lsub):
    L = CHUNK
    nsub = L // lsub
    nchunk = tm // L
    it = iter(refs)
    x_ref = next(it)
    if sample:
        cvx_ref, mrow_ref = next(it), next(it)
    (gn_ref, wqk_ref, wv_ref, wo_ref, wif_ref, wift_ref, bif_ref, bift_ref, cw_ref, cb_ref, nw_ref,
     wout_ref) = [next(it) for _ in range(12)]
    if sample:
        c0_ref, n0_ref = next(it), next(it)
    o_ref, cout_ref, nout_ref, mout_ref, cvout_ref = [next(it) for _ in range(5)]
    xb_s, q_s, k_s, v_s, og_s, gc_s, ho_s, ext_s = [next(it) for _ in range(8)]
    if not sample:
        cp_s, np_s, m_s = [next(it) for _ in range(3)]
    nqk = wqk_ref.shape[1]
    half = nqk // 2
    nheads = half // ML_DK
    ngate = 2 * nheads
    npair = nheads // 2

    lane = lax.broadcasted_iota(jnp.int32, (1, LANES), 1)
    mlo = lane < ML_DK
    glane = lax.broadcasted_iota(jnp.int32, (1, ngate), 1)
    isf_c = glane >= nheads
    grow = lax.broadcasted_iota(jnp.int32, (ngate, 1), 0)
    isf_r = grow >= nheads
    hlane = lax.broadcasted_iota(jnp.int32, (1, nheads), 1)

    x = x_ref[...]
    xb = _b(_rms(x, gn_ref[...]))
    xb_s[...] = xb
    raw = _dot(xb, wqk_ref[...])
    row = lax.broadcasted_iota(jnp.int32, (tm, 1), 0)
    if sample:
        srow = row & (seq_rows - 1)
        raw = jnp.where((srow >= npad - (ML_CONV - 1)) & (srow < npad), cvx_ref[...], raw)
        cvout_ref[...] = raw
        ext_s[:SUBLANES, :] = jnp.zeros((SUBLANES, nqk), F32)
    else:
        j = pl.program_id(1)

        @pl.when(j == 0)
        def _():
            ext_s[:SUBLANES, :] = jnp.zeros((SUBLANES, nqk), F32)
            cp_s[...] = jnp.zeros(cp_s.shape, F32)
            np_s[...] = jnp.zeros(np_s.shape, F32)
            m_s[...] = jnp.zeros(m_s.shape, F32)

        cvout_ref[...] = raw[tm - SUBLANES:tm, :]
    ext_s[SUBLANES:, :] = raw
    qk = cb_ref[...] + cw_ref[ML_CONV - 1:ML_CONV, :] * raw
    for s in range(1, ML_CONV):
        qk = qk + cw_ref[ML_CONV - 1 - s:ML_CONV - s, :] * ext_s[SUBLANES - s:SUBLANES - s + tm, :]
    if not sample:
        ext_s[:SUBLANES, :] = raw[tm - SUBLANES:tm, :]
    qk = qk * _sigmoid(qk)
    q_s[...] = qk[:, :half] * (ML_DK ** -0.5)
    k_s[...] = qk[:, half:]
    v_s[...] = _dot(xb, wv_ref[...])
    og_s[...] = _sigmoid(_dot(xb, wo_ref[...]))
    ifp = _dot(xb, wif_ref[...]) + bif_ref[...]
    gcol = jnp.where(isf_c, -_softplus(-ifp), ifp)
    if npad:
        keep = (row & (seq_rows - 1)) >= npad
        gcol = jnp.where(keep, gcol, jnp.where(isf_c, 0.0, NEG_BIG))
    gc_s[...] = gcol

    ti = lax.broadcasted_iota(jnp.int32, (L, L), 0)
    si = lax.broadcasted_iota(jnp.int32, (L, L), 1)
    sh = _log2(lsub)
    same = (ti >> sh) == (si >> sh)
    causal = same & (si <= ti)
    tril = jnp.where(causal, 1.0, 0.0).astype(F32)
    triu = jnp.where(same & (ti <= si), 1.0, 0.0).astype(F32)
    lrow = lax.broadcasted_iota(jnp.int32, (1, L), 1)

    def chunk(c, carry):
        r0 = pl.multiple_of(c * L, L)
        rows = pl.ds(r0, L)
        gc = gc_s[rows, :]
        gt = _dot_nt(wift_ref[...], xb_s[rows, :]) + bift_ref[...]
        gt = jnp.where(isf_r, -_softplus(-gt), gt)
        if npad:
            gt = jnp.where((lrow & (seq_rows - 1)) >= npad, gt, jnp.where(isf_r, 0.0, NEG_BIG))
        bcs = _dot(tril, jnp.where(isf_c, gc, 0.0), HI)
        brs = _dot(jnp.where(isf_r, gt, 0.0), triu, HI)
        blast = _seq_last(bcs, lsub)
        if sample:
            mcols = mrow_ref[rows, :]
        else:
            mcols = m_s[...]
        heads = range(nheads)
        prs = range(npair)
        ps = [slice(pp * LANES, (pp + 1) * LANES) for pp in prs]
        hs = [slice(h * ML_DV, (h + 1) * ML_DV) for h in heads]
        q2 = [q_s[rows, ps[pp]] for pp in prs]
        k2 = [k_s[rows, ps[pp]] for pp in prs]
        k2b = [_b(z) for z in k2]
        if sample:
            c_in = [[jnp.concatenate([c0_ref[c * nsub + q, 2 * pp], c0_ref[c * nsub + q, 2 * pp + 1]], axis=0)
                     for q in range(nsub)] for pp in prs]
            n_in = [[n0_ref[c * nsub + q][:, ps[pp]] for q in range(nsub)] for pp in prs]
            n_rows = [jnp.concatenate([jnp.broadcast_to(n_in[pp][q], (lsub, LANES)) for q in range(nsub)], axis=0)
                      for pp in prs]
        else:
            c_prev = [cp_s[pp] for pp in prs]
            n_prev = [np_s[:, ps[pp]] for pp in prs]
            n_rows = n_prev
        bcol = [_colsel(bcs, glane, nheads + h) for h in heads]
        licol = [_colsel(gc, glane, h) for h in heads]
        mcol = [_colsel(mcols, hlane, h) for h in heads]
        blcol = [_colsel(blast, glane, nheads + h) for h in heads]
        dlog = [jnp.where(causal, bcol[h] - (brs[nheads + h:nheads + h + 1, :] - gt[h:h + 1, :]), -jnp.inf)
                for h in heads]
        ginter = [bcol[h] + mcol[h] for h in heads]
        m_t = [jnp.maximum(ginter[h], jnp.max(dlog[h], axis=-1, keepdims=True)) for h in heads]
        dw = [jnp.exp(dlog[h] - m_t[h]) for h in heads]
        winter = [jnp.exp(ginter[h] - m_t[h]) for h in heads]
        qh = [jnp.where(mlo if h % 2 == 0 else jnp.logical_not(mlo), q2[h // 2], 0.0) for h in heads]
        qhb = [_b(z) for z in qh]
        sc = [_dot_nt(qhb[h], k2b[h // 2]) * dw[h] for h in heads]
        if sample:
            inter = [jnp.concatenate([_dot(qhb[h][q * lsub:(q + 1) * lsub], _b(c_in[h // 2][q]))
                                      for q in range(nsub)], axis=0) for h in heads]
        else:
            cb = [_b(z) for z in c_prev]
            inter = [_dot(qhb[h], cb[h // 2]) for h in heads]
        vh = [v_s[rows, hs[h]] for h in heads]
        num = [winter[h] * inter[h] + _dot(_b(sc[h]), _b(vh[h])) for h in heads]
        den = [winter[h] * jnp.sum(qh[h] * n_rows[h // 2], axis=-1, keepdims=True)
               + jnp.sum(sc[h], axis=-1, keepdims=True) for h in heads]
        for h in heads:
            hout = num[h] / jnp.maximum(jnp.abs(den[h]), jnp.exp(-m_t[h]))
            hn = hout * lax.rsqrt(jnp.mean(hout * hout, axis=-1, keepdims=True) + NORM_EPS) * nw_ref[:, hs[h]]
            ho_s[rows, hs[h]] = _b(hn * og_s[rows, hs[h]])
        mnew = [_seq_last(m_t[h], lsub) for h in heads]
        ws = [jnp.exp(blcol[h] - bcol[h] + licol[h] - mnew[h]) for h in heads]
        wstc = [jnp.exp(blcol[h] + mcol[h] - mnew[h]) for h in heads]
        mt_all = jnp.zeros((L, nheads), F32)
        for h in heads:
            mt_all = jnp.where(hlane == h, m_t[h], mt_all)
        for pp in prs:
            lo, hi = 2 * pp, 2 * pp + 1
            wsk = jnp.where(mlo, ws[lo], ws[hi]) * k2[pp]
            wst = jnp.where(mlo, wstc[lo], wstc[hi])
            wvb = _b(jnp.concatenate([ws[lo] * vh[lo], ws[hi] * vh[hi]], axis=0))
            kmb = jnp.concatenate([jnp.where(mlo, k2b[pp], 0), jnp.where(mlo, 0, k2b[pp])], axis=0)
            if sample:
                for q in range(nsub):
                    last = (q + 1) * lsub - 1
                    sel = [slice(blk * L + q * lsub, blk * L + (q + 1) * lsub) for blk in range(2)]
                    wq = wst[last:last + 1, :]
                    wrow = jnp.concatenate([jnp.broadcast_to(wstc[h_][last:last + 1, :], (ML_DK, 1)) for h_ in (lo, hi)],
                                           axis=0)
                    upd = _dot_tn(jnp.concatenate([kmb[s_] for s_ in sel], axis=0),
                                  jnp.concatenate([wvb[s_] for s_ in sel], axis=0))
                    c_new = c_in[pp][q] * wrow + upd
                    cout_ref[c * nsub + q, 2 * pp] = c_new[:ML_DK]
                    cout_ref[c * nsub + q, 2 * pp + 1] = c_new[ML_DK:]
                    nq = n_in[pp][q] * wq + jnp.sum(wsk[q * lsub:(q + 1) * lsub], axis=0, keepdims=True)
                    nout_ref[c * nsub + q, :, ps[pp]] = nq
            else:
                wrow = jnp.concatenate([jnp.broadcast_to(wstc[h_], (ML_DK, 1)) for h_ in (lo, hi)], axis=0)
                cp_s[pp] = c_prev[pp] * wrow + _dot_tn(kmb, wvb)
                np_s[:, ps[pp]] = n_prev[pp] * wst + jnp.sum(wsk, axis=0, keepdims=True)
        if sample:
            mout_ref[rows, :] = mt_all
        else:
            m_s[...] = mt_all[L - 1:L, :]
        return carry

    lax.fori_loop(0, nchunk, chunk, 0)
    if not sample:
        for pp in range(npair):
            cout_ref[0, 2 * pp] = cp_s[pp][:ML_DK]
            cout_ref[0, 2 * pp + 1] = cp_s[pp][ML_DK:]
        nout_ref[0] = np_s[...]
        mout_ref[0] = m_s[...]
    o_ref[...] = x_ref[...] + _dot(ho_s[...], wout_ref[...])


def _mlstm_call(x, cvx, mrow, c0p, n0, wts, *, nseq, seq_rows, npad, sample, tm):
    n, d = x.shape
    nqk = wts[1].shape[1]
    nv = wts[2].shape[1]
    nheads = nqk // 2 // ML_DK
    npair = nheads // 2
    lsub = seq_rows if sample else CHUNK
    kern = functools.partial(_mlstm_kernel, tm=tm, seq_rows=seq_rows, npad=npad, sample=sample, lsub=lsub)
    if sample:
        grid = (n // tm,)
        tile = lambda i: (i, 0)
        spt = tm // seq_rows
        c_spec = pl.BlockSpec((spt, nheads, ML_DK, ML_DV), lambda i: (i, 0, 0, 0))
        n_spec = pl.BlockSpec((spt, 1, nqk // 2), lambda i: (i, 0, 0))
        m_spec = pl.BlockSpec((tm, nheads), tile)
        m_shape = (n, nheads)
        cv_spec = pl.BlockSpec((tm, nqk), tile)
        cv_shape = (n, nqk)
        sem = ("arbitrary",)
    else:
        tps = seq_rows // tm
        grid = (nseq, tps)
        tile = lambda b, j: (b * tps + j, 0)
        c_spec = pl.BlockSpec((1, nheads, ML_DK, ML_DV), lambda b, j: (b, 0, 0, 0))
        n_spec = pl.BlockSpec((1, 1, nqk // 2), lambda b, j: (b, 0, 0))
        m_spec = pl.BlockSpec((1, 1, nheads), lambda b, j: (b, 0, 0))
        m_shape = (nseq, 1, nheads)
        cv_spec = pl.BlockSpec((SUBLANES, nqk), lambda b, j: (b, 0))
        cv_shape = (nseq * SUBLANES, nqk)
        sem = ("arbitrary", "arbitrary")
    x_spec = pl.BlockSpec((tm, d), tile)
    in_specs = [x_spec]
    args = [x]
    if sample:
        in_specs += [pl.BlockSpec((tm, nqk), tile), pl.BlockSpec((tm, nheads), tile)]
        args += [cvx, mrow]
    for wt in wts:
        in_specs.append(_const_spec(wt.shape))
        args.append(wt)
    if sample:
        in_specs += [c_spec, n_spec]
        args += [c0p, n0]
    scratch = [pltpu.VMEM((tm, d), BF16), pltpu.VMEM((tm, nqk // 2), F32), pltpu.VMEM((tm, nqk // 2), F32),
               pltpu.VMEM((tm, nv), F32), pltpu.VMEM((tm, nv), F32), pltpu.VMEM((tm, 2 * nheads), F32),
               pltpu.VMEM((tm, nv), BF16), pltpu.VMEM((tm + SUBLANES, nqk), F32)]
    if not sample:
        scratch += [pltpu.VMEM((npair, LANES, LANES), F32),
                    pltpu.VMEM((1, nqk // 2), F32), pltpu.VMEM((1, nheads), F32)]
    return pl.pallas_call(
        kern,
        grid=grid,
        in_specs=in_specs,
        out_specs=[x_spec, c_spec, n_spec, m_spec, cv_spec],
        out_shape=[jax.ShapeDtypeStruct((n, d), F32),
                   jax.ShapeDtypeStruct((nseq, nheads, ML_DK, ML_DV), F32),
                   jax.ShapeDtypeStruct((nseq, 1, nqk // 2), F32),
                   jax.ShapeDtypeStruct(m_shape, F32),
                   jax.ShapeDtypeStruct(cv_shape, F32)],
        scratch_shapes=scratch,
        compiler_params=pltpu.CompilerParams(dimension_semantics=sem, vmem_limit_bytes=VMEM_LIMIT),
        name="mlstm_sample" if sample else "mlstm_prompt",
    )(*args)


def kernel(x_prompt, x_sample, state_rwkv_S, state_rwkv_shift, state_mlstm_C, state_mlstm_n, state_mlstm_m,
           state_mlstm_conv, norm_ffa, ffa_wg, ffa_wu, ffa_wd, norm_mix, norm_ffb, ffb_wg, ffb_wu, ffb_wd,
           rw_mu, rw_wr, rw_wk, rw_wv, rw_wo, rw_w0, rw_w1, rw_w2, rw_a0, rw_a1, rw_a2, rw_g1, rw_g2,
           rw_k_k, rw_k_a, rw_r_k, rw_gn_w, rw_gn_b, ml_w_in, ml_b_if, ml_conv_w, ml_conv_b, ml_norm_w,
           ml_w_out, norm_final):
    bp, tp, d = x_prompt.shape
    bs, ts, _ = x_sample.shape
    depth = norm_ffa.shape[0]
    slot = SUBLANES
    npad = slot - ts
    assert 0 < ts <= slot and npad >= ML_CONV - 1
    tm_p = min(256, tp)
    tm_s = min(2 * CHUNK, bs * slot)
    assert tp % tm_p == 0 and tm_p % CHUNK == 0 and (bs * slot) % tm_s == 0 and tm_s % CHUNK == 0
    ml_heads = ml_b_if.shape[1] // 2
    nqk = 2 * ml_heads * ML_DK
    nv = ml_heads * ML_DV

    xp = x_prompt.reshape(bp * tp, d)
    xs = jnp.concatenate([jnp.zeros((bs, npad, d), F32), x_sample], axis=1).reshape(bs * slot, d)
    row2 = lambda a: a.reshape(1, -1)

    tm_ffn = min(512, bs * slot)
    assert (bp * tp) % tm_ffn == 0 and (bs * slot) % tm_ffn == 0
    ffa = (norm_ffa.reshape(depth, 1, d), ffa_wg, ffa_wu, ffa_wd)
    ffb = (norm_ffb.reshape(depth, 1, d), ffb_wg, ffb_wu, ffb_wd)

    def ffn(xp_, xs_, wset, layer, final):
        return _ffn_call(xp_, xs_, *wset, row2(norm_final), layer, final_norm=final, tm=tm_ffn)

    new_p = {k_: [] for k_ in ("S", "shift", "C", "n", "m", "conv")}
    new_s = {k_: [] for k_ in ("S", "shift", "C", "n", "m", "conv")}
    for i in range(depth):
        xp, xs = ffn(xp, xs, ffa, i, False)
        j = i // 2
        if i % 2 == 0:
            wts = [row2(norm_mix[i]), rw_mu[j], _b(rw_wr[j]), _b(rw_wk[j]), _b(rw_wv[j]), _b(rw_wo[j]),
                   row2(rw_w0[j]), _b(rw_w1[j]), _b(rw_w2[j]), row2(rw_a0[j]), _b(rw_a1[j]), _b(rw_a2[j]),
                   _b(rw_g1[j]), _b(rw_g2[j]), row2(rw_k_k[j]), row2(rw_k_a[j]), row2(rw_r_k[j]),
                   row2(rw_gn_w[j]), row2(rw_gn_b[j])]
            xp, sbd, tail = _rwkv_call(xp, None, None, wts, nseq=bp, seq_rows=tp, npad=0, sample=False, tm=tm_p)
            new_p["S"].append(sbd)
            new_p["shift"].append(tail.reshape(bp, SUBLANES, d)[:, SUBLANES - 1])
            shx = jnp.repeat(state_rwkv_shift[j], slot, axis=0)
            xs, sbd, xn = _rwkv_call(xs, shx, state_rwkv_S[j], wts, nseq=bs, seq_rows=slot,
                                     npad=npad, sample=True, tm=tm_s)
            new_s["S"].append(sbd)
            new_s["shift"].append(xn.reshape(bs, slot, d)[:, slot - 1])
        else:
            w_in = ml_w_in[j]
            w_if = w_in[:, nqk + nv + d:]
            wts = [row2(norm_mix[i]), _b(w_in[:, :nqk]), _b(w_in[:, nqk:nqk + nv]), _b(w_in[:, nqk + nv:nqk + nv + d]),
                   _b(w_if), _b(w_if.T), row2(ml_b_if[j]), ml_b_if[j].reshape(-1, 1), ml_conv_w[j],
                   row2(ml_conv_b[j]), row2(ml_norm_w[j]), _b(ml_w_out[j])]
            xp, cp, n_, m_, tail = _mlstm_call(xp, None, None, None, None, wts, nseq=bp, seq_rows=tp, npad=0,
                                               sample=False, tm=tm_p)
            new_p["C"].append(jnp.swapaxes(cp, -1, -2))
            new_p["n"].append(n_.reshape(bp, ml_heads, ML_DK))
            new_p["m"].append(m_.reshape(bp, ml_heads))
            new_p["conv"].append(tail.reshape(bp, SUBLANES, nqk)[:, SUBLANES - (ML_CONV - 1):])
            conv0 = state_mlstm_conv[j]
            cvx = jnp.concatenate([jnp.zeros((bs, npad - (ML_CONV - 1), nqk), F32), conv0,
                                   jnp.zeros((bs, slot - npad, nqk), F32)], axis=1).reshape(bs * slot, nqk)
            mrow = jnp.repeat(state_mlstm_m[j], slot, axis=0)
            xs, cp, n_, mt, raw = _mlstm_call(xs, cvx, mrow, jnp.swapaxes(state_mlstm_C[j], -1, -2),
                                              state_mlstm_n[j].reshape(bs, 1, ml_heads * ML_DK), wts, nseq=bs,
                                              seq_rows=slot, npad=npad, sample=True, tm=tm_s)
            new_s["C"].append(jnp.swapaxes(cp, -1, -2))
            new_s["n"].append(n_.reshape(bs, ml_heads, ML_DK))
            new_s["m"].append(mt.reshape(bs, slot, ml_heads)[:, slot - 1])
            new_s["conv"].append(raw.reshape(bs, slot, nqk)[:, slot - (ML_CONV - 1):])
        xp, xs = ffn(xp, xs, ffb, i, i == depth - 1)
    y_prompt = xp.reshape(bp, tp, d)
    y_sample = xs.reshape(bs, slot, d)[:, npad:]
    st = lambda lst: jnp.stack(lst)
    return (y_prompt, y_sample,
            st(new_p["S"]), st(new_p["shift"]), st(new_p["C"]), st(new_p["n"]), st(new_p["m"]), st(new_p["conv"]),
            st(new_s["S"]), st(new_s["shift"]), st(new_s["C"]), st(new_s["n"]), st(new_s["m"]), st(new_s["conv"]))
```

```python
import functools
import math

import jax
import jax.numpy as jnp
from jax import lax
from jax.experimental import pallas as pl
from jax.experimental.pallas import tpu as pltpu

F32 = jnp.float32
BF16 = jnp.bfloat16

NORM_EPS = 1e-6
RW_GN_EPS = 64e-5
RW_HEAD = 64
ML_DK = 64
ML_DV = 128
ML_CONV = 4

LANES = 128
SUBLANES = 8
MXU_COLS = 256
CHUNK = 64
NEG_BIG = -1e30
VMEM_LIMIT = 58 * 1024 * 1024
HI = lax.Precision.HIGHEST


def _dot(a, b, precision=None):
    return jnp.dot(a, b, preferred_element_type=F32, precision=precision)


def _dot_nt(a, b, precision=None):
    return lax.dot_general(a, b, (((1,), (1,)), ((), ())), preferred_element_type=F32, precision=precision)


def _dot_tn(a, b, precision=None):
    return lax.dot_general(a, b, (((0,), (0,)), ((), ())), preferred_element_type=F32, precision=precision)


def _b(x):
    return x.astype(BF16)


def _rms(x, g):
    return x * lax.rsqrt(jnp.mean(x * x, axis=-1, keepdims=True) + NORM_EPS) * g


def _sigmoid(x):
    return 1.0 / (1.0 + jnp.exp(-x))


def _softplus(x):
    return jnp.maximum(x, 0.0) + jnp.log1p(jnp.exp(-jnp.abs(x)))


def _segsum(x, mlo):
    lo = jnp.sum(jnp.where(mlo, x, 0.0), axis=-1, keepdims=True)
    hi = jnp.sum(jnp.where(mlo, 0.0, x), axis=-1, keepdims=True)
    return jnp.where(mlo, lo, hi)


def _colsel(x, lane_idx, j):
    return jnp.sum(jnp.where(lane_idx == j, x, 0.0), axis=-1, keepdims=True)


def _seq_last(x, lsub):
    n = x.shape[0]
    if lsub == n:
        return x[n - 1:n]
    parts = [jnp.broadcast_to(x[q * lsub + lsub - 1:q * lsub + lsub], (lsub,) + x.shape[1:])
             for q in range(n // lsub)]
    return jnp.concatenate(parts, axis=0)


def _blockdiag(a, b):
    z = jnp.zeros(a.shape, a.dtype)
    return jnp.concatenate([jnp.concatenate([a, z], axis=1), jnp.concatenate([z, b], axis=1)], axis=0)


def _const_spec(shape, single_buffer=False):
    nd = len(shape)
    if single_buffer:
        return pl.BlockSpec(shape, lambda *_: (0,) * nd, pipeline_mode=pl.Buffered(1))
    return pl.BlockSpec(shape, lambda *_: (0,) * nd)


def _log2(n):
    k = int(math.log2(n))
    assert 1 << k == n, n
    return k


def _ffn_kernel(xp_ref, xs_ref, g_ref, wg_ref, wu_ref, wd_ref, gf_ref, op_ref, os_ref, *, fchunk, final_norm, np_steps):
    is_p = pl.program_id(0) < np_steps
    x = jnp.where(is_p, xp_ref[...], xs_ref[...])
    xb = _b(_rms(x, g_ref[...]))
    acc = jnp.zeros(x.shape, F32)
    for lo, hi in zip(fchunk[:-1], fchunk[1:]):
        sl = slice(lo, hi)
        gate = _dot(xb, wg_ref[:, sl])
        up = _dot(xb, wu_ref[:, sl])
        h = _b(gate * _sigmoid(gate) * up)
        acc = acc + _dot(h, wd_ref[sl, :])
    out = x + 0.5 * acc
    if final_norm:
        out = _rms(out, gf_ref[...])

    @pl.when(is_p)
    def _():
        op_ref[...] = out

    @pl.when(jnp.logical_not(is_p))
    def _():
        os_ref[...] = out


def _ffn_call(xp, xs, g_all, wg_all, wu_all, wd_all, gf, layer, *, final_norm, tm):
    (n_p, d), n_s = xp.shape, xs.shape[0]
    nf = wg_all.shape[2]
    np_steps, ns_steps = n_p // tm, n_s // tm
    ntile = nf // MXU_COLS
    fchunk = (0, -(-ntile // 2) * MXU_COLS, nf) if nf % MXU_COLS == 0 and ntile > 1 else (0, nf)
    kern = functools.partial(_ffn_kernel, fchunk=fchunk, final_norm=final_norm, np_steps=np_steps)
    p_spec = pl.BlockSpec((tm, d), lambda i: (jnp.minimum(i, np_steps - 1), 0))
    s_spec = pl.BlockSpec((tm, d), lambda i: (jnp.maximum(i - np_steps, 0), 0))
    w_spec = lambda a, b: pl.BlockSpec((None, a, b), lambda i: (layer, 0, 0), pipeline_mode=pl.Buffered(1))
    return pl.pallas_call(
        kern,
        grid=(np_steps + ns_steps,),
        in_specs=[p_spec, s_spec, pl.BlockSpec((None, 1, d), lambda i: (layer, 0, 0)),
                  w_spec(d, nf), w_spec(d, nf), w_spec(nf, d), _const_spec((1, d))],
        out_specs=[p_spec, s_spec],
        out_shape=[jax.ShapeDtypeStruct((n_p, d), F32), jax.ShapeDtypeStruct((n_s, d), F32)],
        compiler_params=pltpu.CompilerParams(dimension_semantics=("arbitrary",), vmem_limit_bytes=VMEM_LIMIT),
        name="ffn",
    )(xp, xs, g_all, wg_all, wu_all, wd_all, gf)


def _rw_token_part(xn, xx, row, mlo, put, seq_rows, npad, wrefs):
    mu_ref, wr_ref, wk_ref, wv_ref, w0_ref, w1_ref, w2_ref, a0_ref, a1_ref, a2_ref, g1_ref, g2_ref, kk_ref, ka_ref, rk_ref = wrefs
    cols = [slice(p * LANES, (p + 1) * LANES) for p in range(xn.shape[1] // LANES)]

    def put_all(name, z):
        for p, cs in enumerate(cols):
            put(name, p, z[:, cs])

    def mix(i):
        return _b(xn + xx * mu_ref[i:i + 1, :])

    r = _dot(mix(0), wr_ref[...])
    put_all("r", r)
    wl = _b(jnp.tanh(_dot(mix(1), w1_ref[...])))
    ld = -math.exp(-0.5) * _sigmoid(w0_ref[...] + _dot(wl, w2_ref[...]))
    k = _dot(mix(2), wk_ref[...])
    v = _dot(mix(3), wv_ref[...])
    al = _b(_dot(mix(4), a1_ref[...]))
    a = _sigmoid(a0_ref[...] + _dot(al, a2_ref[...]))
    gl = _b(_sigmoid(_dot(mix(5), g1_ref[...])))
    put_all("g", _dot(gl, g2_ref[...]))
    k2 = k * (1.0 + (a - 1.0) * ka_ref[...])
    kk = k * kk_ref[...]
    rkk = r * k2 * rk_ref[...]
    if npad:
        keep = (row & (seq_rows - 1)) >= npad
        ld = jnp.where(keep, ld, 0.0)
        k2 = jnp.where(keep, k2, 0.0)
        kk = jnp.where(keep, kk, 0.0)
        v = jnp.where(keep, v, 0.0)
    put_all("ld", ld)
    put_all("k", k2)
    put_all("v", v)
    for p, cs in enumerate(cols):
        kkp = kk[:, cs]
        kn = kkp / jnp.maximum(jnp.sqrt(_segsum(kkp * kkp, mlo)), 1e-12)
        put("kn", p, kn)
        put("b", p, kn * a[:, cs])
        put("bon", p, _segsum(rkk[:, cs], mlo) * v[:, cs])


def _rwkv_kernel(*refs, tm, seq_rows, npad, sample, lsub):
    L = CHUNK
    nsub = L // lsub
    nchunk = tm // L
    nsteps = _log2(lsub)
    it = iter(refs)
    x_ref = next(it)
    shx_ref = next(it) if sample else None
    (gn_ref, mu_ref, wr_ref, wk_ref, wv_ref, wo_ref, w0_ref, w1_ref, w2_ref, a0_ref, a1_ref, a2_ref,
     g1_ref, g2_ref, kk_ref, ka_ref, rk_ref, gw_ref, gb_ref) = [next(it) for _ in range(19)]
    s0_ref = next(it) if sample else None
    o_ref, sout_ref, xn_ref = next(it), next(it), next(it)
    set_a = [next(it) for _ in range(8)]
    yg_s = next(it)
    if not sample:
        carry_s, sbd_s = next(it), next(it)
    d = x_ref.shape[1]
    npair = d // LANES

    lane = lax.broadcasted_iota(jnp.int32, (1, LANES), 1)
    mlo = lane < RW_HEAD
    row = lax.broadcasted_iota(jnp.int32, (tm, 1), 0)

    xn = _rms(x_ref[...], gn_ref[...])
    rolled = pltpu.roll(xn, 1, 0)
    if sample:
        sbd_s = None
        xprev = jnp.where((row & (seq_rows - 1)) == npad, shx_ref[...], rolled)
        xn_ref[...] = xn
    else:
        @pl.when(pl.program_id(1) == 0)
        def _():
            carry_s[...] = jnp.zeros(carry_s.shape, F32)
            sbd_s[...] = jnp.zeros(sbd_s.shape, F32)

        xprev = jnp.where(row == 0, carry_s[SUBLANES - 1:SUBLANES, :], rolled)
        carry_s[...] = xn[tm - SUBLANES:tm, :]
        xn_ref[...] = xn[tm - SUBLANES:tm, :]
    sref = dict(zip(("r", "ld", "k", "v", "kn", "b", "g", "bon"), set_a))

    def put(name, p, tile):
        sref[name][:, p * LANES:(p + 1) * LANES] = tile

    _rw_token_part(xn, xprev - xn, row, mlo, put, seq_rows, npad,
                   (mu_ref, wr_ref, wk_ref, wv_ref, w0_ref, w1_ref, w2_ref, a0_ref, a1_ref, a2_ref,
                    g1_ref, g2_ref, kk_ref, ka_ref, rk_ref))
    _rw_back(set_a, sbd_s, sample=sample, lsub=lsub, nchunk=nchunk, nsub=nsub, nsteps=nsteps, npair=npair, mlo=mlo,
             s0_ref=s0_ref, sout_ref=sout_ref, yg_s=yg_s, gw_ref=gw_ref, gb_ref=gb_ref, wo_ref=wo_ref,
             xres_ref=x_ref, o_ref=o_ref)


def _rw_back(sset, sbd_s, *, sample, lsub, nchunk, nsub, nsteps, npair, mlo, s0_ref, sout_ref, yg_s,
             gw_ref, gb_ref, wo_ref, xres_ref, o_ref):
    r_s, ld_s, k_s, v_s, kn_s, b_s, g_s, bon_s = sset
    L = CHUNK
    sh = _log2(lsub)
    ti = lax.broadcasted_iota(jnp.int32, (L, 3 * L), 0)
    si = lax.broadcasted_iota(jnp.int32, (L, 3 * L), 1) & (L - 1)
    tril3 = jnp.where(((ti >> sh) == (si >> sh)) & (si <= ti), 1.0, 0.0).astype(BF16)
    gi = lax.broadcasted_iota(jnp.int32, (2 * L, 4 * L), 0)
    gj = lax.broadcasted_iota(jnp.int32, (2 * L, 4 * L), 1)
    gt = gi & (L - 1)
    gs = gj & (L - 1)
    gmask = ((gt >> sh) == (gs >> sh)) & ((gs < gt) | ((gi >= L) & (gs == gt)))
    pairs = range(npair)

    def stack2(z):
        return jnp.concatenate([jnp.where(mlo, z, 0), jnp.where(mlo, 0, z)], axis=0)

    cs = [slice(p * LANES, (p + 1) * LANES) for p in pairs]
    rw = {(c, p): slice(c * L, (c + 1) * L) for c in range(nchunk) for p in pairs}
    cum, a2f, a2, q4, vv2f, vv2, makv, mr, wcat, pl_, qpl = ({} for _ in range(11))

    def phase_a(chunks):
        chains = [(c, p) for c in chunks for p in pairs]
        for key in chains:
            ldc = ld_s[rw[key], cs[key[1]]]
            hi = _b(ldc)
            r1 = ldc - hi.astype(F32)
            mid = _b(r1)
            lo = _b(r1 - mid.astype(F32))
            cum[key] = _dot(tril3, jnp.concatenate([hi, mid, lo], axis=0))
        for key in chains:
            rows, c_ = rw[key], cs[key[1]]
            ep = jnp.exp(cum[key])
            em = jnp.exp(-cum[key])
            at = -(kn_s[rows, c_] * jnp.exp(cum[key] - ld_s[rows, c_]))
            a2f[key] = jnp.concatenate([at, r_s[rows, c_] * ep], axis=0)
            a2[key] = _b(a2f[key])
            q4[key] = jnp.concatenate([stack2(b_s[rows, c_] * em), stack2(k_s[rows, c_] * em)], axis=0)
            vv2f[key] = stack2(v_s[rows, c_])
            vv2[key] = _b(vv2f[key])
        g = {key: jnp.where(gmask, _dot_nt(a2[key], _b(q4[key])), 0.0) for key in chains}
        for key in chains:
            makv[key] = _dot(_b(g[key][:, LANES:]), vv2[key])
            mr[key] = _b(g[key][L:, :LANES])
        w = {key: g[key][:L, :LANES] for key in chains}
        pf = {}
        for key in chains:
            pc = _b(w[key])
            pf[key] = _dot(pc, stack2(pc))
        for k in range(1, nsteps):
            for key in chains:
                pc = _b(pf[key])
                if k + 1 < nsteps:
                    both = _dot(pc, jnp.concatenate([stack2(pc), stack2(_b(w[key]))], axis=1))
                    w[key] = w[key] + pf[key] + both[:, LANES:]
                    pf[key] = both[:, :LANES]
                else:
                    w[key] = w[key] + pf[key] + _dot(pc, stack2(_b(w[key])))
        for key in chains:
            wcat[key] = _b(w[key])
            if not sample:
                pl_[key] = jnp.exp(cum[key][L - 1:L, :])
                qpl[key] = _b(q4[key] * pl_[key])

    if not sample:
        state = [sbd_s[p] for p in pairs]
    phase_a(range(nchunk))
    for c in range(nchunk):
        keys = [(c, p) for p in pairs]
        if sample:
            s_in = [[_blockdiag(s0_ref[c * nsub + q, 2 * p], s0_ref[c * nsub + q, 2 * p + 1]) for q in range(nsub)]
                    for p in pairs]
            as_a, as_r = [], []
            for p in pairs:
                asa, asr = [], []
                for q in range(nsub):
                    a2q = _b(jnp.concatenate([a2f[c, p][q * lsub:(q + 1) * lsub],
                                              a2f[c, p][L + q * lsub:L + (q + 1) * lsub]], axis=0))
                    asq = _dot_nt(a2q, _b(s_in[p][q]))
                    asa.append(asq[:lsub])
                    asr.append(asq[lsub:])
                as_a.append(jnp.concatenate(asa, axis=0))
                as_r.append(jnp.concatenate(asr, axis=0))
        else:
            as_ = [_dot_nt(a2[key], _b(state[key[1]])) for key in keys]
            as_a = [z[:L] for z in as_]
            as_r = [z[L:] for z in as_]
        rhs = [as_a[p] + makv[c, p][:L] for p in pairs]
        u = [rhs[p] + _dot(wcat[c, p], stack2(_b(rhs[p]))) for p in pairs]
        u2 = [stack2(_b(u[p])) for p in pairs]
        if sample:
            uvf = [jnp.concatenate([stack2(u[p]), vv2f[c, p]], axis=0) for p in pairs]
        else:
            uv = [jnp.concatenate([u2[p], vv2[c, p]], axis=0) for p in pairs]
        y = [as_r[p] + makv[c, p][L:] + _dot(mr[c, p], u2[p]) for p in pairs]
        for p in pairs:
            if sample:
                for q in range(nsub):
                    sel = [slice(blk * L + q * lsub, blk * L + (q + 1) * lsub) for blk in range(4)]
                    pl_q = jnp.exp(cum[c, p][(q + 1) * lsub - 1:(q + 1) * lsub, :])
                    uvq = _b(jnp.concatenate([uvf[p][s_] for s_ in sel], axis=0))
                    qq = _b(jnp.concatenate([q4[c, p][s_] for s_ in sel], axis=0) * pl_q)
                    s_new = s_in[p][q] * pl_q + _dot_tn(uvq, qq)
                    sout_ref[c * nsub + q, 2 * p] = s_new[:RW_HEAD, :RW_HEAD]
                    sout_ref[c * nsub + q, 2 * p + 1] = s_new[RW_HEAD:, RW_HEAD:]
            else:
                state[p] = state[p] * pl_[c, p] + _dot_tn(uv[p], qpl[c, p])
        rows = rw[c, 0]
        for p in pairs:
            mean = _segsum(y[p], mlo) * (1.0 / RW_HEAD)
            yc = y[p] - mean
            var = _segsum(yc * yc, mlo) * (1.0 / RW_HEAD)
            yn = yc * lax.rsqrt(var + RW_GN_EPS) * gw_ref[:, cs[p]] + gb_ref[:, cs[p]] + bon_s[rows, cs[p]]
            yg_s[rows, cs[p]] = _b(yn * g_s[rows, cs[p]])
    if not sample:
        for p in pairs:
            sbd_s[p] = state[p]
            sout_ref[0, 2 * p] = state[p][:RW_HEAD, :RW_HEAD]
            sout_ref[0, 2 * p + 1] = state[p][RW_HEAD:, RW_HEAD:]
    o_ref[...] = xres_ref[...] + _dot(yg_s[...], wo_ref[...])


def _rwkv_call(x, shx, s0bd, wts, *, nseq, seq_rows, npad, sample, tm):
    n, d = x.shape
    npair = d // LANES
    lsub = seq_rows if sample else CHUNK
    kern = functools.partial(_rwkv_kernel, tm=tm, seq_rows=seq_rows, npad=npad, sample=sample, lsub=lsub)
    if sample:
        grid = (n // tm,)
        x_spec = pl.BlockSpec((tm, d), lambda i: (i, 0))
        spt = tm // seq_rows
        st_spec = pl.BlockSpec((spt, 2 * npair, RW_HEAD, RW_HEAD), lambda i: (i, 0, 0, 0))
        xn_spec = x_spec
        xn_shape = (n, d)
        in_specs = [x_spec, x_spec]
        args = [x, shx]
        sem = ("arbitrary",)
    else:
        tps = seq_rows // tm
        grid = (nseq, tps)
        x_spec = pl.BlockSpec((tm, d), lambda b, j: (b * tps + j, 0))
        st_spec = pl.BlockSpec((1, 2 * npair, RW_HEAD, RW_HEAD), lambda b, j: (b, 0, 0, 0))
        xn_spec = pl.BlockSpec((SUBLANES, d), lambda b, j: (b, 0))
        xn_shape = (nseq * SUBLANES, d)
        in_specs = [x_spec]
        args = [x]
        sem = ("arbitrary", "arbitrary")
    for wt in wts:
        in_specs.append(_const_spec(wt.shape, single_buffer=True))
        args.append(wt)
    if sample:
        in_specs.append(st_spec)
        args.append(s0bd)
    scratch = [pltpu.VMEM((tm, d), F32) for _ in range(8)] + [pltpu.VMEM((tm, d), BF16)]
    if not sample:
        scratch += [pltpu.VMEM((SUBLANES, d), F32), pltpu.VMEM((npair, LANES, LANES), F32)]
    return pl.pallas_call(
        kern,
        grid=grid,
        in_specs=in_specs,
        out_specs=[x_spec, st_spec, xn_spec],
        out_shape=[jax.ShapeDtypeStruct((n, d), F32),
                   jax.ShapeDtypeStruct((nseq, 2 * npair, RW_HEAD, RW_HEAD), F32),
                   jax.ShapeDtypeStruct(xn_shape, F32)],
        scratch_shapes=scratch,
        compiler_params=pltpu.CompilerParams(dimension_semantics=sem, vmem_limit_bytes=VMEM_LIMIT),
        name="rwkv_sample" if sample else "rwkv_prompt",
    )(*args)


def _rwkv_decode_kernel(*refs, nb, nt):
    it = iter(refs)
    x_ref, sh_ref = next(it), next(it)
    (gn_ref, mu_ref, wr_ref, wk_ref, wv_ref, wo_ref, w0_ref, w1_ref, w2_ref, a0_ref, a1_ref, a2_ref,
     g1_ref, g2_ref, kk_ref, ka_ref, rk_ref, gw_ref, gb_ref) = [next(it) for _ in range(19)]
    s0_ref = next(it)
    o_ref, sout_ref, xn_ref = next(it), next(it), next(it)
    tr = dict(zip(("r", "ld", "k", "v", "kn", "b"), [next(it) for _ in range(6)]))
    yt_s, g_s, bon_s, yg_s = [next(it) for _ in range(4)]
    n, d = x_ref.shape
    npair = d // LANES
    nhead = 2 * npair
    h = pl.program_id(0)
    lane = lax.broadcasted_iota(jnp.int32, (1, LANES), 1)
    mlo = lane < RW_HEAD

    @pl.when(h == 0)
    def _():
        xn = _rms(x_ref[...], gn_ref[...])
        xprev = jnp.concatenate([sh_ref[...], xn[:n - nb]], axis=0)
        xn_ref[...] = xn[n - nb:]

        def put(name, p, tile):
            if name == "g":
                g_s[:, p * LANES:(p + 1) * LANES] = tile
            elif name == "bon":
                bon_s[:, p * LANES:(p + 1) * LANES] = tile
            else:
                tr[name][p] = (jnp.exp(tile) if name == "ld" else tile).T

        _rw_token_part(xn, xprev - xn, None, mlo, put, 0, 0,
                       (mu_ref, wr_ref, wk_ref, wv_ref, w0_ref, w1_ref, w2_ref, a0_ref, a1_ref, a2_ref,
                        g1_ref, g2_ref, kk_ref, ka_ref, rk_ref))

    p = h >> 1
    base = pl.multiple_of((h & 1) * RW_HEAD, RW_HEAD)
    sub = lax.broadcasted_iota(jnp.int32, (SUBLANES, 1), 0)
    hrows = pl.ds(base, RW_HEAD)

    def group(gi, carry):
        r0 = pl.multiple_of(base + gi * SUBLANES, SUBLANES)
        vg = [tr["v"][p, pl.ds(r0, SUBLANES), t * nb:(t + 1) * nb] for t in range(nt)]
        ys = [jnp.zeros((SUBLANES, nb), F32) for _ in range(nt)]
        for vi in range(SUBLANES):
            v_idx = gi * SUBLANES + vi
            s = s0_ref[0, v_idx]
            for t in range(nt):
                tc = slice(t * nb, (t + 1) * nb)
                sa = -jnp.sum(s * tr["kn"][p, hrows, tc], axis=0, keepdims=True)
                s = s * tr["ld"][p, hrows, tc] + sa * tr["b"][p, hrows, tc] + vg[t][vi:vi + 1, :] * tr["k"][p, hrows, tc]
                yrow = jnp.sum(s * tr["r"][p, hrows, tc], axis=0, keepdims=True)
                ys[t] = jnp.where(sub == vi, yrow, ys[t])
            sout_ref[0, v_idx] = s
        for t in range(nt):
            yt_s[p, pl.ds(r0, SUBLANES), t * nb:(t + 1) * nb] = ys[t]
        return carry

    lax.fori_loop(0, RW_HEAD // SUBLANES, group, 0)

    @pl.when(h == nhead - 1)
    def _():
        for q in range(npair):
            cs = slice(q * LANES, (q + 1) * LANES)
            y = yt_s[q].T
            mean = _segsum(y, mlo) * (1.0 / RW_HEAD)
            yc = y - mean
            var = _segsum(yc * yc, mlo) * (1.0 / RW_HEAD)
            yn = yc * lax.rsqrt(var + RW_GN_EPS) * gw_ref[:, cs] + gb_ref[:, cs] + bon_s[:, cs]
            yg_s[:, cs] = _b(yn * g_s[:, cs])
        o_ref[...] = x_ref[...] + _dot(yg_s[...], wo_ref[...])


def _rwkv_decode_call(xt, shift0, s0t, wts, *, nb, nt):
    n, d = xt.shape
    npair = d // LANES
    nhead = 2 * npair
    kern = functools.partial(_rwkv_decode_kernel, nb=nb, nt=nt)
    full = lambda shape: pl.BlockSpec(shape, lambda h: (0,) * len(shape))
    st_spec = pl.BlockSpec((1, RW_HEAD, RW_HEAD, nb), lambda h: (h, 0, 0, 0))
    in_specs = [full((n, d)), full((nb, d))] + [_const_spec(wt.shape, single_buffer=True) for wt in wts] + [st_spec]
    scratch = ([pltpu.VMEM((npair, LANES, n), F32) for _ in range(7)]
               + [pltpu.VMEM((n, d), F32), pltpu.VMEM((n, d), F32), pltpu.VMEM((n, d), BF16)])
    return pl.pallas_call(
        kern,
        grid=(nhead,),
        in_specs=in_specs,
        out_specs=[full((n, d)), st_spec, full((nb, d))],
        out_shape=[jax.ShapeDtypeStruct((n, d), F32), jax.ShapeDtypeStruct(s0t.shape, F32),
                   jax.ShapeDtypeStruct((nb, d), F32)],
        scratch_shapes=scratch,
        compiler_params=pltpu.CompilerParams(dimension_semantics=("arbitrary",), vmem_limit_bytes=VMEM_LIMIT),
        name="rwkv_decode",
    )(xt, shift0, *wts, s0t)


def _mlstm_kernel(*refs, tm, seq_rows, npad, sample, lsub):
    L = CHUNK
    nsub = L // lsub
    nchunk = tm // L
    it = iter(refs)
    x_ref = next(it)
    if sample:
        cvx_ref, mrow_ref = next(it), next(it)
    gn_ref, wqk_ref, wv_ref, wo_ref, wif_ref, bif_ref, cw_ref, cb_ref, nw_ref, wout_ref = [next(it) for _ in range(10)]
    if sample:
        c0_ref, n0_ref = next(it), next(it)
    o_ref, cout_ref, nout_ref, mout_ref, cvout_ref = [next(it) for _ in range(5)]
    gt_s, q_s, k_s, v_s, og_s, gc_s, ho_s, ext_s = [next(it) for _ in range(8)]
    if not sample:
        cp_s, np_s, m_s = [next(it) for _ in range(3)]
    nqk = wqk_ref.shape[1]
    half = nqk // 2
    nheads = half // ML_DK
    ngate = 2 * nheads
    npair = nheads // 2

    lane = lax.broadcasted_iota(jnp.int32, (1, LANES), 1)
    mlo = lane < ML_DK
    glane = lax.broadcasted_iota(jnp.int32, (1, ngate), 1)
    isf_c = glane >= nheads
    grow = lax.broadcasted_iota(jnp.int32, (ngate, 1), 0)
    isf_r = grow >= nheads
    hlane = lax.broadcasted_iota(jnp.int32, (1, nheads), 1)

    x = x_ref[...]
    xb = _b(_rms(x, gn_ref[...]))
    raw = _dot(xb, wqk_ref[...])
    row = lax.broadcasted_iota(jnp.int32, (tm, 1), 0)
    if sample:
        srow = row & (seq_rows - 1)
        raw = jnp.where((srow >= npad - (ML_CONV - 1)) & (srow < npad), cvx_ref[...], raw)
        cvout_ref[...] = raw
        ext_s[:SUBLANES, :] = jnp.zeros((SUBLANES, nqk), F32)
    else:
        j = pl.program_id(1)

        @pl.when(j == 0)
        def _():
            ext_s[:SUBLANES, :] = jnp.zeros((SUBLANES, nqk), F32)
            cp_s[...] = jnp.zeros(cp_s.shape, F32)
            np_s[...] = jnp.zeros(np_s.shape, F32)
            m_s[...] = jnp.zeros(m_s.shape, F32)

        cvout_ref[...] = raw[tm - SUBLANES:tm, :]
    ext_s[SUBLANES:, :] = raw
    qk = cb_ref[...] + cw_ref[ML_CONV - 1:ML_CONV, :] * raw
    for s in range(1, ML_CONV):
        qk = qk + cw_ref[ML_CONV - 1 - s:ML_CONV - s, :] * ext_s[SUBLANES - s:SUBLANES - s + tm, :]
    if not sample:
        ext_s[:SUBLANES, :] = raw[tm - SUBLANES:tm, :]
    qk = qk * _sigmoid(qk)
    q_s[...] = qk[:, :half] * (ML_DK ** -0.5)
    k_s[...] = qk[:, half:]
    v_s[...] = _dot(xb, wv_ref[...])
    og_s[...] = _sigmoid(_dot(xb, wo_ref[...]))
    ifp = _dot(xb, wif_ref[...]) + bif_ref[...]
    gcol = jnp.where(isf_c, -_softplus(-ifp), ifp)
    if npad:
        keep = (row & (seq_rows - 1)) >= npad
        gcol = jnp.where(keep, gcol, jnp.where(isf_c, 0.0, NEG_BIG))
    gc_s[...] = gcol
    gct = gcol.T
    for c in range(nchunk):
        gt_s[c] = gct[:, c * L:(c + 1) * L]

    ti = lax.broadcasted_iota(jnp.int32, (L, L), 0)
    si = lax.broadcasted_iota(jnp.int32, (L, L), 1)
    sh = _log2(lsub)
    same = (ti >> sh) == (si >> sh)
    causal = same & (si <= ti)
    tril = jnp.where(causal, 1.0, 0.0).astype(F32)
    triu = jnp.where(same & (ti <= si), 1.0, 0.0).astype(F32)

    def chunk(c, carry):
        r0 = pl.multiple_of(c * L, L)
        rows = pl.ds(r0, L)
        gc = gc_s[rows, :]
        gt = gt_s[c]
        bcs = _dot(tril, jnp.where(isf_c, gc, 0.0), HI)
        brs = _dot(jnp.where(isf_r, gt, 0.0), triu, HI)
        blast = _seq_last(bcs, lsub)
        if sample:
            mcols = mrow_ref[rows, :]
        else:
            mcols = m_s[...]
        heads = range(nheads)
        prs = range(npair)
        ps = [slice(pp * LANES, (pp + 1) * LANES) for pp in prs]
        hs = [slice(h * ML_DV, (h + 1) * ML_DV) for h in heads]
        q2 = [q_s[rows, ps[pp]] for pp in prs]
        k2 = [k_s[rows, ps[pp]] for pp in prs]
        k2b = [_b(z) for z in k2]
        if sample:
            c_in = [[jnp.concatenate([c0_ref[c * nsub + q, 2 * pp], c0_ref[c * nsub + q, 2 * pp + 1]], axis=0)
                     for q in range(nsub)] for pp in prs]
            n_in = [[n0_ref[c * nsub + q][:, ps[pp]] for q in range(nsub)] for pp in prs]
            n_rows = [jnp.concatenate([jnp.broadcast_to(n_in[pp][q], (lsub, LANES)) for q in range(nsub)], axis=0)
                      for pp in prs]
        else:
            c_prev = [cp_s[pp] for pp in prs]
            n_prev = [np_s[:, ps[pp]] for pp in prs]
            n_rows = n_prev
        bcol = [_colsel(bcs, glane, nheads + h) for h in heads]
        licol = [_colsel(gc, glane, h) for h in heads]
        mcol = [_colsel(mcols, hlane, h) for h in heads]
        blcol = [_colsel(blast, glane, nheads + h) for h in heads]
        dlog = [jnp.where(causal, bcol[h] - (brs[nheads + h:nheads + h + 1, :] - gt[h:h + 1, :]), -jnp.inf)
                for h in heads]
        ginter = [bcol[h] + mcol[h] for h in heads]
        m_t = [jnp.maximum(ginter[h], jnp.max(dlog[h], axis=-1, keepdims=True)) for h in heads]
        dw = [jnp.exp(dlog[h] - m_t[h]) for h in heads]
        winter = [jnp.exp(ginter[h] - m_t[h]) for h in heads]
        qh = [jnp.where(mlo if h % 2 == 0 else jnp.logical_not(mlo), q2[h // 2], 0.0) for h in heads]
        qhb = [_b(z) for z in qh]
        sc = [_dot_nt(qhb[h], k2b[h // 2]) * dw[h] for h in heads]
        if sample:
            inter = [jnp.concatenate([_dot(qhb[h][q * lsub:(q + 1) * lsub], _b(c_in[h // 2][q]))
                                      for q in range(nsub)], axis=0) for h in heads]
        else:
            cb = [_b(z) for z in c_prev]
            inter = [_dot(qhb[h], cb[h // 2]) for h in heads]
        vh = [v_s[rows, hs[h]] for h in heads]
        num = [winter[h] * inter[h] + _dot(_b(sc[h]), _b(vh[h])) for h in heads]
        den = [winter[h] * jnp.sum(qh[h] * n_rows[h // 2], axis=-1, keepdims=True)
               + jnp.sum(sc[h], axis=-1, keepdims=True) for h in heads]
        for h in heads:
            hout = num[h] / jnp.maximum(jnp.abs(den[h]), jnp.exp(-m_t[h]))
            hn = hout * lax.rsqrt(jnp.mean(hout * hout, axis=-1, keepdims=True) + NORM_EPS) * nw_ref[:, hs[h]]
            ho_s[rows, hs[h]] = _b(hn * og_s[rows, hs[h]])
        mnew = [_seq_last(m_t[h], lsub) for h in heads]
        ws = [jnp.exp(blcol[h] - bcol[h] + licol[h] - mnew[h]) for h in heads]
        wstc = [jnp.exp(blcol[h] + mcol[h] - mnew[h]) for h in heads]
        mt_all = jnp.zeros((L, nheads), F32)
        for h in heads:
            mt_all = jnp.where(hlane == h, m_t[h], mt_all)
        for pp in prs:
            lo, hi = 2 * pp, 2 * pp + 1
            wsk = jnp.where(mlo, ws[lo], ws[hi]) * k2[pp]
            wst = jnp.where(mlo, wstc[lo], wstc[hi])
            wvb = _b(jnp.concatenate([ws[lo] * vh[lo], ws[hi] * vh[hi]], axis=0))
            kmb = jnp.concatenate([jnp.where(mlo, k2b[pp], 0), jnp.where(mlo, 0, k2b[pp])], axis=0)
            if sample:
                for q in range(nsub):
                    last = (q + 1) * lsub - 1
                    sel = [slice(blk * L + q * lsub, blk * L + (q + 1) * lsub) for blk in range(2)]
                    wq = wst[last:last + 1, :]
                    wrow = jnp.concatenate([jnp.broadcast_to(wstc[h_][last:last + 1, :], (ML_DK, 1)) for h_ in (lo, hi)],
                                           axis=0)
                    upd = _dot_tn(jnp.concatenate([kmb[s_] for s_ in sel], axis=0),
                                  jnp.concatenate([wvb[s_] for s_ in sel], axis=0))
                    c_new = c_in[pp][q] * wrow + upd
                    cout_ref[c * nsub + q, 2 * pp] = c_new[:ML_DK]
                    cout_ref[c * nsub + q, 2 * pp + 1] = c_new[ML_DK:]
                    nq = n_in[pp][q] * wq + jnp.sum(wsk[q * lsub:(q + 1) * lsub], axis=0, keepdims=True)
                    nout_ref[c * nsub + q, :, ps[pp]] = nq
            else:
                wrow = jnp.concatenate([jnp.broadcast_to(wstc[h_], (ML_DK, 1)) for h_ in (lo, hi)], axis=0)
                cp_s[pp] = c_prev[pp] * wrow + _dot_tn(kmb, wvb)
                np_s[:, ps[pp]] = n_prev[pp] * wst + jnp.sum(wsk, axis=0, keepdims=True)
        if sample:
            mout_ref[rows, :] = mt_all
        else:
            m_s[...] = mt_all[L - 1:L, :]
        return carry

    lax.fori_loop(0, nchunk, chunk, 0)
    if not sample:
        for pp in range(npair):
            cout_ref[0, 2 * pp] = cp_s[pp][:ML_DK]
            cout_ref[0, 2 * pp + 1] = cp_s[pp][ML_DK:]
        nout_ref[0] = np_s[...]
        mout_ref[0] = m_s[...]
    o_ref[...] = x_ref[...] + _dot(ho_s[...], wout_ref[...])


def _mlstm_call(x, cvx, mrow, c0p, n0, wts, *, nseq, seq_rows, npad, sample, tm):
    n, d = x.shape
    nqk = wts[1].shape[1]
    nv = wts[2].shape[1]
    nheads = nqk // 2 // ML_DK
    npair = nheads // 2
    lsub = seq_rows if sample else CHUNK
    kern = functools.partial(_mlstm_kernel, tm=tm, seq_rows=seq_rows, npad=npad, sample=sample, lsub=lsub)
    if sample:
        grid = (n // tm,)
        tile = lambda i: (i, 0)
        spt = tm // seq_rows
        c_spec = pl.BlockSpec((spt, nheads, ML_DK, ML_DV), lambda i: (i, 0, 0, 0))
        n_spec = pl.BlockSpec((spt, 1, nqk // 2), lambda i: (i, 0, 0))
        m_spec = pl.BlockSpec((tm, nheads), tile)
        m_shape = (n, nheads)
        cv_spec = pl.BlockSpec((tm, nqk), tile)
        cv_shape = (n, nqk)
        sem = ("arbitrary",)
    else:
        tps = seq_rows // tm
        grid = (nseq, tps)
        tile = lambda b, j: (b * tps + j, 0)
        c_spec = pl.BlockSpec((1, nheads, ML_DK, ML_DV), lambda b, j: (b, 0, 0, 0))
        n_spec = pl.BlockSpec((1, 1, nqk // 2), lambda b, j: (b, 0, 0))
        m_spec = pl.BlockSpec((1, 1, nheads), lambda b, j: (b, 0, 0))
        m_shape = (nseq, 1, nheads)
        cv_spec = pl.BlockSpec((SUBLANES, nqk), lambda b, j: (b, 0))
        cv_shape = (nseq * SUBLANES, nqk)
        sem = ("arbitrary", "arbitrary")
    x_spec = pl.BlockSpec((tm, d), tile)
    in_specs = [x_spec]
    args = [x]
    if sample:
        in_specs += [pl.BlockSpec((tm, nqk), tile), pl.BlockSpec((tm, nheads), tile)]
        args += [cvx, mrow]
    for wt in wts:
        in_specs.append(_const_spec(wt.shape))
        args.append(wt)
    if sample:
        in_specs += [c_spec, n_spec]
        args += [c0p, n0]
    scratch = [pltpu.VMEM((tm // CHUNK, 2 * nheads, CHUNK), F32), pltpu.VMEM((tm, nqk // 2), F32),
               pltpu.VMEM((tm, nqk // 2), F32),
               pltpu.VMEM((tm, nv), F32), pltpu.VMEM((tm, nv), F32), pltpu.VMEM((tm, 2 * nheads), F32),
               pltpu.VMEM((tm, nv), BF16), pltpu.VMEM((tm + SUBLANES, nqk), F32)]
    if not sample:
        scratch += [pltpu.VMEM((npair, LANES, LANES), F32),
                    pltpu.VMEM((1, nqk // 2), F32), pltpu.VMEM((1, nheads), F32)]
    return pl.pallas_call(
        kern,
        grid=grid,
        in_specs=in_specs,
        out_specs=[x_spec, c_spec, n_spec, m_spec, cv_spec],
        out_shape=[jax.ShapeDtypeStruct((n, d), F32),
                   jax.ShapeDtypeStruct((nseq, nheads, ML_DK, ML_DV), F32),
                   jax.ShapeDtypeStruct((nseq, 1, nqk // 2), F32),
                   jax.ShapeDtypeStruct(m_shape, F32),
                   jax.ShapeDtypeStruct(cv_shape, F32)],
        scratch_shapes=scratch,
        compiler_params=pltpu.CompilerParams(dimension_semantics=sem, vmem_limit_bytes=VMEM_LIMIT),
        name="mlstm_sample" if sample else "mlstm_prompt",
    )(*args)


def kernel(x_prompt, x_sample, state_rwkv_S, state_rwkv_shift, state_mlstm_C, state_mlstm_n, state_mlstm_m,
           state_mlstm_conv, norm_ffa, ffa_wg, ffa_wu, ffa_wd, norm_mix, norm_ffb, ffb_wg, ffb_wu, ffb_wd,
           rw_mu, rw_wr, rw_wk, rw_wv, rw_wo, rw_w0, rw_w1, rw_w2, rw_a0, rw_a1, rw_a2, rw_g1, rw_g2,
           rw_k_k, rw_k_a, rw_r_k, rw_gn_w, rw_gn_b, ml_w_in, ml_b_if, ml_conv_w, ml_conv_b, ml_norm_w,
           ml_w_out, norm_final):
    bp, tp, d = x_prompt.shape
    bs, ts, _ = x_sample.shape
    depth = norm_ffa.shape[0]
    slot = SUBLANES
    npad = slot - ts
    assert 0 < ts <= slot and npad >= ML_CONV - 1
    tm_p = min(256, tp)
    tm_s = min(2 * CHUNK, bs * slot)
    assert tp % tm_p == 0 and tm_p % CHUNK == 0 and (bs * slot) % tm_s == 0 and tm_s % CHUNK == 0
    ml_heads = ml_b_if.shape[1] // 2
    nqk = 2 * ml_heads * ML_DK
    nv = ml_heads * ML_DV

    xp = x_prompt.reshape(bp * tp, d)
    xs = jnp.concatenate([jnp.zeros((bs, npad, d), F32), x_sample], axis=1).reshape(bs * slot, d)
    row2 = lambda a: a.reshape(1, -1)

    tm_ffn = min(512, bs * slot)
    assert (bp * tp) % tm_ffn == 0 and (bs * slot) % tm_ffn == 0
    ffa = (norm_ffa.reshape(depth, 1, d), _b(ffa_wg), _b(ffa_wu), _b(ffa_wd))
    ffb = (norm_ffb.reshape(depth, 1, d), _b(ffb_wg), _b(ffb_wu), _b(ffb_wd))

    def ffn(xp_, xs_, wset, layer, final):
        return _ffn_call(xp_, xs_, *wset, row2(norm_final), layer, final_norm=final, tm=tm_ffn)

    new_p = {k_: [] for k_ in ("S", "shift", "C", "n", "m", "conv")}
    new_s = {k_: [] for k_ in ("S", "shift", "C", "n", "m", "conv")}
    for i in range(depth):
        xp, xs = ffn(xp, xs, ffa, i, False)
        j = i // 2
        if i % 2 == 0:
            wts = [row2(norm_mix[i]), rw_mu[j], _b(rw_wr[j]), _b(rw_wk[j]), _b(rw_wv[j]), _b(rw_wo[j]),
                   row2(rw_w0[j]), _b(rw_w1[j]), _b(rw_w2[j]), row2(rw_a0[j]), _b(rw_a1[j]), _b(rw_a2[j]),
                   _b(rw_g1[j]), _b(rw_g2[j]), row2(rw_k_k[j]), row2(rw_k_a[j]), row2(rw_r_k[j]),
                   row2(rw_gn_w[j]), row2(rw_gn_b[j])]
            xp, sbd, tail = _rwkv_call(xp, None, None, wts, nseq=bp, seq_rows=tp, npad=0, sample=False, tm=tm_p)
            new_p["S"].append(sbd)
            new_p["shift"].append(tail.reshape(bp, SUBLANES, d)[:, SUBLANES - 1])
            xt = xs.reshape(bs, slot, d)[:, npad:].transpose(1, 0, 2).reshape(ts * bs, d)
            xt, s_t, sh_new = _rwkv_decode_call(xt, state_rwkv_shift[j], jnp.transpose(state_rwkv_S[j], (1, 2, 3, 0)),
                                                wts, nb=bs, nt=ts)
            xs = jnp.concatenate([jnp.zeros((bs, npad, d), F32), xt.reshape(ts, bs, d).transpose(1, 0, 2)],
                                 axis=1).reshape(bs * slot, d)
            new_s["S"].append(jnp.transpose(s_t, (3, 0, 1, 2)))
            new_s["shift"].append(sh_new)
        else:
            w_in = ml_w_in[j]
            w_if = w_in[:, nqk + nv + d:]
            wts = [row2(norm_mix[i]), _b(w_in[:, :nqk]), _b(w_in[:, nqk:nqk + nv]), _b(w_in[:, nqk + nv:nqk + nv + d]),
                   _b(w_if), row2(ml_b_if[j]), ml_conv_w[j],
                   row2(ml_conv_b[j]), row2(ml_norm_w[j]), _b(ml_w_out[j])]
            xp, cp, n_, m_, tail = _mlstm_call(xp, None, None, None, None, wts, nseq=bp, seq_rows=tp, npad=0,
                                               sample=False, tm=tm_p)
            new_p["C"].append(jnp.swapaxes(cp, -1, -2))
            new_p["n"].append(n_.reshape(bp, ml_heads, ML_DK))
            new_p["m"].append(m_.reshape(bp, ml_heads))
            new_p["conv"].append(tail.reshape(bp, SUBLANES, nqk)[:, SUBLANES - (ML_CONV - 1):])
            conv0 = state_mlstm_conv[j]
            cvx = jnp.concatenate([jnp.zeros((bs, npad - (ML_CONV - 1), nqk), F32), conv0,
                                   jnp.zeros((bs, slot - npad, nqk), F32)], axis=1).reshape(bs * slot, nqk)
            mrow = jnp.repeat(state_mlstm_m[j], slot, axis=0)
            xs, cp, n_, mt, raw = _mlstm_call(xs, cvx, mrow, jnp.swapaxes(state_mlstm_C[j], -1, -2),
                                              state_mlstm_n[j].reshape(bs, 1, ml_heads * ML_DK), wts, nseq=bs,
                                              seq_rows=slot, npad=npad, sample=True, tm=tm_s)
            new_s["C"].append(jnp.swapaxes(cp, -1, -2))
            new_s["n"].append(n_.reshape(bs, ml_heads, ML_DK))
            new_s["m"].append(mt.reshape(bs, slot, ml_heads)[:, slot - 1])
            new_s["conv"].append(raw.reshape(bs, slot, nqk)[:, slot - (ML_CONV - 1):])
        xp, xs = ffn(xp, xs, ffb, i, i == depth - 1)
    y_prompt = xp.reshape(bp, tp, d)
    y_sample = xs.reshape(bs, slot, d)[:, npad:]
    st = lambda lst: jnp.stack(lst)
    return (y_prompt, y_sample,
            st(new_p["S"]), st(new_p["shift"]), st(new_p["C"]), st(new_p["n"]), st(new_p["m"]), st(new_p["conv"]),
            st(new_s["S"]), st(new_s["shift"]), st(new_s["C"]), st(new_s["n"]), st(new_s["m"]), st(new_s["conv"]))
```

```python
import functools
import math

import jax
import jax.numpy as jnp
from jax import lax
from jax.experimental import pallas as pl
from jax.experimental.pallas import tpu as pltpu

F32 = jnp.float32
BF16 = jnp.bfloat16

NORM_EPS = 1e-6
RW_GN_EPS = 64e-5
RW_HEAD = 64
ML_DK = 64
ML_DV = 128
ML_CONV = 4

LANES = 128
SUBLANES = 8
MXU_COLS = 256
CHUNK = 64
NEG_BIG = -1e30
VMEM_LIMIT = 58 * 1024 * 1024
HI = lax.Precision.HIGHEST


def _dot(a, b, precision=None):
    return jnp.dot(a, b, preferred_element_type=F32, precision=precision)


def _dot_nt(a, b, precision=None):
    return lax.dot_general(a, b, (((1,), (1,)), ((), ())), preferred_element_type=F32, precision=precision)


def _dot_tn(a, b, precision=None):
    return lax.dot_general(a, b, (((0,), (0,)), ((), ())), preferred_element_type=F32, precision=precision)


def _b(x):
    return x.astype(BF16)


def _rms(x, g):
    return x * lax.rsqrt(jnp.mean(x * x, axis=-1, keepdims=True) + NORM_EPS) * g


def _sigmoid(x):
    return 1.0 / (1.0 + jnp.exp(-x))


def _softplus(x):
    return jnp.maximum(x, 0.0) + jnp.log1p(jnp.exp(-jnp.abs(x)))


def _segsum(x, mlo):
    lo = jnp.sum(jnp.where(mlo, x, 0.0), axis=-1, keepdims=True)
    hi = jnp.sum(jnp.where(mlo, 0.0, x), axis=-1, keepdims=True)
    return jnp.where(mlo, lo, hi)


def _colsel(x, lane_idx, j):
    return jnp.sum(jnp.where(lane_idx == j, x, 0.0), axis=-1, keepdims=True)


def _seq_last(x, lsub):
    n = x.shape[0]
    if lsub == n:
        return x[n - 1:n]
    parts = [jnp.broadcast_to(x[q * lsub + lsub - 1:q * lsub + lsub], (lsub,) + x.shape[1:])
             for q in range(n // lsub)]
    return jnp.concatenate(parts, axis=0)


def _const_spec(shape, single_buffer=False):
    nd = len(shape)
    if single_buffer:
        return pl.BlockSpec(shape, lambda *_: (0,) * nd, pipeline_mode=pl.Buffered(1))
    return pl.BlockSpec(shape, lambda *_: (0,) * nd)


def _log2(n):
    k = int(math.log2(n))
    assert 1 << k == n, n
    return k


def _ffn_kernel(xp_ref, xs_ref, g_ref, wg_ref, wu_ref, wd_ref, gf_ref, op_ref, os_ref, *, fchunk, final_norm, np_steps):
    is_p = pl.program_id(0) < np_steps
    x = jnp.where(is_p, xp_ref[...], xs_ref[...])
    xb = _b(_rms(x, g_ref[...]))
    acc = jnp.zeros(x.shape, F32)
    for lo, hi in zip(fchunk[:-1], fchunk[1:]):
        sl = slice(lo, hi)
        gate = _dot(xb, wg_ref[:, sl])
        up = _dot(xb, wu_ref[:, sl])
        h = _b(gate * _sigmoid(gate) * up)
        acc = acc + _dot(h, wd_ref[sl, :])
    out = x + 0.5 * acc
    if final_norm:
        out = _rms(out, gf_ref[...])

    @pl.when(is_p)
    def _():
        op_ref[...] = out

    @pl.when(jnp.logical_not(is_p))
    def _():
        os_ref[...] = out


def _ffn_call(xp, xs, g_all, wg_all, wu_all, wd_all, gf, layer, *, final_norm, tm):
    (n_p, d), n_s = xp.shape, xs.shape[0]
    nf = wg_all.shape[2]
    np_steps, ns_steps = n_p // tm, n_s // tm
    ntile = nf // MXU_COLS
    fchunk = (0, -(-ntile // 2) * MXU_COLS, nf) if nf % MXU_COLS == 0 and ntile > 1 else (0, nf)
    kern = functools.partial(_ffn_kernel, fchunk=fchunk, final_norm=final_norm, np_steps=np_steps)
    p_spec = pl.BlockSpec((tm, d), lambda i: (jnp.minimum(i, np_steps - 1), 0))
    s_spec = pl.BlockSpec((tm, d), lambda i: (jnp.maximum(i - np_steps, 0), 0))
    w_spec = lambda a, b: pl.BlockSpec((None, a, b), lambda i: (layer, 0, 0), pipeline_mode=pl.Buffered(1))
    return pl.pallas_call(
        kern,
        grid=(np_steps + ns_steps,),
        in_specs=[p_spec, s_spec, pl.BlockSpec((None, 1, d), lambda i: (layer, 0, 0)),
                  w_spec(d, nf), w_spec(d, nf), w_spec(nf, d), _const_spec((1, d))],
        out_specs=[p_spec, s_spec],
        out_shape=[jax.ShapeDtypeStruct((n_p, d), F32), jax.ShapeDtypeStruct((n_s, d), F32)],
        compiler_params=pltpu.CompilerParams(dimension_semantics=("arbitrary",), vmem_limit_bytes=VMEM_LIMIT),
        name="ffn",
    )(xp, xs, g_all, wg_all, wu_all, wd_all, gf)


def _rw_token_part(xn, xx, mlo, put, wrefs):
    mu_ref, wr_ref, wk_ref, wv_ref, w0_ref, w1_ref, w2_ref, a0_ref, a1_ref, a2_ref, g1_ref, g2_ref, kk_ref, ka_ref, rk_ref = wrefs
    cols = [slice(p * LANES, (p + 1) * LANES) for p in range(xn.shape[1] // LANES)]

    def put_all(name, z):
        for p, cs in enumerate(cols):
            put(name, p, z[:, cs])

    def mix(i):
        return _b(xn + xx * mu_ref[i:i + 1, :])

    r = _dot(mix(0), wr_ref[...])
    put_all("r", r)
    wl = _b(jnp.tanh(_dot(mix(1), w1_ref[...])))
    ld = -math.exp(-0.5) * _sigmoid(w0_ref[...] + _dot(wl, w2_ref[...]))
    k = _dot(mix(2), wk_ref[...])
    v = _dot(mix(3), wv_ref[...])
    al = _b(_dot(mix(4), a1_ref[...]))
    a = _sigmoid(a0_ref[...] + _dot(al, a2_ref[...]))
    gl = _b(_sigmoid(_dot(mix(5), g1_ref[...])))
    put_all("g", _dot(gl, g2_ref[...]))
    k2 = k * (1.0 + (a - 1.0) * ka_ref[...])
    kk = k * kk_ref[...]
    rkk = r * k2 * rk_ref[...]
    put_all("ld", ld)
    put_all("k", k2)
    put_all("v", v)
    for p, cs in enumerate(cols):
        kkp = kk[:, cs]
        kn = kkp / jnp.maximum(jnp.sqrt(_segsum(kkp * kkp, mlo)), 1e-12)
        put("kn", p, kn)
        put("b", p, kn * a[:, cs])
        put("bon", p, _segsum(rkk[:, cs], mlo) * v[:, cs])


def _rwkv_kernel(*refs, tm):
    L = CHUNK
    nchunk = tm // L
    nsteps = _log2(L)
    it = iter(refs)
    x_ref = next(it)
    (gn_ref, mu_ref, wr_ref, wk_ref, wv_ref, wo_ref, w0_ref, w1_ref, w2_ref, a0_ref, a1_ref, a2_ref,
     g1_ref, g2_ref, kk_ref, ka_ref, rk_ref, gw_ref, gb_ref) = [next(it) for _ in range(19)]
    o_ref, sout_ref, xn_ref = next(it), next(it), next(it)
    r_s, ld_s, k_s, v_s, kn_s, b_s, g_s, bon_s = sset = [next(it) for _ in range(8)]
    yg_s, carry_s, sbd_s = next(it), next(it), next(it)
    d = x_ref.shape[1]
    npair = d // LANES

    lane = lax.broadcasted_iota(jnp.int32, (1, LANES), 1)
    mlo = lane < RW_HEAD
    row = lax.broadcasted_iota(jnp.int32, (tm, 1), 0)

    @pl.when(pl.program_id(1) == 0)
    def _():
        carry_s[...] = jnp.zeros(carry_s.shape, F32)
        sbd_s[...] = jnp.zeros(sbd_s.shape, F32)

    xn = _rms(x_ref[...], gn_ref[...])
    xprev = jnp.where(row == 0, carry_s[SUBLANES - 1:SUBLANES, :], pltpu.roll(xn, 1, 0))
    carry_s[...] = xn[tm - SUBLANES:tm, :]
    xn_ref[...] = xn[tm - SUBLANES:tm, :]
    sref = dict(zip(("r", "ld", "k", "v", "kn", "b", "g", "bon"), sset))

    def put(name, p, tile):
        sref[name][:, p * LANES:(p + 1) * LANES] = tile

    _rw_token_part(xn, xprev - xn, mlo, put,
                   (mu_ref, wr_ref, wk_ref, wv_ref, w0_ref, w1_ref, w2_ref, a0_ref, a1_ref, a2_ref,
                    g1_ref, g2_ref, kk_ref, ka_ref, rk_ref))

    ti = lax.broadcasted_iota(jnp.int32, (L, 3 * L), 0)
    si = lax.broadcasted_iota(jnp.int32, (L, 3 * L), 1) & (L - 1)
    tril3 = jnp.where(si <= ti, 1.0, 0.0).astype(BF16)
    gi = lax.broadcasted_iota(jnp.int32, (2 * L, 4 * L), 0)
    gj = lax.broadcasted_iota(jnp.int32, (2 * L, 4 * L), 1)
    gt = gi & (L - 1)
    gs = gj & (L - 1)
    gmask = (gs < gt) | ((gi >= L) & (gs == gt))
    pairs = range(npair)

    def stack2(z):
        return jnp.concatenate([jnp.where(mlo, z, 0), jnp.where(mlo, 0, z)], axis=0)

    cs = [slice(p * LANES, (p + 1) * LANES) for p in pairs]
    rw = {(c, p): slice(c * L, (c + 1) * L) for c in range(nchunk) for p in pairs}
    chains = [(c, p) for c in range(nchunk) for p in pairs]

    cum, a2, q4, vv2 = {}, {}, {}, {}
    for key in chains:
        ldc = ld_s[rw[key], cs[key[1]]]
        hi = _b(ldc)
        r1 = ldc - hi.astype(F32)
        mid = _b(r1)
        lo = _b(r1 - mid.astype(F32))
        cum[key] = _dot(tril3, jnp.concatenate([hi, mid, lo], axis=0))
    for key in chains:
        rows, c_ = rw[key], cs[key[1]]
        ep = jnp.exp(cum[key])
        em = jnp.exp(-cum[key])
        at = -(kn_s[rows, c_] * jnp.exp(cum[key] - ld_s[rows, c_]))
        a2[key] = _b(jnp.concatenate([at, r_s[rows, c_] * ep], axis=0))
        q4[key] = jnp.concatenate([stack2(b_s[rows, c_] * em), stack2(k_s[rows, c_] * em)], axis=0)
        vv2[key] = _b(stack2(v_s[rows, c_]))
    g = {key: jnp.where(gmask, _dot_nt(a2[key], _b(q4[key])), 0.0) for key in chains}
    makv = {key: _dot(_b(g[key][:, LANES:]), vv2[key]) for key in chains}
    mr = {key: _b(g[key][L:, :LANES]) for key in chains}
    w = {key: g[key][:L, :LANES] for key in chains}
    pf = {}
    for key in chains:
        pc = _b(w[key])
        pf[key] = _dot(pc, stack2(pc))
    for k in range(1, nsteps):
        for key in chains:
            pc = _b(pf[key])
            if k + 1 < nsteps:
                both = _dot(pc, jnp.concatenate([stack2(pc), stack2(_b(w[key]))], axis=1))
                w[key] = w[key] + pf[key] + both[:, LANES:]
                pf[key] = both[:, :LANES]
            else:
                w[key] = w[key] + pf[key] + _dot(pc, stack2(_b(w[key])))
    wcat = {key: _b(w[key]) for key in chains}
    pl_ = {key: jnp.exp(cum[key][L - 1:L, :]) for key in chains}
    qpl = {key: _b(q4[key] * pl_[key]) for key in chains}

    state = [sbd_s[p] for p in pairs]
    for c in range(nchunk):
        as_ = [_dot_nt(a2[c, p], _b(state[p])) for p in pairs]
        rhs = [as_[p][:L] + makv[c, p][:L] for p in pairs]
        u = [rhs[p] + _dot(wcat[c, p], stack2(_b(rhs[p]))) for p in pairs]
        u2 = [stack2(_b(u[p])) for p in pairs]
        uv = [jnp.concatenate([u2[p], vv2[c, p]], axis=0) for p in pairs]
        y = [as_[p][L:] + makv[c, p][L:] + _dot(mr[c, p], u2[p]) for p in pairs]
        for p in pairs:
            state[p] = state[p] * pl_[c, p] + _dot_tn(uv[p], qpl[c, p])
        rows = rw[c, 0]
        for p in pairs:
            mean = _segsum(y[p], mlo) * (1.0 / RW_HEAD)
            yc = y[p] - mean
            var = _segsum(yc * yc, mlo) * (1.0 / RW_HEAD)
            yn = yc * lax.rsqrt(var + RW_GN_EPS) * gw_ref[:, cs[p]] + gb_ref[:, cs[p]] + bon_s[rows, cs[p]]
            yg_s[rows, cs[p]] = _b(yn * g_s[rows, cs[p]])
    for p in pairs:
        sbd_s[p] = state[p]
        sout_ref[0, 2 * p] = state[p][:RW_HEAD, :RW_HEAD]
        sout_ref[0, 2 * p + 1] = state[p][RW_HEAD:, RW_HEAD:]
    o_ref[...] = x_ref[...] + _dot(yg_s[...], wo_ref[...])


def _rwkv_call(x, wts, *, nseq, seq_rows, tm):
    n, d = x.shape
    npair = d // LANES
    tps = seq_rows // tm
    kern = functools.partial(_rwkv_kernel, tm=tm)
    x_spec = pl.BlockSpec((tm, d), lambda b, j: (b * tps + j, 0))
    st_spec = pl.BlockSpec((1, 2 * npair, RW_HEAD, RW_HEAD), lambda b, j: (b, 0, 0, 0))
    xn_spec = pl.BlockSpec((SUBLANES, d), lambda b, j: (b, 0))
    scratch = ([pltpu.VMEM((tm, d), F32) for _ in range(8)] + [pltpu.VMEM((tm, d), BF16)]
               + [pltpu.VMEM((SUBLANES, d), F32), pltpu.VMEM((npair, LANES, LANES), F32)])
    return pl.pallas_call(
        kern,
        grid=(nseq, tps),
        in_specs=[x_spec] + [_const_spec(wt.shape, single_buffer=True) for wt in wts],
        out_specs=[x_spec, st_spec, xn_spec],
        out_shape=[jax.ShapeDtypeStruct((n, d), F32),
                   jax.ShapeDtypeStruct((nseq, 2 * npair, RW_HEAD, RW_HEAD), F32),
                   jax.ShapeDtypeStruct((nseq * SUBLANES, d), F32)],
        scratch_shapes=scratch,
        compiler_params=pltpu.CompilerParams(dimension_semantics=("arbitrary", "arbitrary"),
                                             vmem_limit_bytes=VMEM_LIMIT),
        name="rwkv_prompt",
    )(x, *wts)


def _rwkv_decode_kernel(*refs, nb, nt):
    it = iter(refs)
    x_ref, sh_ref = next(it), next(it)
    (gn_ref, mu_ref, wr_ref, wk_ref, wv_ref, wo_ref, w0_ref, w1_ref, w2_ref, a0_ref, a1_ref, a2_ref,
     g1_ref, g2_ref, kk_ref, ka_ref, rk_ref, gw_ref, gb_ref) = [next(it) for _ in range(19)]
    s0_ref = next(it)
    o_ref, sout_ref, xn_ref = next(it), next(it), next(it)
    tr = dict(zip(("r", "ld", "k", "v", "kn", "b"), [next(it) for _ in range(6)]))
    yt_s, g_s, bon_s, yg_s = [next(it) for _ in range(4)]
    n, d = x_ref.shape
    npair = d // LANES
    nhead = 2 * npair
    h = pl.program_id(0)
    lane = lax.broadcasted_iota(jnp.int32, (1, LANES), 1)
    mlo = lane < RW_HEAD

    @pl.when(h == 0)
    def _():
        xn = _rms(x_ref[...], gn_ref[...])
        xprev = jnp.concatenate([sh_ref[...], xn[:n - nb]], axis=0)
        xn_ref[...] = xn[n - nb:]

        def put(name, p, tile):
            if name == "g":
                g_s[:, p * LANES:(p + 1) * LANES] = tile
            elif name == "bon":
                bon_s[:, p * LANES:(p + 1) * LANES] = tile
            else:
                tr[name][p] = (jnp.exp(tile) if name == "ld" else tile).T

        _rw_token_part(xn, xprev - xn, mlo, put,
                       (mu_ref, wr_ref, wk_ref, wv_ref, w0_ref, w1_ref, w2_ref, a0_ref, a1_ref, a2_ref,
                        g1_ref, g2_ref, kk_ref, ka_ref, rk_ref))

    p = h >> 1
    base = pl.multiple_of((h & 1) * RW_HEAD, RW_HEAD)
    sub = lax.broadcasted_iota(jnp.int32, (SUBLANES, 1), 0)
    hrows = pl.ds(base, RW_HEAD)

    def group(gi, carry):
        r0 = pl.multiple_of(base + gi * SUBLANES, SUBLANES)
        vg = [tr["v"][p, pl.ds(r0, SUBLANES), t * nb:(t + 1) * nb] for t in range(nt)]
        ys = [jnp.zeros((SUBLANES, nb), F32) for _ in range(nt)]
        for vi in range(SUBLANES):
            v_idx = gi * SUBLANES + vi
            s = s0_ref[0, v_idx]
            for t in range(nt):
                tc = slice(t * nb, (t + 1) * nb)
                sa = -jnp.sum(s * tr["kn"][p, hrows, tc], axis=0, keepdims=True)
                s = s * tr["ld"][p, hrows, tc] + sa * tr["b"][p, hrows, tc] + vg[t][vi:vi + 1, :] * tr["k"][p, hrows, tc]
                yrow = jnp.sum(s * tr["r"][p, hrows, tc], axis=0, keepdims=True)
                ys[t] = jnp.where(sub == vi, yrow, ys[t])
            sout_ref[0, v_idx] = s
        for t in range(nt):
            yt_s[p, pl.ds(r0, SUBLANES), t * nb:(t + 1) * nb] = ys[t]
        return carry

    lax.fori_loop(0, RW_HEAD // SUBLANES, group, 0)

    @pl.when(h == nhead - 1)
    def _():
        for q in range(npair):
            cs = slice(q * LANES, (q + 1) * LANES)
            y = yt_s[q].T
            mean = _segsum(y, mlo) * (1.0 / RW_HEAD)
            yc = y - mean
            var = _segsum(yc * yc, mlo) * (1.0 / RW_HEAD)
            yn = yc * lax.rsqrt(var + RW_GN_EPS) * gw_ref[:, cs] + gb_ref[:, cs] + bon_s[:, cs]
            yg_s[:, cs] = _b(yn * g_s[:, cs])
        o_ref[...] = x_ref[...] + _dot(yg_s[...], wo_ref[...])


def _rwkv_decode_call(xt, shift0, s0t, wts, *, nb, nt):
    n, d = xt.shape
    npair = d // LANES
    nhead = 2 * npair
    kern = functools.partial(_rwkv_decode_kernel, nb=nb, nt=nt)
    full = lambda shape: pl.BlockSpec(shape, lambda h: (0,) * len(shape))
    st_spec = pl.BlockSpec((1, RW_HEAD, RW_HEAD, nb), lambda h: (h, 0, 0, 0))
    in_specs = [full((n, d)), full((nb, d))] + [_const_spec(wt.shape, single_buffer=True) for wt in wts] + [st_spec]
    scratch = ([pltpu.VMEM((npair, LANES, n), F32) for _ in range(7)]
               + [pltpu.VMEM((n, d), F32), pltpu.VMEM((n, d), F32), pltpu.VMEM((n, d), BF16)])
    return pl.pallas_call(
        kern,
        grid=(nhead,),
        in_specs=in_specs,
        out_specs=[full((n, d)), st_spec, full((nb, d))],
        out_shape=[jax.ShapeDtypeStruct((n, d), F32), jax.ShapeDtypeStruct(s0t.shape, F32),
                   jax.ShapeDtypeStruct((nb, d), F32)],
        scratch_shapes=scratch,
        compiler_params=pltpu.CompilerParams(dimension_semantics=("arbitrary",), vmem_limit_bytes=VMEM_LIMIT),
        name="rwkv_decode",
    )(xt, shift0, *wts, s0t)


def _mlstm_kernel(*refs, tm, seq_rows, npad, sample, lsub):
    L = CHUNK
    nsub = L // lsub
    nchunk = tm // L
    it = iter(refs)
    x_ref = next(it)
    if sample:
        cvx_ref, mrow_ref = next(it), next(it)
    gn_ref, wqk_ref, wv_ref, wo_ref, wif_ref, bif_ref, cw_ref, cb_ref, nw_ref, wout_ref = [next(it) for _ in range(10)]
    if sample:
        c0_ref, n0_ref = next(it), next(it)
    o_ref, cout_ref, nout_ref, mout_ref, cvout_ref = [next(it) for _ in range(5)]
    gt_s, q_s, k_s, v_s, og_s, gc_s, ho_s, ext_s = [next(it) for _ in range(8)]
    if not sample:
        cp_s, np_s, m_s = [next(it) for _ in range(3)]
    nqk = wqk_ref.shape[1]
    half = nqk // 2
    nheads = half // ML_DK
    ngate = 2 * nheads
    npair = nheads // 2

    lane = lax.broadcasted_iota(jnp.int32, (1, LANES), 1)
    mlo = lane < ML_DK
    glane = lax.broadcasted_iota(jnp.int32, (1, ngate), 1)
    isf_c = glane >= nheads
    grow = lax.broadcasted_iota(jnp.int32, (ngate, 1), 0)
    isf_r = grow >= nheads
    hlane = lax.broadcasted_iota(jnp.int32, (1, nheads), 1)

    x = x_ref[...]
    xb = _b(_rms(x, gn_ref[...]))
    raw = _dot(xb, wqk_ref[...])
    row = lax.broadcasted_iota(jnp.int32, (tm, 1), 0)
    if sample:
        srow = row & (seq_rows - 1)
        raw = jnp.where((srow >= npad - (ML_CONV - 1)) & (srow < npad), cvx_ref[...], raw)
        cvout_ref[...] = raw
        ext_s[:SUBLANES, :] = jnp.zeros((SUBLANES, nqk), F32)
    else:
        j = pl.program_id(1)

        @pl.when(j == 0)
        def _():
            ext_s[:SUBLANES, :] = jnp.zeros((SUBLANES, nqk), F32)
            cp_s[...] = jnp.zeros(cp_s.shape, F32)
            np_s[...] = jnp.zeros(np_s.shape, F32)
            m_s[...] = jnp.zeros(m_s.shape, F32)

        cvout_ref[...] = raw[tm - SUBLANES:tm, :]
    ext_s[SUBLANES:, :] = raw
    qk = cb_ref[...] + cw_ref[ML_CONV - 1:ML_CONV, :] * raw
    for s in range(1, ML_CONV):
        qk = qk + cw_ref[ML_CONV - 1 - s:ML_CONV - s, :] * ext_s[SUBLANES - s:SUBLANES - s + tm, :]
    if not sample:
        ext_s[:SUBLANES, :] = raw[tm - SUBLANES:tm, :]
    qk = qk * _sigmoid(qk)
    q_s[...] = qk[:, :half] * (ML_DK ** -0.5)
    k_s[...] = qk[:, half:]
    v_s[...] = _dot(xb, wv_ref[...])
    og_s[...] = _sigmoid(_dot(xb, wo_ref[...]))
    ifp = _dot(xb, wif_ref[...]) + bif_ref[...]
    gcol = jnp.where(isf_c, -_softplus(-ifp), ifp)
    if npad:
        keep = (row & (seq_rows - 1)) >= npad
        gcol = jnp.where(keep, gcol, jnp.where(isf_c, 0.0, NEG_BIG))
    gc_s[...] = gcol
    gct = gcol.T
    for c in range(nchunk):
        gt_s[c] = gct[:, c * L:(c + 1) * L]

    ti = lax.broadcasted_iota(jnp.int32, (L, L), 0)
    si = lax.broadcasted_iota(jnp.int32, (L, L), 1)
    sh = _log2(lsub)
    same = (ti >> sh) == (si >> sh)
    causal = same & (si <= ti)
    tril = jnp.where(causal, 1.0, 0.0).astype(F32)
    triu = jnp.where(same & (ti <= si), 1.0, 0.0).astype(F32)

    def chunk(c, carry):
        r0 = pl.multiple_of(c * L, L)
        rows = pl.ds(r0, L)
        gc = gc_s[rows, :]
        gt = gt_s[c]
        bcs = _dot(tril, jnp.where(isf_c, gc, 0.0), HI)
        brs = _dot(jnp.where(isf_r, gt, 0.0), triu, HI)
        blast = _seq_last(bcs, lsub)
        if sample:
            mcols = mrow_ref[rows, :]
        else:
            mcols = m_s[...]
        heads = range(nheads)
        prs = range(npair)
        ps = [slice(pp * LANES, (pp + 1) * LANES) for pp in prs]
        hs = [slice(h * ML_DV, (h + 1) * ML_DV) for h in heads]
        q2 = [q_s[rows, ps[pp]] for pp in prs]
        k2 = [k_s[rows, ps[pp]] for pp in prs]
        k2b = [_b(z) for z in k2]
        if sample:
            c_in = [[jnp.concatenate([c0_ref[c * nsub + q, 2 * pp], c0_ref[c * nsub + q, 2 * pp + 1]], axis=0)
                     for q in range(nsub)] for pp in prs]
            n_in = [[n0_ref[c * nsub + q][:, ps[pp]] for q in range(nsub)] for pp in prs]
            n_rows = [jnp.concatenate([jnp.broadcast_to(n_in[pp][q], (lsub, LANES)) for q in range(nsub)], axis=0)
                      for pp in prs]
        else:
            c_prev = [cp_s[pp] for pp in prs]
            n_prev = [np_s[:, ps[pp]] for pp in prs]
            n_rows = n_prev
        bcol = [_colsel(bcs, glane, nheads + h) for h in heads]
        licol = [_colsel(gc, glane, h) for h in heads]
        mcol = [_colsel(mcols, hlane, h) for h in heads]
        blcol = [_colsel(blast, glane, nheads + h) for h in heads]
        dlog = [jnp.where(causal, bcol[h] - (brs[nheads + h:nheads + h + 1, :] - gt[h:h + 1, :]), -jnp.inf)
                for h in heads]
        ginter = [bcol[h] + mcol[h] for h in heads]
        m_t = [jnp.maximum(ginter[h], jnp.max(dlog[h], axis=-1, keepdims=True)) for h in heads]
        dw = [jnp.exp(dlog[h] - m_t[h]) for h in heads]
        winter = [jnp.exp(ginter[h] - m_t[h]) for h in heads]
        qh = [jnp.where(mlo if h % 2 == 0 else jnp.logical_not(mlo), q2[h // 2], 0.0) for h in heads]
        qhb = [_b(z) for z in qh]
        sc = [_dot_nt(qhb[h], k2b[h // 2]) * dw[h] for h in heads]
        if sample:
            inter = [jnp.concatenate([_dot(qhb[h][q * lsub:(q + 1) * lsub], _b(c_in[h // 2][q]))
                                      for q in range(nsub)], axis=0) for h in heads]
        else:
            cb = [_b(z) for z in c_prev]
            inter = [_dot(qhb[h], cb[h // 2]) for h in heads]
        vh = [v_s[rows, hs[h]] for h in heads]
        num = [winter[h] * inter[h] + _dot(_b(sc[h]), _b(vh[h])) for h in heads]
        den = [winter[h] * jnp.sum(qh[h] * n_rows[h // 2], axis=-1, keepdims=True)
               + jnp.sum(sc[h], axis=-1, keepdims=True) for h in heads]
        for h in heads:
            hout = num[h] / jnp.maximum(jnp.abs(den[h]), jnp.exp(-m_t[h]))
            hn = hout * lax.rsqrt(jnp.mean(hout * hout, axis=-1, keepdims=True) + NORM_EPS) * nw_ref[:, hs[h]]
            ho_s[rows, hs[h]] = _b(hn * og_s[rows, hs[h]])
        mnew = [_seq_last(m_t[h], lsub) for h in heads]
        ws = [jnp.exp(blcol[h] - bcol[h] + licol[h] - mnew[h]) for h in heads]
        wstc = [jnp.exp(blcol[h] + mcol[h] - mnew[h]) for h in heads]
        mt_all = jnp.zeros((L, nheads), F32)
        for h in heads:
            mt_all = jnp.where(hlane == h, m_t[h], mt_all)
        for pp in prs:
            lo, hi = 2 * pp, 2 * pp + 1
            wsk = jnp.where(mlo, ws[lo], ws[hi]) * k2[pp]
            wst = jnp.where(mlo, wstc[lo], wstc[hi])
            wvb = _b(jnp.concatenate([ws[lo] * vh[lo], ws[hi] * vh[hi]], axis=0))
            kmb = jnp.concatenate([jnp.where(mlo, k2b[pp], 0), jnp.where(mlo, 0, k2b[pp])], axis=0)
            if sample:
                for q in range(nsub):
                    last = (q + 1) * lsub - 1
                    sel = [slice(blk * L + q * lsub, blk * L + (q + 1) * lsub) for blk in range(2)]
                    wq = wst[last:last + 1, :]
                    wrow = jnp.concatenate([jnp.broadcast_to(wstc[h_][last:last + 1, :], (ML_DK, 1)) for h_ in (lo, hi)],
                                           axis=0)
                    upd = _dot_tn(jnp.concatenate([kmb[s_] for s_ in sel], axis=0),
                                  jnp.concatenate([wvb[s_] for s_ in sel], axis=0))
                    c_new = c_in[pp][q] * wrow + upd
                    cout_ref[c * nsub + q, 2 * pp] = c_new[:ML_DK]
                    cout_ref[c * nsub + q, 2 * pp + 1] = c_new[ML_DK:]
                    nq = n_in[pp][q] * wq + jnp.sum(wsk[q * lsub:(q + 1) * lsub], axis=0, keepdims=True)
                    nout_ref[c * nsub + q, :, ps[pp]] = nq
            else:
                wrow = jnp.concatenate([jnp.broadcast_to(wstc[h_], (ML_DK, 1)) for h_ in (lo, hi)], axis=0)
                cp_s[pp] = c_prev[pp] * wrow + _dot_tn(kmb, wvb)
                np_s[:, ps[pp]] = n_prev[pp] * wst + jnp.sum(wsk, axis=0, keepdims=True)
        if sample:
            mout_ref[rows, :] = mt_all
        else:
            m_s[...] = mt_all[L - 1:L, :]
        return carry

    lax.fori_loop(0, nchunk, chunk, 0)
    if not sample:
        for pp in range(npair):
            cout_ref[0, 2 * pp] = cp_s[pp][:ML_DK]
            cout_ref[0, 2 * pp + 1] = cp_s[pp][ML_DK:]
        nout_ref[0] = np_s[...]
        mout_ref[0] = m_s[...]
    o_ref[...] = x_ref[...] + _dot(ho_s[...], wout_ref[...])


def _mlstm_call(x, cvx, mrow, c0p, n0, wts, *, nseq, seq_rows, npad, sample, tm):
    n, d = x.shape
    nqk = wts[1].shape[1]
    nv = wts[2].shape[1]
    nheads = nqk // 2 // ML_DK
    npair = nheads // 2
    lsub = seq_rows if sample else CHUNK
    kern = functools.partial(_mlstm_kernel, tm=tm, seq_rows=seq_rows, npad=npad, sample=sample, lsub=lsub)
    if sample:
        grid = (n // tm,)
        tile = lambda i: (i, 0)
        spt = tm // seq_rows
        c_spec = pl.BlockSpec((spt, nheads, ML_DK, ML_DV), lambda i: (i, 0, 0, 0))
        n_spec = pl.BlockSpec((spt, 1, nqk // 2), lambda i: (i, 0, 0))
        m_spec = pl.BlockSpec((tm, nheads), tile)
        m_shape = (n, nheads)
        cv_spec = pl.BlockSpec((tm, nqk), tile)
        cv_shape = (n, nqk)
        sem = ("arbitrary",)
    else:
        tps = seq_rows // tm
        grid = (nseq, tps)
        tile = lambda b, j: (b * tps + j, 0)
        c_spec = pl.BlockSpec((1, nheads, ML_DK, ML_DV), lambda b, j: (b, 0, 0, 0))
        n_spec = pl.BlockSpec((1, 1, nqk // 2), lambda b, j: (b, 0, 0))
        m_spec = pl.BlockSpec((1, 1, nheads), lambda b, j: (b, 0, 0))
        m_shape = (nseq, 1, nheads)
        cv_spec = pl.BlockSpec((SUBLANES, nqk), lambda b, j: (b, 0))
        cv_shape = (nseq * SUBLANES, nqk)
        sem = ("arbitrary", "arbitrary")
    x_spec = pl.BlockSpec((tm, d), tile)
    in_specs = [x_spec]
    args = [x]
    if sample:
        in_specs += [pl.BlockSpec((tm, nqk), tile), pl.BlockSpec((tm, nheads), tile)]
        args += [cvx, mrow]
    for wt in wts:
        in_specs.append(_const_spec(wt.shape))
        args.append(wt)
    if sample:
        in_specs += [c_spec, n_spec]
        args += [c0p, n0]
    scratch = [pltpu.VMEM((tm // CHUNK, 2 * nheads, CHUNK), F32), pltpu.VMEM((tm, nqk // 2), F32),
               pltpu.VMEM((tm, nqk // 2), F32),
               pltpu.VMEM((tm, nv), F32), pltpu.VMEM((tm, nv), F32), pltpu.VMEM((tm, 2 * nheads), F32),
               pltpu.VMEM((tm, nv), BF16), pltpu.VMEM((tm + SUBLANES, nqk), F32)]
    if not sample:
        scratch += [pltpu.VMEM((npair, LANES, LANES), F32),
                    pltpu.VMEM((1, nqk // 2), F32), pltpu.VMEM((1, nheads), F32)]
    return pl.pallas_call(
        kern,
        grid=grid,
        in_specs=in_specs,
        out_specs=[x_spec, c_spec, n_spec, m_spec, cv_spec],
        out_shape=[jax.ShapeDtypeStruct((n, d), F32),
                   jax.ShapeDtypeStruct((nseq, nheads, ML_DK, ML_DV), F32),
                   jax.ShapeDtypeStruct((nseq, 1, nqk // 2), F32),
                   jax.ShapeDtypeStruct(m_shape, F32),
                   jax.ShapeDtypeStruct(cv_shape, F32)],
        scratch_shapes=scratch,
        compiler_params=pltpu.CompilerParams(dimension_semantics=sem, vmem_limit_bytes=VMEM_LIMIT),
        name="mlstm_sample" if sample else "mlstm_prompt",
    )(*args)


def kernel(x_prompt, x_sample, state_rwkv_S, state_rwkv_shift, state_mlstm_C, state_mlstm_n, state_mlstm_m,
           state_mlstm_conv, norm_ffa, ffa_wg, ffa_wu, ffa_wd, norm_mix, norm_ffb, ffb_wg, ffb_wu, ffb_wd,
           rw_mu, rw_wr, rw_wk, rw_wv, rw_wo, rw_w0, rw_w1, rw_w2, rw_a0, rw_a1, rw_a2, rw_g1, rw_g2,
           rw_k_k, rw_k_a, rw_r_k, rw_gn_w, rw_gn_b, ml_w_in, ml_b_if, ml_conv_w, ml_conv_b, ml_norm_w,
           ml_w_out, norm_final):
    bp, tp, d = x_prompt.shape
    bs, ts, _ = x_sample.shape
    depth = norm_ffa.shape[0]
    slot = SUBLANES
    npad = slot - ts
    assert 0 < ts <= slot and npad >= ML_CONV - 1
    tm_p = min(256, tp)
    tm_s = min(2 * CHUNK, bs * slot)
    assert tp % tm_p == 0 and tm_p % CHUNK == 0 and (bs * slot) % tm_s == 0 and tm_s % CHUNK == 0
    ml_heads = ml_b_if.shape[1] // 2
    nqk = 2 * ml_heads * ML_DK
    nv = ml_heads * ML_DV

    xp = x_prompt.reshape(bp * tp, d)
    xs = x_sample.transpose(1, 0, 2).reshape(ts * bs, d)

    def to_slots(z):
        return jnp.concatenate([jnp.zeros((bs, npad, d), F32), z.reshape(ts, bs, d).transpose(1, 0, 2)],
                               axis=1).reshape(bs * slot, d)

    def from_slots(z):
        return z.reshape(bs, slot, d)[:, npad:].transpose(1, 0, 2).reshape(ts * bs, d)

    row2 = lambda a: a.reshape(1, -1)

    tm_ffn = min(512, bs * ts)
    assert (bp * tp) % tm_ffn == 0 and (bs * ts) % tm_ffn == 0
    ffa = (norm_ffa.reshape(depth, 1, d), _b(ffa_wg), _b(ffa_wu), _b(ffa_wd))
    ffb = (norm_ffb.reshape(depth, 1, d), _b(ffb_wg), _b(ffb_wu), _b(ffb_wd))

    def ffn(xp_, xs_, wset, layer, final):
        return _ffn_call(xp_, xs_, *wset, row2(norm_final), layer, final_norm=final, tm=tm_ffn)

    new_p = {k_: [] for k_ in ("S", "shift", "C", "n", "m", "conv")}
    new_s = {k_: [] for k_ in ("S", "shift", "C", "n", "m", "conv")}
    for i in range(depth):
        xp, xs = ffn(xp, xs, ffa, i, False)
        j = i // 2
        if i % 2 == 0:
            wts = [row2(norm_mix[i]), rw_mu[j], _b(rw_wr[j]), _b(rw_wk[j]), _b(rw_wv[j]), _b(rw_wo[j]),
                   row2(rw_w0[j]), _b(rw_w1[j]), _b(rw_w2[j]), row2(rw_a0[j]), _b(rw_a1[j]), _b(rw_a2[j]),
                   _b(rw_g1[j]), _b(rw_g2[j]), row2(rw_k_k[j]), row2(rw_k_a[j]), row2(rw_r_k[j]),
                   row2(rw_gn_w[j]), row2(rw_gn_b[j])]
            xp, sbd, tail = _rwkv_call(xp, wts, nseq=bp, seq_rows=tp, tm=tm_p)
            new_p["S"].append(sbd)
            new_p["shift"].append(tail.reshape(bp, SUBLANES, d)[:, SUBLANES - 1])
            xs, s_t, sh_new = _rwkv_decode_call(xs, state_rwkv_shift[j], jnp.transpose(state_rwkv_S[j], (1, 2, 3, 0)),
                                                wts, nb=bs, nt=ts)
            new_s["S"].append(jnp.transpose(s_t, (3, 0, 1, 2)))
            new_s["shift"].append(sh_new)
        else:
            w_in = ml_w_in[j]
            w_if = w_in[:, nqk + nv + d:]
            wts = [row2(norm_mix[i]), _b(w_in[:, :nqk]), _b(w_in[:, nqk:nqk + nv]), _b(w_in[:, nqk + nv:nqk + nv + d]),
                   _b(w_if), row2(ml_b_if[j]), ml_conv_w[j],
                   row2(ml_conv_b[j]), row2(ml_norm_w[j]), _b(ml_w_out[j])]
            xp, cp, n_, m_, tail = _mlstm_call(xp, None, None, None, None, wts, nseq=bp, seq_rows=tp, npad=0,
                                               sample=False, tm=tm_p)
            new_p["C"].append(jnp.swapaxes(cp, -1, -2))
            new_p["n"].append(n_.reshape(bp, ml_heads, ML_DK))
            new_p["m"].append(m_.reshape(bp, ml_heads))
            new_p["conv"].append(tail.reshape(bp, SUBLANES, nqk)[:, SUBLANES - (ML_CONV - 1):])
            conv0 = state_mlstm_conv[j]
            cvx = jnp.concatenate([jnp.zeros((bs, npad - (ML_CONV - 1), nqk), F32), conv0,
                                   jnp.zeros((bs, slot - npad, nqk), F32)], axis=1).reshape(bs * slot, nqk)
            mrow = jnp.repeat(state_mlstm_m[j], slot, axis=0)
            xs, cp, n_, mt, raw = _mlstm_call(to_slots(xs), cvx, mrow, jnp.swapaxes(state_mlstm_C[j], -1, -2),
                                              state_mlstm_n[j].reshape(bs, 1, ml_heads * ML_DK), wts, nseq=bs,
                                              seq_rows=slot, npad=npad, sample=True, tm=tm_s)
            xs = from_slots(xs)
            new_s["C"].append(jnp.swapaxes(cp, -1, -2))
            new_s["n"].append(n_.reshape(bs, ml_heads, ML_DK))
            new_s["m"].append(mt.reshape(bs, slot, ml_heads)[:, slot - 1])
            new_s["conv"].append(raw.reshape(bs, slot, nqk)[:, slot - (ML_CONV - 1):])
        xp, xs = ffn(xp, xs, ffb, i, i == depth - 1)
    y_prompt = xp.reshape(bp, tp, d)
    y_sample = xs.reshape(ts, bs, d).transpose(1, 0, 2)
    st = lambda lst: jnp.stack(lst)
    return (y_prompt, y_sample,
            st(new_p["S"]), st(new_p["shift"]), st(new_p["C"]), st(new_p["n"]), st(new_p["m"]), st(new_p["conv"]),
            st(new_s["S"]), st(new_s["shift"]), st(new_s["C"]), st(new_s["n"]), st(new_s["m"]), st(new_s["conv"]))
```

```python
import functools
import math

import jax
import jax.numpy as jnp
from jax import lax
from jax.experimental import pallas as pl
from jax.experimental.pallas import tpu as pltpu

F32 = jnp.float32
BF16 = jnp.bfloat16

NORM_EPS = 1e-6
RW_GN_EPS = 64e-5
RW_HEAD = 64
ML_DK = 64
ML_DV = 128
ML_CONV = 4

LANES = 128
SUBLANES = 8
MXU_COLS = 256
CHUNK = 64
NEG_BIG = -1e30
VMEM_LIMIT = 58 * 1024 * 1024
HI = lax.Precision.HIGHEST


def _dot(a, b, precision=None):
    return jnp.dot(a, b, preferred_element_type=F32, precision=precision)


def _dot_nt(a, b, precision=None):
    return lax.dot_general(a, b, (((1,), (1,)), ((), ())), preferred_element_type=F32, precision=precision)


def _dot_tn(a, b, precision=None):
    return lax.dot_general(a, b, (((0,), (0,)), ((), ())), preferred_element_type=F32, precision=precision)


def _b(x):
    return x.astype(BF16)


def _rms(x, g):
    return x * lax.rsqrt(jnp.mean(x * x, axis=-1, keepdims=True) + NORM_EPS) * g


def _sigmoid(x):
    return 1.0 / (1.0 + jnp.exp(-x))


def _softplus(x):
    return jnp.maximum(x, 0.0) + jnp.log1p(jnp.exp(-jnp.abs(x)))


def _segsum(x, mlo):
    lo = jnp.sum(jnp.where(mlo, x, 0.0), axis=-1, keepdims=True)
    hi = jnp.sum(jnp.where(mlo, 0.0, x), axis=-1, keepdims=True)
    return jnp.where(mlo, lo, hi)


def _colsel(x, lane_idx, j):
    return jnp.sum(jnp.where(lane_idx == j, x, 0.0), axis=-1, keepdims=True)


def _seq_last(x, lsub):
    n = x.shape[0]
    if lsub == n:
        return x[n - 1:n]
    parts = [jnp.broadcast_to(x[q * lsub + lsub - 1:q * lsub + lsub], (lsub,) + x.shape[1:])
             for q in range(n // lsub)]
    return jnp.concatenate(parts, axis=0)


def _const_spec(shape, single_buffer=False):
    nd = len(shape)
    if single_buffer:
        return pl.BlockSpec(shape, lambda *_: (0,) * nd, pipeline_mode=pl.Buffered(1))
    return pl.BlockSpec(shape, lambda *_: (0,) * nd)


def _log2(n):
    k = int(math.log2(n))
    assert 1 << k == n, n
    return k


def _ffn_kernel(xp_ref, xs_ref, g_ref, wg_ref, wu_ref, wd_ref, gf_ref, op_ref, os_ref, *, fchunk, final_norm, np_steps):
    def tile(x_ref, o_ref):
        x = x_ref[...]
        xb = _b(_rms(x, g_ref[...]))
        acc = jnp.zeros(x.shape, F32)
        for lo, hi in zip(fchunk[:-1], fchunk[1:]):
            sl = slice(lo, hi)
            gate = _dot(xb, wg_ref[:, sl])
            up = _dot(xb, wu_ref[:, sl])
            h = _b(gate * _sigmoid(gate) * up)
            acc = acc + _dot(h, wd_ref[sl, :])
        out = x + 0.5 * acc
        if final_norm:
            out = _rms(out, gf_ref[...])
        o_ref[...] = out

    is_p = pl.program_id(0) < np_steps
    pl.when(is_p)(functools.partial(tile, xp_ref, op_ref))
    pl.when(jnp.logical_not(is_p))(functools.partial(tile, xs_ref, os_ref))


def _ffn_call(xp, xs, g_all, wg_all, wu_all, wd_all, gf, layer, *, final_norm, tm):
    (n_p, d), n_s = xp.shape, xs.shape[0]
    nf = wg_all.shape[2]
    np_steps, ns_steps = n_p // tm, n_s // tm
    ntile = nf // MXU_COLS
    fchunk = (0, -(-ntile // 2) * MXU_COLS, nf) if nf % MXU_COLS == 0 and ntile > 1 else (0, nf)
    kern = functools.partial(_ffn_kernel, fchunk=fchunk, final_norm=final_norm, np_steps=np_steps)
    p_spec = pl.BlockSpec((tm, d), lambda i: (jnp.minimum(i, np_steps - 1), 0))
    s_spec = pl.BlockSpec((tm, d), lambda i: (jnp.maximum(i - np_steps, 0), 0))
    w_spec = lambda a, b: pl.BlockSpec((None, a, b), lambda i: (layer, 0, 0), pipeline_mode=pl.Buffered(1))
    return pl.pallas_call(
        kern,
        grid=(np_steps + ns_steps,),
        in_specs=[p_spec, s_spec, pl.BlockSpec((None, 1, d), lambda i: (layer, 0, 0)),
                  w_spec(d, nf), w_spec(d, nf), w_spec(nf, d), _const_spec((1, d))],
        out_specs=[p_spec, s_spec],
        out_shape=[jax.ShapeDtypeStruct((n_p, d), F32), jax.ShapeDtypeStruct((n_s, d), F32)],
        compiler_params=pltpu.CompilerParams(dimension_semantics=("arbitrary",), vmem_limit_bytes=VMEM_LIMIT),
        name="ffn",
    )(xp, xs, g_all, wg_all, wu_all, wd_all, gf)


def _rw_token_part(xn, xx, mlo, put, wrefs):
    mu_ref, wr_ref, wk_ref, wv_ref, w0_ref, w1_ref, w2_ref, a0_ref, a1_ref, a2_ref, g1_ref, g2_ref, kk_ref, ka_ref, rk_ref = wrefs
    cols = [slice(p * LANES, (p + 1) * LANES) for p in range(xn.shape[1] // LANES)]

    def put_all(name, z):
        for p, cs in enumerate(cols):
            put(name, p, z[:, cs])

    def mix(i):
        return _b(xn + xx * mu_ref[i:i + 1, :])

    r = _dot(mix(0), wr_ref[...])
    put_all("r", r)
    wl = _b(jnp.tanh(_dot(mix(1), w1_ref[...])))
    ld = -math.exp(-0.5) * _sigmoid(w0_ref[...] + _dot(wl, w2_ref[...]))
    k = _dot(mix(2), wk_ref[...])
    v = _dot(mix(3), wv_ref[...])
    al = _b(_dot(mix(4), a1_ref[...]))
    a = _sigmoid(a0_ref[...] + _dot(al, a2_ref[...]))
    gl = _b(_sigmoid(_dot(mix(5), g1_ref[...])))
    put_all("g", _dot(gl, g2_ref[...]))
    k2 = k * (1.0 + (a - 1.0) * ka_ref[...])
    kk = k * kk_ref[...]
    rkk = r * k2 * rk_ref[...]
    put_all("ld", ld)
    put_all("k", k2)
    put_all("v", v)
    for p, cs in enumerate(cols):
        kkp = kk[:, cs]
        kn = kkp / jnp.maximum(jnp.sqrt(_segsum(kkp * kkp, mlo)), 1e-12)
        put("kn", p, kn)
        put("b", p, kn * a[:, cs])
        put("bon", p, _segsum(rkk[:, cs], mlo) * v[:, cs])


def _rwkv_kernel(*refs, tm):
    L = CHUNK
    nchunk = tm // L
    nsteps = _log2(L)
    it = iter(refs)
    x_ref = next(it)
    (gn_ref, mu_ref, wr_ref, wk_ref, wv_ref, wo_ref, w0_ref, w1_ref, w2_ref, a0_ref, a1_ref, a2_ref,
     g1_ref, g2_ref, kk_ref, ka_ref, rk_ref, gw_ref, gb_ref) = [next(it) for _ in range(19)]
    o_ref, sout_ref, xn_ref = next(it), next(it), next(it)
    r_s, ld_s, k_s, v_s, kn_s, b_s, g_s, bon_s = sset = [next(it) for _ in range(8)]
    yg_s, carry_s, sbd_s = next(it), next(it), next(it)
    d = x_ref.shape[1]
    npair = d // LANES

    lane = lax.broadcasted_iota(jnp.int32, (1, LANES), 1)
    mlo = lane < RW_HEAD
    row = lax.broadcasted_iota(jnp.int32, (tm, 1), 0)

    @pl.when(pl.program_id(1) == 0)
    def _():
        carry_s[...] = jnp.zeros(carry_s.shape, F32)
        sbd_s[...] = jnp.zeros(sbd_s.shape, F32)

    xn = _rms(x_ref[...], gn_ref[...])
    xprev = jnp.where(row == 0, carry_s[SUBLANES - 1:SUBLANES, :], pltpu.roll(xn, 1, 0))
    carry_s[...] = xn[tm - SUBLANES:tm, :]
    xn_ref[...] = xn[tm - SUBLANES:tm, :]
    sref = dict(zip(("r", "ld", "k", "v", "kn", "b", "g", "bon"), sset))

    def put(name, p, tile):
        sref[name][:, p * LANES:(p + 1) * LANES] = tile

    _rw_token_part(xn, xprev - xn, mlo, put,
                   (mu_ref, wr_ref, wk_ref, wv_ref, w0_ref, w1_ref, w2_ref, a0_ref, a1_ref, a2_ref,
                    g1_ref, g2_ref, kk_ref, ka_ref, rk_ref))

    ti = lax.broadcasted_iota(jnp.int32, (L, 3 * L), 0)
    si = lax.broadcasted_iota(jnp.int32, (L, 3 * L), 1) & (L - 1)
    tril3 = jnp.where(si <= ti, 1.0, 0.0).astype(BF16)
    gi = lax.broadcasted_iota(jnp.int32, (2 * L, 4 * L), 0)
    gj = lax.broadcasted_iota(jnp.int32, (2 * L, 4 * L), 1)
    gt = gi & (L - 1)
    gs = gj & (L - 1)
    gmask = (gs < gt) | ((gi >= L) & (gs == gt))
    pairs = range(npair)

    def stack2(z):
        return jnp.concatenate([jnp.where(mlo, z, 0), jnp.where(mlo, 0, z)], axis=0)

    cs = [slice(p * LANES, (p + 1) * LANES) for p in pairs]
    rw = {(c, p): slice(c * L, (c + 1) * L) for c in range(nchunk) for p in pairs}
    chains = [(c, p) for c in range(nchunk) for p in pairs]

    cum, a2, q4, vv2 = {}, {}, {}, {}
    for key in chains:
        ldc = ld_s[rw[key], cs[key[1]]]
        hi = _b(ldc)
        r1 = ldc - hi.astype(F32)
        mid = _b(r1)
        lo = _b(r1 - mid.astype(F32))
        cum[key] = _dot(tril3, jnp.concatenate([hi, mid, lo], axis=0))
    for key in chains:
        rows, c_ = rw[key], cs[key[1]]
        ep = jnp.exp(cum[key])
        em = jnp.exp(-cum[key])
        at = -(kn_s[rows, c_] * jnp.exp(cum[key] - ld_s[rows, c_]))
        a2[key] = _b(jnp.concatenate([at, r_s[rows, c_] * ep], axis=0))
        q4[key] = jnp.concatenate([stack2(b_s[rows, c_] * em), stack2(k_s[rows, c_] * em)], axis=0)
        vv2[key] = _b(stack2(v_s[rows, c_]))
    g = {key: jnp.where(gmask, _dot_nt(a2[key], _b(q4[key])), 0.0) for key in chains}
    makv = {key: _dot(_b(g[key][:, LANES:]), vv2[key]) for key in chains}
    mr = {key: _b(g[key][L:, :LANES]) for key in chains}
    w = {key: g[key][:L, :LANES] for key in chains}
    pf = {}
    for key in chains:
        pc = _b(w[key])
        pf[key] = _dot(pc, stack2(pc))
    for k in range(1, nsteps):
        for key in chains:
            pc = _b(pf[key])
            if k + 1 < nsteps:
                both = _dot(pc, jnp.concatenate([stack2(pc), stack2(_b(w[key]))], axis=1))
                w[key] = w[key] + pf[key] + both[:, LANES:]
                pf[key] = both[:, :LANES]
            else:
                w[key] = w[key] + pf[key] + _dot(pc, stack2(_b(w[key])))
    wcat = {key: _b(w[key]) for key in chains}
    pl_ = {key: jnp.exp(cum[key][L - 1:L, :]) for key in chains}
    qpl = {key: _b(q4[key] * pl_[key]) for key in chains}

    state = [sbd_s[p] for p in pairs]
    for c in range(nchunk):
        as_ = [_dot_nt(a2[c, p], _b(state[p])) for p in pairs]
        rhs = [as_[p][:L] + makv[c, p][:L] for p in pairs]
        u = [rhs[p] + _dot(wcat[c, p], stack2(_b(rhs[p]))) for p in pairs]
        u2 = [stack2(_b(u[p])) for p in pairs]
        uv = [jnp.concatenate([u2[p], vv2[c, p]], axis=0) for p in pairs]
        y = [as_[p][L:] + makv[c, p][L:] + _dot(mr[c, p], u2[p]) for p in pairs]
        for p in pairs:
            state[p] = state[p] * pl_[c, p] + _dot_tn(uv[p], qpl[c, p])
        rows = rw[c, 0]
        for p in pairs:
            mean = _segsum(y[p], mlo) * (1.0 / RW_HEAD)
            yc = y[p] - mean
            var = _segsum(yc * yc, mlo) * (1.0 / RW_HEAD)
            yn = yc * lax.rsqrt(var + RW_GN_EPS) * gw_ref[:, cs[p]] + gb_ref[:, cs[p]] + bon_s[rows, cs[p]]
            yg_s[rows, cs[p]] = _b(yn * g_s[rows, cs[p]])
    for p in pairs:
        sbd_s[p] = state[p]
        sout_ref[0, 2 * p] = state[p][:RW_HEAD, :RW_HEAD]
        sout_ref[0, 2 * p + 1] = state[p][RW_HEAD:, RW_HEAD:]
    o_ref[...] = x_ref[...] + _dot(yg_s[...], wo_ref[...])


def _rwkv_call(x, wts, *, nseq, seq_rows, tm):
    n, d = x.shape
    npair = d // LANES
    tps = seq_rows // tm
    kern = functools.partial(_rwkv_kernel, tm=tm)
    x_spec = pl.BlockSpec((tm, d), lambda b, j: (b * tps + j, 0))
    st_spec = pl.BlockSpec((1, 2 * npair, RW_HEAD, RW_HEAD), lambda b, j: (b, 0, 0, 0))
    xn_spec = pl.BlockSpec((SUBLANES, d), lambda b, j: (b, 0))
    scratch = ([pltpu.VMEM((tm, d), F32) for _ in range(8)] + [pltpu.VMEM((tm, d), BF16)]
               + [pltpu.VMEM((SUBLANES, d), F32), pltpu.VMEM((npair, LANES, LANES), F32)])
    return pl.pallas_call(
        kern,
        grid=(nseq, tps),
        in_specs=[x_spec] + [_const_spec(wt.shape, single_buffer=True) for wt in wts],
        out_specs=[x_spec, st_spec, xn_spec],
        out_shape=[jax.ShapeDtypeStruct((n, d), F32),
                   jax.ShapeDtypeStruct((nseq, 2 * npair, RW_HEAD, RW_HEAD), F32),
                   jax.ShapeDtypeStruct((nseq * SUBLANES, d), F32)],
        scratch_shapes=scratch,
        compiler_params=pltpu.CompilerParams(dimension_semantics=("arbitrary", "arbitrary"),
                                             vmem_limit_bytes=VMEM_LIMIT),
        name="rwkv_prompt",
    )(x, *wts)


def _rwkv_decode_kernel(*refs, nb, nt):
    it = iter(refs)
    x_ref, sh_ref = next(it), next(it)
    (gn_ref, mu_ref, wr_ref, wk_ref, wv_ref, wo_ref, w0_ref, w1_ref, w2_ref, a0_ref, a1_ref, a2_ref,
     g1_ref, g2_ref, kk_ref, ka_ref, rk_ref, gw_ref, gb_ref) = [next(it) for _ in range(19)]
    s0_ref = next(it)
    o_ref, sout_ref, xn_ref = next(it), next(it), next(it)
    tr = dict(zip(("r", "ld", "k", "v", "kn", "b"), [next(it) for _ in range(6)]))
    yt_s, g_s, bon_s, yg_s = [next(it) for _ in range(4)]
    n, d = x_ref.shape
    npair = d // LANES
    nhead = 2 * npair
    h = pl.program_id(0)
    lane = lax.broadcasted_iota(jnp.int32, (1, LANES), 1)
    mlo = lane < RW_HEAD

    @pl.when(h == 0)
    def _():
        xn = _rms(x_ref[...], gn_ref[...])
        xprev = jnp.concatenate([sh_ref[...], xn[:n - nb]], axis=0)
        xn_ref[...] = xn[n - nb:]

        def put(name, p, tile):
            if name == "g":
                g_s[:, p * LANES:(p + 1) * LANES] = tile
            elif name == "bon":
                bon_s[:, p * LANES:(p + 1) * LANES] = tile
            else:
                tr[name][p] = (jnp.exp(tile) if name == "ld" else tile).T

        _rw_token_part(xn, xprev - xn, mlo, put,
                       (mu_ref, wr_ref, wk_ref, wv_ref, w0_ref, w1_ref, w2_ref, a0_ref, a1_ref, a2_ref,
                        g1_ref, g2_ref, kk_ref, ka_ref, rk_ref))

    p = h >> 1
    base = pl.multiple_of((h & 1) * RW_HEAD, RW_HEAD)
    sub = lax.broadcasted_iota(jnp.int32, (SUBLANES, 1), 0)
    hrows = pl.ds(base, RW_HEAD)

    def group(gi, carry):
        r0 = pl.multiple_of(base + gi * SUBLANES, SUBLANES)
        vg = [tr["v"][p, pl.ds(r0, SUBLANES), t * nb:(t + 1) * nb] for t in range(nt)]
        ys = [jnp.zeros((SUBLANES, nb), F32) for _ in range(nt)]
        for vi in range(SUBLANES):
            v_idx = gi * SUBLANES + vi
            s = s0_ref[0, v_idx]
            for t in range(nt):
                tc = slice(t * nb, (t + 1) * nb)
                sa = -jnp.sum(s * tr["kn"][p, hrows, tc], axis=0, keepdims=True)
                s = s * tr["ld"][p, hrows, tc] + sa * tr["b"][p, hrows, tc] + vg[t][vi:vi + 1, :] * tr["k"][p, hrows, tc]
                yrow = jnp.sum(s * tr["r"][p, hrows, tc], axis=0, keepdims=True)
                ys[t] = jnp.where(sub == vi, yrow, ys[t])
            sout_ref[0, v_idx] = s
        for t in range(nt):
            yt_s[p, pl.ds(r0, SUBLANES), t * nb:(t + 1) * nb] = ys[t]
        return carry

    lax.fori_loop(0, RW_HEAD // SUBLANES, group, 0)

    @pl.when(h == nhead - 1)
    def _():
        for q in range(npair):
            cs = slice(q * LANES, (q + 1) * LANES)
            y = yt_s[q].T
            mean = _segsum(y, mlo) * (1.0 / RW_HEAD)
            yc = y - mean
            var = _segsum(yc * yc, mlo) * (1.0 / RW_HEAD)
            yn = yc * lax.rsqrt(var + RW_GN_EPS) * gw_ref[:, cs] + gb_ref[:, cs] + bon_s[:, cs]
            yg_s[:, cs] = _b(yn * g_s[:, cs])
        o_ref[...] = x_ref[...] + _dot(yg_s[...], wo_ref[...])


def _rwkv_decode_call(xt, shift0, s0t, wts, *, nb, nt):
    n, d = xt.shape
    npair = d // LANES
    nhead = 2 * npair
    kern = functools.partial(_rwkv_decode_kernel, nb=nb, nt=nt)
    full = lambda shape: pl.BlockSpec(shape, lambda h: (0,) * len(shape))
    st_spec = pl.BlockSpec((1, RW_HEAD, RW_HEAD, nb), lambda h: (h, 0, 0, 0))
    in_specs = [full((n, d)), full((nb, d))] + [_const_spec(wt.shape, single_buffer=True) for wt in wts] + [st_spec]
    scratch = ([pltpu.VMEM((npair, LANES, n), F32) for _ in range(7)]
               + [pltpu.VMEM((n, d), F32), pltpu.VMEM((n, d), F32), pltpu.VMEM((n, d), BF16)])
    return pl.pallas_call(
        kern,
        grid=(nhead,),
        in_specs=in_specs,
        out_specs=[full((n, d)), st_spec, full((nb, d))],
        out_shape=[jax.ShapeDtypeStruct((n, d), F32), jax.ShapeDtypeStruct(s0t.shape, F32),
                   jax.ShapeDtypeStruct((nb, d), F32)],
        scratch_shapes=scratch,
        compiler_params=pltpu.CompilerParams(dimension_semantics=("arbitrary",), vmem_limit_bytes=VMEM_LIMIT),
        name="rwkv_decode",
    )(xt, shift0, *wts, s0t)


def _mlstm_kernel(*refs, tm, seq_rows, npad, sample, lsub):
    L = CHUNK
    nsub = L // lsub
    nchunk = tm // L
    it = iter(refs)
    x_ref = next(it)
    if sample:
        cvx_ref, mrow_ref = next(it), next(it)
    gn_ref, wqk_ref, wv_ref, wo_ref, wif_ref, bif_ref, cw_ref, cb_ref, nw_ref, wout_ref = [next(it) for _ in range(10)]
    if sample:
        c0_ref, n0_ref = next(it), next(it)
    o_ref, cout_ref, nout_ref, mout_ref, cvout_ref = [next(it) for _ in range(5)]
    gt_s, q_s, k_s, v_s, og_s, gc_s, ho_s, ext_s = [next(it) for _ in range(8)]
    if not sample:
        cp_s, np_s, m_s = [next(it) for _ in range(3)]
    nqk = wqk_ref.shape[1]
    half = nqk // 2
    nheads = half // ML_DK
    ngate = 2 * nheads
    npair = nheads // 2

    lane = lax.broadcasted_iota(jnp.int32, (1, LANES), 1)
    mlo = lane < ML_DK
    glane = lax.broadcasted_iota(jnp.int32, (1, ngate), 1)
    isf_c = glane >= nheads
    grow = lax.broadcasted_iota(jnp.int32, (ngate, 1), 0)
    isf_r = grow >= nheads
    hlane = lax.broadcasted_iota(jnp.int32, (1, nheads), 1)

    x = x_ref[...]
    xb = _b(_rms(x, gn_ref[...]))
    raw = _dot(xb, wqk_ref[...])
    row = lax.broadcasted_iota(jnp.int32, (tm, 1), 0)
    if sample:
        srow = row & (seq_rows - 1)
        raw = jnp.where((srow >= npad - (ML_CONV - 1)) & (srow < npad), cvx_ref[...], raw)
        cvout_ref[...] = raw
        ext_s[:SUBLANES, :] = jnp.zeros((SUBLANES, nqk), F32)
    else:
        j = pl.program_id(1)

        @pl.when(j == 0)
        def _():
            ext_s[:SUBLANES, :] = jnp.zeros((SUBLANES, nqk), F32)
            cp_s[...] = jnp.zeros(cp_s.shape, F32)
            np_s[...] = jnp.zeros(np_s.shape, F32)
            m_s[...] = jnp.zeros(m_s.shape, F32)

        cvout_ref[...] = raw[tm - SUBLANES:tm, :]
    ext_s[SUBLANES:, :] = raw
    qk = cb_ref[...] + cw_ref[ML_CONV - 1:ML_CONV, :] * raw
    for s in range(1, ML_CONV):
        qk = qk + cw_ref[ML_CONV - 1 - s:ML_CONV - s, :] * ext_s[SUBLANES - s:SUBLANES - s + tm, :]
    if not sample:
        ext_s[:SUBLANES, :] = raw[tm - SUBLANES:tm, :]
    qk = qk * _sigmoid(qk)
    q_s[...] = qk[:, :half] * (ML_DK ** -0.5)
    k_s[...] = qk[:, half:]
    v_s[...] = _dot(xb, wv_ref[...])
    og_s[...] = _sigmoid(_dot(xb, wo_ref[...]))
    ifp = _dot(xb, wif_ref[...]) + bif_ref[...]
    gcol = jnp.where(isf_c, -_softplus(-ifp), ifp)
    if npad:
        keep = (row & (seq_rows - 1)) >= npad
        gcol = jnp.where(keep, gcol, jnp.where(isf_c, 0.0, NEG_BIG))
    gc_s[...] = gcol
    gct = gcol.T
    for c in range(nchunk):
        gt_s[c] = gct[:, c * L:(c + 1) * L]

    ti = lax.broadcasted_iota(jnp.int32, (L, L), 0)
    si = lax.broadcasted_iota(jnp.int32, (L, L), 1)
    sh = _log2(lsub)
    same = (ti >> sh) == (si >> sh)
    causal = same & (si <= ti)
    tril = jnp.where(causal, 1.0, 0.0).astype(F32)
    triu = jnp.where(same & (ti <= si), 1.0, 0.0).astype(F32)

    def chunk(c, carry):
        r0 = pl.multiple_of(c * L, L)
        rows = pl.ds(r0, L)
        gc = gc_s[rows, :]
        gt = gt_s[c]
        bcs = _dot(tril, jnp.where(isf_c, gc, 0.0), HI)
        brs = _dot(jnp.where(isf_r, gt, 0.0), triu, HI)
        blast = _seq_last(bcs, lsub)
        if sample:
            mcols = mrow_ref[rows, :]
        else:
            mcols = m_s[...]
        heads = range(nheads)
        prs = range(npair)
        ps = [slice(pp * LANES, (pp + 1) * LANES) for pp in prs]
        hs = [slice(h * ML_DV, (h + 1) * ML_DV) for h in heads]
        q2 = [q_s[rows, ps[pp]] for pp in prs]
        k2 = [k_s[rows, ps[pp]] for pp in prs]
        k2b = [_b(z) for z in k2]
        if sample:
            c_in = [[jnp.concatenate([c0_ref[c * nsub + q, 2 * pp], c0_ref[c * nsub + q, 2 * pp + 1]], axis=0)
                     for q in range(nsub)] for pp in prs]
            n_in = [[n0_ref[c * nsub + q][:, ps[pp]] for q in range(nsub)] for pp in prs]
            n_rows = [jnp.concatenate([jnp.broadcast_to(n_in[pp][q], (lsub, LANES)) for q in range(nsub)], axis=0)
                      for pp in prs]
        else:
            c_prev = [cp_s[pp] for pp in prs]
            n_prev = [np_s[:, ps[pp]] for pp in prs]
            n_rows = n_prev
        bcol = [_colsel(bcs, glane, nheads + h) for h in heads]
        licol = [_colsel(gc, glane, h) for h in heads]
        mcol = [_colsel(mcols, hlane, h) for h in heads]
        blcol = [_colsel(blast, glane, nheads + h) for h in heads]
        dlog = [jnp.where(causal, bcol[h] - (brs[nheads + h:nheads + h + 1, :] - gt[h:h + 1, :]), -jnp.inf)
                for h in heads]
        ginter = [bcol[h] + mcol[h] for h in heads]
        m_t = [jnp.maximum(ginter[h], jnp.max(dlog[h], axis=-1, keepdims=True)) for h in heads]
        dw = [jnp.exp(dlog[h] - m_t[h]) for h in heads]
        winter = [jnp.exp(ginter[h] - m_t[h]) for h in heads]
        qh = [jnp.where(mlo if h % 2 == 0 else jnp.logical_not(mlo), q2[h // 2], 0.0) for h in heads]
        qhb = [_b(z) for z in qh]
        sc = [_dot_nt(qhb[h], k2b[h // 2]) * dw[h] for h in heads]
        if sample:
            inter = [jnp.concatenate([_dot(qhb[h][q * lsub:(q + 1) * lsub], _b(c_in[h // 2][q]))
                                      for q in range(nsub)], axis=0) for h in heads]
        else:
            cb = [_b(z) for z in c_prev]
            inter = [_dot(qhb[h], cb[h // 2]) for h in heads]
        vh = [v_s[rows, hs[h]] for h in heads]
        num = [winter[h] * inter[h] + _dot(_b(sc[h]), _b(vh[h])) for h in heads]
        den = [winter[h] * jnp.sum(qh[h] * n_rows[h // 2], axis=-1, keepdims=True)
               + jnp.sum(sc[h], axis=-1, keepdims=True) for h in heads]
        for h in heads:
            hout = num[h] / jnp.maximum(jnp.abs(den[h]), jnp.exp(-m_t[h]))
            hn = hout * lax.rsqrt(jnp.mean(hout * hout, axis=-1, keepdims=True) + NORM_EPS) * nw_ref[:, hs[h]]
            ho_s[rows, hs[h]] = _b(hn * og_s[rows, hs[h]])
        mnew = [_seq_last(m_t[h], lsub) for h in heads]
        ws = [jnp.exp(blcol[h] - bcol[h] + licol[h] - mnew[h]) for h in heads]
        wstc = [jnp.exp(blcol[h] + mcol[h] - mnew[h]) for h in heads]
        mt_all = jnp.zeros((L, nheads), F32)
        for h in heads:
            mt_all = jnp.where(hlane == h, m_t[h], mt_all)
        for pp in prs:
            lo, hi = 2 * pp, 2 * pp + 1
            wsk = jnp.where(mlo, ws[lo], ws[hi]) * k2[pp]
            wst = jnp.where(mlo, wstc[lo], wstc[hi])
            wvb = _b(jnp.concatenate([ws[lo] * vh[lo], ws[hi] * vh[hi]], axis=0))
            kmb = jnp.concatenate([jnp.where(mlo, k2b[pp], 0), jnp.where(mlo, 0, k2b[pp])], axis=0)
            if sample:
                for q in range(nsub):
                    last = (q + 1) * lsub - 1
                    sel = [slice(blk * L + q * lsub, blk * L + (q + 1) * lsub) for blk in range(2)]
                    wq = wst[last:last + 1, :]
                    wrow = jnp.concatenate([jnp.broadcast_to(wstc[h_][last:last + 1, :], (ML_DK, 1)) for h_ in (lo, hi)],
                                           axis=0)
                    upd = _dot_tn(jnp.concatenate([kmb[s_] for s_ in sel], axis=0),
                                  jnp.concatenate([wvb[s_] for s_ in sel], axis=0))
                    c_new = c_in[pp][q] * wrow + upd
                    cout_ref[c * nsub + q, 2 * pp] = c_new[:ML_DK]
                    cout_ref[c * nsub + q, 2 * pp + 1] = c_new[ML_DK:]
                    nq = n_in[pp][q] * wq + jnp.sum(wsk[q * lsub:(q + 1) * lsub], axis=0, keepdims=True)
                    nout_ref[c * nsub + q, :, ps[pp]] = nq
            else:
                wrow = jnp.concatenate([jnp.broadcast_to(wstc[h_], (ML_DK, 1)) for h_ in (lo, hi)], axis=0)
                cp_s[pp] = c_prev[pp] * wrow + _dot_tn(kmb, wvb)
                np_s[:, ps[pp]] = n_prev[pp] * wst + jnp.sum(wsk, axis=0, keepdims=True)
        if sample:
            mout_ref[rows, :] = mt_all
        else:
            m_s[...] = mt_all[L - 1:L, :]
        return carry

    lax.fori_loop(0, nchunk, chunk, 0)
    if not sample:
        for pp in range(npair):
            cout_ref[0, 2 * pp] = cp_s[pp][:ML_DK]
            cout_ref[0, 2 * pp + 1] = cp_s[pp][ML_DK:]
        nout_ref[0] = np_s[...]
        mout_ref[0] = m_s[...]
    o_ref[...] = x_ref[...] + _dot(ho_s[...], wout_ref[...])


def _mlstm_call(x, cvx, mrow, c0p, n0, wts, *, nseq, seq_rows, npad, sample, tm):
    n, d = x.shape
    nqk = wts[1].shape[1]
    nv = wts[2].shape[1]
    nheads = nqk // 2 // ML_DK
    npair = nheads // 2
    lsub = seq_rows if sample else CHUNK
    kern = functools.partial(_mlstm_kernel, tm=tm, seq_rows=seq_rows, npad=npad, sample=sample, lsub=lsub)
    if sample:
        grid = (n // tm,)
        tile = lambda i: (i, 0)
        spt = tm // seq_rows
        c_spec = pl.BlockSpec((spt, nheads, ML_DK, ML_DV), lambda i: (i, 0, 0, 0))
        n_spec = pl.BlockSpec((spt, 1, nqk // 2), lambda i: (i, 0, 0))
        m_spec = pl.BlockSpec((tm, nheads), tile)
        m_shape = (n, nheads)
        cv_spec = pl.BlockSpec((tm, nqk), tile)
        cv_shape = (n, nqk)
        sem = ("arbitrary",)
    else:
        tps = seq_rows // tm
        grid = (nseq, tps)
        tile = lambda b, j: (b * tps + j, 0)
        c_spec = pl.BlockSpec((1, nheads, ML_DK, ML_DV), lambda b, j: (b, 0, 0, 0))
        n_spec = pl.BlockSpec((1, 1, nqk // 2), lambda b, j: (b, 0, 0))
        m_spec = pl.BlockSpec((1, 1, nheads), lambda b, j: (b, 0, 0))
        m_shape = (nseq, 1, nheads)
        cv_spec = pl.BlockSpec((SUBLANES, nqk), lambda b, j: (b, 0))
        cv_shape = (nseq * SUBLANES, nqk)
        sem = ("arbitrary", "arbitrary")
    x_spec = pl.BlockSpec((tm, d), tile)
    in_specs = [x_spec]
    args = [x]
    if sample:
        in_specs += [pl.BlockSpec((tm, nqk), tile), pl.BlockSpec((tm, nheads), tile)]
        args += [cvx, mrow]
    for wt in wts:
        in_specs.append(_const_spec(wt.shape))
        args.append(wt)
    if sample:
        in_specs += [c_spec, n_spec]
        args += [c0p, n0]
    scratch = [pltpu.VMEM((tm // CHUNK, 2 * nheads, CHUNK), F32), pltpu.VMEM((tm, nqk // 2), F32),
               pltpu.VMEM((tm, nqk // 2), F32),
               pltpu.VMEM((tm, nv), F32), pltpu.VMEM((tm, nv), F32), pltpu.VMEM((tm, 2 * nheads), F32),
               pltpu.VMEM((tm, nv), BF16), pltpu.VMEM((tm + SUBLANES, nqk), F32)]
    if not sample:
        scratch += [pltpu.VMEM((npair, LANES, LANES), F32),
                    pltpu.VMEM((1, nqk // 2), F32), pltpu.VMEM((1, nheads), F32)]
    return pl.pallas_call(
        kern,
        grid=grid,
        in_specs=in_specs,
        out_specs=[x_spec, c_spec, n_spec, m_spec, cv_spec],
        out_shape=[jax.ShapeDtypeStruct((n, d), F32),
                   jax.ShapeDtypeStruct((nseq, nheads, ML_DK, ML_DV), F32),
                   jax.ShapeDtypeStruct((nseq, 1, nqk // 2), F32),
                   jax.ShapeDtypeStruct(m_shape, F32),
                   jax.ShapeDtypeStruct(cv_shape, F32)],
        scratch_shapes=scratch,
        compiler_params=pltpu.CompilerParams(dimension_semantics=sem, vmem_limit_bytes=VMEM_LIMIT),
        name="mlstm_sample" if sample else "mlstm_prompt",
    )(*args)


def kernel(x_prompt, x_sample, state_rwkv_S, state_rwkv_shift, state_mlstm_C, state_mlstm_n, state_mlstm_m,
           state_mlstm_conv, norm_ffa, ffa_wg, ffa_wu, ffa_wd, norm_mix, norm_ffb, ffb_wg, ffb_wu, ffb_wd,
           rw_mu, rw_wr, rw_wk, rw_wv, rw_wo, rw_w0, rw_w1, rw_w2, rw_a0, rw_a1, rw_a2, rw_g1, rw_g2,
           rw_k_k, rw_k_a, rw_r_k, rw_gn_w, rw_gn_b, ml_w_in, ml_b_if, ml_conv_w, ml_conv_b, ml_norm_w,
           ml_w_out, norm_final):
    bp, tp, d = x_prompt.shape
    bs, ts, _ = x_sample.shape
    depth = norm_ffa.shape[0]
    slot = SUBLANES
    npad = slot - ts
    assert 0 < ts <= slot and npad >= ML_CONV - 1
    tm_p = min(256, tp)
    tm_ml = min(512, tp)
    tm_s = min(2 * CHUNK, bs * slot)
    assert tp % tm_p == 0 and tm_p % CHUNK == 0 and (bs * slot) % tm_s == 0 and tm_s % CHUNK == 0
    ml_heads = ml_b_if.shape[1] // 2
    nqk = 2 * ml_heads * ML_DK
    nv = ml_heads * ML_DV

    xp = x_prompt.reshape(bp * tp, d)
    xs = x_sample.transpose(1, 0, 2).reshape(ts * bs, d)

    def to_slots(z):
        return jnp.concatenate([jnp.zeros((bs, npad, d), F32), z.reshape(ts, bs, d).transpose(1, 0, 2)],
                               axis=1).reshape(bs * slot, d)

    def from_slots(z):
        return z.reshape(bs, slot, d)[:, npad:].transpose(1, 0, 2).reshape(ts * bs, d)

    row2 = lambda a: a.reshape(1, -1)

    tm_ffn = min(512, bs * ts)
    assert (bp * tp) % tm_ffn == 0 and (bs * ts) % tm_ffn == 0
    ffa = (norm_ffa.reshape(depth, 1, d), _b(ffa_wg), _b(ffa_wu), _b(ffa_wd))
    ffb = (norm_ffb.reshape(depth, 1, d), _b(ffb_wg), _b(ffb_wu), _b(ffb_wd))

    def ffn(xp_, xs_, wset, layer, final):
        return _ffn_call(xp_, xs_, *wset, row2(norm_final), layer, final_norm=final, tm=tm_ffn)

    new_p = {k_: [] for k_ in ("S", "shift", "C", "n", "m", "conv")}
    new_s = {k_: [] for k_ in ("S", "shift", "C", "n", "m", "conv")}
    for i in range(depth):
        xp, xs = ffn(xp, xs, ffa, i, False)
        j = i // 2
        if i % 2 == 0:
            wts = [row2(norm_mix[i]), rw_mu[j], _b(rw_wr[j]), _b(rw_wk[j]), _b(rw_wv[j]), _b(rw_wo[j]),
                   row2(rw_w0[j]), _b(rw_w1[j]), _b(rw_w2[j]), row2(rw_a0[j]), _b(rw_a1[j]), _b(rw_a2[j]),
                   _b(rw_g1[j]), _b(rw_g2[j]), row2(rw_k_k[j]), row2(rw_k_a[j]), row2(rw_r_k[j]),
                   row2(rw_gn_w[j]), row2(rw_gn_b[j])]
            xp, sbd, tail = _rwkv_call(xp, wts, nseq=bp, seq_rows=tp, tm=tm_p)
            new_p["S"].append(sbd)
            new_p["shift"].append(tail.reshape(bp, SUBLANES, d)[:, SUBLANES - 1])
            xs, s_t, sh_new = _rwkv_decode_call(xs, state_rwkv_shift[j], jnp.transpose(state_rwkv_S[j], (1, 2, 3, 0)),
                                                wts, nb=bs, nt=ts)
            new_s["S"].append(jnp.transpose(s_t, (3, 0, 1, 2)))
            new_s["shift"].append(sh_new)
        else:
            w_in = ml_w_in[j]
            w_if = w_in[:, nqk + nv + d:]
            wts = [row2(norm_mix[i]), _b(w_in[:, :nqk]), _b(w_in[:, nqk:nqk + nv]), _b(w_in[:, nqk + nv:nqk + nv + d]),
                   _b(w_if), row2(ml_b_if[j]), ml_conv_w[j],
                   row2(ml_conv_b[j]), row2(ml_norm_w[j]), _b(ml_w_out[j])]
            xp, cp, n_, m_, tail = _mlstm_call(xp, None, None, None, None, wts, nseq=bp, seq_rows=tp, npad=0,
                                               sample=False, tm=tm_ml)
            new_p["C"].append(jnp.swapaxes(cp, -1, -2))
            new_p["n"].append(n_.reshape(bp, ml_heads, ML_DK))
            new_p["m"].append(m_.reshape(bp, ml_heads))
            new_p["conv"].append(tail.reshape(bp, SUBLANES, nqk)[:, SUBLANES - (ML_CONV - 1):])
            conv0 = state_mlstm_conv[j]
            cvx = jnp.concatenate([jnp.zeros((bs, npad - (ML_CONV - 1), nqk), F32), conv0,
                                   jnp.zeros((bs, slot - npad, nqk), F32)], axis=1).reshape(bs * slot, nqk)
            mrow = jnp.repeat(state_mlstm_m[j], slot, axis=0)
            xs, cp, n_, mt, raw = _mlstm_call(to_slots(xs), cvx, mrow, jnp.swapaxes(state_mlstm_C[j], -1, -2),
                                              state_mlstm_n[j].reshape(bs, 1, ml_heads * ML_DK), wts, nseq=bs,
                                              seq_rows=slot, npad=npad, sample=True, tm=tm_s)
            xs = from_slots(xs)
            new_s["C"].append(jnp.swapaxes(cp, -1, -2))
            new_s["n"].append(n_.reshape(bs, ml_heads, ML_DK))
            new_s["m"].append(mt.reshape(bs, slot, ml_heads)[:, slot - 1])
            new_s["conv"].append(raw.reshape(bs, slot, nqk)[:, slot - (ML_CONV - 1):])
        xp, xs = ffn(xp, xs, ffb, i, i == depth - 1)
    y_prompt = xp.reshape(bp, tp, d)
    y_sample = xs.reshape(ts, bs, d).transpose(1, 0, 2)
    st = lambda lst: jnp.stack(lst)
    return (y_prompt, y_sample,
            st(new_p["S"]), st(new_p["shift"]), st(new_p["C"]), st(new_p["n"]), st(new_p["m"]), st(new_p["conv"]),
            st(new_s["S"]), st(new_s["shift"]), st(new_s["C"]), st(new_s["n"]), st(new_s["m"]), st(new_s["conv"]))
```

```python
import functools
import math

import jax
import jax.numpy as jnp
from jax import lax
from jax.experimental import pallas as pl
from jax.experimental.pallas import tpu as pltpu

F32 = jnp.float32
BF16 = jnp.bfloat16

NORM_EPS = 1e-6
RW_GN_EPS = 64e-5
RW_HEAD = 64
ML_DK = 64
ML_DV = 128
ML_CONV = 4

LANES = 128
SUBLANES = 8
MXU_COLS = 256
CHUNK = 64
NEG_BIG = -1e30
VMEM_LIMIT = 58 * 1024 * 1024
HI = lax.Precision.HIGHEST


def _dot(a, b, precision=None):
    return jnp.dot(a, b, preferred_element_type=F32, precision=precision)


def _dot_nt(a, b, precision=None):
    return lax.dot_general(a, b, (((1,), (1,)), ((), ())), preferred_element_type=F32, precision=precision)


def _dot_tn(a, b, precision=None):
    return lax.dot_general(a, b, (((0,), (0,)), ((), ())), preferred_element_type=F32, precision=precision)


def _b(x):
    return x.astype(BF16)


def _rms(x, g):
    return x * lax.rsqrt(jnp.mean(x * x, axis=-1, keepdims=True) + NORM_EPS) * g


def _sigmoid(x):
    return 1.0 / (1.0 + jnp.exp(-x))


def _softplus(x):
    return jnp.maximum(x, 0.0) + jnp.log1p(jnp.exp(-jnp.abs(x)))


def _segsum(x, mlo):
    lo = jnp.sum(jnp.where(mlo, x, 0.0), axis=-1, keepdims=True)
    hi = jnp.sum(jnp.where(mlo, 0.0, x), axis=-1, keepdims=True)
    return jnp.where(mlo, lo, hi)


def _colsel(x, lane_idx, j):
    return jnp.sum(jnp.where(lane_idx == j, x, 0.0), axis=-1, keepdims=True)


def _seq_last(x, lsub):
    n = x.shape[0]
    if lsub == n:
        return x[n - 1:n]
    parts = [jnp.broadcast_to(x[q * lsub + lsub - 1:q * lsub + lsub], (lsub,) + x.shape[1:])
             for q in range(n // lsub)]
    return jnp.concatenate(parts, axis=0)


def _const_spec(shape, single_buffer=False):
    nd = len(shape)
    if single_buffer:
        return pl.BlockSpec(shape, lambda *_: (0,) * nd, pipeline_mode=pl.Buffered(1))
    return pl.BlockSpec(shape, lambda *_: (0,) * nd)


def _log2(n):
    k = int(math.log2(n))
    assert 1 << k == n, n
    return k


def _ffn_kernel(xp_ref, xs_ref, g_ref, wg_ref, wu_ref, wd_ref, gf_ref, op_ref, os_ref, *, fchunk, final_norm, np_steps):
    def tile(x_ref, o_ref):
        x = x_ref[...]
        xb = _b(_rms(x, g_ref[...]))
        acc = jnp.zeros(x.shape, F32)
        for lo, hi in zip(fchunk[:-1], fchunk[1:]):
            sl = slice(lo, hi)
            gate = _dot(xb, wg_ref[:, sl])
            up = _dot(xb, wu_ref[:, sl])
            h = _b(gate * _sigmoid(gate) * up)
            acc = acc + _dot(h, wd_ref[sl, :])
        out = x + 0.5 * acc
        if final_norm:
            out = _rms(out, gf_ref[...])
        o_ref[...] = out

    is_p = pl.program_id(0) < np_steps
    pl.when(is_p)(functools.partial(tile, xp_ref, op_ref))
    pl.when(jnp.logical_not(is_p))(functools.partial(tile, xs_ref, os_ref))


def _ffn_call(xp, xs, g_all, wg_all, wu_all, wd_all, gf, layer, *, final_norm, tm):
    (n_p, d), n_s = xp.shape, xs.shape[0]
    nf = wg_all.shape[2]
    np_steps, ns_steps = n_p // tm, n_s // tm
    ntile = nf // MXU_COLS
    fchunk = (0, -(-ntile // 2) * MXU_COLS, nf) if nf % MXU_COLS == 0 and ntile > 1 else (0, nf)
    kern = functools.partial(_ffn_kernel, fchunk=fchunk, final_norm=final_norm, np_steps=np_steps)
    p_spec = pl.BlockSpec((tm, d), lambda i: (jnp.minimum(i, np_steps - 1), 0))
    s_spec = pl.BlockSpec((tm, d), lambda i: (jnp.maximum(i - np_steps, 0), 0))
    w_spec = lambda a, b: pl.BlockSpec((None, a, b), lambda i: (layer, 0, 0), pipeline_mode=pl.Buffered(1))
    return pl.pallas_call(
        kern,
        grid=(np_steps + ns_steps,),
        in_specs=[p_spec, s_spec, pl.BlockSpec((None, 1, d), lambda i: (layer, 0, 0)),
                  w_spec(d, nf), w_spec(d, nf), w_spec(nf, d), _const_spec((1, d))],
        out_specs=[p_spec, s_spec],
        out_shape=[jax.ShapeDtypeStruct((n_p, d), F32), jax.ShapeDtypeStruct((n_s, d), F32)],
        compiler_params=pltpu.CompilerParams(dimension_semantics=("arbitrary",), vmem_limit_bytes=VMEM_LIMIT),
        name="ffn",
    )(xp, xs, g_all, wg_all, wu_all, wd_all, gf)


def _rw_token_part(xn, xx, mlo, put, wrefs):
    mu_ref, wr_ref, wk_ref, wv_ref, w0_ref, w1_ref, w2_ref, a0_ref, a1_ref, a2_ref, g1_ref, g2_ref, kk_ref, ka_ref, rk_ref = wrefs
    cols = [slice(p * LANES, (p + 1) * LANES) for p in range(xn.shape[1] // LANES)]

    def put_all(name, z):
        for p, cs in enumerate(cols):
            put(name, p, z[:, cs])

    def mix(i):
        return _b(xn + xx * mu_ref[i:i + 1, :])

    r = _dot(mix(0), wr_ref[...])
    put_all("r", r)
    wl = _b(jnp.tanh(_dot(mix(1), w1_ref[...])))
    ld = -math.exp(-0.5) * _sigmoid(w0_ref[...] + _dot(wl, w2_ref[...]))
    k = _dot(mix(2), wk_ref[...])
    v = _dot(mix(3), wv_ref[...])
    al = _b(_dot(mix(4), a1_ref[...]))
    a = _sigmoid(a0_ref[...] + _dot(al, a2_ref[...]))
    gl = _b(_sigmoid(_dot(mix(5), g1_ref[...])))
    put_all("g", _dot(gl, g2_ref[...]))
    k2 = k * (1.0 + (a - 1.0) * ka_ref[...])
    kk = k * kk_ref[...]
    rkk = r * k2 * rk_ref[...]
    put_all("ld", ld)
    put_all("k", k2)
    put_all("v", v)
    for p, cs in enumerate(cols):
        kkp = kk[:, cs]
        kn = kkp / jnp.maximum(jnp.sqrt(_segsum(kkp * kkp, mlo)), 1e-12)
        put("kn", p, kn)
        put("b", p, kn * a[:, cs])
        put("bon", p, _segsum(rkk[:, cs], mlo) * v[:, cs])


def _rwkv_kernel(*refs, tm):
    L = CHUNK
    nchunk = tm // L
    nsteps = _log2(L)
    it = iter(refs)
    x_ref = next(it)
    (gn_ref, mu_ref, wr_ref, wk_ref, wv_ref, wo_ref, w0_ref, w1_ref, w2_ref, a0_ref, a1_ref, a2_ref,
     g1_ref, g2_ref, kk_ref, ka_ref, rk_ref, gw_ref, gb_ref) = [next(it) for _ in range(19)]
    o_ref, sout_ref, xn_ref = next(it), next(it), next(it)
    r_s, ld_s, k_s, v_s, kn_s, b_s, g_s, bon_s = sset = [next(it) for _ in range(8)]
    yg_s, carry_s, sbd_s = next(it), next(it), next(it)
    d = x_ref.shape[1]
    npair = d // LANES

    lane = lax.broadcasted_iota(jnp.int32, (1, LANES), 1)
    mlo = lane < RW_HEAD
    row = lax.broadcasted_iota(jnp.int32, (tm, 1), 0)

    @pl.when(pl.program_id(1) == 0)
    def _():
        carry_s[...] = jnp.zeros(carry_s.shape, F32)
        sbd_s[...] = jnp.zeros(sbd_s.shape, F32)

    xn = _rms(x_ref[...], gn_ref[...])
    xprev = jnp.where(row == 0, carry_s[SUBLANES - 1:SUBLANES, :], pltpu.roll(xn, 1, 0))
    carry_s[...] = xn[tm - SUBLANES:tm, :]
    xn_ref[...] = xn[tm - SUBLANES:tm, :]
    sref = dict(zip(("r", "ld", "k", "v", "kn", "b", "g", "bon"), sset))

    def put(name, p, tile):
        sref[name][:, p * LANES:(p + 1) * LANES] = tile

    _rw_token_part(xn, xprev - xn, mlo, put,
                   (mu_ref, wr_ref, wk_ref, wv_ref, w0_ref, w1_ref, w2_ref, a0_ref, a1_ref, a2_ref,
                    g1_ref, g2_ref, kk_ref, ka_ref, rk_ref))

    ti = lax.broadcasted_iota(jnp.int32, (L, 3 * L), 0)
    si = lax.broadcasted_iota(jnp.int32, (L, 3 * L), 1) & (L - 1)
    tril3 = jnp.where(si <= ti, 1.0, 0.0).astype(BF16)
    gi = lax.broadcasted_iota(jnp.int32, (2 * L, 4 * L), 0)
    gj = lax.broadcasted_iota(jnp.int32, (2 * L, 4 * L), 1)
    gt = gi & (L - 1)
    gs = gj & (L - 1)
    gmask = (gs < gt) | ((gi >= L) & (gs == gt))
    pairs = range(npair)

    def stack2(z):
        return jnp.concatenate([jnp.where(mlo, z, 0), jnp.where(mlo, 0, z)], axis=0)

    cs = [slice(p * LANES, (p + 1) * LANES) for p in pairs]
    rw = {(c, p): slice(c * L, (c + 1) * L) for c in range(nchunk) for p in pairs}
    chains = [(c, p) for c in range(nchunk) for p in pairs]

    cum, a2, q4, vv2 = {}, {}, {}, {}
    for key in chains:
        ldc = ld_s[rw[key], cs[key[1]]]
        hi = _b(ldc)
        r1 = ldc - hi.astype(F32)
        mid = _b(r1)
        lo = _b(r1 - mid.astype(F32))
        cum[key] = _dot(tril3, jnp.concatenate([hi, mid, lo], axis=0))
    for key in chains:
        rows, c_ = rw[key], cs[key[1]]
        ep = jnp.exp(cum[key])
        em = jnp.exp(-cum[key])
        at = -(kn_s[rows, c_] * jnp.exp(cum[key] - ld_s[rows, c_]))
        a2[key] = _b(jnp.concatenate([at, r_s[rows, c_] * ep], axis=0))
        q4[key] = jnp.concatenate([stack2(b_s[rows, c_] * em), stack2(k_s[rows, c_] * em)], axis=0)
        vv2[key] = _b(stack2(v_s[rows, c_]))
    g = {key: jnp.where(gmask, _dot_nt(a2[key], _b(q4[key])), 0.0) for key in chains}
    makv = {key: _dot(_b(g[key][:, LANES:]), vv2[key]) for key in chains}
    mr = {key: _b(g[key][L:, :LANES]) for key in chains}
    w = {key: g[key][:L, :LANES] for key in chains}
    pf = {}
    for key in chains:
        pc = _b(w[key])
        pf[key] = _dot(pc, stack2(pc))
    for k in range(1, nsteps):
        for key in chains:
            pc = _b(pf[key])
            if k + 1 < nsteps:
                both = _dot(pc, jnp.concatenate([stack2(pc), stack2(_b(w[key]))], axis=1))
                w[key] = w[key] + pf[key] + both[:, LANES:]
                pf[key] = both[:, :LANES]
            else:
                w[key] = w[key] + pf[key] + _dot(pc, stack2(_b(w[key])))
    wcat = {key: _b(w[key]) for key in chains}
    pl_ = {key: jnp.exp(cum[key][L - 1:L, :]) for key in chains}
    qpl = {key: _b(q4[key] * pl_[key]) for key in chains}

    state = [sbd_s[p] for p in pairs]
    for c in range(nchunk):
        as_ = [_dot_nt(a2[c, p], _b(state[p])) for p in pairs]
        rhs = [as_[p][:L] + makv[c, p][:L] for p in pairs]
        u = [rhs[p] + _dot(wcat[c, p], stack2(_b(rhs[p]))) for p in pairs]
        u2 = [stack2(_b(u[p])) for p in pairs]
        uv = [jnp.concatenate([u2[p], vv2[c, p]], axis=0) for p in pairs]
        y = [as_[p][L:] + makv[c, p][L:] + _dot(mr[c, p], u2[p]) for p in pairs]
        for p in pairs:
            state[p] = state[p] * pl_[c, p] + _dot_tn(uv[p], qpl[c, p])
        rows = rw[c, 0]
        for p in pairs:
            mean = _segsum(y[p], mlo) * (1.0 / RW_HEAD)
            yc = y[p] - mean
            var = _segsum(yc * yc, mlo) * (1.0 / RW_HEAD)
            yn = yc * lax.rsqrt(var + RW_GN_EPS) * gw_ref[:, cs[p]] + gb_ref[:, cs[p]] + bon_s[rows, cs[p]]
            yg_s[rows, cs[p]] = _b(yn * g_s[rows, cs[p]])
    for p in pairs:
        sbd_s[p] = state[p]
        sout_ref[0, 2 * p] = state[p][:RW_HEAD, :RW_HEAD]
        sout_ref[0, 2 * p + 1] = state[p][RW_HEAD:, RW_HEAD:]
    o_ref[...] = x_ref[...] + _dot(yg_s[...], wo_ref[...])


def _rwkv_call(x, wts, *, nseq, seq_rows, tm):
    n, d = x.shape
    npair = d // LANES
    tps = seq_rows // tm
    kern = functools.partial(_rwkv_kernel, tm=tm)
    x_spec = pl.BlockSpec((tm, d), lambda b, j: (b * tps + j, 0))
    st_spec = pl.BlockSpec((1, 2 * npair, RW_HEAD, RW_HEAD), lambda b, j: (b, 0, 0, 0))
    xn_spec = pl.BlockSpec((SUBLANES, d), lambda b, j: (b, 0))
    scratch = ([pltpu.VMEM((tm, d), F32) for _ in range(8)] + [pltpu.VMEM((tm, d), BF16)]
               + [pltpu.VMEM((SUBLANES, d), F32), pltpu.VMEM((npair, LANES, LANES), F32)])
    return pl.pallas_call(
        kern,
        grid=(nseq, tps),
        in_specs=[x_spec] + [_const_spec(wt.shape, single_buffer=True) for wt in wts],
        out_specs=[x_spec, st_spec, xn_spec],
        out_shape=[jax.ShapeDtypeStruct((n, d), F32),
                   jax.ShapeDtypeStruct((nseq, 2 * npair, RW_HEAD, RW_HEAD), F32),
                   jax.ShapeDtypeStruct((nseq * SUBLANES, d), F32)],
        scratch_shapes=scratch,
        compiler_params=pltpu.CompilerParams(dimension_semantics=("arbitrary", "arbitrary"),
                                             vmem_limit_bytes=VMEM_LIMIT),
        name="rwkv_prompt",
    )(x, *wts)


def _rwkv_decode_kernel(*refs, nb, nt):
    it = iter(refs)
    x_ref, sh_ref = next(it), next(it)
    (gn_ref, mu_ref, wr_ref, wk_ref, wv_ref, wo_ref, w0_ref, w1_ref, w2_ref, a0_ref, a1_ref, a2_ref,
     g1_ref, g2_ref, kk_ref, ka_ref, rk_ref, gw_ref, gb_ref) = [next(it) for _ in range(19)]
    s0_ref = next(it)
    o_ref, sout_ref, xn_ref = next(it), next(it), next(it)
    tr = dict(zip(("r", "ld", "k", "v", "kn", "b"), [next(it) for _ in range(6)]))
    yt_s, g_s, bon_s, yg_s = [next(it) for _ in range(4)]
    n, d = x_ref.shape
    npair = d // LANES
    nhead = 2 * npair
    h = pl.program_id(0)
    lane = lax.broadcasted_iota(jnp.int32, (1, LANES), 1)
    mlo = lane < RW_HEAD

    @pl.when(h == 0)
    def _():
        xn = _rms(x_ref[...], gn_ref[...])
        xprev = jnp.concatenate([sh_ref[...], xn[:n - nb]], axis=0)
        xn_ref[...] = xn[n - nb:]

        def put(name, p, tile):
            if name == "g":
                g_s[:, p * LANES:(p + 1) * LANES] = tile
            elif name == "bon":
                bon_s[:, p * LANES:(p + 1) * LANES] = tile
            else:
                tr[name][p] = (jnp.exp(tile) if name == "ld" else tile).T

        _rw_token_part(xn, xprev - xn, mlo, put,
                       (mu_ref, wr_ref, wk_ref, wv_ref, w0_ref, w1_ref, w2_ref, a0_ref, a1_ref, a2_ref,
                        g1_ref, g2_ref, kk_ref, ka_ref, rk_ref))

    p = h >> 1
    base = pl.multiple_of((h & 1) * RW_HEAD, RW_HEAD)
    sub = lax.broadcasted_iota(jnp.int32, (SUBLANES, 1), 0)
    hrows = pl.ds(base, RW_HEAD)

    def group(gi, carry):
        r0 = pl.multiple_of(base + gi * SUBLANES, SUBLANES)
        vg = [tr["v"][p, pl.ds(r0, SUBLANES), t * nb:(t + 1) * nb] for t in range(nt)]
        ys = [jnp.zeros((SUBLANES, nb), F32) for _ in range(nt)]
        for vi in range(SUBLANES):
            v_idx = gi * SUBLANES + vi
            s = s0_ref[0, v_idx]
            for t in range(nt):
                tc = slice(t * nb, (t + 1) * nb)
                sa = -jnp.sum(s * tr["kn"][p, hrows, tc], axis=0, keepdims=True)
                s = s * tr["ld"][p, hrows, tc] + sa * tr["b"][p, hrows, tc] + vg[t][vi:vi + 1, :] * tr["k"][p, hrows, tc]
                yrow = jnp.sum(s * tr["r"][p, hrows, tc], axis=0, keepdims=True)
                ys[t] = jnp.where(sub == vi, yrow, ys[t])
            sout_ref[0, v_idx] = s
        for t in range(nt):
            yt_s[p, pl.ds(r0, SUBLANES), t * nb:(t + 1) * nb] = ys[t]
        return carry

    lax.fori_loop(0, RW_HEAD // SUBLANES, group, 0)

    @pl.when(h == nhead - 1)
    def _():
        for q in range(npair):
            cs = slice(q * LANES, (q + 1) * LANES)
            y = yt_s[q].T
            mean = _segsum(y, mlo) * (1.0 / RW_HEAD)
            yc = y - mean
            var = _segsum(yc * yc, mlo) * (1.0 / RW_HEAD)
            yn = yc * lax.rsqrt(var + RW_GN_EPS) * gw_ref[:, cs] + gb_ref[:, cs] + bon_s[:, cs]
            yg_s[:, cs] = _b(yn * g_s[:, cs])
        o_ref[...] = x_ref[...] + _dot(yg_s[...], wo_ref[...])


def _rwkv_decode_call(xt, shift0, s0t, wts, *, nb, nt):
    n, d = xt.shape
    npair = d // LANES
    nhead = 2 * npair
    kern = functools.partial(_rwkv_decode_kernel, nb=nb, nt=nt)
    full = lambda shape: pl.BlockSpec(shape, lambda h: (0,) * len(shape))
    st_spec = pl.BlockSpec((1, RW_HEAD, RW_HEAD, nb), lambda h: (h, 0, 0, 0))
    in_specs = [full((n, d)), full((nb, d))] + [_const_spec(wt.shape, single_buffer=True) for wt in wts] + [st_spec]
    scratch = ([pltpu.VMEM((npair, LANES, n), F32) for _ in range(7)]
               + [pltpu.VMEM((n, d), F32), pltpu.VMEM((n, d), F32), pltpu.VMEM((n, d), BF16)])
    return pl.pallas_call(
        kern,
        grid=(nhead,),
        in_specs=in_specs,
        out_specs=[full((n, d)), st_spec, full((nb, d))],
        out_shape=[jax.ShapeDtypeStruct((n, d), F32), jax.ShapeDtypeStruct(s0t.shape, F32),
                   jax.ShapeDtypeStruct((nb, d), F32)],
        scratch_shapes=scratch,
        compiler_params=pltpu.CompilerParams(dimension_semantics=("arbitrary",), vmem_limit_bytes=VMEM_LIMIT),
        name="rwkv_decode",
    )(xt, shift0, *wts, s0t)


def _mlstm_kernel(*refs, tm, seq_rows, npad, sample, lsub):
    L = CHUNK
    nsub = L // lsub
    nchunk = tm // L
    it = iter(refs)
    x_ref = next(it)
    if sample:
        cvx_ref, mrow_ref = next(it), next(it)
    gn_ref, wqk_ref, wv_ref, wo_ref, wif_ref, bif_ref, cw_ref, cb_ref, nw_ref, wout_ref = [next(it) for _ in range(10)]
    if sample:
        c0_ref, n0_ref = next(it), next(it)
    o_ref, cout_ref, nout_ref, mout_ref, cvout_ref = [next(it) for _ in range(5)]
    gt_s, q_s, k_s, v_s, og_s, gc_s, ho_s, ext_s = [next(it) for _ in range(8)]
    if not sample:
        cp_s, np_s, m_s = [next(it) for _ in range(3)]
    nqk = wqk_ref.shape[1]
    half = nqk // 2
    nheads = half // ML_DK
    ngate = 2 * nheads
    npair = nheads // 2

    lane = lax.broadcasted_iota(jnp.int32, (1, LANES), 1)
    mlo = lane < ML_DK
    glane = lax.broadcasted_iota(jnp.int32, (1, ngate), 1)
    isf_c = glane >= nheads
    grow = lax.broadcasted_iota(jnp.int32, (ngate, 1), 0)
    isf_r = grow >= nheads
    hlane = lax.broadcasted_iota(jnp.int32, (1, nheads), 1)

    x = x_ref[...]
    xb = _b(_rms(x, gn_ref[...]))
    raw = _dot(xb, wqk_ref[...])
    row = lax.broadcasted_iota(jnp.int32, (tm, 1), 0)
    if sample:
        srow = row & (seq_rows - 1)
        raw = jnp.where((srow >= npad - (ML_CONV - 1)) & (srow < npad), cvx_ref[...], raw)
        cvout_ref[...] = raw
        ext_s[:SUBLANES, :] = jnp.zeros((SUBLANES, nqk), F32)
    else:
        j = pl.program_id(1)

        @pl.when(j == 0)
        def _():
            ext_s[:SUBLANES, :] = jnp.zeros((SUBLANES, nqk), F32)
            cp_s[...] = jnp.zeros(cp_s.shape, F32)
            np_s[...] = jnp.zeros(np_s.shape, F32)
            m_s[...] = jnp.zeros(m_s.shape, F32)

        cvout_ref[...] = raw[tm - SUBLANES:tm, :]
    ext_s[SUBLANES:, :] = raw
    qk = cb_ref[...] + cw_ref[ML_CONV - 1:ML_CONV, :] * raw
    for s in range(1, ML_CONV):
        qk = qk + cw_ref[ML_CONV - 1 - s:ML_CONV - s, :] * ext_s[SUBLANES - s:SUBLANES - s + tm, :]
    if not sample:
        ext_s[:SUBLANES, :] = raw[tm - SUBLANES:tm, :]
    qk = qk * _sigmoid(qk)
    q_s[...] = qk[:, :half] * (ML_DK ** -0.5)
    k_s[...] = qk[:, half:]
    v_s[...] = _dot(xb, wv_ref[...])
    og_s[...] = _sigmoid(_dot(xb, wo_ref[...]))
    ifp = _dot(xb, wif_ref[...]) + bif_ref[...]
    gcol = jnp.where(isf_c, -_softplus(-ifp), ifp)
    if npad:
        keep = (row & (seq_rows - 1)) >= npad
        gcol = jnp.where(keep, gcol, jnp.where(isf_c, 0.0, NEG_BIG))
    gc_s[...] = gcol
    gct = gcol.T
    for c in range(nchunk):
        gt_s[c] = gct[:, c * L:(c + 1) * L]

    ti = lax.broadcasted_iota(jnp.int32, (L, L), 0)
    si = lax.broadcasted_iota(jnp.int32, (L, L), 1)
    sh = _log2(lsub)
    same = (ti >> sh) == (si >> sh)
    causal = same & (si <= ti)
    tril = jnp.where(causal, 1.0, 0.0).astype(F32)
    triu = jnp.where(same & (ti <= si), 1.0, 0.0).astype(F32)

    def chunk(c, carry):
        r0 = pl.multiple_of(c * L, L)
        rows = pl.ds(r0, L)
        gc = gc_s[rows, :]
        gt = gt_s[c]
        bcs = _dot(tril, jnp.where(isf_c, gc, 0.0), HI)
        brs = _dot(jnp.where(isf_r, gt, 0.0), triu, HI)
        blast = _seq_last(bcs, lsub)
        if sample:
            mcols = mrow_ref[rows, :]
        else:
            mcols = m_s[...]
        heads = range(nheads)
        prs = range(npair)
        ps = [slice(pp * LANES, (pp + 1) * LANES) for pp in prs]
        hs = [slice(h * ML_DV, (h + 1) * ML_DV) for h in heads]
        q2 = [q_s[rows, ps[pp]] for pp in prs]
        k2 = [k_s[rows, ps[pp]] for pp in prs]
        k2b = [_b(z) for z in k2]
        if sample:
            c_in = [[jnp.concatenate([c0_ref[c * nsub + q, 2 * pp], c0_ref[c * nsub + q, 2 * pp + 1]], axis=0)
                     for q in range(nsub)] for pp in prs]
            n_in = [[n0_ref[c * nsub + q][:, ps[pp]] for q in range(nsub)] for pp in prs]
            n_rows = [jnp.concatenate([jnp.broadcast_to(n_in[pp][q], (lsub, LANES)) for q in range(nsub)], axis=0)
                      for pp in prs]
        else:
            c_prev = [cp_s[pp] for pp in prs]
            n_prev = [np_s[:, ps[pp]] for pp in prs]
            n_rows = n_prev
        bcol = [_colsel(bcs, glane, nheads + h) for h in heads]
        licol = [_colsel(gc, glane, h) for h in heads]
        mcol = [_colsel(mcols, hlane, h) for h in heads]
        blcol = [_colsel(blast, glane, nheads + h) for h in heads]
        dlog = [jnp.where(causal, bcol[h] - (brs[nheads + h:nheads + h + 1, :] - gt[h:h + 1, :]), -jnp.inf)
                for h in heads]
        ginter = [bcol[h] + mcol[h] for h in heads]
        m_t = [jnp.maximum(ginter[h], jnp.max(dlog[h], axis=-1, keepdims=True)) for h in heads]
        dw = [jnp.exp(dlog[h] - m_t[h]) for h in heads]
        winter = [jnp.exp(ginter[h] - m_t[h]) for h in heads]
        qh = [jnp.where(mlo if h % 2 == 0 else jnp.logical_not(mlo), q2[h // 2], 0.0) for h in heads]
        qhb = [_b(z) for z in qh]
        sc = [_dot_nt(qhb[h], k2b[h // 2]) * dw[h] for h in heads]
        if sample:
            inter = [jnp.concatenate([_dot(qhb[h][q * lsub:(q + 1) * lsub], _b(c_in[h // 2][q]))
                                      for q in range(nsub)], axis=0) for h in heads]
        else:
            cb = [_b(z) for z in c_prev]
            inter = [_dot(qhb[h], cb[h // 2]) for h in heads]
        vh = [v_s[rows, hs[h]] for h in heads]
        num = [winter[h] * inter[h] + _dot(_b(sc[h]), _b(vh[h])) for h in heads]
        den = [winter[h] * jnp.sum(qh[h] * n_rows[h // 2], axis=-1, keepdims=True)
               + jnp.sum(sc[h], axis=-1, keepdims=True) for h in heads]
        for h in heads:
            hout = num[h] / jnp.maximum(jnp.abs(den[h]), jnp.exp(-m_t[h]))
            hn = hout * lax.rsqrt(jnp.mean(hout * hout, axis=-1, keepdims=True) + NORM_EPS) * nw_ref[:, hs[h]]
            ho_s[rows, hs[h]] = _b(hn * og_s[rows, hs[h]])
        mnew = [_seq_last(m_t[h], lsub) for h in heads]
        ws = [jnp.exp(blcol[h] - bcol[h] + licol[h] - mnew[h]) for h in heads]
        wstc = [jnp.exp(blcol[h] + mcol[h] - mnew[h]) for h in heads]
        mt_all = jnp.zeros((L, nheads), F32)
        for h in heads:
            mt_all = jnp.where(hlane == h, m_t[h], mt_all)
        for pp in prs:
            lo, hi = 2 * pp, 2 * pp + 1
            wsk = jnp.where(mlo, ws[lo], ws[hi]) * k2[pp]
            wst = jnp.where(mlo, wstc[lo], wstc[hi])
            wvb = _b(jnp.concatenate([ws[lo] * vh[lo], ws[hi] * vh[hi]], axis=0))
            kmb = jnp.concatenate([jnp.where(mlo, k2b[pp], 0), jnp.where(mlo, 0, k2b[pp])], axis=0)
            if sample:
                for q in range(nsub):
                    last = (q + 1) * lsub - 1
                    sel = [slice(blk * L + q * lsub, blk * L + (q + 1) * lsub) for blk in range(2)]
                    wq = wst[last:last + 1, :]
                    wrow = jnp.concatenate([jnp.broadcast_to(wstc[h_][last:last + 1, :], (ML_DK, 1)) for h_ in (lo, hi)],
                                           axis=0)
                    upd = _dot_tn(jnp.concatenate([kmb[s_] for s_ in sel], axis=0),
                                  jnp.concatenate([wvb[s_] for s_ in sel], axis=0))
                    c_new = c_in[pp][q] * wrow + upd
                    cout_ref[c * nsub + q, 2 * pp] = c_new[:ML_DK]
                    cout_ref[c * nsub + q, 2 * pp + 1] = c_new[ML_DK:]
                    nq = n_in[pp][q] * wq + jnp.sum(wsk[q * lsub:(q + 1) * lsub], axis=0, keepdims=True)
                    nout_ref[c * nsub + q, :, ps[pp]] = nq
            else:
                wrow = jnp.concatenate([jnp.broadcast_to(wstc[h_], (ML_DK, 1)) for h_ in (lo, hi)], axis=0)
                cp_s[pp] = c_prev[pp] * wrow + _dot_tn(kmb, wvb)
                np_s[:, ps[pp]] = n_prev[pp] * wst + jnp.sum(wsk, axis=0, keepdims=True)
        if sample:
            mout_ref[rows, :] = mt_all
        else:
            m_s[...] = mt_all[L - 1:L, :]
        return carry

    lax.fori_loop(0, nchunk, chunk, 0)
    if not sample:
        for pp in range(npair):
            cout_ref[0, 2 * pp] = cp_s[pp][:ML_DK]
            cout_ref[0, 2 * pp + 1] = cp_s[pp][ML_DK:]
        nout_ref[0] = np_s[...]
        mout_ref[0] = m_s[...]
    o_ref[...] = x_ref[...] + _dot(ho_s[...], wout_ref[...])


def _mlstm_call(x, cvx, mrow, c0p, n0, wts, *, nseq, seq_rows, npad, sample, tm):
    n, d = x.shape
    nqk = wts[1].shape[1]
    nv = wts[2].shape[1]
    nheads = nqk // 2 // ML_DK
    npair = nheads // 2
    lsub = seq_rows if sample else CHUNK
    kern = functools.partial(_mlstm_kernel, tm=tm, seq_rows=seq_rows, npad=npad, sample=sample, lsub=lsub)
    if sample:
        grid = (n // tm,)
        tile = lambda i: (i, 0)
        spt = tm // seq_rows
        c_spec = pl.BlockSpec((spt, nheads, ML_DK, ML_DV), lambda i: (i, 0, 0, 0))
        n_spec = pl.BlockSpec((spt, 1, nqk // 2), lambda i: (i, 0, 0))
        m_spec = pl.BlockSpec((tm, nheads), tile)
        m_shape = (n, nheads)
        cv_spec = pl.BlockSpec((tm, nqk), tile)
        cv_shape = (n, nqk)
        sem = ("arbitrary",)
    else:
        tps = seq_rows // tm
        grid = (nseq, tps)
        tile = lambda b, j: (b * tps + j, 0)
        c_spec = pl.BlockSpec((1, nheads, ML_DK, ML_DV), lambda b, j: (b, 0, 0, 0))
        n_spec = pl.BlockSpec((1, 1, nqk // 2), lambda b, j: (b, 0, 0))
        m_spec = pl.BlockSpec((1, 1, nheads), lambda b, j: (b, 0, 0))
        m_shape = (nseq, 1, nheads)
        cv_spec = pl.BlockSpec((SUBLANES, nqk), lambda b, j: (b, 0))
        cv_shape = (nseq * SUBLANES, nqk)
        sem = ("arbitrary", "arbitrary")
    x_spec = pl.BlockSpec((tm, d), tile)
    in_specs = [x_spec]
    args = [x]
    if sample:
        in_specs += [pl.BlockSpec((tm, nqk), tile), pl.BlockSpec((tm, nheads), tile)]
        args += [cvx, mrow]
    for wt in wts:
        in_specs.append(_const_spec(wt.shape, single_buffer=True))
        args.append(wt)
    if sample:
        in_specs += [c_spec, n_spec]
        args += [c0p, n0]
    scratch = [pltpu.VMEM((tm // CHUNK, 2 * nheads, CHUNK), F32), pltpu.VMEM((tm, nqk // 2), F32),
               pltpu.VMEM((tm, nqk // 2), F32),
               pltpu.VMEM((tm, nv), F32), pltpu.VMEM((tm, nv), F32), pltpu.VMEM((tm, 2 * nheads), F32),
               pltpu.VMEM((tm, nv), BF16), pltpu.VMEM((tm + SUBLANES, nqk), F32)]
    if not sample:
        scratch += [pltpu.VMEM((npair, LANES, LANES), F32),
                    pltpu.VMEM((1, nqk // 2), F32), pltpu.VMEM((1, nheads), F32)]
    return pl.pallas_call(
        kern,
        grid=grid,
        in_specs=in_specs,
        out_specs=[x_spec, c_spec, n_spec, m_spec, cv_spec],
        out_shape=[jax.ShapeDtypeStruct((n, d), F32),
                   jax.ShapeDtypeStruct((nseq, nheads, ML_DK, ML_DV), F32),
                   jax.ShapeDtypeStruct((nseq, 1, nqk // 2), F32),
                   jax.ShapeDtypeStruct(m_shape, F32),
                   jax.ShapeDtypeStruct(cv_shape, F32)],
        scratch_shapes=scratch,
        compiler_params=pltpu.CompilerParams(dimension_semantics=sem, vmem_limit_bytes=VMEM_LIMIT),
        name="mlstm_sample" if sample else "mlstm_prompt",
    )(*args)


def kernel(x_prompt, x_sample, state_rwkv_S, state_rwkv_shift, state_mlstm_C, state_mlstm_n, state_mlstm_m,
           state_mlstm_conv, norm_ffa, ffa_wg, ffa_wu, ffa_wd, norm_mix, norm_ffb, ffb_wg, ffb_wu, ffb_wd,
           rw_mu, rw_wr, rw_wk, rw_wv, rw_wo, rw_w0, rw_w1, rw_w2, rw_a0, rw_a1, rw_a2, rw_g1, rw_g2,
           rw_k_k, rw_k_a, rw_r_k, rw_gn_w, rw_gn_b, ml_w_in, ml_b_if, ml_conv_w, ml_conv_b, ml_norm_w,
           ml_w_out, norm_final):
    bp, tp, d = x_prompt.shape
    bs, ts, _ = x_sample.shape
    depth = norm_ffa.shape[0]
    slot = SUBLANES
    npad = slot - ts
    assert 0 < ts <= slot and npad >= ML_CONV - 1
    tm_p = min(512, tp)
    tm_ml = min(1024, tp)
    tm_s = min(2 * CHUNK, bs * slot)
    assert tp % tm_p == 0 and tm_p % CHUNK == 0 and (bs * slot) % tm_s == 0 and tm_s % CHUNK == 0
    ml_heads = ml_b_if.shape[1] // 2
    nqk = 2 * ml_heads * ML_DK
    nv = ml_heads * ML_DV

    xp = x_prompt.reshape(bp * tp, d)
    xs = x_sample.transpose(1, 0, 2).reshape(ts * bs, d)

    def to_slots(z):
        return jnp.concatenate([jnp.zeros((bs, npad, d), F32), z.reshape(ts, bs, d).transpose(1, 0, 2)],
                               axis=1).reshape(bs * slot, d)

    def from_slots(z):
        return z.reshape(bs, slot, d)[:, npad:].transpose(1, 0, 2).reshape(ts * bs, d)

    row2 = lambda a: a.reshape(1, -1)

    tm_ffn = min(512, bs * ts)
    assert (bp * tp) % tm_ffn == 0 and (bs * ts) % tm_ffn == 0
    ffa = (norm_ffa.reshape(depth, 1, d), _b(ffa_wg), _b(ffa_wu), _b(ffa_wd))
    ffb = (norm_ffb.reshape(depth, 1, d), _b(ffb_wg), _b(ffb_wu), _b(ffb_wd))

    def ffn(xp_, xs_, wset, layer, final):
        return _ffn_call(xp_, xs_, *wset, row2(norm_final), layer, final_norm=final, tm=tm_ffn)

    new_p = {k_: [] for k_ in ("S", "shift", "C", "n", "m", "conv")}
    new_s = {k_: [] for k_ in ("S", "shift", "C", "n", "m", "conv")}
    for i in range(depth):
        xp, xs = ffn(xp, xs, ffa, i, False)
        j = i // 2
        if i % 2 == 0:
            wts = [row2(norm_mix[i]), rw_mu[j], _b(rw_wr[j]), _b(rw_wk[j]), _b(rw_wv[j]), _b(rw_wo[j]),
                   row2(rw_w0[j]), _b(rw_w1[j]), _b(rw_w2[j]), row2(rw_a0[j]), _b(rw_a1[j]), _b(rw_a2[j]),
                   _b(rw_g1[j]), _b(rw_g2[j]), row2(rw_k_k[j]), row2(rw_k_a[j]), row2(rw_r_k[j]),
                   row2(rw_gn_w[j]), row2(rw_gn_b[j])]
            xp, sbd, tail = _rwkv_call(xp, wts, nseq=bp, seq_rows=tp, tm=tm_p)
            new_p["S"].append(sbd)
            new_p["shift"].append(tail.reshape(bp, SUBLANES, d)[:, SUBLANES - 1])
            xs, s_t, sh_new = _rwkv_decode_call(xs, state_rwkv_shift[j], jnp.transpose(state_rwkv_S[j], (1, 2, 3, 0)),
                                                wts, nb=bs, nt=ts)
            new_s["S"].append(jnp.transpose(s_t, (3, 0, 1, 2)))
            new_s["shift"].append(sh_new)
        else:
            w_in = ml_w_in[j]
            w_if = w_in[:, nqk + nv + d:]
            wts = [row2(norm_mix[i]), _b(w_in[:, :nqk]), _b(w_in[:, nqk:nqk + nv]), _b(w_in[:, nqk + nv:nqk + nv + d]),
                   _b(w_if), row2(ml_b_if[j]), ml_conv_w[j],
                   row2(ml_conv_b[j]), row2(ml_norm_w[j]), _b(ml_w_out[j])]
            xp, cp, n_, m_, tail = _mlstm_call(xp, None, None, None, None, wts, nseq=bp, seq_rows=tp, npad=0,
                                               sample=False, tm=tm_ml)
            new_p["C"].append(jnp.swapaxes(cp, -1, -2))
            new_p["n"].append(n_.reshape(bp, ml_heads, ML_DK))
            new_p["m"].append(m_.reshape(bp, ml_heads))
            new_p["conv"].append(tail.reshape(bp, SUBLANES, nqk)[:, SUBLANES - (ML_CONV - 1):])
            conv0 = state_mlstm_conv[j]
            cvx = jnp.concatenate([jnp.zeros((bs, npad - (ML_CONV - 1), nqk), F32), conv0,
                                   jnp.zeros((bs, slot - npad, nqk), F32)], axis=1).reshape(bs * slot, nqk)
            mrow = jnp.repeat(state_mlstm_m[j], slot, axis=0)
            xs, cp, n_, mt, raw = _mlstm_call(to_slots(xs), cvx, mrow, jnp.swapaxes(state_mlstm_C[j], -1, -2),
                                              state_mlstm_n[j].reshape(bs, 1, ml_heads * ML_DK), wts, nseq=bs,
                                              seq_rows=slot, npad=npad, sample=True, tm=tm_s)
            xs = from_slots(xs)
            new_s["C"].append(jnp.swapaxes(cp, -1, -2))
            new_s["n"].append(n_.reshape(bs, ml_heads, ML_DK))
            new_s["m"].append(mt.reshape(bs, slot, ml_heads)[:, slot - 1])
            new_s["conv"].append(raw.reshape(bs, slot, nqk)[:, slot - (ML_CONV - 1):])
        xp, xs = ffn(xp, xs, ffb, i, i == depth - 1)
    y_prompt = xp.reshape(bp, tp, d)
    y_sample = xs.reshape(ts, bs, d).transpose(1, 0, 2)
    st = lambda lst: jnp.stack(lst)
    return (y_prompt, y_sample,
            st(new_p["S"]), st(new_p["shift"]), st(new_p["C"]), st(new_p["n"]), st(new_p["m"]), st(new_p["conv"]),
            st(new_s["S"]), st(new_s["shift"]), st(new_s["C"]), st(new_s["n"]), st(new_s["m"]), st(new_s["conv"]))
```

```python
import functools
import math

import jax
import jax.numpy as jnp
from jax import lax
from jax.experimental import pallas as pl
from jax.experimental.pallas import tpu as pltpu

F32 = jnp.float32
BF16 = jnp.bfloat16

NORM_EPS = 1e-6
RW_GN_EPS = 64e-5
RW_HEAD = 64
ML_DK = 64
ML_DV = 128
ML_CONV = 4

LANES = 128
SUBLANES = 8
MXU_COLS = 256
CHUNK = 64
ML_CHUNK = 128
NEG_BIG = -1e30
VMEM_LIMIT = 58 * 1024 * 1024
HI = lax.Precision.HIGHEST


def _dot(a, b, precision=None):
    return jnp.dot(a, b, preferred_element_type=F32, precision=precision)


def _dot_nt(a, b, precision=None):
    return lax.dot_general(a, b, (((1,), (1,)), ((), ())), preferred_element_type=F32, precision=precision)


def _dot_tn(a, b, precision=None):
    return lax.dot_general(a, b, (((0,), (0,)), ((), ())), preferred_element_type=F32, precision=precision)


def _b(x):
    return x.astype(BF16)


def _rms(x, g):
    return x * lax.rsqrt(jnp.mean(x * x, axis=-1, keepdims=True) + NORM_EPS) * g


def _sigmoid(x):
    return 1.0 / (1.0 + jnp.exp(-x))


def _softplus(x):
    return jnp.maximum(x, 0.0) + jnp.log1p(jnp.exp(-jnp.abs(x)))


def _segsum(x, mlo):
    lo = jnp.sum(jnp.where(mlo, x, 0.0), axis=-1, keepdims=True)
    hi = jnp.sum(jnp.where(mlo, 0.0, x), axis=-1, keepdims=True)
    return jnp.where(mlo, lo, hi)


def _colsel(x, lane_idx, j):
    return jnp.sum(jnp.where(lane_idx == j, x, 0.0), axis=-1, keepdims=True)


def _seq_last(x, lsub):
    n = x.shape[0]
    if lsub == n:
        return x[n - 1:n]
    parts = [jnp.broadcast_to(x[q * lsub + lsub - 1:q * lsub + lsub], (lsub,) + x.shape[1:])
             for q in range(n // lsub)]
    return jnp.concatenate(parts, axis=0)


def _const_spec(shape, single_buffer=False):
    nd = len(shape)
    if single_buffer:
        return pl.BlockSpec(shape, lambda *_: (0,) * nd, pipeline_mode=pl.Buffered(1))
    return pl.BlockSpec(shape, lambda *_: (0,) * nd)


def _log2(n):
    k = int(math.log2(n))
    assert 1 << k == n, n
    return k


def _ffn_kernel(xp_ref, xs_ref, g_ref, wg_ref, wu_ref, wd_ref, gf_ref, op_ref, os_ref, *, fchunk, final_norm, np_steps):
    def tile(x_ref, o_ref):
        x = x_ref[...]
        xb = _b(_rms(x, g_ref[...]))
        acc = jnp.zeros(x.shape, F32)
        for lo, hi in zip(fchunk[:-1], fchunk[1:]):
            sl = slice(lo, hi)
            gate = _dot(xb, wg_ref[:, sl])
            up = _dot(xb, wu_ref[:, sl])
            h = _b(gate * _sigmoid(gate) * up)
            acc = acc + _dot(h, wd_ref[sl, :])
        out = x + 0.5 * acc
        if final_norm:
            out = _rms(out, gf_ref[...])
        o_ref[...] = out

    is_p = pl.program_id(0) < np_steps
    pl.when(is_p)(functools.partial(tile, xp_ref, op_ref))
    pl.when(jnp.logical_not(is_p))(functools.partial(tile, xs_ref, os_ref))


def _ffn_call(xp, xs, g_all, wg_all, wu_all, wd_all, gf, layer, *, final_norm, tm):
    (n_p, d), n_s = xp.shape, xs.shape[0]
    nf = wg_all.shape[2]
    np_steps, ns_steps = n_p // tm, n_s // tm
    ntile = nf // MXU_COLS
    fchunk = (0, -(-ntile // 2) * MXU_COLS, nf) if nf % MXU_COLS == 0 and ntile > 1 else (0, nf)
    kern = functools.partial(_ffn_kernel, fchunk=fchunk, final_norm=final_norm, np_steps=np_steps)
    p_spec = pl.BlockSpec((tm, d), lambda i: (jnp.minimum(i, np_steps - 1), 0))
    s_spec = pl.BlockSpec((tm, d), lambda i: (jnp.maximum(i - np_steps, 0), 0))
    w_spec = lambda a, b: pl.BlockSpec((None, a, b), lambda i: (layer, 0, 0), pipeline_mode=pl.Buffered(1))
    return pl.pallas_call(
        kern,
        grid=(np_steps + ns_steps,),
        in_specs=[p_spec, s_spec, pl.BlockSpec((None, 1, d), lambda i: (layer, 0, 0)),
                  w_spec(d, nf), w_spec(d, nf), w_spec(nf, d), _const_spec((1, d))],
        out_specs=[p_spec, s_spec],
        out_shape=[jax.ShapeDtypeStruct((n_p, d), F32), jax.ShapeDtypeStruct((n_s, d), F32)],
        compiler_params=pltpu.CompilerParams(dimension_semantics=("arbitrary",), vmem_limit_bytes=VMEM_LIMIT),
        name="ffn",
    )(xp, xs, g_all, wg_all, wu_all, wd_all, gf)


def _rw_token_part(xn, xx, mlo, put, wrefs):
    mu_ref, wr_ref, wk_ref, wv_ref, w0_ref, w1_ref, w2_ref, a0_ref, a1_ref, a2_ref, g1_ref, g2_ref, kk_ref, ka_ref, rk_ref = wrefs
    cols = [slice(p * LANES, (p + 1) * LANES) for p in range(xn.shape[1] // LANES)]

    def put_all(name, z):
        for p, cs in enumerate(cols):
            put(name, p, z[:, cs])

    def mix(i):
        return _b(xn + xx * mu_ref[i:i + 1, :])

    r = _dot(mix(0), wr_ref[...])
    put_all("r", r)
    wl = _b(jnp.tanh(_dot(mix(1), w1_ref[...])))
    ld = -math.exp(-0.5) * _sigmoid(w0_ref[...] + _dot(wl, w2_ref[...]))
    k = _dot(mix(2), wk_ref[...])
    v = _dot(mix(3), wv_ref[...])
    al = _b(_dot(mix(4), a1_ref[...]))
    a = _sigmoid(a0_ref[...] + _dot(al, a2_ref[...]))
    gl = _b(_sigmoid(_dot(mix(5), g1_ref[...])))
    put_all("g", _dot(gl, g2_ref[...]))
    k2 = k * (1.0 + (a - 1.0) * ka_ref[...])
    kk = k * kk_ref[...]
    rkk = r * k2 * rk_ref[...]
    put_all("ld", ld)
    put_all("k", k2)
    put_all("v", v)
    for p, cs in enumerate(cols):
        kkp = kk[:, cs]
        kn = kkp / jnp.maximum(jnp.sqrt(_segsum(kkp * kkp, mlo)), 1e-12)
        put("kn", p, kn)
        put("b", p, kn * a[:, cs])
        put("bon", p, _segsum(rkk[:, cs], mlo) * v[:, cs])


def _rwkv_kernel(*refs, tm):
    L = CHUNK
    nchunk = tm // L
    nsteps = _log2(L)
    it = iter(refs)
    x_ref = next(it)
    (gn_ref, mu_ref, wr_ref, wk_ref, wv_ref, wo_ref, w0_ref, w1_ref, w2_ref, a0_ref, a1_ref, a2_ref,
     g1_ref, g2_ref, kk_ref, ka_ref, rk_ref, gw_ref, gb_ref) = [next(it) for _ in range(19)]
    o_ref, sout_ref, xn_ref = next(it), next(it), next(it)
    r_s, ld_s, k_s, v_s, kn_s, b_s, g_s, bon_s = sset = [next(it) for _ in range(8)]
    yg_s, carry_s, sbd_s = next(it), next(it), next(it)
    d = x_ref.shape[1]
    npair = d // LANES

    lane = lax.broadcasted_iota(jnp.int32, (1, LANES), 1)
    mlo = lane < RW_HEAD
    row = lax.broadcasted_iota(jnp.int32, (tm, 1), 0)

    @pl.when(pl.program_id(1) == 0)
    def _():
        carry_s[...] = jnp.zeros(carry_s.shape, F32)
        sbd_s[...] = jnp.zeros(sbd_s.shape, F32)

    xn = _rms(x_ref[...], gn_ref[...])
    xprev = jnp.where(row == 0, carry_s[SUBLANES - 1:SUBLANES, :], pltpu.roll(xn, 1, 0))
    carry_s[...] = xn[tm - SUBLANES:tm, :]
    xn_ref[...] = xn[tm - SUBLANES:tm, :]
    sref = dict(zip(("r", "ld", "k", "v", "kn", "b", "g", "bon"), sset))

    def put(name, p, tile):
        sref[name][:, p * LANES:(p + 1) * LANES] = tile

    _rw_token_part(xn, xprev - xn, mlo, put,
                   (mu_ref, wr_ref, wk_ref, wv_ref, w0_ref, w1_ref, w2_ref, a0_ref, a1_ref, a2_ref,
                    g1_ref, g2_ref, kk_ref, ka_ref, rk_ref))

    ti = lax.broadcasted_iota(jnp.int32, (L, 3 * L), 0)
    si = lax.broadcasted_iota(jnp.int32, (L, 3 * L), 1) & (L - 1)
    tril3 = jnp.where(si <= ti, 1.0, 0.0).astype(BF16)
    gi = lax.broadcasted_iota(jnp.int32, (2 * L, 4 * L), 0)
    gj = lax.broadcasted_iota(jnp.int32, (2 * L, 4 * L), 1)
    gt = gi & (L - 1)
    gs = gj & (L - 1)
    gmask = (gs < gt) | ((gi >= L) & (gs == gt))
    pairs = range(npair)

    def stack2(z):
        return jnp.concatenate([jnp.where(mlo, z, 0), jnp.where(mlo, 0, z)], axis=0)

    cs = [slice(p * LANES, (p + 1) * LANES) for p in pairs]
    rw = {(c, p): slice(c * L, (c + 1) * L) for c in range(nchunk) for p in pairs}
    chains = [(c, p) for c in range(nchunk) for p in pairs]

    cum, a2, q4, vv2 = {}, {}, {}, {}
    for key in chains:
        ldc = ld_s[rw[key], cs[key[1]]]
        hi = _b(ldc)
        r1 = ldc - hi.astype(F32)
        mid = _b(r1)
        lo = _b(r1 - mid.astype(F32))
        cum[key] = _dot(tril3, jnp.concatenate([hi, mid, lo], axis=0))
    for key in chains:
        rows, c_ = rw[key], cs[key[1]]
        ep = jnp.exp(cum[key])
        em = jnp.exp(-cum[key])
        at = -(kn_s[rows, c_] * jnp.exp(cum[key] - ld_s[rows, c_]))
        a2[key] = _b(jnp.concatenate([at, r_s[rows, c_] * ep], axis=0))
        q4[key] = jnp.concatenate([stack2(b_s[rows, c_] * em), stack2(k_s[rows, c_] * em)], axis=0)
        vv2[key] = _b(stack2(v_s[rows, c_]))
    g = {key: jnp.where(gmask, _dot_nt(a2[key], _b(q4[key])), 0.0) for key in chains}
    makv = {key: _dot(_b(g[key][:, LANES:]), vv2[key]) for key in chains}
    mr = {key: _b(g[key][L:, :LANES]) for key in chains}
    w = {key: g[key][:L, :LANES] for key in chains}
    pf = {}
    for key in chains:
        pc = _b(w[key])
        pf[key] = _dot(pc, stack2(pc))
    for k in range(1, nsteps):
        for key in chains:
            pc = _b(pf[key])
            if k + 1 < nsteps:
                both = _dot(pc, jnp.concatenate([stack2(pc), stack2(_b(w[key]))], axis=1))
                w[key] = w[key] + pf[key] + both[:, LANES:]
                pf[key] = both[:, :LANES]
            else:
                w[key] = w[key] + pf[key] + _dot(pc, stack2(_b(w[key])))
    wcat = {key: _b(w[key]) for key in chains}
    pl_ = {key: jnp.exp(cum[key][L - 1:L, :]) for key in chains}
    qpl = {key: _b(q4[key] * pl_[key]) for key in chains}

    state = [sbd_s[p] for p in pairs]
    for c in range(nchunk):
        as_ = [_dot_nt(a2[c, p], _b(state[p])) for p in pairs]
        rhs = [as_[p][:L] + makv[c, p][:L] for p in pairs]
        u = [rhs[p] + _dot(wcat[c, p], stack2(_b(rhs[p]))) for p in pairs]
        u2 = [stack2(_b(u[p])) for p in pairs]
        uv = [jnp.concatenate([u2[p], vv2[c, p]], axis=0) for p in pairs]
        y = [as_[p][L:] + makv[c, p][L:] + _dot(mr[c, p], u2[p]) for p in pairs]
        for p in pairs:
            state[p] = state[p] * pl_[c, p] + _dot_tn(uv[p], qpl[c, p])
        rows = rw[c, 0]
        for p in pairs:
            mean = _segsum(y[p], mlo) * (1.0 / RW_HEAD)
            yc = y[p] - mean
            var = _segsum(yc * yc, mlo) * (1.0 / RW_HEAD)
            yn = yc * lax.rsqrt(var + RW_GN_EPS) * gw_ref[:, cs[p]] + gb_ref[:, cs[p]] + bon_s[rows, cs[p]]
            yg_s[rows, cs[p]] = _b(yn * g_s[rows, cs[p]])
    for p in pairs:
        sbd_s[p] = state[p]
        sout_ref[0, 2 * p] = state[p][:RW_HEAD, :RW_HEAD]
        sout_ref[0, 2 * p + 1] = state[p][RW_HEAD:, RW_HEAD:]
    o_ref[...] = x_ref[...] + _dot(yg_s[...], wo_ref[...])


def _rwkv_call(x, wts, *, nseq, seq_rows, tm):
    n, d = x.shape
    npair = d // LANES
    tps = seq_rows // tm
    kern = functools.partial(_rwkv_kernel, tm=tm)
    x_spec = pl.BlockSpec((tm, d), lambda b, j: (b * tps + j, 0))
    st_spec = pl.BlockSpec((1, 2 * npair, RW_HEAD, RW_HEAD), lambda b, j: (b, 0, 0, 0))
    xn_spec = pl.BlockSpec((SUBLANES, d), lambda b, j: (b, 0))
    scratch = ([pltpu.VMEM((tm, d), F32) for _ in range(8)] + [pltpu.VMEM((tm, d), BF16)]
               + [pltpu.VMEM((SUBLANES, d), F32), pltpu.VMEM((npair, LANES, LANES), F32)])
    return pl.pallas_call(
        kern,
        grid=(nseq, tps),
        in_specs=[x_spec] + [_const_spec(wt.shape, single_buffer=True) for wt in wts],
        out_specs=[x_spec, st_spec, xn_spec],
        out_shape=[jax.ShapeDtypeStruct((n, d), F32),
                   jax.ShapeDtypeStruct((nseq, 2 * npair, RW_HEAD, RW_HEAD), F32),
                   jax.ShapeDtypeStruct((nseq * SUBLANES, d), F32)],
        scratch_shapes=scratch,
        compiler_params=pltpu.CompilerParams(dimension_semantics=("arbitrary", "arbitrary"),
                                             vmem_limit_bytes=VMEM_LIMIT),
        name="rwkv_prompt",
    )(x, *wts)


def _rwkv_decode_kernel(*refs, nb, nt):
    it = iter(refs)
    x_ref, sh_ref = next(it), next(it)
    (gn_ref, mu_ref, wr_ref, wk_ref, wv_ref, wo_ref, w0_ref, w1_ref, w2_ref, a0_ref, a1_ref, a2_ref,
     g1_ref, g2_ref, kk_ref, ka_ref, rk_ref, gw_ref, gb_ref) = [next(it) for _ in range(19)]
    s0_ref = next(it)
    o_ref, sout_ref, xn_ref = next(it), next(it), next(it)
    tr = dict(zip(("r", "ld", "k", "v", "kn", "b"), [next(it) for _ in range(6)]))
    yt_s, g_s, bon_s, yg_s = [next(it) for _ in range(4)]
    n, d = x_ref.shape
    npair = d // LANES
    nhead = 2 * npair
    h = pl.program_id(0)
    lane = lax.broadcasted_iota(jnp.int32, (1, LANES), 1)
    mlo = lane < RW_HEAD

    @pl.when(h == 0)
    def _():
        xn = _rms(x_ref[...], gn_ref[...])
        xprev = jnp.concatenate([sh_ref[...], xn[:n - nb]], axis=0)
        xn_ref[...] = xn[n - nb:]

        def put(name, p, tile):
            if name == "g":
                g_s[:, p * LANES:(p + 1) * LANES] = tile
            elif name == "bon":
                bon_s[:, p * LANES:(p + 1) * LANES] = tile
            else:
                tr[name][p] = (jnp.exp(tile) if name == "ld" else tile).T

        _rw_token_part(xn, xprev - xn, mlo, put,
                       (mu_ref, wr_ref, wk_ref, wv_ref, w0_ref, w1_ref, w2_ref, a0_ref, a1_ref, a2_ref,
                        g1_ref, g2_ref, kk_ref, ka_ref, rk_ref))

    p = h >> 1
    base = pl.multiple_of((h & 1) * RW_HEAD, RW_HEAD)
    sub = lax.broadcasted_iota(jnp.int32, (SUBLANES, 1), 0)
    hrows = pl.ds(base, RW_HEAD)

    def group(gi, carry):
        r0 = pl.multiple_of(base + gi * SUBLANES, SUBLANES)
        vg = [tr["v"][p, pl.ds(r0, SUBLANES), t * nb:(t + 1) * nb] for t in range(nt)]
        ys = [jnp.zeros((SUBLANES, nb), F32) for _ in range(nt)]
        for vi in range(SUBLANES):
            v_idx = gi * SUBLANES + vi
            s = s0_ref[0, v_idx]
            for t in range(nt):
                tc = slice(t * nb, (t + 1) * nb)
                sa = -jnp.sum(s * tr["kn"][p, hrows, tc], axis=0, keepdims=True)
                s = s * tr["ld"][p, hrows, tc] + sa * tr["b"][p, hrows, tc] + vg[t][vi:vi + 1, :] * tr["k"][p, hrows, tc]
                yrow = jnp.sum(s * tr["r"][p, hrows, tc], axis=0, keepdims=True)
                ys[t] = jnp.where(sub == vi, yrow, ys[t])
            sout_ref[0, v_idx] = s
        for t in range(nt):
            yt_s[p, pl.ds(r0, SUBLANES), t * nb:(t + 1) * nb] = ys[t]
        return carry

    lax.fori_loop(0, RW_HEAD // SUBLANES, group, 0)

    @pl.when(h == nhead - 1)
    def _():
        for q in range(npair):
            cs = slice(q * LANES, (q + 1) * LANES)
            y = yt_s[q].T
            mean = _segsum(y, mlo) * (1.0 / RW_HEAD)
            yc = y - mean
            var = _segsum(yc * yc, mlo) * (1.0 / RW_HEAD)
            yn = yc * lax.rsqrt(var + RW_GN_EPS) * gw_ref[:, cs] + gb_ref[:, cs] + bon_s[:, cs]
            yg_s[:, cs] = _b(yn * g_s[:, cs])
        o_ref[...] = x_ref[...] + _dot(yg_s[...], wo_ref[...])


def _rwkv_decode_call(xt, shift0, s0t, wts, *, nb, nt):
    n, d = xt.shape
    npair = d // LANES
    nhead = 2 * npair
    kern = functools.partial(_rwkv_decode_kernel, nb=nb, nt=nt)
    full = lambda shape: pl.BlockSpec(shape, lambda h: (0,) * len(shape))
    st_spec = pl.BlockSpec((1, RW_HEAD, RW_HEAD, nb), lambda h: (h, 0, 0, 0))
    in_specs = [full((n, d)), full((nb, d))] + [_const_spec(wt.shape, single_buffer=True) for wt in wts] + [st_spec]
    scratch = ([pltpu.VMEM((npair, LANES, n), F32) for _ in range(7)]
               + [pltpu.VMEM((n, d), F32), pltpu.VMEM((n, d), F32), pltpu.VMEM((n, d), BF16)])
    return pl.pallas_call(
        kern,
        grid=(nhead,),
        in_specs=in_specs,
        out_specs=[full((n, d)), st_spec, full((nb, d))],
        out_shape=[jax.ShapeDtypeStruct((n, d), F32), jax.ShapeDtypeStruct(s0t.shape, F32),
                   jax.ShapeDtypeStruct((nb, d), F32)],
        scratch_shapes=scratch,
        compiler_params=pltpu.CompilerParams(dimension_semantics=("arbitrary",), vmem_limit_bytes=VMEM_LIMIT),
        name="rwkv_decode",
    )(xt, shift0, *wts, s0t)


def _mlstm_kernel(*refs, tm, seq_rows, npad, sample, L, lsub):
    nsub = L // lsub
    nchunk = tm // L
    it = iter(refs)
    x_ref = next(it)
    if sample:
        cvx_ref, mrow_ref = next(it), next(it)
    gn_ref, wqk_ref, wv_ref, wo_ref, wif_ref, bif_ref, cw_ref, cb_ref, nw_ref, wout_ref = [next(it) for _ in range(10)]
    if sample:
        c0_ref, n0_ref = next(it), next(it)
    o_ref, cout_ref, nout_ref, mout_ref, cvout_ref = [next(it) for _ in range(5)]
    gt_s, q_s, k_s, v_s, og_s, gc_s, ho_s, ext_s = [next(it) for _ in range(8)]
    if not sample:
        cp_s, np_s, m_s = [next(it) for _ in range(3)]
    nqk = wqk_ref.shape[1]
    half = nqk // 2
    nheads = half // ML_DK
    ngate = 2 * nheads
    npair = nheads // 2

    lane = lax.broadcasted_iota(jnp.int32, (1, LANES), 1)
    mlo = lane < ML_DK
    glane = lax.broadcasted_iota(jnp.int32, (1, ngate), 1)
    isf_c = glane >= nheads
    grow = lax.broadcasted_iota(jnp.int32, (ngate, 1), 0)
    isf_r = grow >= nheads
    hlane = lax.broadcasted_iota(jnp.int32, (1, nheads), 1)

    x = x_ref[...]
    xb = _b(_rms(x, gn_ref[...]))
    raw = _dot(xb, wqk_ref[...])
    row = lax.broadcasted_iota(jnp.int32, (tm, 1), 0)
    if sample:
        srow = row & (seq_rows - 1)
        raw = jnp.where((srow >= npad - (ML_CONV - 1)) & (srow < npad), cvx_ref[...], raw)
        cvout_ref[...] = raw
        ext_s[:SUBLANES, :] = jnp.zeros((SUBLANES, nqk), F32)
    else:
        j = pl.program_id(1)

        @pl.when(j == 0)
        def _():
            ext_s[:SUBLANES, :] = jnp.zeros((SUBLANES, nqk), F32)
            cp_s[...] = jnp.zeros(cp_s.shape, F32)
            np_s[...] = jnp.zeros(np_s.shape, F32)
            m_s[...] = jnp.zeros(m_s.shape, F32)

        cvout_ref[...] = raw[tm - SUBLANES:tm, :]
    ext_s[SUBLANES:, :] = raw
    qk = cb_ref[...] + cw_ref[ML_CONV - 1:ML_CONV, :] * raw
    for s in range(1, ML_CONV):
        qk = qk + cw_ref[ML_CONV - 1 - s:ML_CONV - s, :] * ext_s[SUBLANES - s:SUBLANES - s + tm, :]
    if not sample:
        ext_s[:SUBLANES, :] = raw[tm - SUBLANES:tm, :]
    qk = qk * _sigmoid(qk)
    q_s[...] = qk[:, :half] * (ML_DK ** -0.5)
    k_s[...] = qk[:, half:]
    v_s[...] = _dot(xb, wv_ref[...])
    og_s[...] = _sigmoid(_dot(xb, wo_ref[...]))
    ifp = _dot(xb, wif_ref[...]) + bif_ref[...]
    gcol = jnp.where(isf_c, -_softplus(-ifp), ifp)
    if npad:
        keep = (row & (seq_rows - 1)) >= npad
        gcol = jnp.where(keep, gcol, jnp.where(isf_c, 0.0, NEG_BIG))
    gc_s[...] = gcol
    gct = gcol.T
    for c in range(nchunk):
        gt_s[c] = gct[:, c * L:(c + 1) * L]

    ti = lax.broadcasted_iota(jnp.int32, (L, L), 0)
    si = lax.broadcasted_iota(jnp.int32, (L, L), 1)
    sh = _log2(lsub)
    same = (ti >> sh) == (si >> sh)
    causal = same & (si <= ti)
    tril = jnp.where(causal, 1.0, 0.0).astype(F32)
    triu = jnp.where(same & (ti <= si), 1.0, 0.0).astype(F32)

    def chunk(c, carry):
        r0 = pl.multiple_of(c * L, L)
        rows = pl.ds(r0, L)
        gc = gc_s[rows, :]
        gt = gt_s[c]
        bcs = _dot(tril, jnp.where(isf_c, gc, 0.0), HI)
        brs = _dot(jnp.where(isf_r, gt, 0.0), triu, HI)
        blast = _seq_last(bcs, lsub)
        if sample:
            mcols = mrow_ref[rows, :]
        else:
            mcols = m_s[...]
        heads = range(nheads)
        prs = range(npair)
        ps = [slice(pp * LANES, (pp + 1) * LANES) for pp in prs]
        hs = [slice(h * ML_DV, (h + 1) * ML_DV) for h in heads]
        q2 = [q_s[rows, ps[pp]] for pp in prs]
        k2 = [k_s[rows, ps[pp]] for pp in prs]
        k2b = [_b(z) for z in k2]
        if sample:
            c_in = [[jnp.concatenate([c0_ref[c * nsub + q, 2 * pp], c0_ref[c * nsub + q, 2 * pp + 1]], axis=0)
                     for q in range(nsub)] for pp in prs]
            n_in = [[n0_ref[c * nsub + q][:, ps[pp]] for q in range(nsub)] for pp in prs]
            n_rows = [jnp.concatenate([jnp.broadcast_to(n_in[pp][q], (lsub, LANES)) for q in range(nsub)], axis=0)
                      for pp in prs]
        else:
            c_prev = [cp_s[pp] for pp in prs]
            n_prev = [np_s[:, ps[pp]] for pp in prs]
            n_rows = n_prev
        bcol = [_colsel(bcs, glane, nheads + h) for h in heads]
        licol = [_colsel(gc, glane, h) for h in heads]
        mcol = [_colsel(mcols, hlane, h) for h in heads]
        blcol = [_colsel(blast, glane, nheads + h) for h in heads]
        dlog = [jnp.where(causal, bcol[h] - (brs[nheads + h:nheads + h + 1, :] - gt[h:h + 1, :]), -jnp.inf)
                for h in heads]
        ginter = [bcol[h] + mcol[h] for h in heads]
        m_t = [jnp.maximum(ginter[h], jnp.max(dlog[h], axis=-1, keepdims=True)) for h in heads]
        dw = [jnp.exp(dlog[h] - m_t[h]) for h in heads]
        winter = [jnp.exp(ginter[h] - m_t[h]) for h in heads]
        qh = [jnp.where(mlo if h % 2 == 0 else jnp.logical_not(mlo), q2[h // 2], 0.0) for h in heads]
        qhb = [_b(z) for z in qh]
        sc = [_dot_nt(qhb[h], k2b[h // 2]) * dw[h] for h in heads]
        if sample:
            inter = [jnp.concatenate([_dot(qhb[h][q * lsub:(q + 1) * lsub], _b(c_in[h // 2][q]))
                                      for q in range(nsub)], axis=0) for h in heads]
        else:
            cb = [_b(z) for z in c_prev]
            inter = [_dot(qhb[h], cb[h // 2]) for h in heads]
        vh = [v_s[rows, hs[h]] for h in heads]
        num = [winter[h] * inter[h] + _dot(_b(sc[h]), _b(vh[h])) for h in heads]
        den = [winter[h] * jnp.sum(qh[h] * n_rows[h // 2], axis=-1, keepdims=True)
               + jnp.sum(sc[h], axis=-1, keepdims=True) for h in heads]
        for h in heads:
            hout = num[h] / jnp.maximum(jnp.abs(den[h]), jnp.exp(-m_t[h]))
            hn = hout * lax.rsqrt(jnp.mean(hout * hout, axis=-1, keepdims=True) + NORM_EPS) * nw_ref[:, hs[h]]
            ho_s[rows, hs[h]] = _b(hn * og_s[rows, hs[h]])
        mnew = [_seq_last(m_t[h], lsub) for h in heads]
        ws = [jnp.exp(blcol[h] - bcol[h] + licol[h] - mnew[h]) for h in heads]
        wstc = [jnp.exp(blcol[h] + mcol[h] - mnew[h]) for h in heads]
        mt_all = jnp.zeros((L, nheads), F32)
        for h in heads:
            mt_all = jnp.where(hlane == h, m_t[h], mt_all)
        for pp in prs:
            lo, hi = 2 * pp, 2 * pp + 1
            wsk = jnp.where(mlo, ws[lo], ws[hi]) * k2[pp]
            wst = jnp.where(mlo, wstc[lo], wstc[hi])
            wvb = _b(jnp.concatenate([ws[lo] * vh[lo], ws[hi] * vh[hi]], axis=0))
            kmb = jnp.concatenate([jnp.where(mlo, k2b[pp], 0), jnp.where(mlo, 0, k2b[pp])], axis=0)
            if sample:
                for q in range(nsub):
                    last = (q + 1) * lsub - 1
                    sel = [slice(blk * L + q * lsub, blk * L + (q + 1) * lsub) for blk in range(2)]
                    wq = wst[last:last + 1, :]
                    wrow = jnp.concatenate([jnp.broadcast_to(wstc[h_][last:last + 1, :], (ML_DK, 1)) for h_ in (lo, hi)],
                                           axis=0)
                    upd = _dot_tn(jnp.concatenate([kmb[s_] for s_ in sel], axis=0),
                                  jnp.concatenate([wvb[s_] for s_ in sel], axis=0))
                    c_new = c_in[pp][q] * wrow + upd
                    cout_ref[c * nsub + q, 2 * pp] = c_new[:ML_DK]
                    cout_ref[c * nsub + q, 2 * pp + 1] = c_new[ML_DK:]
                    nq = n_in[pp][q] * wq + jnp.sum(wsk[q * lsub:(q + 1) * lsub], axis=0, keepdims=True)
                    nout_ref[c * nsub + q, :, ps[pp]] = nq
            else:
                wrow = jnp.concatenate([jnp.broadcast_to(wstc[h_], (ML_DK, 1)) for h_ in (lo, hi)], axis=0)
                cp_s[pp] = c_prev[pp] * wrow + _dot_tn(kmb, wvb)
                np_s[:, ps[pp]] = n_prev[pp] * wst + jnp.sum(wsk, axis=0, keepdims=True)
        if sample:
            mout_ref[rows, :] = mt_all
        else:
            m_s[...] = mt_all[L - 1:L, :]
        return carry

    lax.fori_loop(0, nchunk, chunk, 0)
    if not sample:
        for pp in range(npair):
            cout_ref[0, 2 * pp] = cp_s[pp][:ML_DK]
            cout_ref[0, 2 * pp + 1] = cp_s[pp][ML_DK:]
        nout_ref[0] = np_s[...]
        mout_ref[0] = m_s[...]
    o_ref[...] = x_ref[...] + _dot(ho_s[...], wout_ref[...])


def _mlstm_call(x, cvx, mrow, c0p, n0, wts, *, nseq, seq_rows, npad, sample, tm):
    n, d = x.shape
    nqk = wts[1].shape[1]
    nv = wts[2].shape[1]
    nheads = nqk // 2 // ML_DK
    npair = nheads // 2
    chunk = CHUNK if sample else ML_CHUNK
    lsub = seq_rows if sample else chunk
    kern = functools.partial(_mlstm_kernel, tm=tm, seq_rows=seq_rows, npad=npad, sample=sample, L=chunk, lsub=lsub)
    if sample:
        grid = (n // tm,)
        tile = lambda i: (i, 0)
        spt = tm // seq_rows
        c_spec = pl.BlockSpec((spt, nheads, ML_DK, ML_DV), lambda i: (i, 0, 0, 0))
        n_spec = pl.BlockSpec((spt, 1, nqk // 2), lambda i: (i, 0, 0))
        m_spec = pl.BlockSpec((tm, nheads), tile)
        m_shape = (n, nheads)
        cv_spec = pl.BlockSpec((tm, nqk), tile)
        cv_shape = (n, nqk)
        sem = ("arbitrary",)
    else:
        tps = seq_rows // tm
        grid = (nseq, tps)
        tile = lambda b, j: (b * tps + j, 0)
        c_spec = pl.BlockSpec((1, nheads, ML_DK, ML_DV), lambda b, j: (b, 0, 0, 0))
        n_spec = pl.BlockSpec((1, 1, nqk // 2), lambda b, j: (b, 0, 0))
        m_spec = pl.BlockSpec((1, 1, nheads), lambda b, j: (b, 0, 0))
        m_shape = (nseq, 1, nheads)
        cv_spec = pl.BlockSpec((SUBLANES, nqk), lambda b, j: (b, 0))
        cv_shape = (nseq * SUBLANES, nqk)
        sem = ("arbitrary", "arbitrary")
    x_spec = pl.BlockSpec((tm, d), tile)
    in_specs = [x_spec]
    args = [x]
    if sample:
        in_specs += [pl.BlockSpec((tm, nqk), tile), pl.BlockSpec((tm, nheads), tile)]
        args += [cvx, mrow]
    for wt in wts:
        in_specs.append(_const_spec(wt.shape, single_buffer=True))
        args.append(wt)
    if sample:
        in_specs += [c_spec, n_spec]
        args += [c0p, n0]
    scratch = [pltpu.VMEM((tm // chunk, 2 * nheads, chunk), F32), pltpu.VMEM((tm, nqk // 2), F32),
               pltpu.VMEM((tm, nqk // 2), F32),
               pltpu.VMEM((tm, nv), F32), pltpu.VMEM((tm, nv), F32), pltpu.VMEM((tm, 2 * nheads), F32),
               pltpu.VMEM((tm, nv), BF16), pltpu.VMEM((tm + SUBLANES, nqk), F32)]
    if not sample:
        scratch += [pltpu.VMEM((npair, LANES, LANES), F32),
                    pltpu.VMEM((1, nqk // 2), F32), pltpu.VMEM((1, nheads), F32)]
    return pl.pallas_call(
        kern,
        grid=grid,
        in_specs=in_specs,
        out_specs=[x_spec, c_spec, n_spec, m_spec, cv_spec],
        out_shape=[jax.ShapeDtypeStruct((n, d), F32),
                   jax.ShapeDtypeStruct((nseq, nheads, ML_DK, ML_DV), F32),
                   jax.ShapeDtypeStruct((nseq, 1, nqk // 2), F32),
                   jax.ShapeDtypeStruct(m_shape, F32),
                   jax.ShapeDtypeStruct(cv_shape, F32)],
        scratch_shapes=scratch,
        compiler_params=pltpu.CompilerParams(dimension_semantics=sem, vmem_limit_bytes=VMEM_LIMIT),
        name="mlstm_sample" if sample else "mlstm_prompt",
    )(*args)


def kernel(x_prompt, x_sample, state_rwkv_S, state_rwkv_shift, state_mlstm_C, state_mlstm_n, state_mlstm_m,
           state_mlstm_conv, norm_ffa, ffa_wg, ffa_wu, ffa_wd, norm_mix, norm_ffb, ffb_wg, ffb_wu, ffb_wd,
           rw_mu, rw_wr, rw_wk, rw_wv, rw_wo, rw_w0, rw_w1, rw_w2, rw_a0, rw_a1, rw_a2, rw_g1, rw_g2,
           rw_k_k, rw_k_a, rw_r_k, rw_gn_w, rw_gn_b, ml_w_in, ml_b_if, ml_conv_w, ml_conv_b, ml_norm_w,
           ml_w_out, norm_final):
    bp, tp, d = x_prompt.shape
    bs, ts, _ = x_sample.shape
    depth = norm_ffa.shape[0]
    slot = SUBLANES
    npad = slot - ts
    assert 0 < ts <= slot and npad >= ML_CONV - 1
    tm_p = min(512, tp)
    tm_ml = min(1024, tp)
    tm_s = min(2 * CHUNK, bs * slot)
    assert tp % tm_p == 0 and tm_p % CHUNK == 0 and tp % tm_ml == 0 and tm_ml % ML_CHUNK == 0
    assert (bs * slot) % tm_s == 0 and tm_s % CHUNK == 0
    ml_heads = ml_b_if.shape[1] // 2
    nqk = 2 * ml_heads * ML_DK
    nv = ml_heads * ML_DV

    xp = x_prompt.reshape(bp * tp, d)
    xs = x_sample.transpose(1, 0, 2).reshape(ts * bs, d)

    def to_slots(z):
        return jnp.concatenate([jnp.zeros((bs, npad, d), F32), z.reshape(ts, bs, d).transpose(1, 0, 2)],
                               axis=1).reshape(bs * slot, d)

    def from_slots(z):
        return z.reshape(bs, slot, d)[:, npad:].transpose(1, 0, 2).reshape(ts * bs, d)

    row2 = lambda a: a.reshape(1, -1)

    tm_ffn = min(512, bs * ts)
    assert (bp * tp) % tm_ffn == 0 and (bs * ts) % tm_ffn == 0
    ffa = (norm_ffa.reshape(depth, 1, d), _b(ffa_wg), _b(ffa_wu), _b(ffa_wd))
    ffb = (norm_ffb.reshape(depth, 1, d), _b(ffb_wg), _b(ffb_wu), _b(ffb_wd))

    def ffn(xp_, xs_, wset, layer, final):
        return _ffn_call(xp_, xs_, *wset, row2(norm_final), layer, final_norm=final, tm=tm_ffn)

    new_p = {k_: [] for k_ in ("S", "shift", "C", "n", "m", "conv")}
    new_s = {k_: [] for k_ in ("S", "shift", "C", "n", "m", "conv")}
    for i in range(depth):
        xp, xs = ffn(xp, xs, ffa, i, False)
        j = i // 2
        if i % 2 == 0:
            wts = [row2(norm_mix[i]), rw_mu[j], _b(rw_wr[j]), _b(rw_wk[j]), _b(rw_wv[j]), _b(rw_wo[j]),
                   row2(rw_w0[j]), _b(rw_w1[j]), _b(rw_w2[j]), row2(rw_a0[j]), _b(rw_a1[j]), _b(rw_a2[j]),
                   _b(rw_g1[j]), _b(rw_g2[j]), row2(rw_k_k[j]), row2(rw_k_a[j]), row2(rw_r_k[j]),
                   row2(rw_gn_w[j]), row2(rw_gn_b[j])]
            xp, sbd, tail = _rwkv_call(xp, wts, nseq=bp, seq_rows=tp, tm=tm_p)
            new_p["S"].append(sbd)
            new_p["shift"].append(tail.reshape(bp, SUBLANES, d)[:, SUBLANES - 1])
            xs, s_t, sh_new = _rwkv_decode_call(xs, state_rwkv_shift[j], jnp.transpose(state_rwkv_S[j], (1, 2, 3, 0)),
                                                wts, nb=bs, nt=ts)
            new_s["S"].append(jnp.transpose(s_t, (3, 0, 1, 2)))
            new_s["shift"].append(sh_new)
        else:
            w_in = ml_w_in[j]
            w_if = w_in[:, nqk + nv + d:]
            wts = [row2(norm_mix[i]), _b(w_in[:, :nqk]), _b(w_in[:, nqk:nqk + nv]), _b(w_in[:, nqk + nv:nqk + nv + d]),
                   _b(w_if), row2(ml_b_if[j]), ml_conv_w[j],
                   row2(ml_conv_b[j]), row2(ml_norm_w[j]), _b(ml_w_out[j])]
            xp, cp, n_, m_, tail = _mlstm_call(xp, None, None, None, None, wts, nseq=bp, seq_rows=tp, npad=0,
                                               sample=False, tm=tm_ml)
            new_p["C"].append(jnp.swapaxes(cp, -1, -2))
            new_p["n"].append(n_.reshape(bp, ml_heads, ML_DK))
            new_p["m"].append(m_.reshape(bp, ml_heads))
            new_p["conv"].append(tail.reshape(bp, SUBLANES, nqk)[:, SUBLANES - (ML_CONV - 1):])
            conv0 = state_mlstm_conv[j]
            cvx = jnp.concatenate([jnp.zeros((bs, npad - (ML_CONV - 1), nqk), F32), conv0,
                                   jnp.zeros((bs, slot - npad, nqk), F32)], axis=1).reshape(bs * slot, nqk)
            mrow = jnp.repeat(state_mlstm_m[j], slot, axis=0)
            xs, cp, n_, mt, raw = _mlstm_call(to_slots(xs), cvx, mrow, jnp.swapaxes(state_mlstm_C[j], -1, -2),
                                              state_mlstm_n[j].reshape(bs, 1, ml_heads * ML_DK), wts, nseq=bs,
                                              seq_rows=slot, npad=npad, sample=True, tm=tm_s)
            xs = from_slots(xs)
            new_s["C"].append(jnp.swapaxes(cp, -1, -2))
            new_s["n"].append(n_.reshape(bs, ml_heads, ML_DK))
            new_s["m"].append(mt.reshape(bs, slot, ml_heads)[:, slot - 1])
            new_s["conv"].append(raw.reshape(bs, slot, nqk)[:, slot - (ML_CONV - 1):])
        xp, xs = ffn(xp, xs, ffb, i, i == depth - 1)
    y_prompt = xp.reshape(bp, tp, d)
    y_sample = xs.reshape(ts, bs, d).transpose(1, 0, 2)
    st = lambda lst: jnp.stack(lst)
    return (y_prompt, y_sample,
            st(new_p["S"]), st(new_p["shift"]), st(new_p["C"]), st(new_p["n"]), st(new_p["m"]), st(new_p["conv"]),
            st(new_s["S"]), st(new_s["shift"]), st(new_s["C"]), st(new_s["n"]), st(new_s["m"]), st(new_s["conv"]))
```

```python
import functools
import math

import jax
import jax.numpy as jnp
from jax import lax
from jax.experimental import pallas as pl
from jax.experimental.pallas import tpu as pltpu

F32 = jnp.float32
BF16 = jnp.bfloat16

NORM_EPS = 1e-6
RW_GN_EPS = 64e-5
RW_HEAD = 64
ML_DK = 64
ML_DV = 128
ML_CONV = 4

LANES = 128
SUBLANES = 8
MXU_COLS = 256
CHUNK = 64
ML_CHUNK = 128
NEG_BIG = -1e30
VMEM_LIMIT = 58 * 1024 * 1024
HI = lax.Precision.HIGHEST


def _dot(a, b, precision=None):
    return jnp.dot(a, b, preferred_element_type=F32, precision=precision)


def _dot_nt(a, b, precision=None):
    return lax.dot_general(a, b, (((1,), (1,)), ((), ())), preferred_element_type=F32, precision=precision)


def _dot_tn(a, b, precision=None):
    return lax.dot_general(a, b, (((0,), (0,)), ((), ())), preferred_element_type=F32, precision=precision)


def _b(x):
    return x.astype(BF16)


def _rms(x, g):
    return x * lax.rsqrt(jnp.mean(x * x, axis=-1, keepdims=True) + NORM_EPS) * g


def _sigmoid(x):
    return 1.0 / (1.0 + jnp.exp(-x))


def _softplus(x):
    return jnp.maximum(x, 0.0) + jnp.log1p(jnp.exp(-jnp.abs(x)))


def _segsum(x, mlo):
    lo = jnp.sum(jnp.where(mlo, x, 0.0), axis=-1, keepdims=True)
    hi = jnp.sum(jnp.where(mlo, 0.0, x), axis=-1, keepdims=True)
    return jnp.where(mlo, lo, hi)


def _colsel(x, lane_idx, j):
    return jnp.sum(jnp.where(lane_idx == j, x, 0.0), axis=-1, keepdims=True)


def _seq_last(x, lsub):
    n = x.shape[0]
    if lsub == n:
        return x[n - 1:n]
    parts = [jnp.broadcast_to(x[q * lsub + lsub - 1:q * lsub + lsub], (lsub,) + x.shape[1:])
             for q in range(n // lsub)]
    return jnp.concatenate(parts, axis=0)


def _const_spec(shape, single_buffer=False):
    nd = len(shape)
    if single_buffer:
        return pl.BlockSpec(shape, lambda *_: (0,) * nd, pipeline_mode=pl.Buffered(1))
    return pl.BlockSpec(shape, lambda *_: (0,) * nd)


def _log2(n):
    k = int(math.log2(n))
    assert 1 << k == n, n
    return k


def _ffn_kernel(xp_ref, xs_ref, g_ref, wg_ref, wu_ref, wd_ref, gf_ref, op_ref, os_ref, *, fchunk, final_norm, np_steps):
    def tile(x_ref, o_ref):
        x = x_ref[...]
        xb = _b(_rms(x, g_ref[...]))
        acc = jnp.zeros(x.shape, F32)
        for lo, hi in zip(fchunk[:-1], fchunk[1:]):
            sl = slice(lo, hi)
            gate = _dot(xb, wg_ref[:, sl])
            up = _dot(xb, wu_ref[:, sl])
            h = _b(gate * _sigmoid(gate) * up)
            acc = acc + _dot(h, wd_ref[sl, :])
        out = x + 0.5 * acc
        if final_norm:
            out = _rms(out, gf_ref[...])
        o_ref[...] = out

    is_p = pl.program_id(0) < np_steps
    pl.when(is_p)(functools.partial(tile, xp_ref, op_ref))
    pl.when(jnp.logical_not(is_p))(functools.partial(tile, xs_ref, os_ref))


def _ffn_call(xp, xs, g_all, wg_all, wu_all, wd_all, gf, layer, *, final_norm, tm):
    (n_p, d), n_s = xp.shape, xs.shape[0]
    nf = wg_all.shape[2]
    np_steps, ns_steps = n_p // tm, n_s // tm
    ntile = nf // MXU_COLS
    fchunk = (0, -(-ntile // 2) * MXU_COLS, nf) if nf % MXU_COLS == 0 and ntile > 1 else (0, nf)
    kern = functools.partial(_ffn_kernel, fchunk=fchunk, final_norm=final_norm, np_steps=np_steps)
    p_spec = pl.BlockSpec((tm, d), lambda i: (jnp.minimum(i, np_steps - 1), 0))
    s_spec = pl.BlockSpec((tm, d), lambda i: (jnp.maximum(i - np_steps, 0), 0))
    w_spec = lambda a, b: pl.BlockSpec((None, a, b), lambda i: (layer, 0, 0), pipeline_mode=pl.Buffered(1))
    return pl.pallas_call(
        kern,
        grid=(np_steps + ns_steps,),
        in_specs=[p_spec, s_spec, pl.BlockSpec((None, 1, d), lambda i: (layer, 0, 0)),
                  w_spec(d, nf), w_spec(d, nf), w_spec(nf, d), _const_spec((1, d))],
        out_specs=[p_spec, s_spec],
        out_shape=[jax.ShapeDtypeStruct((n_p, d), F32), jax.ShapeDtypeStruct((n_s, d), F32)],
        compiler_params=pltpu.CompilerParams(dimension_semantics=("arbitrary",), vmem_limit_bytes=VMEM_LIMIT),
        name="ffn",
    )(xp, xs, g_all, wg_all, wu_all, wd_all, gf)


def _rw_token_part(xn, xx, mlo, put, wrefs):
    mu_ref, wr_ref, wk_ref, wv_ref, w0_ref, w1_ref, w2_ref, a0_ref, a1_ref, a2_ref, g1_ref, g2_ref, kk_ref, ka_ref, rk_ref = wrefs
    cols = [slice(p * LANES, (p + 1) * LANES) for p in range(xn.shape[1] // LANES)]

    def put_all(name, z):
        for p, cs in enumerate(cols):
            put(name, p, z[:, cs])

    def mix(i):
        return _b(xn + xx * mu_ref[i:i + 1, :])

    r = _dot(mix(0), wr_ref[...])
    put_all("r", r)
    wl = _b(jnp.tanh(_dot(mix(1), w1_ref[...])))
    ld = -math.exp(-0.5) * _sigmoid(w0_ref[...] + _dot(wl, w2_ref[...]))
    k = _dot(mix(2), wk_ref[...])
    v = _dot(mix(3), wv_ref[...])
    al = _b(_dot(mix(4), a1_ref[...]))
    a = _sigmoid(a0_ref[...] + _dot(al, a2_ref[...]))
    gl = _b(_sigmoid(_dot(mix(5), g1_ref[...])))
    put_all("g", _dot(gl, g2_ref[...]))
    k2 = k * (1.0 + (a - 1.0) * ka_ref[...])
    kk = k * kk_ref[...]
    rkk = r * k2 * rk_ref[...]
    put_all("ld", ld)
    put_all("k", k2)
    put_all("v", v)
    for p, cs in enumerate(cols):
        kkp = kk[:, cs]
        kn = kkp / jnp.maximum(jnp.sqrt(_segsum(kkp * kkp, mlo)), 1e-12)
        put("kn", p, kn)
        put("b", p, kn * a[:, cs])
        put("bon", p, _segsum(rkk[:, cs], mlo) * v[:, cs])


def _rwkv_kernel(*refs, tm):
    L = CHUNK
    nchunk = tm // L
    nsteps = _log2(L)
    it = iter(refs)
    x_ref = next(it)
    (gn_ref, mu_ref, wr_ref, wk_ref, wv_ref, wo_ref, w0_ref, w1_ref, w2_ref, a0_ref, a1_ref, a2_ref,
     g1_ref, g2_ref, kk_ref, ka_ref, rk_ref, gw_ref, gb_ref) = [next(it) for _ in range(19)]
    o_ref, sout_ref, xn_ref = next(it), next(it), next(it)
    r_s, ld_s, k_s, v_s, kn_s, b_s, g_s, bon_s = sset = [next(it) for _ in range(8)]
    yg_s, carry_s, sbd_s = next(it), next(it), next(it)
    d = x_ref.shape[1]
    npair = d // LANES

    lane = lax.broadcasted_iota(jnp.int32, (1, LANES), 1)
    mlo = lane < RW_HEAD
    row = lax.broadcasted_iota(jnp.int32, (tm, 1), 0)

    @pl.when(pl.program_id(1) == 0)
    def _():
        carry_s[...] = jnp.zeros(carry_s.shape, F32)
        sbd_s[...] = jnp.zeros(sbd_s.shape, F32)

    xn = _rms(x_ref[...], gn_ref[...])
    xprev = jnp.where(row == 0, carry_s[SUBLANES - 1:SUBLANES, :], pltpu.roll(xn, 1, 0))
    carry_s[...] = xn[tm - SUBLANES:tm, :]
    xn_ref[...] = xn[tm - SUBLANES:tm, :]
    sref = dict(zip(("r", "ld", "k", "v", "kn", "b", "g", "bon"), sset))

    def put(name, p, tile):
        sref[name][:, p * LANES:(p + 1) * LANES] = tile

    _rw_token_part(xn, xprev - xn, mlo, put,
                   (mu_ref, wr_ref, wk_ref, wv_ref, w0_ref, w1_ref, w2_ref, a0_ref, a1_ref, a2_ref,
                    g1_ref, g2_ref, kk_ref, ka_ref, rk_ref))

    ti = lax.broadcasted_iota(jnp.int32, (L, 3 * L), 0)
    si = lax.broadcasted_iota(jnp.int32, (L, 3 * L), 1) & (L - 1)
    tril3 = jnp.where(si <= ti, 1.0, 0.0).astype(BF16)
    gi = lax.broadcasted_iota(jnp.int32, (2 * L, 4 * L), 0)
    gj = lax.broadcasted_iota(jnp.int32, (2 * L, 4 * L), 1)
    gt = gi & (L - 1)
    gs = gj & (L - 1)
    gmask = (gs < gt) | ((gi >= L) & (gs == gt))
    pairs = range(npair)

    def stack2(z):
        return jnp.concatenate([jnp.where(mlo, z, 0), jnp.where(mlo, 0, z)], axis=0)

    cs = [slice(p * LANES, (p + 1) * LANES) for p in pairs]
    rw = {(c, p): slice(c * L, (c + 1) * L) for c in range(nchunk) for p in pairs}
    chains = [(c, p) for c in range(nchunk) for p in pairs]

    cum, a2, q4, vv2 = {}, {}, {}, {}
    for key in chains:
        ldc = ld_s[rw[key], cs[key[1]]]
        hi = _b(ldc)
        r1 = ldc - hi.astype(F32)
        mid = _b(r1)
        lo = _b(r1 - mid.astype(F32))
        cum[key] = _dot(tril3, jnp.concatenate([hi, mid, lo], axis=0))
    for key in chains:
        rows, c_ = rw[key], cs[key[1]]
        ep = jnp.exp(cum[key])
        em = jnp.exp(-cum[key])
        at = -(kn_s[rows, c_] * jnp.exp(cum[key] - ld_s[rows, c_]))
        a2[key] = _b(jnp.concatenate([at, r_s[rows, c_] * ep], axis=0))
        q4[key] = jnp.concatenate([stack2(b_s[rows, c_] * em), stack2(k_s[rows, c_] * em)], axis=0)
        vv2[key] = _b(stack2(v_s[rows, c_]))
    g = {key: jnp.where(gmask, _dot_nt(a2[key], _b(q4[key])), 0.0) for key in chains}
    makv = {key: _dot(_b(g[key][:, LANES:]), vv2[key]) for key in chains}
    mr = {key: _b(g[key][L:, :LANES]) for key in chains}
    w = {key: g[key][:L, :LANES] for key in chains}
    pf = {}
    for key in chains:
        pc = _b(w[key])
        pf[key] = _dot(pc, stack2(pc))
    for k in range(1, nsteps):
        for key in chains:
            pc = _b(pf[key])
            if k + 1 < nsteps:
                both = _dot(pc, jnp.concatenate([stack2(pc), stack2(_b(w[key]))], axis=1))
                w[key] = w[key] + pf[key] + both[:, LANES:]
                pf[key] = both[:, :LANES]
            else:
                w[key] = w[key] + pf[key] + _dot(pc, stack2(_b(w[key])))
    wcat = {key: _b(w[key]) for key in chains}
    pl_ = {key: jnp.exp(cum[key][L - 1:L, :]) for key in chains}
    qpl = {key: _b(q4[key] * pl_[key]) for key in chains}

    state = [sbd_s[p] for p in pairs]
    for c in range(nchunk):
        as_ = [_dot_nt(a2[c, p], _b(state[p])) for p in pairs]
        rhs = [as_[p][:L] + makv[c, p][:L] for p in pairs]
        u = [rhs[p] + _dot(wcat[c, p], stack2(_b(rhs[p]))) for p in pairs]
        u2 = [stack2(_b(u[p])) for p in pairs]
        uv = [jnp.concatenate([u2[p], vv2[c, p]], axis=0) for p in pairs]
        y = [as_[p][L:] + makv[c, p][L:] + _dot(mr[c, p], u2[p]) for p in pairs]
        for p in pairs:
            state[p] = state[p] * pl_[c, p] + _dot_tn(uv[p], qpl[c, p])
        rows = rw[c, 0]
        for p in pairs:
            mean = _segsum(y[p], mlo) * (1.0 / RW_HEAD)
            yc = y[p] - mean
            var = _segsum(yc * yc, mlo) * (1.0 / RW_HEAD)
            yn = yc * lax.rsqrt(var + RW_GN_EPS) * gw_ref[:, cs[p]] + gb_ref[:, cs[p]] + bon_s[rows, cs[p]]
            yg_s[rows, cs[p]] = _b(yn * g_s[rows, cs[p]])
    for p in pairs:
        sbd_s[p] = state[p]
        sout_ref[0, 2 * p] = state[p][:RW_HEAD, :RW_HEAD]
        sout_ref[0, 2 * p + 1] = state[p][RW_HEAD:, RW_HEAD:]
    o_ref[...] = x_ref[...] + _dot(yg_s[...], wo_ref[...])


def _rwkv_call(x, wts, *, nseq, seq_rows, tm):
    n, d = x.shape
    npair = d // LANES
    tps = seq_rows // tm
    kern = functools.partial(_rwkv_kernel, tm=tm)
    x_spec = pl.BlockSpec((tm, d), lambda b, j: (b * tps + j, 0))
    st_spec = pl.BlockSpec((1, 2 * npair, RW_HEAD, RW_HEAD), lambda b, j: (b, 0, 0, 0))
    xn_spec = pl.BlockSpec((SUBLANES, d), lambda b, j: (b, 0))
    scratch = ([pltpu.VMEM((tm, d), F32) for _ in range(8)] + [pltpu.VMEM((tm, d), BF16)]
               + [pltpu.VMEM((SUBLANES, d), F32), pltpu.VMEM((npair, LANES, LANES), F32)])
    return pl.pallas_call(
        kern,
        grid=(nseq, tps),
        in_specs=[x_spec] + [_const_spec(wt.shape, single_buffer=True) for wt in wts],
        out_specs=[x_spec, st_spec, xn_spec],
        out_shape=[jax.ShapeDtypeStruct((n, d), F32),
                   jax.ShapeDtypeStruct((nseq, 2 * npair, RW_HEAD, RW_HEAD), F32),
                   jax.ShapeDtypeStruct((nseq * SUBLANES, d), F32)],
        scratch_shapes=scratch,
        compiler_params=pltpu.CompilerParams(dimension_semantics=("arbitrary", "arbitrary"),
                                             vmem_limit_bytes=VMEM_LIMIT),
        name="rwkv_prompt",
    )(x, *wts)


def _rwkv_decode_kernel(*refs, nb, nt):
    it = iter(refs)
    x_ref, sh_ref = next(it), next(it)
    (gn_ref, mu_ref, wr_ref, wk_ref, wv_ref, wo_ref, w0_ref, w1_ref, w2_ref, a0_ref, a1_ref, a2_ref,
     g1_ref, g2_ref, kk_ref, ka_ref, rk_ref, gw_ref, gb_ref) = [next(it) for _ in range(19)]
    s0_ref = next(it)
    o_ref, sout_ref, xn_ref = next(it), next(it), next(it)
    tr = dict(zip(("r", "ld", "k", "v", "kn", "b"), [next(it) for _ in range(6)]))
    yt_s, g_s, bon_s, yg_s = [next(it) for _ in range(4)]
    n, d = x_ref.shape
    npair = d // LANES
    nhead = 2 * npair
    h = pl.program_id(0)
    lane = lax.broadcasted_iota(jnp.int32, (1, LANES), 1)
    mlo = lane < RW_HEAD

    @pl.when(h == 0)
    def _():
        xn = _rms(x_ref[...], gn_ref[...])
        xprev = jnp.concatenate([sh_ref[...], xn[:n - nb]], axis=0)
        xn_ref[...] = xn[n - nb:]

        def put(name, p, tile):
            if name == "g":
                g_s[:, p * LANES:(p + 1) * LANES] = tile
            elif name == "bon":
                bon_s[:, p * LANES:(p + 1) * LANES] = tile
            else:
                tr[name][p] = (jnp.exp(tile) if name == "ld" else tile).T

        _rw_token_part(xn, xprev - xn, mlo, put,
                       (mu_ref, wr_ref, wk_ref, wv_ref, w0_ref, w1_ref, w2_ref, a0_ref, a1_ref, a2_ref,
                        g1_ref, g2_ref, kk_ref, ka_ref, rk_ref))

    p = h >> 1
    base = pl.multiple_of((h & 1) * RW_HEAD, RW_HEAD)
    sub = lax.broadcasted_iota(jnp.int32, (SUBLANES, 1), 0)
    hrows = pl.ds(base, RW_HEAD)

    def group(gi, carry):
        r0 = pl.multiple_of(base + gi * SUBLANES, SUBLANES)
        vg = [tr["v"][p, pl.ds(r0, SUBLANES), t * nb:(t + 1) * nb] for t in range(nt)]
        ys = [jnp.zeros((SUBLANES, nb), F32) for _ in range(nt)]
        for vi in range(SUBLANES):
            v_idx = gi * SUBLANES + vi
            s = s0_ref[0, v_idx]
            for t in range(nt):
                tc = slice(t * nb, (t + 1) * nb)
                sa = -jnp.sum(s * tr["kn"][p, hrows, tc], axis=0, keepdims=True)
                s = s * tr["ld"][p, hrows, tc] + sa * tr["b"][p, hrows, tc] + vg[t][vi:vi + 1, :] * tr["k"][p, hrows, tc]
                yrow = jnp.sum(s * tr["r"][p, hrows, tc], axis=0, keepdims=True)
                ys[t] = jnp.where(sub == vi, yrow, ys[t])
            sout_ref[0, v_idx] = s
        for t in range(nt):
            yt_s[p, pl.ds(r0, SUBLANES), t * nb:(t + 1) * nb] = ys[t]
        return carry

    lax.fori_loop(0, RW_HEAD // SUBLANES, group, 0)

    @pl.when(h == nhead - 1)
    def _():
        for q in range(npair):
            cs = slice(q * LANES, (q + 1) * LANES)
            y = yt_s[q].T
            mean = _segsum(y, mlo) * (1.0 / RW_HEAD)
            yc = y - mean
            var = _segsum(yc * yc, mlo) * (1.0 / RW_HEAD)
            yn = yc * lax.rsqrt(var + RW_GN_EPS) * gw_ref[:, cs] + gb_ref[:, cs] + bon_s[:, cs]
            yg_s[:, cs] = _b(yn * g_s[:, cs])
        o_ref[...] = x_ref[...] + _dot(yg_s[...], wo_ref[...])


def _rwkv_decode_call(xt, shift0, s0t, wts, *, nb, nt):
    n, d = xt.shape
    npair = d // LANES
    nhead = 2 * npair
    kern = functools.partial(_rwkv_decode_kernel, nb=nb, nt=nt)
    full = lambda shape: pl.BlockSpec(shape, lambda h: (0,) * len(shape))
    st_spec = pl.BlockSpec((1, RW_HEAD, RW_HEAD, nb), lambda h: (h, 0, 0, 0))
    in_specs = [full((n, d)), full((nb, d))] + [_const_spec(wt.shape, single_buffer=True) for wt in wts] + [st_spec]
    scratch = ([pltpu.VMEM((npair, LANES, n), F32) for _ in range(7)]
               + [pltpu.VMEM((n, d), F32), pltpu.VMEM((n, d), F32), pltpu.VMEM((n, d), BF16)])
    return pl.pallas_call(
        kern,
        grid=(nhead,),
        in_specs=in_specs,
        out_specs=[full((n, d)), st_spec, full((nb, d))],
        out_shape=[jax.ShapeDtypeStruct((n, d), F32), jax.ShapeDtypeStruct(s0t.shape, F32),
                   jax.ShapeDtypeStruct((nb, d), F32)],
        scratch_shapes=scratch,
        compiler_params=pltpu.CompilerParams(dimension_semantics=("arbitrary",), vmem_limit_bytes=VMEM_LIMIT),
        name="rwkv_decode",
    )(xt, shift0, *wts, s0t)


def _mlstm_kernel(*refs, tm, seq_rows, npad, sample, L, lsub):
    nsub = L // lsub
    nchunk = tm // L
    it = iter(refs)
    x_ref = next(it)
    if sample:
        cvx_ref, mrow_ref = next(it), next(it)
    gn_ref, wqk_ref, wv_ref, wo_ref, wif_ref, bif_ref, cw_ref, cb_ref, nw_ref, wout_ref = [next(it) for _ in range(10)]
    if sample:
        c0_ref, n0_ref = next(it), next(it)
    o_ref, cout_ref, nout_ref, mout_ref, cvout_ref = [next(it) for _ in range(5)]
    gt_s, q_s, k_s, v_s, og_s, gc_s, ho_s, ext_s = [next(it) for _ in range(8)]
    if not sample:
        cp_s, np_s, m_s = [next(it) for _ in range(3)]
    nqk = wqk_ref.shape[1]
    half = nqk // 2
    nheads = half // ML_DK
    ngate = 2 * nheads
    npair = nheads // 2

    lane = lax.broadcasted_iota(jnp.int32, (1, LANES), 1)
    mlo = lane < ML_DK
    glane = lax.broadcasted_iota(jnp.int32, (1, ngate), 1)
    isf_c = glane >= nheads
    grow = lax.broadcasted_iota(jnp.int32, (ngate, 1), 0)
    isf_r = grow >= nheads
    hlane = lax.broadcasted_iota(jnp.int32, (1, nheads), 1)

    x = x_ref[...]
    xb = _b(_rms(x, gn_ref[...]))
    raw = _dot(xb, wqk_ref[...])
    row = lax.broadcasted_iota(jnp.int32, (tm, 1), 0)
    if sample:
        srow = row & (seq_rows - 1)
        raw = jnp.where((srow >= npad - (ML_CONV - 1)) & (srow < npad), cvx_ref[...], raw)
        cvout_ref[...] = raw
        ext_s[:SUBLANES, :] = jnp.zeros((SUBLANES, nqk), F32)
    else:
        j = pl.program_id(1)

        @pl.when(j == 0)
        def _():
            ext_s[:SUBLANES, :] = jnp.zeros((SUBLANES, nqk), F32)
            cp_s[...] = jnp.zeros(cp_s.shape, F32)
            np_s[...] = jnp.zeros(np_s.shape, F32)
            m_s[...] = jnp.zeros(m_s.shape, F32)

        cvout_ref[...] = raw[tm - SUBLANES:tm, :]
    ext_s[SUBLANES:, :] = raw
    qk = cb_ref[...] + cw_ref[ML_CONV - 1:ML_CONV, :] * raw
    for s in range(1, ML_CONV):
        qk = qk + cw_ref[ML_CONV - 1 - s:ML_CONV - s, :] * ext_s[SUBLANES - s:SUBLANES - s + tm, :]
    if not sample:
        ext_s[:SUBLANES, :] = raw[tm - SUBLANES:tm, :]
    qk = qk * _sigmoid(qk)
    q_s[...] = qk[:, :half] * (ML_DK ** -0.5)
    k_s[...] = qk[:, half:]
    v_s[...] = _dot(xb, wv_ref[...])
    og_s[...] = _sigmoid(_dot(xb, wo_ref[...]))
    ifp = _dot(xb, wif_ref[...]) + bif_ref[...]
    gcol = jnp.where(isf_c, -_softplus(-ifp), ifp)
    if npad:
        keep = (row & (seq_rows - 1)) >= npad
        gcol = jnp.where(keep, gcol, jnp.where(isf_c, 0.0, NEG_BIG))
    gc_s[...] = gcol
    gct = gcol.T
    for c in range(nchunk):
        gt_s[c] = gct[:, c * L:(c + 1) * L]

    ti = lax.broadcasted_iota(jnp.int32, (L, L), 0)
    si = lax.broadcasted_iota(jnp.int32, (L, L), 1)
    sh = _log2(lsub)
    same = (ti >> sh) == (si >> sh)
    causal = same & (si <= ti)
    tril = jnp.where(causal, 1.0, 0.0).astype(F32)
    triu = jnp.where(same & (ti <= si), 1.0, 0.0).astype(F32)

    def chunk(c, carry):
        r0 = pl.multiple_of(c * L, L)
        rows = pl.ds(r0, L)
        gc = gc_s[rows, :]
        gt = gt_s[c]
        bcs = _dot(tril, jnp.where(isf_c, gc, 0.0), HI)
        brs = _dot(jnp.where(isf_r, gt, 0.0), triu, HI)
        blast = _seq_last(bcs, lsub)
        if sample:
            mcols = mrow_ref[rows, :]
        else:
            mcols = m_s[...]
        heads = range(nheads)
        prs = range(npair)
        ps = [slice(pp * LANES, (pp + 1) * LANES) for pp in prs]
        hs = [slice(h * ML_DV, (h + 1) * ML_DV) for h in heads]
        q2 = [q_s[rows, ps[pp]] for pp in prs]
        k2 = [k_s[rows, ps[pp]] for pp in prs]
        k2b = [_b(z) for z in k2]
        if sample:
            c_in = [[jnp.concatenate([c0_ref[c * nsub + q, 2 * pp], c0_ref[c * nsub + q, 2 * pp + 1]], axis=0)
                     for q in range(nsub)] for pp in prs]
            n_in = [[n0_ref[c * nsub + q][:, ps[pp]] for q in range(nsub)] for pp in prs]
            n_rows = [jnp.concatenate([jnp.broadcast_to(n_in[pp][q], (lsub, LANES)) for q in range(nsub)], axis=0)
                      for pp in prs]
        else:
            c_prev = [cp_s[pp] for pp in prs]
            n_prev = [np_s[:, ps[pp]] for pp in prs]
            n_rows = n_prev
        bcol = [_colsel(bcs, glane, nheads + h) for h in heads]
        licol = [_colsel(gc, glane, h) for h in heads]
        mcol = [_colsel(mcols, hlane, h) for h in heads]
        blcol = [_colsel(blast, glane, nheads + h) for h in heads]
        dlog = [jnp.where(causal, bcol[h] - (brs[nheads + h:nheads + h + 1, :] - gt[h:h + 1, :]), -jnp.inf)
                for h in heads]
        ginter = [bcol[h] + mcol[h] for h in heads]
        m_t = [jnp.maximum(ginter[h], jnp.max(dlog[h], axis=-1, keepdims=True)) for h in heads]
        dw = [jnp.exp(dlog[h] - m_t[h]) for h in heads]
        winter = [jnp.exp(ginter[h] - m_t[h]) for h in heads]
        qh = [jnp.where(mlo if h % 2 == 0 else jnp.logical_not(mlo), q2[h // 2], 0.0) for h in heads]
        qhb = [_b(z) for z in qh]
        sc = [_dot_nt(qhb[h], k2b[h // 2]) * dw[h] for h in heads]
        if sample:
            inter = [jnp.concatenate([_dot(qhb[h][q * lsub:(q + 1) * lsub], _b(c_in[h // 2][q]))
                                      for q in range(nsub)], axis=0) for h in heads]
        else:
            cb = [_b(z) for z in c_prev]
            inter = [_dot(qhb[h], cb[h // 2]) for h in heads]
        vh = [v_s[rows, hs[h]] for h in heads]
        num = [winter[h] * inter[h] + _dot(_b(sc[h]), _b(vh[h])) for h in heads]
        den = [winter[h] * jnp.sum(qh[h] * n_rows[h // 2], axis=-1, keepdims=True)
               + jnp.sum(sc[h], axis=-1, keepdims=True) for h in heads]
        for h in heads:
            hout = num[h] / jnp.maximum(jnp.abs(den[h]), jnp.exp(-m_t[h]))
            hn = hout * lax.rsqrt(jnp.mean(hout * hout, axis=-1, keepdims=True) + NORM_EPS) * nw_ref[:, hs[h]]
            ho_s[rows, hs[h]] = _b(hn * og_s[rows, hs[h]])
        mnew = [_seq_last(m_t[h], lsub) for h in heads]
        ws = [jnp.exp(blcol[h] - bcol[h] + licol[h] - mnew[h]) for h in heads]
        wstc = [jnp.exp(blcol[h] + mcol[h] - mnew[h]) for h in heads]
        mt_all = jnp.zeros((L, nheads), F32)
        for h in heads:
            mt_all = jnp.where(hlane == h, m_t[h], mt_all)
        for pp in prs:
            lo, hi = 2 * pp, 2 * pp + 1
            wsk = jnp.where(mlo, ws[lo], ws[hi]) * k2[pp]
            wst = jnp.where(mlo, wstc[lo], wstc[hi])
            wvb = _b(jnp.concatenate([ws[lo] * vh[lo], ws[hi] * vh[hi]], axis=0))
            kmb = jnp.concatenate([jnp.where(mlo, k2b[pp], 0), jnp.where(mlo, 0, k2b[pp])], axis=0)
            if sample:
                for q in range(nsub):
                    last = (q + 1) * lsub - 1
                    sel = [slice(blk * L + q * lsub, blk * L + (q + 1) * lsub) for blk in range(2)]
                    wq = wst[last:last + 1, :]
                    wrow = jnp.concatenate([jnp.broadcast_to(wstc[h_][last:last + 1, :], (ML_DK, 1)) for h_ in (lo, hi)],
                                           axis=0)
                    upd = _dot_tn(jnp.concatenate([kmb[s_] for s_ in sel], axis=0),
                                  jnp.concatenate([wvb[s_] for s_ in sel], axis=0))
                    c_new = c_in[pp][q] * wrow + upd
                    cout_ref[c * nsub + q, 2 * pp] = c_new[:ML_DK]
                    cout_ref[c * nsub + q, 2 * pp + 1] = c_new[ML_DK:]
                    nq = n_in[pp][q] * wq + jnp.sum(wsk[q * lsub:(q + 1) * lsub], axis=0, keepdims=True)
                    nout_ref[c * nsub + q, :, ps[pp]] = nq
            else:
                wrow = jnp.concatenate([jnp.broadcast_to(wstc[h_], (ML_DK, 1)) for h_ in (lo, hi)], axis=0)
                cp_s[pp] = c_prev[pp] * wrow + _dot_tn(kmb, wvb)
                np_s[:, ps[pp]] = n_prev[pp] * wst + jnp.sum(wsk, axis=0, keepdims=True)
        if sample:
            mout_ref[rows, :] = mt_all
        else:
            m_s[...] = mt_all[L - 1:L, :]
        return carry

    lax.fori_loop(0, nchunk, chunk, 0)
    if not sample:
        for pp in range(npair):
            cout_ref[0, 2 * pp] = cp_s[pp][:ML_DK]
            cout_ref[0, 2 * pp + 1] = cp_s[pp][ML_DK:]
        nout_ref[0] = np_s[...]
        mout_ref[0] = m_s[...]
    o_ref[...] = x_ref[...] + _dot(ho_s[...], wout_ref[...])


def _mlstm_call(x, cvx, mrow, c0p, n0, wts, *, nseq, seq_rows, npad, sample, tm):
    n, d = x.shape
    nqk = wts[1].shape[1]
    nv = wts[2].shape[1]
    nheads = nqk // 2 // ML_DK
    npair = nheads // 2
    chunk = CHUNK if sample else ML_CHUNK
    lsub = seq_rows if sample else chunk
    kern = functools.partial(_mlstm_kernel, tm=tm, seq_rows=seq_rows, npad=npad, sample=sample, L=chunk, lsub=lsub)
    if sample:
        grid = (n // tm,)
        tile = lambda i: (i, 0)
        spt = tm // seq_rows
        c_spec = pl.BlockSpec((spt, nheads, ML_DK, ML_DV), lambda i: (i, 0, 0, 0))
        n_spec = pl.BlockSpec((spt, 1, nqk // 2), lambda i: (i, 0, 0))
        m_spec = pl.BlockSpec((tm, nheads), tile)
        m_shape = (n, nheads)
        cv_spec = pl.BlockSpec((tm, nqk), tile)
        cv_shape = (n, nqk)
        sem = ("arbitrary",)
    else:
        tps = seq_rows // tm
        grid = (nseq, tps)
        tile = lambda b, j: (b * tps + j, 0)
        c_spec = pl.BlockSpec((1, nheads, ML_DK, ML_DV), lambda b, j: (b, 0, 0, 0))
        n_spec = pl.BlockSpec((1, 1, nqk // 2), lambda b, j: (b, 0, 0))
        m_spec = pl.BlockSpec((1, 1, nheads), lambda b, j: (b, 0, 0))
        m_shape = (nseq, 1, nheads)
        cv_spec = pl.BlockSpec((SUBLANES, nqk), lambda b, j: (b, 0))
        cv_shape = (nseq * SUBLANES, nqk)
        sem = ("arbitrary", "arbitrary")
    x_spec = pl.BlockSpec((tm, d), tile)
    in_specs = [x_spec]
    args = [x]
    if sample:
        in_specs += [pl.BlockSpec((tm, nqk), tile), pl.BlockSpec((tm, nheads), tile)]
        args += [cvx, mrow]
    for wt in wts:
        in_specs.append(_const_spec(wt.shape, single_buffer=True))
        args.append(wt)
    if sample:
        in_specs += [c_spec, n_spec]
        args += [c0p, n0]
    scratch = [pltpu.VMEM((tm // chunk, 2 * nheads, chunk), F32), pltpu.VMEM((tm, nqk // 2), F32),
               pltpu.VMEM((tm, nqk // 2), F32),
               pltpu.VMEM((tm, nv), F32), pltpu.VMEM((tm, nv), F32), pltpu.VMEM((tm, 2 * nheads), F32),
               pltpu.VMEM((tm, nv), BF16), pltpu.VMEM((tm + SUBLANES, nqk), F32)]
    if not sample:
        scratch += [pltpu.VMEM((npair, LANES, LANES), F32),
                    pltpu.VMEM((1, nqk // 2), F32), pltpu.VMEM((1, nheads), F32)]
    return pl.pallas_call(
        kern,
        grid=grid,
        in_specs=in_specs,
        out_specs=[x_spec, c_spec, n_spec, m_spec, cv_spec],
        out_shape=[jax.ShapeDtypeStruct((n, d), F32),
                   jax.ShapeDtypeStruct((nseq, nheads, ML_DK, ML_DV), F32),
                   jax.ShapeDtypeStruct((nseq, 1, nqk // 2), F32),
                   jax.ShapeDtypeStruct(m_shape, F32),
                   jax.ShapeDtypeStruct(cv_shape, F32)],
        scratch_shapes=scratch,
        compiler_params=pltpu.CompilerParams(dimension_semantics=sem, vmem_limit_bytes=VMEM_LIMIT),
        name="mlstm_sample" if sample else "mlstm_prompt",
    )(*args)


def kernel(x_prompt, x_sample, state_rwkv_S, state_rwkv_shift, state_mlstm_C, state_mlstm_n, state_mlstm_m,
           state_mlstm_conv, norm_ffa, ffa_wg, ffa_wu, ffa_wd, norm_mix, norm_ffb, ffb_wg, ffb_wu, ffb_wd,
           rw_mu, rw_wr, rw_wk, rw_wv, rw_wo, rw_w0, rw_w1, rw_w2, rw_a0, rw_a1, rw_a2, rw_g1, rw_g2,
           rw_k_k, rw_k_a, rw_r_k, rw_gn_w, rw_gn_b, ml_w_in, ml_b_if, ml_conv_w, ml_conv_b, ml_norm_w,
           ml_w_out, norm_final):
    bp, tp, d = x_prompt.shape
    bs, ts, _ = x_sample.shape
    depth = norm_ffa.shape[0]
    slot = SUBLANES
    npad = slot - ts
    assert 0 < ts <= slot and npad >= ML_CONV - 1
    tm_p = min(512, tp)
    tm_ml = min(1024, tp)
    tm_s = min(4 * CHUNK, bs * slot)
    assert tp % tm_p == 0 and tm_p % CHUNK == 0 and tp % tm_ml == 0 and tm_ml % ML_CHUNK == 0
    assert (bs * slot) % tm_s == 0 and tm_s % CHUNK == 0
    ml_heads = ml_b_if.shape[1] // 2
    nqk = 2 * ml_heads * ML_DK
    nv = ml_heads * ML_DV

    xp = x_prompt.reshape(bp * tp, d)
    xs = x_sample.transpose(1, 0, 2).reshape(ts * bs, d)

    def to_slots(z):
        return jnp.concatenate([jnp.zeros((bs, npad, d), F32), z.reshape(ts, bs, d).transpose(1, 0, 2)],
                               axis=1).reshape(bs * slot, d)

    def from_slots(z):
        return z.reshape(bs, slot, d)[:, npad:].transpose(1, 0, 2).reshape(ts * bs, d)

    row2 = lambda a: a.reshape(1, -1)

    tm_ffn = min(512, bs * ts)
    assert (bp * tp) % tm_ffn == 0 and (bs * ts) % tm_ffn == 0
    ffa = (norm_ffa.reshape(depth, 1, d), _b(ffa_wg), _b(ffa_wu), _b(ffa_wd))
    ffb = (norm_ffb.reshape(depth, 1, d), _b(ffb_wg), _b(ffb_wu), _b(ffb_wd))

    def ffn(xp_, xs_, wset, layer, final):
        return _ffn_call(xp_, xs_, *wset, row2(norm_final), layer, final_norm=final, tm=tm_ffn)

    new_p = {k_: [] for k_ in ("S", "shift", "C", "n", "m", "conv")}
    new_s = {k_: [] for k_ in ("S", "shift", "C", "n", "m", "conv")}
    for i in range(depth):
        xp, xs = ffn(xp, xs, ffa, i, False)
        j = i // 2
        if i % 2 == 0:
            wts = [row2(norm_mix[i]), rw_mu[j], _b(rw_wr[j]), _b(rw_wk[j]), _b(rw_wv[j]), _b(rw_wo[j]),
                   row2(rw_w0[j]), _b(rw_w1[j]), _b(rw_w2[j]), row2(rw_a0[j]), _b(rw_a1[j]), _b(rw_a2[j]),
                   _b(rw_g1[j]), _b(rw_g2[j]), row2(rw_k_k[j]), row2(rw_k_a[j]), row2(rw_r_k[j]),
                   row2(rw_gn_w[j]), row2(rw_gn_b[j])]
            xp, sbd, tail = _rwkv_call(xp, wts, nseq=bp, seq_rows=tp, tm=tm_p)
            new_p["S"].append(sbd)
            new_p["shift"].append(tail.reshape(bp, SUBLANES, d)[:, SUBLANES - 1])
            xs, s_t, sh_new = _rwkv_decode_call(xs, state_rwkv_shift[j], jnp.transpose(state_rwkv_S[j], (1, 2, 3, 0)),
                                                wts, nb=bs, nt=ts)
            new_s["S"].append(jnp.transpose(s_t, (3, 0, 1, 2)))
            new_s["shift"].append(sh_new)
        else:
            w_in = ml_w_in[j]
            w_if = w_in[:, nqk + nv + d:]
            wts = [row2(norm_mix[i]), _b(w_in[:, :nqk]), _b(w_in[:, nqk:nqk + nv]), _b(w_in[:, nqk + nv:nqk + nv + d]),
                   _b(w_if), row2(ml_b_if[j]), ml_conv_w[j],
                   row2(ml_conv_b[j]), row2(ml_norm_w[j]), _b(ml_w_out[j])]
            xp, cp, n_, m_, tail = _mlstm_call(xp, None, None, None, None, wts, nseq=bp, seq_rows=tp, npad=0,
                                               sample=False, tm=tm_ml)
            new_p["C"].append(jnp.swapaxes(cp, -1, -2))
            new_p["n"].append(n_.reshape(bp, ml_heads, ML_DK))
            new_p["m"].append(m_.reshape(bp, ml_heads))
            new_p["conv"].append(tail.reshape(bp, SUBLANES, nqk)[:, SUBLANES - (ML_CONV - 1):])
            conv0 = state_mlstm_conv[j]
            cvx = jnp.concatenate([jnp.zeros((bs, npad - (ML_CONV - 1), nqk), F32), conv0,
                                   jnp.zeros((bs, slot - npad, nqk), F32)], axis=1).reshape(bs * slot, nqk)
            mrow = jnp.repeat(state_mlstm_m[j], slot, axis=0)
            xs, cp, n_, mt, raw = _mlstm_call(to_slots(xs), cvx, mrow, jnp.swapaxes(state_mlstm_C[j], -1, -2),
                                              state_mlstm_n[j].reshape(bs, 1, ml_heads * ML_DK), wts, nseq=bs,
                                              seq_rows=slot, npad=npad, sample=True, tm=tm_s)
            xs = from_slots(xs)
            new_s["C"].append(jnp.swapaxes(cp, -1, -2))
            new_s["n"].append(n_.reshape(bs, ml_heads, ML_DK))
            new_s["m"].append(mt.reshape(bs, slot, ml_heads)[:, slot - 1])
            new_s["conv"].append(raw.reshape(bs, slot, nqk)[:, slot - (ML_CONV - 1):])
        xp, xs = ffn(xp, xs, ffb, i, i == depth - 1)
    y_prompt = xp.reshape(bp, tp, d)
    y_sample = xs.reshape(ts, bs, d).transpose(1, 0, 2)
    st = lambda lst: jnp.stack(lst)
    return (y_prompt, y_sample,
            st(new_p["S"]), st(new_p["shift"]), st(new_p["C"]), st(new_p["n"]), st(new_p["m"]), st(new_p["conv"]),
            st(new_s["S"]), st(new_s["shift"]), st(new_s["C"]), st(new_s["n"]), st(new_s["m"]), st(new_s["conv"]))
```

```python
import functools
import math

import jax
import jax.numpy as jnp
from jax import lax
from jax.experimental import pallas as pl
from jax.experimental.pallas import tpu as pltpu

F32 = jnp.float32
BF16 = jnp.bfloat16

NORM_EPS = 1e-6
RW_GN_EPS = 64e-5
RW_HEAD = 64
ML_DK = 64
ML_DV = 128
ML_CONV = 4

LANES = 128
SUBLANES = 8
MXU_COLS = 256
CHUNK = 64
ML_CHUNK = 128
NEG_BIG = -1e30
VMEM_LIMIT = 58 * 1024 * 1024
HI = lax.Precision.HIGHEST


def _dot(a, b, precision=None):
    return jnp.dot(a, b, preferred_element_type=F32, precision=precision)


def _dot_nt(a, b, precision=None):
    return lax.dot_general(a, b, (((1,), (1,)), ((), ())), preferred_element_type=F32, precision=precision)


def _dot_tn(a, b, precision=None):
    return lax.dot_general(a, b, (((0,), (0,)), ((), ())), preferred_element_type=F32, precision=precision)


def _b(x):
    return x.astype(BF16)


def _rms(x, g):
    return x * lax.rsqrt(jnp.mean(x * x, axis=-1, keepdims=True) + NORM_EPS) * g


def _sigmoid(x):
    return 1.0 / (1.0 + jnp.exp(-x))


def _softplus(x):
    return jnp.maximum(x, 0.0) + jnp.log1p(jnp.exp(-jnp.abs(x)))


def _segsum(x, mlo):
    lo = jnp.sum(jnp.where(mlo, x, 0.0), axis=-1, keepdims=True)
    hi = jnp.sum(jnp.where(mlo, 0.0, x), axis=-1, keepdims=True)
    return jnp.where(mlo, lo, hi)


def _colsel(x, lane_idx, j):
    return jnp.sum(jnp.where(lane_idx == j, x, 0.0), axis=-1, keepdims=True)


def _seq_last(x, lsub):
    n = x.shape[0]
    if lsub == n:
        return x[n - 1:n]
    parts = [jnp.broadcast_to(x[q * lsub + lsub - 1:q * lsub + lsub], (lsub,) + x.shape[1:])
             for q in range(n // lsub)]
    return jnp.concatenate(parts, axis=0)


def _const_spec(shape, single_buffer=False):
    nd = len(shape)
    if single_buffer:
        return pl.BlockSpec(shape, lambda *_: (0,) * nd, pipeline_mode=pl.Buffered(1))
    return pl.BlockSpec(shape, lambda *_: (0,) * nd)


def _log2(n):
    k = int(math.log2(n))
    assert 1 << k == n, n
    return k


def _ffn_kernel(xp_ref, xs_ref, g_ref, wg_ref, wu_ref, wd_ref, gf_ref, op_ref, os_ref, *, fchunk, final_norm, np_steps):
    def tile(x_ref, o_ref):
        x = x_ref[...]
        xb = _b(_rms(x, g_ref[...]))
        acc = jnp.zeros(x.shape, F32)
        for lo, hi in zip(fchunk[:-1], fchunk[1:]):
            sl = slice(lo, hi)
            gate = _dot(xb, wg_ref[:, sl])
            up = _dot(xb, wu_ref[:, sl])
            h = _b(gate * _sigmoid(gate) * up)
            acc = acc + _dot(h, wd_ref[sl, :])
        out = x + 0.5 * acc
        if final_norm:
            out = _rms(out, gf_ref[...])
        o_ref[...] = out

    is_p = pl.program_id(0) < np_steps
    pl.when(is_p)(functools.partial(tile, xp_ref, op_ref))
    pl.when(jnp.logical_not(is_p))(functools.partial(tile, xs_ref, os_ref))


def _ffn_call(xp, xs, g_all, wg_all, wu_all, wd_all, gf, layer, *, final_norm, tm):
    (n_p, d), n_s = xp.shape, xs.shape[0]
    nf = wg_all.shape[2]
    np_steps, ns_steps = n_p // tm, n_s // tm
    ntile = nf // MXU_COLS
    fchunk = (0, -(-ntile // 2) * MXU_COLS, nf) if nf % MXU_COLS == 0 and ntile > 1 else (0, nf)
    kern = functools.partial(_ffn_kernel, fchunk=fchunk, final_norm=final_norm, np_steps=np_steps)
    p_spec = pl.BlockSpec((tm, d), lambda i: (jnp.minimum(i, np_steps - 1), 0))
    s_spec = pl.BlockSpec((tm, d), lambda i: (jnp.maximum(i - np_steps, 0), 0))
    w_spec = lambda a, b: pl.BlockSpec((None, a, b), lambda i: (layer, 0, 0), pipeline_mode=pl.Buffered(1))
    return pl.pallas_call(
        kern,
        grid=(np_steps + ns_steps,),
        in_specs=[p_spec, s_spec, pl.BlockSpec((None, 1, d), lambda i: (layer, 0, 0)),
                  w_spec(d, nf), w_spec(d, nf), w_spec(nf, d), _const_spec((1, d))],
        out_specs=[p_spec, s_spec],
        out_shape=[jax.ShapeDtypeStruct((n_p, d), F32), jax.ShapeDtypeStruct((n_s, d), F32)],
        compiler_params=pltpu.CompilerParams(dimension_semantics=("arbitrary",), vmem_limit_bytes=VMEM_LIMIT),
        name="ffn",
    )(xp, xs, g_all, wg_all, wu_all, wd_all, gf)


def _rw_token_part(xn, xx, mlo, put, wrefs):
    mu_ref, wr_ref, wk_ref, wv_ref, w0_ref, w1_ref, w2_ref, a0_ref, a1_ref, a2_ref, g1_ref, g2_ref, kk_ref, ka_ref, rk_ref = wrefs
    cols = [slice(p * LANES, (p + 1) * LANES) for p in range(xn.shape[1] // LANES)]

    def put_all(name, z):
        for p, cs in enumerate(cols):
            put(name, p, z[:, cs])

    def mix(i):
        return _b(xn + xx * mu_ref[i:i + 1, :])

    r = _dot(mix(0), wr_ref[...])
    put_all("r", r)
    wl = _b(jnp.tanh(_dot(mix(1), w1_ref[...])))
    ld = -math.exp(-0.5) * _sigmoid(w0_ref[...] + _dot(wl, w2_ref[...]))
    k = _dot(mix(2), wk_ref[...])
    v = _dot(mix(3), wv_ref[...])
    al = _b(_dot(mix(4), a1_ref[...]))
    a = _sigmoid(a0_ref[...] + _dot(al, a2_ref[...]))
    gl = _b(_sigmoid(_dot(mix(5), g1_ref[...])))
    put_all("g", _dot(gl, g2_ref[...]))
    k2 = k * (1.0 + (a - 1.0) * ka_ref[...])
    kk = k * kk_ref[...]
    rkk = r * k2 * rk_ref[...]
    put_all("ld", ld)
    put_all("k", k2)
    put_all("v", v)
    for p, cs in enumerate(cols):
        kkp = kk[:, cs]
        kn = kkp / jnp.maximum(jnp.sqrt(_segsum(kkp * kkp, mlo)), 1e-12)
        put("kn", p, kn)
        put("b", p, kn * a[:, cs])
        put("bon", p, _segsum(rkk[:, cs], mlo) * v[:, cs])


def _rwkv_kernel(*refs, tm):
    L = CHUNK
    nchunk = tm // L
    nsteps = _log2(L)
    it = iter(refs)
    x_ref = next(it)
    (gn_ref, mu_ref, wr_ref, wk_ref, wv_ref, wo_ref, w0_ref, w1_ref, w2_ref, a0_ref, a1_ref, a2_ref,
     g1_ref, g2_ref, kk_ref, ka_ref, rk_ref, gw_ref, gb_ref) = [next(it) for _ in range(19)]
    o_ref, sout_ref, xn_ref = next(it), next(it), next(it)
    r_s, ld_s, k_s, v_s, kn_s, b_s, g_s, bon_s = sset = [next(it) for _ in range(8)]
    yg_s, carry_s, sbd_s = next(it), next(it), next(it)
    d = x_ref.shape[1]
    npair = d // LANES

    lane = lax.broadcasted_iota(jnp.int32, (1, LANES), 1)
    mlo = lane < RW_HEAD
    row = lax.broadcasted_iota(jnp.int32, (tm, 1), 0)

    @pl.when(pl.program_id(1) == 0)
    def _():
        carry_s[...] = jnp.zeros(carry_s.shape, F32)
        sbd_s[...] = jnp.zeros(sbd_s.shape, F32)

    xn = _rms(x_ref[...], gn_ref[...])
    xprev = jnp.where(row == 0, carry_s[SUBLANES - 1:SUBLANES, :], pltpu.roll(xn, 1, 0))
    carry_s[...] = xn[tm - SUBLANES:tm, :]
    xn_ref[...] = xn[tm - SUBLANES:tm, :]
    sref = dict(zip(("r", "ld", "k", "v", "kn", "b", "g", "bon"), sset))

    def put(name, p, tile):
        sref[name][:, p * LANES:(p + 1) * LANES] = tile

    _rw_token_part(xn, xprev - xn, mlo, put,
                   (mu_ref, wr_ref, wk_ref, wv_ref, w0_ref, w1_ref, w2_ref, a0_ref, a1_ref, a2_ref,
                    g1_ref, g2_ref, kk_ref, ka_ref, rk_ref))

    ti = lax.broadcasted_iota(jnp.int32, (L, 3 * L), 0)
    si = lax.broadcasted_iota(jnp.int32, (L, 3 * L), 1) & (L - 1)
    tril3 = jnp.where(si <= ti, 1.0, 0.0).astype(BF16)
    gi = lax.broadcasted_iota(jnp.int32, (2 * L, 4 * L), 0)
    gj = lax.broadcasted_iota(jnp.int32, (2 * L, 4 * L), 1)
    gt = gi & (L - 1)
    gs = gj & (L - 1)
    gmask = (gs < gt) | ((gi >= L) & (gs == gt))
    pairs = range(npair)

    def stack2(z):
        return jnp.concatenate([jnp.where(mlo, z, 0), jnp.where(mlo, 0, z)], axis=0)

    cs = [slice(p * LANES, (p + 1) * LANES) for p in pairs]
    rw = {(c, p): slice(c * L, (c + 1) * L) for c in range(nchunk) for p in pairs}
    chains = [(c, p) for c in range(nchunk) for p in pairs]

    cum, a2, q4, vv2 = {}, {}, {}, {}
    for key in chains:
        ldc = ld_s[rw[key], cs[key[1]]]
        hi = _b(ldc)
        r1 = ldc - hi.astype(F32)
        mid = _b(r1)
        lo = _b(r1 - mid.astype(F32))
        cum[key] = _dot(tril3, jnp.concatenate([hi, mid, lo], axis=0))
    for key in chains:
        rows, c_ = rw[key], cs[key[1]]
        ep = jnp.exp(cum[key])
        em = jnp.exp(-cum[key])
        at = -(kn_s[rows, c_] * jnp.exp(cum[key] - ld_s[rows, c_]))
        a2[key] = _b(jnp.concatenate([at, r_s[rows, c_] * ep], axis=0))
        q4[key] = jnp.concatenate([stack2(b_s[rows, c_] * em), stack2(k_s[rows, c_] * em)], axis=0)
        vv2[key] = _b(stack2(v_s[rows, c_]))
    g = {key: jnp.where(gmask, _dot_nt(a2[key], _b(q4[key])), 0.0) for key in chains}
    makv = {key: _dot(_b(g[key][:, LANES:]), vv2[key]) for key in chains}
    mr = {key: _b(g[key][L:, :LANES]) for key in chains}
    w = {key: g[key][:L, :LANES] for key in chains}
    pf = {}
    for key in chains:
        pc = _b(w[key])
        pf[key] = _dot(pc, stack2(pc))
    for k in range(1, nsteps):
        for key in chains:
            pc = _b(pf[key])
            if k + 1 < nsteps:
                both = _dot(pc, jnp.concatenate([stack2(pc), stack2(_b(w[key]))], axis=1))
                w[key] = w[key] + pf[key] + both[:, LANES:]
                pf[key] = both[:, :LANES]
            else:
                w[key] = w[key] + pf[key] + _dot(pc, stack2(_b(w[key])))
    wcat = {key: _b(w[key]) for key in chains}
    pl_ = {key: jnp.exp(cum[key][L - 1:L, :]) for key in chains}
    qpl = {key: _b(q4[key] * pl_[key]) for key in chains}

    state = [sbd_s[p] for p in pairs]
    for c in range(nchunk):
        as_ = [_dot_nt(a2[c, p], _b(state[p])) for p in pairs]
        rhs = [as_[p][:L] + makv[c, p][:L] for p in pairs]
        u = [rhs[p] + _dot(wcat[c, p], stack2(_b(rhs[p]))) for p in pairs]
        u2 = [stack2(_b(u[p])) for p in pairs]
        uv = [jnp.concatenate([u2[p], vv2[c, p]], axis=0) for p in pairs]
        y = [as_[p][L:] + makv[c, p][L:] + _dot(mr[c, p], u2[p]) for p in pairs]
        for p in pairs:
            state[p] = state[p] * pl_[c, p] + _dot_tn(uv[p], qpl[c, p])
        rows = rw[c, 0]
        for p in pairs:
            mean = _segsum(y[p], mlo) * (1.0 / RW_HEAD)
            yc = y[p] - mean
            var = _segsum(yc * yc, mlo) * (1.0 / RW_HEAD)
            yn = yc * lax.rsqrt(var + RW_GN_EPS) * gw_ref[:, cs[p]] + gb_ref[:, cs[p]] + bon_s[rows, cs[p]]
            yg_s[rows, cs[p]] = _b(yn * g_s[rows, cs[p]])
    for p in pairs:
        sbd_s[p] = state[p]
        sout_ref[0, 2 * p] = state[p][:RW_HEAD, :RW_HEAD]
        sout_ref[0, 2 * p + 1] = state[p][RW_HEAD:, RW_HEAD:]
    o_ref[...] = x_ref[...] + _dot(yg_s[...], wo_ref[...])


def _rwkv_call(x, wts, *, nseq, seq_rows, tm):
    n, d = x.shape
    npair = d // LANES
    tps = seq_rows // tm
    kern = functools.partial(_rwkv_kernel, tm=tm)
    x_spec = pl.BlockSpec((tm, d), lambda b, j: (b * tps + j, 0))
    st_spec = pl.BlockSpec((1, 2 * npair, RW_HEAD, RW_HEAD), lambda b, j: (b, 0, 0, 0))
    xn_spec = pl.BlockSpec((SUBLANES, d), lambda b, j: (b, 0))
    scratch = ([pltpu.VMEM((tm, d), F32) for _ in range(8)] + [pltpu.VMEM((tm, d), BF16)]
               + [pltpu.VMEM((SUBLANES, d), F32), pltpu.VMEM((npair, LANES, LANES), F32)])
    return pl.pallas_call(
        kern,
        grid=(nseq, tps),
        in_specs=[x_spec] + [_const_spec(wt.shape, single_buffer=True) for wt in wts],
        out_specs=[x_spec, st_spec, xn_spec],
        out_shape=[jax.ShapeDtypeStruct((n, d), F32),
                   jax.ShapeDtypeStruct((nseq, 2 * npair, RW_HEAD, RW_HEAD), F32),
                   jax.ShapeDtypeStruct((nseq * SUBLANES, d), F32)],
        scratch_shapes=scratch,
        compiler_params=pltpu.CompilerParams(dimension_semantics=("arbitrary", "arbitrary"),
                                             vmem_limit_bytes=VMEM_LIMIT),
        name="rwkv_prompt",
    )(x, *wts)


def _rwkv_decode_kernel(*refs, nb, nt):
    it = iter(refs)
    x_ref, sh_ref = next(it), next(it)
    (gn_ref, mu_ref, wr_ref, wk_ref, wv_ref, wo_ref, w0_ref, w1_ref, w2_ref, a0_ref, a1_ref, a2_ref,
     g1_ref, g2_ref, kk_ref, ka_ref, rk_ref, gw_ref, gb_ref) = [next(it) for _ in range(19)]
    s0_ref = next(it)
    o_ref, sout_ref, xn_ref = next(it), next(it), next(it)
    tr = dict(zip(("r", "ld", "k", "v", "kn", "b"), [next(it) for _ in range(6)]))
    yt_s, g_s, bon_s, yg_s = [next(it) for _ in range(4)]
    n, d = x_ref.shape
    npair = d // LANES
    nhead = 2 * npair
    h = pl.program_id(0)
    lane = lax.broadcasted_iota(jnp.int32, (1, LANES), 1)
    mlo = lane < RW_HEAD

    @pl.when(h == 0)
    def _():
        xn = _rms(x_ref[...], gn_ref[...])
        xprev = jnp.concatenate([sh_ref[...], xn[:n - nb]], axis=0)
        xn_ref[...] = xn[n - nb:]

        def put(name, p, tile):
            if name == "g":
                g_s[:, p * LANES:(p + 1) * LANES] = tile
            elif name == "bon":
                bon_s[:, p * LANES:(p + 1) * LANES] = tile
            else:
                tr[name][p] = (jnp.exp(tile) if name == "ld" else tile).T

        _rw_token_part(xn, xprev - xn, mlo, put,
                       (mu_ref, wr_ref, wk_ref, wv_ref, w0_ref, w1_ref, w2_ref, a0_ref, a1_ref, a2_ref,
                        g1_ref, g2_ref, kk_ref, ka_ref, rk_ref))

    p = h >> 1
    base = pl.multiple_of((h & 1) * RW_HEAD, RW_HEAD)
    sub = lax.broadcasted_iota(jnp.int32, (SUBLANES, 1), 0)
    hrows = pl.ds(base, RW_HEAD)

    def group(gi, carry):
        r0 = pl.multiple_of(base + gi * SUBLANES, SUBLANES)
        vg = [tr["v"][p, pl.ds(r0, SUBLANES), t * nb:(t + 1) * nb] for t in range(nt)]
        ys = [jnp.zeros((SUBLANES, nb), F32) for _ in range(nt)]
        for vi in range(SUBLANES):
            v_idx = gi * SUBLANES + vi
            s = s0_ref[0, v_idx]
            for t in range(nt):
                tc = slice(t * nb, (t + 1) * nb)
                sa = -jnp.sum(s * tr["kn"][p, hrows, tc], axis=0, keepdims=True)
                s = s * tr["ld"][p, hrows, tc] + sa * tr["b"][p, hrows, tc] + vg[t][vi:vi + 1, :] * tr["k"][p, hrows, tc]
                yrow = jnp.sum(s * tr["r"][p, hrows, tc], axis=0, keepdims=True)
                ys[t] = jnp.where(sub == vi, yrow, ys[t])
            sout_ref[0, v_idx] = s
        for t in range(nt):
            yt_s[p, pl.ds(r0, SUBLANES), t * nb:(t + 1) * nb] = ys[t]
        return carry

    lax.fori_loop(0, RW_HEAD // SUBLANES, group, 0)

    @pl.when(h == nhead - 1)
    def _():
        for q in range(npair):
            cs = slice(q * LANES, (q + 1) * LANES)
            y = yt_s[q].T
            mean = _segsum(y, mlo) * (1.0 / RW_HEAD)
            yc = y - mean
            var = _segsum(yc * yc, mlo) * (1.0 / RW_HEAD)
            yn = yc * lax.rsqrt(var + RW_GN_EPS) * gw_ref[:, cs] + gb_ref[:, cs] + bon_s[:, cs]
            yg_s[:, cs] = _b(yn * g_s[:, cs])
        o_ref[...] = x_ref[...] + _dot(yg_s[...], wo_ref[...])


def _rwkv_decode_call(xt, shift0, s0t, wts, *, nb, nt):
    n, d = xt.shape
    npair = d // LANES
    nhead = 2 * npair
    kern = functools.partial(_rwkv_decode_kernel, nb=nb, nt=nt)
    full = lambda shape: pl.BlockSpec(shape, lambda h: (0,) * len(shape))
    st_spec = pl.BlockSpec((1, RW_HEAD, RW_HEAD, nb), lambda h: (h, 0, 0, 0))
    in_specs = [full((n, d)), full((nb, d))] + [_const_spec(wt.shape, single_buffer=True) for wt in wts] + [st_spec]
    scratch = ([pltpu.VMEM((npair, LANES, n), F32) for _ in range(7)]
               + [pltpu.VMEM((n, d), F32), pltpu.VMEM((n, d), F32), pltpu.VMEM((n, d), BF16)])
    return pl.pallas_call(
        kern,
        grid=(nhead,),
        in_specs=in_specs,
        out_specs=[full((n, d)), st_spec, full((nb, d))],
        out_shape=[jax.ShapeDtypeStruct((n, d), F32), jax.ShapeDtypeStruct(s0t.shape, F32),
                   jax.ShapeDtypeStruct((nb, d), F32)],
        scratch_shapes=scratch,
        compiler_params=pltpu.CompilerParams(dimension_semantics=("arbitrary",), vmem_limit_bytes=VMEM_LIMIT),
        name="rwkv_decode",
    )(xt, shift0, *wts, s0t)


def _mlstm_kernel(*refs, tm, seq_rows, npad, sample, L, lsub):
    nsub = L // lsub
    nchunk = tm // L
    it = iter(refs)
    x_ref = next(it)
    if sample:
        cvx_ref, mrow_ref = next(it), next(it)
    gn_ref, win_ref, bif_ref, cw_ref, cb_ref, nw_ref, wout_ref = [next(it) for _ in range(7)]
    if sample:
        c0_ref, n0_ref = next(it), next(it)
    o_ref, cout_ref, nout_ref, mout_ref, cvout_ref = [next(it) for _ in range(5)]
    gt_s, q_s, k_s, v_s, og_s, gc_s, ho_s, ext_s = [next(it) for _ in range(8)]
    if not sample:
        cp_s, np_s, m_s = [next(it) for _ in range(3)]
    nqk, nv, d = cw_ref.shape[1], nw_ref.shape[1], x_ref.shape[1]
    half = nqk // 2
    nheads = half // ML_DK
    ngate = 2 * nheads
    npair = nheads // 2

    lane = lax.broadcasted_iota(jnp.int32, (1, LANES), 1)
    mlo = lane < ML_DK
    glane = lax.broadcasted_iota(jnp.int32, (1, ngate), 1)
    isf_c = glane >= nheads
    grow = lax.broadcasted_iota(jnp.int32, (ngate, 1), 0)
    isf_r = grow >= nheads
    hlane = lax.broadcasted_iota(jnp.int32, (1, nheads), 1)

    x = x_ref[...]
    xb = _b(_rms(x, gn_ref[...]))
    raw = _dot(xb, win_ref[:, :nqk])
    row = lax.broadcasted_iota(jnp.int32, (tm, 1), 0)
    if sample:
        srow = row & (seq_rows - 1)
        raw = jnp.where((srow >= npad - (ML_CONV - 1)) & (srow < npad), cvx_ref[...], raw)
        cvout_ref[...] = raw
        ext_s[:SUBLANES, :] = jnp.zeros((SUBLANES, nqk), F32)
    else:
        j = pl.program_id(1)

        @pl.when(j == 0)
        def _():
            ext_s[:SUBLANES, :] = jnp.zeros((SUBLANES, nqk), F32)
            cp_s[...] = jnp.zeros(cp_s.shape, F32)
            np_s[...] = jnp.zeros(np_s.shape, F32)
            m_s[...] = jnp.zeros(m_s.shape, F32)

        cvout_ref[...] = raw[tm - SUBLANES:tm, :]
    ext_s[SUBLANES:, :] = raw
    qk = cb_ref[...] + cw_ref[ML_CONV - 1:ML_CONV, :] * raw
    for s in range(1, ML_CONV):
        qk = qk + cw_ref[ML_CONV - 1 - s:ML_CONV - s, :] * ext_s[SUBLANES - s:SUBLANES - s + tm, :]
    if not sample:
        ext_s[:SUBLANES, :] = raw[tm - SUBLANES:tm, :]
    qk = qk * _sigmoid(qk)
    q_s[...] = qk[:, :half] * (ML_DK ** -0.5)
    k_s[...] = qk[:, half:]
    v_s[...] = _dot(xb, win_ref[:, nqk:nqk + nv])
    og_s[...] = _sigmoid(_dot(xb, win_ref[:, nqk + nv:nqk + nv + d]))
    ifp = _dot(xb, win_ref[:, nqk + nv + d:]) + bif_ref[...]
    gcol = jnp.where(isf_c, -_softplus(-ifp), ifp)
    if npad:
        keep = (row & (seq_rows - 1)) >= npad
        gcol = jnp.where(keep, gcol, jnp.where(isf_c, 0.0, NEG_BIG))
    gc_s[...] = gcol
    gct = gcol.T
    for c in range(nchunk):
        gt_s[c] = gct[:, c * L:(c + 1) * L]

    ti = lax.broadcasted_iota(jnp.int32, (L, L), 0)
    si = lax.broadcasted_iota(jnp.int32, (L, L), 1)
    sh = _log2(lsub)
    same = (ti >> sh) == (si >> sh)
    causal = same & (si <= ti)
    tril = jnp.where(causal, 1.0, 0.0).astype(F32)
    triu = jnp.where(same & (ti <= si), 1.0, 0.0).astype(F32)

    def chunk(c, carry):
        r0 = pl.multiple_of(c * L, L)
        rows = pl.ds(r0, L)
        gc = gc_s[rows, :]
        gt = gt_s[c]
        bcs = _dot(tril, jnp.where(isf_c, gc, 0.0), HI)
        brs = _dot(jnp.where(isf_r, gt, 0.0), triu, HI)
        blast = _seq_last(bcs, lsub)
        if sample:
            mcols = mrow_ref[rows, :]
        else:
            mcols = m_s[...]
        heads = range(nheads)
        prs = range(npair)
        ps = [slice(pp * LANES, (pp + 1) * LANES) for pp in prs]
        hs = [slice(h * ML_DV, (h + 1) * ML_DV) for h in heads]
        q2 = [q_s[rows, ps[pp]] for pp in prs]
        k2 = [k_s[rows, ps[pp]] for pp in prs]
        k2b = [_b(z) for z in k2]
        if sample:
            c_in = [[jnp.concatenate([c0_ref[c * nsub + q, 2 * pp], c0_ref[c * nsub + q, 2 * pp + 1]], axis=0)
                     for q in range(nsub)] for pp in prs]
            n_in = [[n0_ref[c * nsub + q][:, ps[pp]] for q in range(nsub)] for pp in prs]
            n_rows = [jnp.concatenate([jnp.broadcast_to(n_in[pp][q], (lsub, LANES)) for q in range(nsub)], axis=0)
                      for pp in prs]
        else:
            c_prev = [cp_s[pp] for pp in prs]
            n_prev = [np_s[:, ps[pp]] for pp in prs]
            n_rows = n_prev
        bcol = [_colsel(bcs, glane, nheads + h) for h in heads]
        licol = [_colsel(gc, glane, h) for h in heads]
        mcol = [_colsel(mcols, hlane, h) for h in heads]
        blcol = [_colsel(blast, glane, nheads + h) for h in heads]
        dlog = [jnp.where(causal, bcol[h] - (brs[nheads + h:nheads + h + 1, :] - gt[h:h + 1, :]), -jnp.inf)
                for h in heads]
        ginter = [bcol[h] + mcol[h] for h in heads]
        m_t = [jnp.maximum(ginter[h], jnp.max(dlog[h], axis=-1, keepdims=True)) for h in heads]
        dw = [jnp.exp(dlog[h] - m_t[h]) for h in heads]
        winter = [jnp.exp(ginter[h] - m_t[h]) for h in heads]
        qh = [jnp.where(mlo if h % 2 == 0 else jnp.logical_not(mlo), q2[h // 2], 0.0) for h in heads]
        qhb = [_b(z) for z in qh]
        sc = [_dot_nt(qhb[h], k2b[h // 2]) * dw[h] for h in heads]
        if sample:
            inter = [jnp.concatenate([_dot(qhb[h][q * lsub:(q + 1) * lsub], _b(c_in[h // 2][q]))
                                      for q in range(nsub)], axis=0) for h in heads]
        else:
            cb = [_b(z) for z in c_prev]
            inter = [_dot(qhb[h], cb[h // 2]) for h in heads]
        vh = [v_s[rows, hs[h]] for h in heads]
        num = [winter[h] * inter[h] + _dot(_b(sc[h]), _b(vh[h])) for h in heads]
        den = [winter[h] * jnp.sum(qh[h] * n_rows[h // 2], axis=-1, keepdims=True)
               + jnp.sum(sc[h], axis=-1, keepdims=True) for h in heads]
        for h in heads:
            hout = num[h] / jnp.maximum(jnp.abs(den[h]), jnp.exp(-m_t[h]))
            hn = hout * lax.rsqrt(jnp.mean(hout * hout, axis=-1, keepdims=True) + NORM_EPS) * nw_ref[:, hs[h]]
            ho_s[rows, hs[h]] = _b(hn * og_s[rows, hs[h]])
        mnew = [_seq_last(m_t[h], lsub) for h in heads]
        ws = [jnp.exp(blcol[h] - bcol[h] + licol[h] - mnew[h]) for h in heads]
        wstc = [jnp.exp(blcol[h] + mcol[h] - mnew[h]) for h in heads]
        mt_all = jnp.zeros((L, nheads), F32)
        for h in heads:
            mt_all = jnp.where(hlane == h, m_t[h], mt_all)
        for pp in prs:
            lo, hi = 2 * pp, 2 * pp + 1
            wsk = jnp.where(mlo, ws[lo], ws[hi]) * k2[pp]
            wst = jnp.where(mlo, wstc[lo], wstc[hi])
            wvb = _b(jnp.concatenate([ws[lo] * vh[lo], ws[hi] * vh[hi]], axis=0))
            kmb = jnp.concatenate([jnp.where(mlo, k2b[pp], 0), jnp.where(mlo, 0, k2b[pp])], axis=0)
            if sample:
                for q in range(nsub):
                    last = (q + 1) * lsub - 1
                    sel = [slice(blk * L + q * lsub, blk * L + (q + 1) * lsub) for blk in range(2)]
                    wq = wst[last:last + 1, :]
                    wrow = jnp.concatenate([jnp.broadcast_to(wstc[h_][last:last + 1, :], (ML_DK, 1)) for h_ in (lo, hi)],
                                           axis=0)
                    upd = _dot_tn(jnp.concatenate([kmb[s_] for s_ in sel], axis=0),
                                  jnp.concatenate([wvb[s_] for s_ in sel], axis=0))
                    c_new = c_in[pp][q] * wrow + upd
                    cout_ref[c * nsub + q, 2 * pp] = c_new[:ML_DK]
                    cout_ref[c * nsub + q, 2 * pp + 1] = c_new[ML_DK:]
                    nq = n_in[pp][q] * wq + jnp.sum(wsk[q * lsub:(q + 1) * lsub], axis=0, keepdims=True)
                    nout_ref[c * nsub + q, :, ps[pp]] = nq
            else:
                wrow = jnp.concatenate([jnp.broadcast_to(wstc[h_], (ML_DK, 1)) for h_ in (lo, hi)], axis=0)
                cp_s[pp] = c_prev[pp] * wrow + _dot_tn(kmb, wvb)
                np_s[:, ps[pp]] = n_prev[pp] * wst + jnp.sum(wsk, axis=0, keepdims=True)
        if sample:
            mout_ref[rows, :] = mt_all
        else:
            m_s[...] = mt_all[L - 1:L, :]
        return carry

    lax.fori_loop(0, nchunk, chunk, 0)
    if not sample:
        for pp in range(npair):
            cout_ref[0, 2 * pp] = cp_s[pp][:ML_DK]
            cout_ref[0, 2 * pp + 1] = cp_s[pp][ML_DK:]
        nout_ref[0] = np_s[...]
        mout_ref[0] = m_s[...]
    o_ref[...] = x_ref[...] + _dot(ho_s[...], wout_ref[...])


def _mlstm_call(x, cvx, mrow, c0p, n0, wts, *, nseq, seq_rows, npad, sample, tm):
    n, d = x.shape
    nqk = wts[3].shape[1]
    nv = wts[5].shape[1]
    assert nqk % LANES == 0 and nv % LANES == 0 and d % LANES == 0
    nheads = nqk // 2 // ML_DK
    npair = nheads // 2
    chunk = CHUNK if sample else ML_CHUNK
    lsub = seq_rows if sample else chunk
    kern = functools.partial(_mlstm_kernel, tm=tm, seq_rows=seq_rows, npad=npad, sample=sample, L=chunk, lsub=lsub)
    if sample:
        grid = (n // tm,)
        tile = lambda i: (i, 0)
        spt = tm // seq_rows
        c_spec = pl.BlockSpec((spt, nheads, ML_DK, ML_DV), lambda i: (i, 0, 0, 0))
        n_spec = pl.BlockSpec((spt, 1, nqk // 2), lambda i: (i, 0, 0))
        m_spec = pl.BlockSpec((tm, nheads), tile)
        m_shape = (n, nheads)
        cv_spec = pl.BlockSpec((tm, nqk), tile)
        cv_shape = (n, nqk)
        sem = ("arbitrary",)
    else:
        tps = seq_rows // tm
        grid = (nseq, tps)
        tile = lambda b, j: (b * tps + j, 0)
        c_spec = pl.BlockSpec((1, nheads, ML_DK, ML_DV), lambda b, j: (b, 0, 0, 0))
        n_spec = pl.BlockSpec((1, 1, nqk // 2), lambda b, j: (b, 0, 0))
        m_spec = pl.BlockSpec((1, 1, nheads), lambda b, j: (b, 0, 0))
        m_shape = (nseq, 1, nheads)
        cv_spec = pl.BlockSpec((SUBLANES, nqk), lambda b, j: (b, 0))
        cv_shape = (nseq * SUBLANES, nqk)
        sem = ("arbitrary", "arbitrary")
    x_spec = pl.BlockSpec((tm, d), tile)
    in_specs = [x_spec]
    args = [x]
    if sample:
        in_specs += [pl.BlockSpec((tm, nqk), tile), pl.BlockSpec((tm, nheads), tile)]
        args += [cvx, mrow]
    for wt in wts:
        in_specs.append(_const_spec(wt.shape, single_buffer=True))
        args.append(wt)
    if sample:
        in_specs += [c_spec, n_spec]
        args += [c0p, n0]
    scratch = [pltpu.VMEM((tm // chunk, 2 * nheads, chunk), F32), pltpu.VMEM((tm, nqk // 2), F32),
               pltpu.VMEM((tm, nqk // 2), F32),
               pltpu.VMEM((tm, nv), F32), pltpu.VMEM((tm, nv), F32), pltpu.VMEM((tm, 2 * nheads), F32),
               pltpu.VMEM((tm, nv), BF16), pltpu.VMEM((tm + SUBLANES, nqk), F32)]
    if not sample:
        scratch += [pltpu.VMEM((npair, LANES, LANES), F32),
                    pltpu.VMEM((1, nqk // 2), F32), pltpu.VMEM((1, nheads), F32)]
    return pl.pallas_call(
        kern,
        grid=grid,
        in_specs=in_specs,
        out_specs=[x_spec, c_spec, n_spec, m_spec, cv_spec],
        out_shape=[jax.ShapeDtypeStruct((n, d), F32),
                   jax.ShapeDtypeStruct((nseq, nheads, ML_DK, ML_DV), F32),
                   jax.ShapeDtypeStruct((nseq, 1, nqk // 2), F32),
                   jax.ShapeDtypeStruct(m_shape, F32),
                   jax.ShapeDtypeStruct(cv_shape, F32)],
        scratch_shapes=scratch,
        compiler_params=pltpu.CompilerParams(dimension_semantics=sem, vmem_limit_bytes=VMEM_LIMIT),
        name="mlstm_sample" if sample else "mlstm_prompt",
    )(*args)


def kernel(x_prompt, x_sample, state_rwkv_S, state_rwkv_shift, state_mlstm_C, state_mlstm_n, state_mlstm_m,
           state_mlstm_conv, norm_ffa, ffa_wg, ffa_wu, ffa_wd, norm_mix, norm_ffb, ffb_wg, ffb_wu, ffb_wd,
           rw_mu, rw_wr, rw_wk, rw_wv, rw_wo, rw_w0, rw_w1, rw_w2, rw_a0, rw_a1, rw_a2, rw_g1, rw_g2,
           rw_k_k, rw_k_a, rw_r_k, rw_gn_w, rw_gn_b, ml_w_in, ml_b_if, ml_conv_w, ml_conv_b, ml_norm_w,
           ml_w_out, norm_final):
    bp, tp, d = x_prompt.shape
    bs, ts, _ = x_sample.shape
    depth = norm_ffa.shape[0]
    slot = SUBLANES
    npad = slot - ts
    assert 0 < ts <= slot and npad >= ML_CONV - 1
    tm_p = min(512, tp)
    tm_ml = min(1024, tp)
    tm_s = min(4 * CHUNK, bs * slot)
    assert tp % tm_p == 0 and tm_p % CHUNK == 0 and tp % tm_ml == 0 and tm_ml % ML_CHUNK == 0
    assert (bs * slot) % tm_s == 0 and tm_s % CHUNK == 0
    ml_heads = ml_b_if.shape[1] // 2
    nqk = 2 * ml_heads * ML_DK
    nv = ml_heads * ML_DV

    xp = x_prompt.reshape(bp * tp, d)
    xs = x_sample.transpose(1, 0, 2).reshape(ts * bs, d)

    def to_slots(z):
        return jnp.concatenate([jnp.zeros((bs, npad, d), F32), z.reshape(ts, bs, d).transpose(1, 0, 2)],
                               axis=1).reshape(bs * slot, d)

    def from_slots(z):
        return z.reshape(bs, slot, d)[:, npad:].transpose(1, 0, 2).reshape(ts * bs, d)

    row2 = lambda a: a.reshape(1, -1)

    tm_ffn = min(512, bs * ts)
    assert (bp * tp) % tm_ffn == 0 and (bs * ts) % tm_ffn == 0
    ffa = (norm_ffa.reshape(depth, 1, d), _b(ffa_wg), _b(ffa_wu), _b(ffa_wd))
    ffb = (norm_ffb.reshape(depth, 1, d), _b(ffb_wg), _b(ffb_wu), _b(ffb_wd))

    def ffn(xp_, xs_, wset, layer, final):
        return _ffn_call(xp_, xs_, *wset, row2(norm_final), layer, final_norm=final, tm=tm_ffn)

    new_p = {k_: [] for k_ in ("S", "shift", "C", "n", "m", "conv")}
    new_s = {k_: [] for k_ in ("S", "shift", "C", "n", "m", "conv")}
    for i in range(depth):
        xp, xs = ffn(xp, xs, ffa, i, False)
        j = i // 2
        if i % 2 == 0:
            wts = [row2(norm_mix[i]), rw_mu[j], _b(rw_wr[j]), _b(rw_wk[j]), _b(rw_wv[j]), _b(rw_wo[j]),
                   row2(rw_w0[j]), _b(rw_w1[j]), _b(rw_w2[j]), row2(rw_a0[j]), _b(rw_a1[j]), _b(rw_a2[j]),
                   _b(rw_g1[j]), _b(rw_g2[j]), row2(rw_k_k[j]), row2(rw_k_a[j]), row2(rw_r_k[j]),
                   row2(rw_gn_w[j]), row2(rw_gn_b[j])]
            xp, sbd, tail = _rwkv_call(xp, wts, nseq=bp, seq_rows=tp, tm=tm_p)
            new_p["S"].append(sbd)
            new_p["shift"].append(tail.reshape(bp, SUBLANES, d)[:, SUBLANES - 1])
            xs, s_t, sh_new = _rwkv_decode_call(xs, state_rwkv_shift[j], jnp.transpose(state_rwkv_S[j], (1, 2, 3, 0)),
                                                wts, nb=bs, nt=ts)
            new_s["S"].append(jnp.transpose(s_t, (3, 0, 1, 2)))
            new_s["shift"].append(sh_new)
        else:
            wts = [row2(norm_mix[i]), _b(ml_w_in[j]), row2(ml_b_if[j]), ml_conv_w[j],
                   row2(ml_conv_b[j]), row2(ml_norm_w[j]), _b(ml_w_out[j])]
            xp, cp, n_, m_, tail = _mlstm_call(xp, None, None, None, None, wts, nseq=bp, seq_rows=tp, npad=0,
                                               sample=False, tm=tm_ml)
            new_p["C"].append(jnp.swapaxes(cp, -1, -2))
            new_p["n"].append(n_.reshape(bp, ml_heads, ML_DK))
            new_p["m"].append(m_.reshape(bp, ml_heads))
            new_p["conv"].append(tail.reshape(bp, SUBLANES, nqk)[:, SUBLANES - (ML_CONV - 1):])
            conv0 = state_mlstm_conv[j]
            cvx = jnp.concatenate([jnp.zeros((bs, npad - (ML_CONV - 1), nqk), F32), conv0,
                                   jnp.zeros((bs, slot - npad, nqk), F32)], axis=1).reshape(bs * slot, nqk)
            mrow = jnp.repeat(state_mlstm_m[j], slot, axis=0)
            xs, cp, n_, mt, raw = _mlstm_call(to_slots(xs), cvx, mrow, jnp.swapaxes(state_mlstm_C[j], -1, -2),
                                              state_mlstm_n[j].reshape(bs, 1, ml_heads * ML_DK), wts, nseq=bs,
                                              seq_rows=slot, npad=npad, sample=True, tm=tm_s)
            xs = from_slots(xs)
            new_s["C"].append(jnp.swapaxes(cp, -1, -2))
            new_s["n"].append(n_.reshape(bs, ml_heads, ML_DK))
            new_s["m"].append(mt.reshape(bs, slot, ml_heads)[:, slot - 1])
            new_s["conv"].append(raw.reshape(bs, slot, nqk)[:, slot - (ML_CONV - 1):])
        xp, xs = ffn(xp, xs, ffb, i, i == depth - 1)
    y_prompt = xp.reshape(bp, tp, d)
    y_sample = xs.reshape(ts, bs, d).transpose(1, 0, 2)
    st = lambda lst: jnp.stack(lst)
    return (y_prompt, y_sample,
            st(new_p["S"]), st(new_p["shift"]), st(new_p["C"]), st(new_p["n"]), st(new_p["m"]), st(new_p["conv"]),
            st(new_s["S"]), st(new_s["shift"]), st(new_s["C"]), st(new_s["n"]), st(new_s["m"]), st(new_s["conv"]))
```

```python
import functools
import math

import jax
import jax.numpy as jnp
from jax import lax
from jax.experimental import pallas as pl
from jax.experimental.pallas import tpu as pltpu

F32 = jnp.float32
BF16 = jnp.bfloat16

NORM_EPS = 1e-6
RW_GN_EPS = 64e-5
RW_HEAD = 64
ML_DK = 64
ML_DV = 128
ML_CONV = 4

LANES = 128
SUBLANES = 8
MXU_COLS = 256
CHUNK = 64
ML_CHUNK = 128
NEG_BIG = -1e30
VMEM_LIMIT = 58 * 1024 * 1024
HI = lax.Precision.HIGHEST


def _dot(a, b, precision=None):
    return jnp.dot(a, b, preferred_element_type=F32, precision=precision)


def _dot_nt(a, b, precision=None):
    return lax.dot_general(a, b, (((1,), (1,)), ((), ())), preferred_element_type=F32, precision=precision)


def _dot_tn(a, b, precision=None):
    return lax.dot_general(a, b, (((0,), (0,)), ((), ())), preferred_element_type=F32, precision=precision)


def _b(x):
    return x.astype(BF16)


def _rms(x, g):
    return x * lax.rsqrt(jnp.mean(x * x, axis=-1, keepdims=True) + NORM_EPS) * g


def _sigmoid(x):
    return 1.0 / (1.0 + jnp.exp(-x))


def _softplus(x):
    return jnp.maximum(x, 0.0) + jnp.log1p(jnp.exp(-jnp.abs(x)))


def _segsum(x, mlo):
    lo = jnp.sum(jnp.where(mlo, x, 0.0), axis=-1, keepdims=True)
    hi = jnp.sum(jnp.where(mlo, 0.0, x), axis=-1, keepdims=True)
    return jnp.where(mlo, lo, hi)


def _colsel(x, lane_idx, j):
    return jnp.sum(jnp.where(lane_idx == j, x, 0.0), axis=-1, keepdims=True)


def _seq_last(x, lsub):
    n = x.shape[0]
    if lsub == n:
        return x[n - 1:n]
    parts = [jnp.broadcast_to(x[q * lsub + lsub - 1:q * lsub + lsub], (lsub,) + x.shape[1:])
             for q in range(n // lsub)]
    return jnp.concatenate(parts, axis=0)


def _const_spec(shape, single_buffer=False):
    nd = len(shape)
    if single_buffer:
        return pl.BlockSpec(shape, lambda *_: (0,) * nd, pipeline_mode=pl.Buffered(1))
    return pl.BlockSpec(shape, lambda *_: (0,) * nd)


def _log2(n):
    k = int(math.log2(n))
    assert 1 << k == n, n
    return k


def _ffn_kernel(xp_ref, xs_ref, g_ref, wg_ref, wu_ref, wd_ref, gf_ref, op_ref, os_ref, *, fchunk, final_norm, np_steps):
    def tile(x_ref, o_ref):
        x = x_ref[...]
        xb = _b(_rms(x, g_ref[...]))
        acc = jnp.zeros(x.shape, F32)
        for lo, hi in zip(fchunk[:-1], fchunk[1:]):
            sl = slice(lo, hi)
            gate = _dot(xb, wg_ref[:, sl])
            up = _dot(xb, wu_ref[:, sl])
            h = _b(gate * _sigmoid(gate) * up)
            acc = acc + _dot(h, wd_ref[sl, :])
        out = x + 0.5 * acc
        if final_norm:
            out = _rms(out, gf_ref[...])
        o_ref[...] = out

    is_p = pl.program_id(0) < np_steps
    pl.when(is_p)(functools.partial(tile, xp_ref, op_ref))
    pl.when(jnp.logical_not(is_p))(functools.partial(tile, xs_ref, os_ref))


def _ffn_call(xp, xs, g_all, wg_all, wu_all, wd_all, gf, layer, *, final_norm, tm):
    (n_p, d), n_s = xp.shape, xs.shape[0]
    nf = wg_all.shape[2]
    np_steps, ns_steps = n_p // tm, n_s // tm
    ntile = nf // MXU_COLS
    fchunk = (0, -(-ntile // 2) * MXU_COLS, nf) if nf % MXU_COLS == 0 and ntile > 1 else (0, nf)
    kern = functools.partial(_ffn_kernel, fchunk=fchunk, final_norm=final_norm, np_steps=np_steps)
    p_spec = pl.BlockSpec((tm, d), lambda i: (jnp.minimum(i, np_steps - 1), 0))
    s_spec = pl.BlockSpec((tm, d), lambda i: (jnp.maximum(i - np_steps, 0), 0))
    w_spec = lambda a, b: pl.BlockSpec((None, a, b), lambda i: (layer, 0, 0), pipeline_mode=pl.Buffered(1))
    return pl.pallas_call(
        kern,
        grid=(np_steps + ns_steps,),
        in_specs=[p_spec, s_spec, pl.BlockSpec((None, 1, d), lambda i: (layer, 0, 0)),
                  w_spec(d, nf), w_spec(d, nf), w_spec(nf, d), _const_spec((1, d))],
        out_specs=[p_spec, s_spec],
        out_shape=[jax.ShapeDtypeStruct((n_p, d), F32), jax.ShapeDtypeStruct((n_s, d), F32)],
        compiler_params=pltpu.CompilerParams(dimension_semantics=("arbitrary",), vmem_limit_bytes=VMEM_LIMIT),
        name="ffn",
    )(xp, xs, g_all, wg_all, wu_all, wd_all, gf)


def _rw_vector_rows(vec_ref):
    return [vec_ref.at[pl.ds(i, 1), :] for i in range(vec_ref.shape[0])]


def _rw_token_part(xn, xx, mlo, put, wrefs):
    mu_ref, wr_ref, wk_ref, wv_ref, w0_ref, w1_ref, w2_ref, a0_ref, a1_ref, a2_ref, g1_ref, g2_ref, kk_ref, ka_ref, rk_ref = wrefs
    cols = [slice(p * LANES, (p + 1) * LANES) for p in range(xn.shape[1] // LANES)]

    def put_all(name, z):
        for p, cs in enumerate(cols):
            put(name, p, z[:, cs])

    def mix(i):
        return _b(xn + xx * mu_ref[i:i + 1, :])

    r = _dot(mix(0), wr_ref[...])
    put_all("r", r)
    wl = _b(jnp.tanh(_dot(mix(1), w1_ref[...])))
    ld = -math.exp(-0.5) * _sigmoid(w0_ref[...] + _dot(wl, w2_ref[...]))
    k = _dot(mix(2), wk_ref[...])
    v = _dot(mix(3), wv_ref[...])
    al = _b(_dot(mix(4), a1_ref[...]))
    a = _sigmoid(a0_ref[...] + _dot(al, a2_ref[...]))
    gl = _b(_sigmoid(_dot(mix(5), g1_ref[...])))
    put_all("g", _dot(gl, g2_ref[...]))
    k2 = k * (1.0 + (a - 1.0) * ka_ref[...])
    kk = k * kk_ref[...]
    rkk = r * k2 * rk_ref[...]
    put_all("ld", ld)
    put_all("k", k2)
    put_all("v", v)
    for p, cs in enumerate(cols):
        kkp = kk[:, cs]
        kn = kkp / jnp.maximum(jnp.sqrt(_segsum(kkp * kkp, mlo)), 1e-12)
        put("kn", p, kn)
        put("b", p, kn * a[:, cs])
        put("bon", p, _segsum(rkk[:, cs], mlo) * v[:, cs])


def _rwkv_kernel(*refs, tm):
    L = CHUNK
    nchunk = tm // L
    nsteps = _log2(L)
    it = iter(refs)
    x_ref = next(it)
    gn_ref, w0_ref, a0_ref, kk_ref, ka_ref, rk_ref, gw_ref, gb_ref = _rw_vector_rows(next(it))
    mu_ref, wr_ref, wk_ref, wv_ref, wo_ref, w1_ref, w2_ref, a1_ref, a2_ref, g1_ref, g2_ref = [next(it) for _ in range(11)]
    o_ref, sout_ref, xn_ref = next(it), next(it), next(it)
    r_s, ld_s, k_s, v_s, kn_s, b_s, g_s, bon_s = sset = [next(it) for _ in range(8)]
    yg_s, carry_s, sbd_s = next(it), next(it), next(it)
    d = x_ref.shape[1]
    npair = d // LANES

    lane = lax.broadcasted_iota(jnp.int32, (1, LANES), 1)
    mlo = lane < RW_HEAD
    row = lax.broadcasted_iota(jnp.int32, (tm, 1), 0)

    @pl.when(pl.program_id(1) == 0)
    def _():
        carry_s[...] = jnp.zeros(carry_s.shape, F32)
        sbd_s[...] = jnp.zeros(sbd_s.shape, F32)

    xn = _rms(x_ref[...], gn_ref[...])
    xprev = jnp.where(row == 0, carry_s[SUBLANES - 1:SUBLANES, :], pltpu.roll(xn, 1, 0))
    carry_s[...] = xn[tm - SUBLANES:tm, :]
    xn_ref[...] = xn[tm - SUBLANES:tm, :]
    sref = dict(zip(("r", "ld", "k", "v", "kn", "b", "g", "bon"), sset))

    def put(name, p, tile):
        sref[name][:, p * LANES:(p + 1) * LANES] = tile

    _rw_token_part(xn, xprev - xn, mlo, put,
                   (mu_ref, wr_ref, wk_ref, wv_ref, w0_ref, w1_ref, w2_ref, a0_ref, a1_ref, a2_ref,
                    g1_ref, g2_ref, kk_ref, ka_ref, rk_ref))

    ti = lax.broadcasted_iota(jnp.int32, (L, 3 * L), 0)
    si = lax.broadcasted_iota(jnp.int32, (L, 3 * L), 1) & (L - 1)
    tril3 = jnp.where(si <= ti, 1.0, 0.0).astype(BF16)
    gi = lax.broadcasted_iota(jnp.int32, (2 * L, 4 * L), 0)
    gj = lax.broadcasted_iota(jnp.int32, (2 * L, 4 * L), 1)
    gt = gi & (L - 1)
    gs = gj & (L - 1)
    gmask = (gs < gt) | ((gi >= L) & (gs == gt))
    pairs = range(npair)

    def stack2(z):
        return jnp.concatenate([jnp.where(mlo, z, 0), jnp.where(mlo, 0, z)], axis=0)

    cs = [slice(p * LANES, (p + 1) * LANES) for p in pairs]
    rw = {(c, p): slice(c * L, (c + 1) * L) for c in range(nchunk) for p in pairs}
    chains = [(c, p) for c in range(nchunk) for p in pairs]

    cum, a2, q4, vv2 = {}, {}, {}, {}
    for key in chains:
        ldc = ld_s[rw[key], cs[key[1]]]
        hi = _b(ldc)
        r1 = ldc - hi.astype(F32)
        mid = _b(r1)
        lo = _b(r1 - mid.astype(F32))
        cum[key] = _dot(tril3, jnp.concatenate([hi, mid, lo], axis=0))
    for key in chains:
        rows, c_ = rw[key], cs[key[1]]
        ep = jnp.exp(cum[key])
        em = jnp.exp(-cum[key])
        at = -(kn_s[rows, c_] * jnp.exp(cum[key] - ld_s[rows, c_]))
        a2[key] = _b(jnp.concatenate([at, r_s[rows, c_] * ep], axis=0))
        q4[key] = jnp.concatenate([stack2(b_s[rows, c_] * em), stack2(k_s[rows, c_] * em)], axis=0)
        vv2[key] = _b(stack2(v_s[rows, c_]))
    g = {key: jnp.where(gmask, _dot_nt(a2[key], _b(q4[key])), 0.0) for key in chains}
    makv = {key: _dot(_b(g[key][:, LANES:]), vv2[key]) for key in chains}
    mr = {key: _b(g[key][L:, :LANES]) for key in chains}
    w = {key: g[key][:L, :LANES] for key in chains}
    pf = {}
    for key in chains:
        pc = _b(w[key])
        pf[key] = _dot(pc, stack2(pc))
    for k in range(1, nsteps):
        for key in chains:
            pc = _b(pf[key])
            if k + 1 < nsteps:
                both = _dot(pc, jnp.concatenate([stack2(pc), stack2(_b(w[key]))], axis=1))
                w[key] = w[key] + pf[key] + both[:, LANES:]
                pf[key] = both[:, :LANES]
            else:
                w[key] = w[key] + pf[key] + _dot(pc, stack2(_b(w[key])))
    wcat = {key: _b(w[key]) for key in chains}
    pl_ = {key: jnp.exp(cum[key][L - 1:L, :]) for key in chains}
    qpl = {key: _b(q4[key] * pl_[key]) for key in chains}

    state = [sbd_s[p] for p in pairs]
    for c in range(nchunk):
        as_ = [_dot_nt(a2[c, p], _b(state[p])) for p in pairs]
        rhs = [as_[p][:L] + makv[c, p][:L] for p in pairs]
        u = [rhs[p] + _dot(wcat[c, p], stack2(_b(rhs[p]))) for p in pairs]
        u2 = [stack2(_b(u[p])) for p in pairs]
        uv = [jnp.concatenate([u2[p], vv2[c, p]], axis=0) for p in pairs]
        y = [as_[p][L:] + makv[c, p][L:] + _dot(mr[c, p], u2[p]) for p in pairs]
        for p in pairs:
            state[p] = state[p] * pl_[c, p] + _dot_tn(uv[p], qpl[c, p])
        rows = rw[c, 0]
        for p in pairs:
            mean = _segsum(y[p], mlo) * (1.0 / RW_HEAD)
            yc = y[p] - mean
            var = _segsum(yc * yc, mlo) * (1.0 / RW_HEAD)
            yn = yc * lax.rsqrt(var + RW_GN_EPS) * gw_ref[:, cs[p]] + gb_ref[:, cs[p]] + bon_s[rows, cs[p]]
            yg_s[rows, cs[p]] = _b(yn * g_s[rows, cs[p]])
    for p in pairs:
        sbd_s[p] = state[p]
        sout_ref[0, 2 * p] = state[p][:RW_HEAD, :RW_HEAD]
        sout_ref[0, 2 * p + 1] = state[p][RW_HEAD:, RW_HEAD:]
    o_ref[...] = x_ref[...] + _dot(yg_s[...], wo_ref[...])


def _rwkv_call(x, wts, *, nseq, seq_rows, tm):
    n, d = x.shape
    npair = d // LANES
    tps = seq_rows // tm
    kern = functools.partial(_rwkv_kernel, tm=tm)
    x_spec = pl.BlockSpec((tm, d), lambda b, j: (b * tps + j, 0))
    st_spec = pl.BlockSpec((1, 2 * npair, RW_HEAD, RW_HEAD), lambda b, j: (b, 0, 0, 0))
    xn_spec = pl.BlockSpec((SUBLANES, d), lambda b, j: (b, 0))
    scratch = ([pltpu.VMEM((tm, d), F32) for _ in range(8)] + [pltpu.VMEM((tm, d), BF16)]
               + [pltpu.VMEM((SUBLANES, d), F32), pltpu.VMEM((npair, LANES, LANES), F32)])
    return pl.pallas_call(
        kern,
        grid=(nseq, tps),
        in_specs=[x_spec] + [_const_spec(wt.shape, single_buffer=True) for wt in wts],
        out_specs=[x_spec, st_spec, xn_spec],
        out_shape=[jax.ShapeDtypeStruct((n, d), F32),
                   jax.ShapeDtypeStruct((nseq, 2 * npair, RW_HEAD, RW_HEAD), F32),
                   jax.ShapeDtypeStruct((nseq * SUBLANES, d), F32)],
        scratch_shapes=scratch,
        compiler_params=pltpu.CompilerParams(dimension_semantics=("arbitrary", "arbitrary"),
                                             vmem_limit_bytes=VMEM_LIMIT),
        name="rwkv_prompt",
    )(x, *wts)


def _rwkv_decode_kernel(*refs, nb, nt):
    it = iter(refs)
    x_ref, sh_ref = next(it), next(it)
    gn_ref, w0_ref, a0_ref, kk_ref, ka_ref, rk_ref, gw_ref, gb_ref = _rw_vector_rows(next(it))
    mu_ref, wr_ref, wk_ref, wv_ref, wo_ref, w1_ref, w2_ref, a1_ref, a2_ref, g1_ref, g2_ref = [next(it) for _ in range(11)]
    s0_ref = next(it)
    o_ref, sout_ref, xn_ref = next(it), next(it), next(it)
    tr = dict(zip(("r", "ld", "k", "v", "kn", "b"), [next(it) for _ in range(6)]))
    yt_s, g_s, bon_s, yg_s = [next(it) for _ in range(4)]
    n, d = x_ref.shape
    npair = d // LANES
    nhead = 2 * npair
    h = pl.program_id(0)
    lane = lax.broadcasted_iota(jnp.int32, (1, LANES), 1)
    mlo = lane < RW_HEAD

    @pl.when(h == 0)
    def _():
        xn = _rms(x_ref[...], gn_ref[...])
        xprev = jnp.concatenate([sh_ref[...], xn[:n - nb]], axis=0)
        xn_ref[...] = xn[n - nb:]

        def put(name, p, tile):
            if name == "g":
                g_s[:, p * LANES:(p + 1) * LANES] = tile
            elif name == "bon":
                bon_s[:, p * LANES:(p + 1) * LANES] = tile
            else:
                tr[name][p] = (jnp.exp(tile) if name == "ld" else tile).T

        _rw_token_part(xn, xprev - xn, mlo, put,
                       (mu_ref, wr_ref, wk_ref, wv_ref, w0_ref, w1_ref, w2_ref, a0_ref, a1_ref, a2_ref,
                        g1_ref, g2_ref, kk_ref, ka_ref, rk_ref))

    p = h >> 1
    base = pl.multiple_of((h & 1) * RW_HEAD, RW_HEAD)
    sub = lax.broadcasted_iota(jnp.int32, (SUBLANES, 1), 0)
    hrows = pl.ds(base, RW_HEAD)

    def group(gi, carry):
        r0 = pl.multiple_of(base + gi * SUBLANES, SUBLANES)
        vg = [tr["v"][p, pl.ds(r0, SUBLANES), t * nb:(t + 1) * nb] for t in range(nt)]
        ys = [jnp.zeros((SUBLANES, nb), F32) for _ in range(nt)]
        for vi in range(SUBLANES):
            v_idx = gi * SUBLANES + vi
            s = s0_ref[0, v_idx]
            for t in range(nt):
                tc = slice(t * nb, (t + 1) * nb)
                sa = -jnp.sum(s * tr["kn"][p, hrows, tc], axis=0, keepdims=True)
                s = s * tr["ld"][p, hrows, tc] + sa * tr["b"][p, hrows, tc] + vg[t][vi:vi + 1, :] * tr["k"][p, hrows, tc]
                yrow = jnp.sum(s * tr["r"][p, hrows, tc], axis=0, keepdims=True)
                ys[t] = jnp.where(sub == vi, yrow, ys[t])
            sout_ref[0, v_idx] = s
        for t in range(nt):
            yt_s[p, pl.ds(r0, SUBLANES), t * nb:(t + 1) * nb] = ys[t]
        return carry

    lax.fori_loop(0, RW_HEAD // SUBLANES, group, 0)

    @pl.when(h == nhead - 1)
    def _():
        for q in range(npair):
            cs = slice(q * LANES, (q + 1) * LANES)
            y = yt_s[q].T
            mean = _segsum(y, mlo) * (1.0 / RW_HEAD)
            yc = y - mean
            var = _segsum(yc * yc, mlo) * (1.0 / RW_HEAD)
            yn = yc * lax.rsqrt(var + RW_GN_EPS) * gw_ref[:, cs] + gb_ref[:, cs] + bon_s[:, cs]
            yg_s[:, cs] = _b(yn * g_s[:, cs])
        o_ref[...] = x_ref[...] + _dot(yg_s[...], wo_ref[...])


def _rwkv_decode_call(xt, shift0, s0t, wts, *, nb, nt):
    n, d = xt.shape
    npair = d // LANES
    nhead = 2 * npair
    kern = functools.partial(_rwkv_decode_kernel, nb=nb, nt=nt)
    full = lambda shape: pl.BlockSpec(shape, lambda h: (0,) * len(shape))
    st_spec = pl.BlockSpec((1, RW_HEAD, RW_HEAD, nb), lambda h: (h, 0, 0, 0))
    in_specs = [full((n, d)), full((nb, d))] + [_const_spec(wt.shape, single_buffer=True) for wt in wts] + [st_spec]
    scratch = ([pltpu.VMEM((npair, LANES, n), F32) for _ in range(7)]
               + [pltpu.VMEM((n, d), F32), pltpu.VMEM((n, d), F32), pltpu.VMEM((n, d), BF16)])
    return pl.pallas_call(
        kern,
        grid=(nhead,),
        in_specs=in_specs,
        out_specs=[full((n, d)), st_spec, full((nb, d))],
        out_shape=[jax.ShapeDtypeStruct((n, d), F32), jax.ShapeDtypeStruct(s0t.shape, F32),
                   jax.ShapeDtypeStruct((nb, d), F32)],
        scratch_shapes=scratch,
        compiler_params=pltpu.CompilerParams(dimension_semantics=("arbitrary",), vmem_limit_bytes=VMEM_LIMIT),
        name="rwkv_decode",
    )(xt, shift0, *wts, s0t)


def _mlstm_kernel(*refs, tm, seq_rows, npad, sample, L, lsub):
    nsub = L // lsub
    nchunk = tm // L
    it = iter(refs)
    x_ref = next(it)
    if sample:
        cvx_ref, mrow_ref = next(it), next(it)
    gn_ref, win_ref, bif_ref, cw_ref, cb_ref, nw_ref, wout_ref = [next(it) for _ in range(7)]
    if sample:
        c0_ref, n0_ref = next(it), next(it)
    o_ref, cout_ref, nout_ref, mout_ref, cvout_ref = [next(it) for _ in range(5)]
    gt_s, q_s, k_s, v_s, og_s, gc_s, ho_s, ext_s = [next(it) for _ in range(8)]
    if not sample:
        cp_s, np_s, m_s = [next(it) for _ in range(3)]
    nqk, nv, d = cw_ref.shape[1], nw_ref.shape[1], x_ref.shape[1]
    half = nqk // 2
    nheads = half // ML_DK
    ngate = 2 * nheads
    npair = nheads // 2

    lane = lax.broadcasted_iota(jnp.int32, (1, LANES), 1)
    mlo = lane < ML_DK
    glane = lax.broadcasted_iota(jnp.int32, (1, ngate), 1)
    isf_c = glane >= nheads
    grow = lax.broadcasted_iota(jnp.int32, (ngate, 1), 0)
    isf_r = grow >= nheads
    hlane = lax.broadcasted_iota(jnp.int32, (1, nheads), 1)

    x = x_ref[...]
    xb = _b(_rms(x, gn_ref[...]))
    raw = _dot(xb, win_ref[:, :nqk])
    row = lax.broadcasted_iota(jnp.int32, (tm, 1), 0)
    if sample:
        srow = row & (seq_rows - 1)
        raw = jnp.where((srow >= npad - (ML_CONV - 1)) & (srow < npad), cvx_ref[...], raw)
        cvout_ref[...] = raw
        ext_s[:SUBLANES, :] = jnp.zeros((SUBLANES, nqk), F32)
    else:
        j = pl.program_id(1)

        @pl.when(j == 0)
        def _():
            ext_s[:SUBLANES, :] = jnp.zeros((SUBLANES, nqk), F32)
            cp_s[...] = jnp.zeros(cp_s.shape, F32)
            np_s[...] = jnp.zeros(np_s.shape, F32)
            m_s[...] = jnp.zeros(m_s.shape, F32)

        cvout_ref[...] = raw[tm - SUBLANES:tm, :]
    ext_s[SUBLANES:, :] = raw
    qk = cb_ref[...] + cw_ref[ML_CONV - 1:ML_CONV, :] * raw
    for s in range(1, ML_CONV):
        qk = qk + cw_ref[ML_CONV - 1 - s:ML_CONV - s, :] * ext_s[SUBLANES - s:SUBLANES - s + tm, :]
    if not sample:
        ext_s[:SUBLANES, :] = raw[tm - SUBLANES:tm, :]
    qk = qk * _sigmoid(qk)
    q_s[...] = qk[:, :half] * (ML_DK ** -0.5)
    k_s[...] = qk[:, half:]
    v_s[...] = _dot(xb, win_ref[:, nqk:nqk + nv])
    og_s[...] = _sigmoid(_dot(xb, win_ref[:, nqk + nv:nqk + nv + d]))
    ifp = _dot(xb, win_ref[:, nqk + nv + d:]) + bif_ref[...]
    gcol = jnp.where(isf_c, -_softplus(-ifp), ifp)
    if npad:
        keep = (row & (seq_rows - 1)) >= npad
        gcol = jnp.where(keep, gcol, jnp.where(isf_c, 0.0, NEG_BIG))
    gc_s[...] = gcol
    gct = gcol.T
    for c in range(nchunk):
        gt_s[c] = gct[:, c * L:(c + 1) * L]

    ti = lax.broadcasted_iota(jnp.int32, (L, L), 0)
    si = lax.broadcasted_iota(jnp.int32, (L, L), 1)
    sh = _log2(lsub)
    same = (ti >> sh) == (si >> sh)
    causal = same & (si <= ti)
    tril = jnp.where(causal, 1.0, 0.0).astype(F32)
    triu = jnp.where(same & (ti <= si), 1.0, 0.0).astype(F32)

    def chunk(c, carry):
        r0 = pl.multiple_of(c * L, L)
        rows = pl.ds(r0, L)
        gc = gc_s[rows, :]
        gt = gt_s[c]
        bcs = _dot(tril, jnp.where(isf_c, gc, 0.0), HI)
        brs = _dot(jnp.where(isf_r, gt, 0.0), triu, HI)
        blast = _seq_last(bcs, lsub)
        if sample:
            mcols = mrow_ref[rows, :]
        else:
            mcols = m_s[...]
        heads = range(nheads)
        prs = range(npair)
        ps = [slice(pp * LANES, (pp + 1) * LANES) for pp in prs]
        hs = [slice(h * ML_DV, (h + 1) * ML_DV) for h in heads]
        q2 = [q_s[rows, ps[pp]] for pp in prs]
        k2 = [k_s[rows, ps[pp]] for pp in prs]
        k2b = [_b(z) for z in k2]
        if sample:
            c_in = [[jnp.concatenate([c0_ref[c * nsub + q, 2 * pp], c0_ref[c * nsub + q, 2 * pp + 1]], axis=0)
                     for q in range(nsub)] for pp in prs]
            n_in = [[n0_ref[c * nsub + q][:, ps[pp]] for q in range(nsub)] for pp in prs]
            n_rows = [jnp.concatenate([jnp.broadcast_to(n_in[pp][q], (lsub, LANES)) for q in range(nsub)], axis=0)
                      for pp in prs]
        else:
            c_prev = [cp_s[pp] for pp in prs]
            n_prev = [np_s[:, ps[pp]] for pp in prs]
            n_rows = n_prev
        bcol = [_colsel(bcs, glane, nheads + h) for h in heads]
        licol = [_colsel(gc, glane, h) for h in heads]
        mcol = [_colsel(mcols, hlane, h) for h in heads]
        blcol = [_colsel(blast, glane, nheads + h) for h in heads]
        dlog = [jnp.where(causal, bcol[h] - (brs[nheads + h:nheads + h + 1, :] - gt[h:h + 1, :]), -jnp.inf)
                for h in heads]
        ginter = [bcol[h] + mcol[h] for h in heads]
        m_t = [jnp.maximum(ginter[h], jnp.max(dlog[h], axis=-1, keepdims=True)) for h in heads]
        dw = [jnp.exp(dlog[h] - m_t[h]) for h in heads]
        winter = [jnp.exp(ginter[h] - m_t[h]) for h in heads]
        qh = [jnp.where(mlo if h % 2 == 0 else jnp.logical_not(mlo), q2[h // 2], 0.0) for h in heads]
        qhb = [_b(z) for z in qh]
        sc = [_dot_nt(qhb[h], k2b[h // 2]) * dw[h] for h in heads]
        if sample:
            inter = [jnp.concatenate([_dot(qhb[h][q * lsub:(q + 1) * lsub], _b(c_in[h // 2][q]))
                                      for q in range(nsub)], axis=0) for h in heads]
        else:
            cb = [_b(z) for z in c_prev]
            inter = [_dot(qhb[h], cb[h // 2]) for h in heads]
        vh = [v_s[rows, hs[h]] for h in heads]
        num = [winter[h] * inter[h] + _dot(_b(sc[h]), _b(vh[h])) for h in heads]
        den = [winter[h] * jnp.sum(qh[h] * n_rows[h // 2], axis=-1, keepdims=True)
               + jnp.sum(sc[h], axis=-1, keepdims=True) for h in heads]
        for h in heads:
            hout = num[h] / jnp.maximum(jnp.abs(den[h]), jnp.exp(-m_t[h]))
            hn = hout * lax.rsqrt(jnp.mean(hout * hout, axis=-1, keepdims=True) + NORM_EPS) * nw_ref[:, hs[h]]
            ho_s[rows, hs[h]] = _b(hn * og_s[rows, hs[h]])
        mnew = [_seq_last(m_t[h], lsub) for h in heads]
        ws = [jnp.exp(blcol[h] - bcol[h] + licol[h] - mnew[h]) for h in heads]
        wstc = [jnp.exp(blcol[h] + mcol[h] - mnew[h]) for h in heads]
        mt_all = jnp.zeros((L, nheads), F32)
        for h in heads:
            mt_all = jnp.where(hlane == h, m_t[h], mt_all)
        for pp in prs:
            lo, hi = 2 * pp, 2 * pp + 1
            wsk = jnp.where(mlo, ws[lo], ws[hi]) * k2[pp]
            wst = jnp.where(mlo, wstc[lo], wstc[hi])
            wvb = _b(jnp.concatenate([ws[lo] * vh[lo], ws[hi] * vh[hi]], axis=0))
            kmb = jnp.concatenate([jnp.where(mlo, k2b[pp], 0), jnp.where(mlo, 0, k2b[pp])], axis=0)
            if sample:
                for q in range(nsub):
                    last = (q + 1) * lsub - 1
                    sel = [slice(blk * L + q * lsub, blk * L + (q + 1) * lsub) for blk in range(2)]
                    wq = wst[last:last + 1, :]
                    wrow = jnp.concatenate([jnp.broadcast_to(wstc[h_][last:last + 1, :], (ML_DK, 1)) for h_ in (lo, hi)],
                                           axis=0)
                    upd = _dot_tn(jnp.concatenate([kmb[s_] for s_ in sel], axis=0),
                                  jnp.concatenate([wvb[s_] for s_ in sel], axis=0))
                    c_new = c_in[pp][q] * wrow + upd
                    cout_ref[c * nsub + q, 2 * pp] = c_new[:ML_DK]
                    cout_ref[c * nsub + q, 2 * pp + 1] = c_new[ML_DK:]
                    nq = n_in[pp][q] * wq + jnp.sum(wsk[q * lsub:(q + 1) * lsub], axis=0, keepdims=True)
                    nout_ref[c * nsub + q, :, ps[pp]] = nq
            else:
                wrow = jnp.concatenate([jnp.broadcast_to(wstc[h_], (ML_DK, 1)) for h_ in (lo, hi)], axis=0)
                cp_s[pp] = c_prev[pp] * wrow + _dot_tn(kmb, wvb)
                np_s[:, ps[pp]] = n_prev[pp] * wst + jnp.sum(wsk, axis=0, keepdims=True)
        if sample:
            mout_ref[rows, :] = mt_all
        else:
            m_s[...] = mt_all[L - 1:L, :]
        return carry

    lax.fori_loop(0, nchunk, chunk, 0)
    if not sample:
        for pp in range(npair):
            cout_ref[0, 2 * pp] = cp_s[pp][:ML_DK]
            cout_ref[0, 2 * pp + 1] = cp_s[pp][ML_DK:]
        nout_ref[0] = np_s[...]
        mout_ref[0] = m_s[...]
    o_ref[...] = x_ref[...] + _dot(ho_s[...], wout_ref[...])


def _mlstm_call(x, cvx, mrow, c0p, n0, wts, *, nseq, seq_rows, npad, sample, tm):
    n, d = x.shape
    nqk = wts[3].shape[1]
    nv = wts[5].shape[1]
    assert nqk % LANES == 0 and nv % LANES == 0 and d % LANES == 0
    nheads = nqk // 2 // ML_DK
    npair = nheads // 2
    chunk = CHUNK if sample else ML_CHUNK
    lsub = seq_rows if sample else chunk
    kern = functools.partial(_mlstm_kernel, tm=tm, seq_rows=seq_rows, npad=npad, sample=sample, L=chunk, lsub=lsub)
    if sample:
        grid = (n // tm,)
        tile = lambda i: (i, 0)
        spt = tm // seq_rows
        c_spec = pl.BlockSpec((spt, nheads, ML_DK, ML_DV), lambda i: (i, 0, 0, 0))
        n_spec = pl.BlockSpec((spt, 1, nqk // 2), lambda i: (i, 0, 0))
        m_spec = pl.BlockSpec((tm, nheads), tile)
        m_shape = (n, nheads)
        cv_spec = pl.BlockSpec((tm, nqk), tile)
        cv_shape = (n, nqk)
        sem = ("arbitrary",)
    else:
        tps = seq_rows // tm
        grid = (nseq, tps)
        tile = lambda b, j: (b * tps + j, 0)
        c_spec = pl.BlockSpec((1, nheads, ML_DK, ML_DV), lambda b, j: (b, 0, 0, 0))
        n_spec = pl.BlockSpec((1, 1, nqk // 2), lambda b, j: (b, 0, 0))
        m_spec = pl.BlockSpec((1, 1, nheads), lambda b, j: (b, 0, 0))
        m_shape = (nseq, 1, nheads)
        cv_spec = pl.BlockSpec((SUBLANES, nqk), lambda b, j: (b, 0))
        cv_shape = (nseq * SUBLANES, nqk)
        sem = ("arbitrary", "arbitrary")
    x_spec = pl.BlockSpec((tm, d), tile)
    in_specs = [x_spec]
    args = [x]
    if sample:
        in_specs += [pl.BlockSpec((tm, nqk), tile), pl.BlockSpec((tm, nheads), tile)]
        args += [cvx, mrow]
    for wt in wts:
        in_specs.append(_const_spec(wt.shape, single_buffer=True))
        args.append(wt)
    if sample:
        in_specs += [c_spec, n_spec]
        args += [c0p, n0]
    scratch = [pltpu.VMEM((tm // chunk, 2 * nheads, chunk), F32), pltpu.VMEM((tm, nqk // 2), F32),
               pltpu.VMEM((tm, nqk // 2), F32),
               pltpu.VMEM((tm, nv), F32), pltpu.VMEM((tm, nv), F32), pltpu.VMEM((tm, 2 * nheads), F32),
               pltpu.VMEM((tm, nv), BF16), pltpu.VMEM((tm + SUBLANES, nqk), F32)]
    if not sample:
        scratch += [pltpu.VMEM((npair, LANES, LANES), F32),
                    pltpu.VMEM((1, nqk // 2), F32), pltpu.VMEM((1, nheads), F32)]
    return pl.pallas_call(
        kern,
        grid=grid,
        in_specs=in_specs,
        out_specs=[x_spec, c_spec, n_spec, m_spec, cv_spec],
        out_shape=[jax.ShapeDtypeStruct((n, d), F32),
                   jax.ShapeDtypeStruct((nseq, nheads, ML_DK, ML_DV), F32),
                   jax.ShapeDtypeStruct((nseq, 1, nqk // 2), F32),
                   jax.ShapeDtypeStruct(m_shape, F32),
                   jax.ShapeDtypeStruct(cv_shape, F32)],
        scratch_shapes=scratch,
        compiler_params=pltpu.CompilerParams(dimension_semantics=sem, vmem_limit_bytes=VMEM_LIMIT),
        name="mlstm_sample" if sample else "mlstm_prompt",
    )(*args)


def kernel(x_prompt, x_sample, state_rwkv_S, state_rwkv_shift, state_mlstm_C, state_mlstm_n, state_mlstm_m,
           state_mlstm_conv, norm_ffa, ffa_wg, ffa_wu, ffa_wd, norm_mix, norm_ffb, ffb_wg, ffb_wu, ffb_wd,
           rw_mu, rw_wr, rw_wk, rw_wv, rw_wo, rw_w0, rw_w1, rw_w2, rw_a0, rw_a1, rw_a2, rw_g1, rw_g2,
           rw_k_k, rw_k_a, rw_r_k, rw_gn_w, rw_gn_b, ml_w_in, ml_b_if, ml_conv_w, ml_conv_b, ml_norm_w,
           ml_w_out, norm_final):
    bp, tp, d = x_prompt.shape
    bs, ts, _ = x_sample.shape
    depth = norm_ffa.shape[0]
    slot = SUBLANES
    npad = slot - ts
    assert 0 < ts <= slot and npad >= ML_CONV - 1
    tm_p = min(512, tp)
    tm_ml = min(1024, tp)
    tm_s = min(4 * CHUNK, bs * slot)
    assert tp % tm_p == 0 and tm_p % CHUNK == 0 and tp % tm_ml == 0 and tm_ml % ML_CHUNK == 0
    assert (bs * slot) % tm_s == 0 and tm_s % CHUNK == 0
    ml_heads = ml_b_if.shape[1] // 2
    nqk = 2 * ml_heads * ML_DK
    nv = ml_heads * ML_DV

    xp = x_prompt.reshape(bp * tp, d)
    xs = x_sample.transpose(1, 0, 2).reshape(ts * bs, d)

    def to_slots(z):
        return jnp.concatenate([jnp.zeros((bs, npad, d), F32), z.reshape(ts, bs, d).transpose(1, 0, 2)],
                               axis=1).reshape(bs * slot, d)

    def from_slots(z):
        return z.reshape(bs, slot, d)[:, npad:].transpose(1, 0, 2).reshape(ts * bs, d)

    row2 = lambda a: a.reshape(1, -1)

    tm_ffn = min(512, bs * ts)
    assert (bp * tp) % tm_ffn == 0 and (bs * ts) % tm_ffn == 0
    ffa = (norm_ffa.reshape(depth, 1, d), _b(ffa_wg), _b(ffa_wu), _b(ffa_wd))
    ffb = (norm_ffb.reshape(depth, 1, d), _b(ffb_wg), _b(ffb_wu), _b(ffb_wd))

    def ffn(xp_, xs_, wset, layer, final):
        return _ffn_call(xp_, xs_, *wset, row2(norm_final), layer, final_norm=final, tm=tm_ffn)

    new_p = {k_: [] for k_ in ("S", "shift", "C", "n", "m", "conv")}
    new_s = {k_: [] for k_ in ("S", "shift", "C", "n", "m", "conv")}
    for i in range(depth):
        xp, xs = ffn(xp, xs, ffa, i, False)
        j = i // 2
        if i % 2 == 0:
            vec = jnp.concatenate([row2(v_) for v_ in (norm_mix[i], rw_w0[j], rw_a0[j], rw_k_k[j], rw_k_a[j], rw_r_k[j],
                                                        rw_gn_w[j], rw_gn_b[j])], axis=0)
            wts = [vec, rw_mu[j], _b(rw_wr[j]), _b(rw_wk[j]), _b(rw_wv[j]), _b(rw_wo[j]),
                   _b(rw_w1[j]), _b(rw_w2[j]), _b(rw_a1[j]), _b(rw_a2[j]), _b(rw_g1[j]), _b(rw_g2[j])]
            xp, sbd, tail = _rwkv_call(xp, wts, nseq=bp, seq_rows=tp, tm=tm_p)
            new_p["S"].append(sbd)
            new_p["shift"].append(tail.reshape(bp, SUBLANES, d)[:, SUBLANES - 1])
            xs, s_t, sh_new = _rwkv_decode_call(xs, state_rwkv_shift[j], jnp.transpose(state_rwkv_S[j], (1, 2, 3, 0)),
                                                wts, nb=bs, nt=ts)
            new_s["S"].append(jnp.transpose(s_t, (3, 0, 1, 2)))
            new_s["shift"].append(sh_new)
        else:
            wts = [row2(norm_mix[i]), _b(ml_w_in[j]), row2(ml_b_if[j]), ml_conv_w[j],
                   row2(ml_conv_b[j]), row2(ml_norm_w[j]), _b(ml_w_out[j])]
            xp, cp, n_, m_, tail = _mlstm_call(xp, None, None, None, None, wts, nseq=bp, seq_rows=tp, npad=0,
                                               sample=False, tm=tm_ml)
            new_p["C"].append(jnp.swapaxes(cp, -1, -2))
            new_p["n"].append(n_.reshape(bp, ml_heads, ML_DK))
            new_p["m"].append(m_.reshape(bp, ml_heads))
            new_p["conv"].append(tail.reshape(bp, SUBLANES, nqk)[:, SUBLANES - (ML_CONV - 1):])
            conv0 = state_mlstm_conv[j]
            cvx = jnp.concatenate([jnp.zeros((bs, npad - (ML_CONV - 1), nqk), F32), conv0,
                                   jnp.zeros((bs, slot - npad, nqk), F32)], axis=1).reshape(bs * slot, nqk)
            mrow = jnp.repeat(state_mlstm_m[j], slot, axis=0)
            xs, cp, n_, mt, raw = _mlstm_call(to_slots(xs), cvx, mrow, jnp.swapaxes(state_mlstm_C[j], -1, -2),
                                              state_mlstm_n[j].reshape(bs, 1, ml_heads * ML_DK), wts, nseq=bs,
                                              seq_rows=slot, npad=npad, sample=True, tm=tm_s)
            xs = from_slots(xs)
            new_s["C"].append(jnp.swapaxes(cp, -1, -2))
            new_s["n"].append(n_.reshape(bs, ml_heads, ML_DK))
            new_s["m"].append(mt.reshape(bs, slot, ml_heads)[:, slot - 1])
            new_s["conv"].append(raw.reshape(bs, slot, nqk)[:, slot - (ML_CONV - 1):])
        xp, xs = ffn(xp, xs, ffb, i, i == depth - 1)
    y_prompt = xp.reshape(bp, tp, d)
    y_sample = xs.reshape(ts, bs, d).transpose(1, 0, 2)
    st = lambda lst: jnp.stack(lst)
    return (y_prompt, y_sample,
            st(new_p["S"]), st(new_p["shift"]), st(new_p["C"]), st(new_p["n"]), st(new_p["m"]), st(new_p["conv"]),
            st(new_s["S"]), st(new_s["shift"]), st(new_s["C"]), st(new_s["n"]), st(new_s["m"]), st(new_s["conv"]))
```
